```python
import math
import jax, jax.numpy as jnp
from jax import lax
import numpy as np

D_MODEL = 1024
BATCH = 4
SEQ = 8192
DEPTH = 2

D_SSM = D_MODEL // 4
D_ATT = D_MODEL // 2
D_GM = D_MODEL // 4
SSM_GROUP = 16
SSM_GROUPS = D_SSM // SSM_GROUP
SSM_STATE = 64
ATT_HEAD_DIM = 64
ATT_HEADS = D_ATT // ATT_HEAD_DIM
GM_HEADS = 4
GM_HEAD_DIM = D_GM // GM_HEADS
CHUNK = 128
Q_BLOCK = 128
D_IN = D_SSM + 3 * D_ATT + ATT_HEADS + 2 * D_GM
IN_SPLITS = (D_SSM, D_SSM + D_ATT, D_SSM + 2 * D_ATT, D_SSM + 3 * D_ATT,
             D_SSM + 3 * D_ATT + ATT_HEADS)
OUT_SPLITS = (D_SSM, D_SSM + D_ATT)
D_FF = 2816
N_EXPERTS = 8
TOP_K = 2
D_FF_EXPERT = 3584
N_DENSE = (DEPTH + 1) // 2
N_MOE = DEPTH // 2
EPS = 1e-6

kernel_name = "hymba_s5_fox_gmlp_moe_trunk"


def rmsnorm(x, g):
    xf = x.astype(jnp.float32)
    y = xf * lax.rsqrt(jnp.mean(xf * xf, axis=-1, keepdims=True) + EPS)
    return (y * g.astype(jnp.float32)).astype(x.dtype)


def swiglu(h, w_gate, w_up, w_down):
    return (jax.nn.silu(h @ w_gate) * (h @ w_up)) @ w_down


def s5_mixer(u, lam_re, lam_im, log_dt, b_re, b_im, c_re, c_im, d_skip, w_glu):
    bsz, seq, _ = u.shape
    f32 = jnp.float32
    uf = u.astype(f32).reshape(bsz, seq, SSM_GROUPS, SSM_GROUP)
    lam = lax.complex(lam_re.astype(f32), lam_im.astype(f32))
    dt = jnp.exp(log_dt.astype(f32))[:, None]
    lam_bar = jnp.exp(lam * dt)
    b = lax.complex(b_re.astype(f32), b_im.astype(f32))
    b_bar = ((lam_bar - 1.0) / lam)[..., None] * b
    bu = jnp.einsum('gph,bsgh->bsgp', b_bar, uf.astype(jnp.complex64))
    a = jnp.broadcast_to(lam_bar, bu.shape)

    def combine(left, right):
        a_l, s_l = left
        a_r, s_r = right
        return a_r * a_l, a_r * s_l + s_r

    _, states = lax.associative_scan(combine, (a, bu), axis=1)
    c = lax.complex(c_re.astype(f32), c_im.astype(f32))
    y = jnp.real(jnp.einsum('ghp,bsgp->bsgh', c, states))
    y = y + d_skip.astype(f32).reshape(SSM_GROUPS, SSM_GROUP) * uf
    y = jax.nn.gelu(y.reshape(bsz, seq, D_SSM)).astype(u.dtype)
    return y * jax.nn.sigmoid(y @ w_glu)


def forgetting_attention(q, k, v, fgate_logit):
    bsz, seq = q.shape[0], q.shape[1]
    f32 = jnp.float32
    log_f = jax.nn.log_sigmoid(fgate_logit.astype(f32))
    cum = jnp.cumsum(log_f, axis=1).transpose(0, 2, 1)
    kh = k.transpose(0, 2, 1, 3)
    vh = v.transpose(0, 2, 1, 3)
    n_blocks = seq // Q_BLOCK
    qb = q.transpose(0, 2, 1, 3).reshape(bsz, ATT_HEADS, n_blocks, Q_BLOCK, ATT_HEAD_DIM)
    qb = qb.transpose(2, 0, 1, 3, 4)
    cb = cum.reshape(bsz, ATT_HEADS, n_blocks, Q_BLOCK).transpose(2, 0, 1, 3)
    starts = jnp.arange(n_blocks, dtype=jnp.int32) * Q_BLOCK
    key_pos = jnp.arange(seq, dtype=jnp.int32)
    scale = ATT_HEAD_DIM ** -0.5

    def block(args):
        q_blk, c_blk, start = args
        s = jnp.einsum('bhqd,bhkd->bhqk', q_blk, kh).astype(f32) * scale
        s = s + c_blk[..., :, None] - cum[..., None, :]
        q_pos = start + jnp.arange(Q_BLOCK, dtype=jnp.int32)
        s = jnp.where(key_pos[None, :] <= q_pos[:, None], s, -jnp.inf)
        p = jax.nn.softmax(s, axis=-1)
        return jnp.einsum('bhqk,bhkd->bhqd', p.astype(vh.dtype), vh)

    out = lax.map(block, (qb, cb, starts))
    return out.transpose(1, 0, 3, 2, 4).reshape(bsz, seq, D_ATT)


def spatial_gating(z, ln_g, ln_b, w_s, b_s):
    bsz, seq = z.shape[0], z.shape[1]
    f32 = jnp.float32
    u, v = jnp.split(z, 2, axis=-1)
    vf = v.astype(f32)
    mean = jnp.mean(vf, axis=-1, keepdims=True)
    var = jnp.mean(jnp.square(vf - mean), axis=-1, keepdims=True)
    vn = (vf - mean) * lax.rsqrt(var + EPS) * ln_g.astype(f32) + ln_b.astype(f32)
    vn = vn.reshape(bsz, seq // CHUNK, CHUNK, GM_HEADS, GM_HEAD_DIM)
    causal = jnp.tril(jnp.ones((CHUNK, CHUNK), f32))
    ws = w_s.astype(f32) * causal[None]
    mixed = jnp.einsum('gts,bnsgc->bntgc', ws, vn)
    mixed = mixed + b_s.astype(f32).T[None, None, :, :, None]
    return u * mixed.reshape(bsz, seq, D_GM).astype(u.dtype)


def mixer_layer(h, w_in, b_forget, lam_re, lam_im, log_dt, b_re, b_im, c_re, c_im,
                d_skip, w_glu, ln_g, ln_b, w_s, b_s, group_norm_g, w_out):
    bsz, seq, _ = h.shape
    proj = h @ w_in
    u_ssm, q, k, v, f_logit, z = jnp.split(proj, IN_SPLITS, axis=-1)
    heads = (bsz, seq, ATT_HEADS, ATT_HEAD_DIM)
    y_ssm = s5_mixer(u_ssm, lam_re, lam_im, log_dt, b_re, b_im, c_re, c_im, d_skip, w_glu)
    y_att = forgetting_attention(q.reshape(heads), k.reshape(heads), v.reshape(heads),
                                 f_logit + b_forget)
    y_gm = spatial_gating(jax.nn.gelu(z), ln_g, ln_b, w_s, b_s)
    g_ssm, g_att, g_gm = jnp.split(group_norm_g, OUT_SPLITS)
    y = jnp.concatenate([rmsnorm(y_ssm, g_ssm), rmsnorm(y_att, g_att), rmsnorm(y_gm, g_gm)],
                        axis=-1)
    return y @ w_out


def moe_swiglu(h, router_w, w_gate, w_up, w_down):
    f32 = jnp.float32
    logits = (h @ router_w).astype(f32)
    top_vals, top_idx = lax.top_k(logits, TOP_K)
    top_w = jax.nn.softmax(top_vals, axis=-1)
    gates = jnp.sum(jax.nn.one_hot(top_idx, N_EXPERTS, dtype=f32) * top_w[..., None], axis=-2)
    out = jnp.zeros(h.shape, f32)
    for e in range(N_EXPERTS):
        out = out + gates[..., e:e + 1] * swiglu(h, w_gate[e], w_up[e], w_down[e]).astype(f32)
    return out.astype(h.dtype)


def setup_inputs(seed: int = 0) -> dict:
    key = jax.random.key(seed)
    ks = jax.random.split(key, 32)
    f32 = jnp.float32
    nrm = lambda k, shape, s: jax.random.normal(k, shape, f32) * s
    gain = lambda k, shape: 1.0 + 0.02 * jax.random.normal(k, shape, f32)
    L, G, P, H = DEPTH, SSM_GROUPS, SSM_STATE, SSM_GROUP
    n_idx = jnp.arange(P, dtype=f32)
    return {
        "x": jax.random.normal(ks[0], (BATCH, SEQ, D_MODEL), f32),
        "mix_norm_g": gain(ks[1], (L, D_MODEL)),
        "w_in": nrm(ks[2], (L, D_MODEL, D_IN), D_MODEL ** -0.5),
        "b_forget": 3.0 + 0.1 * jax.random.normal(ks[3], (L, ATT_HEADS), f32),
        "ssm_lambda_re": -0.5 + 0.01 * jax.random.normal(ks[4], (L, G, P), f32),
        "ssm_lambda_im": math.pi * n_idx + 0.01 * jax.random.normal(ks[5], (L, G, P), f32),
        "ssm_log_dt": jax.random.uniform(ks[6], (L, G), f32, math.log(1e-3), math.log(1e-1)),
        "ssm_b_re": nrm(ks[7], (L, G, P, H), (2.0 * H) ** -0.5),
        "ssm_b_im": nrm(ks[8], (L, G, P, H), (2.0 * H) ** -0.5),
        "ssm_c_re": nrm(ks[9], (L, G, H, P), (2.0 * P) ** -0.5),
        "ssm_c_im": nrm(ks[10], (L, G, H, P), (2.0 * P) ** -0.5),
        "ssm_d": nrm(ks[11], (L, D_SSM), 1.0),
        "ssm_w_glu": nrm(ks[12], (L, D_SSM, D_SSM), D_SSM ** -0.5),
        "gm_ln_g": gain(ks[13], (L, D_GM)),
        "gm_ln_b": nrm(ks[14], (L, D_GM), 0.02),
        "gm_w_s": nrm(ks[15], (L, GM_HEADS, CHUNK, CHUNK), CHUNK ** -0.5),
        "gm_b_s": 1.0 + 0.1 * jax.random.normal(ks[16], (L, GM_HEADS, CHUNK), f32),
        "group_norm_g": gain(ks[17], (L, D_MODEL)),
        "w_out": nrm(ks[18], (L, D_MODEL, D_MODEL), D_MODEL ** -0.5),
        "ffn_norm_g": gain(ks[19], (L, D_MODEL)),
        "dense_w_gate": nrm(ks[20], (N_DENSE, D_MODEL, D_FF), D_MODEL ** -0.5),
        "dense_w_up": nrm(ks[21], (N_DENSE, D_MODEL, D_FF), D_MODEL ** -0.5),
        "dense_w_down": nrm(ks[22], (N_DENSE, D_FF, D_MODEL), D_FF ** -0.5),
        "router_w": nrm(ks[23], (N_MOE, D_MODEL, N_EXPERTS), D_MODEL ** -0.5),
        "moe_w_gate": nrm(ks[24], (N_MOE, N_EXPERTS, D_MODEL, D_FF_EXPERT), D_MODEL ** -0.5),
        "moe_w_up": nrm(ks[25], (N_MOE, N_EXPERTS, D_MODEL, D_FF_EXPERT), D_MODEL ** -0.5),
        "moe_w_down": nrm(ks[26], (N_MOE, N_EXPERTS, D_FF_EXPERT, D_MODEL), D_FF_EXPERT ** -0.5),
        "final_norm_g": gain(ks[27], (D_MODEL,)),
    }


def reference(x, mix_norm_g, w_in, b_forget, ssm_lambda_re, ssm_lambda_im, ssm_log_dt,
              ssm_b_re, ssm_b_im, ssm_c_re, ssm_c_im, ssm_d, ssm_w_glu, gm_ln_g, gm_ln_b,
              gm_w_s, gm_b_s, group_norm_g, w_out, ffn_norm_g, dense_w_gate, dense_w_up,
              dense_w_down, router_w, moe_w_gate, moe_w_up, moe_w_down, final_norm_g):
    for layer in range(DEPTH):
        h = rmsnorm(x, mix_norm_g[layer])
        x = x + mixer_layer(h, w_in[layer], b_forget[layer], ssm_lambda_re[layer],
                            ssm_lambda_im[layer], ssm_log_dt[layer], ssm_b_re[layer],
                            ssm_b_im[layer], ssm_c_re[layer], ssm_c_im[layer], ssm_d[layer],
                            ssm_w_glu[layer], gm_ln_g[layer], gm_ln_b[layer], gm_w_s[layer],
                            gm_b_s[layer], group_norm_g[layer], w_out[layer])
        h = rmsnorm(x, ffn_norm_g[layer])
        j = layer // 2
        if layer % 2 == 0:
            x = x + swiglu(h, dense_w_gate[j], dense_w_up[j], dense_w_down[j])
        else:
            x = x + moe_swiglu(h, router_w[j], moe_w_gate[j], moe_w_up[j], moe_w_down[j])
    return rmsnorm(x, final_norm_g)
```

```python
import functools
import math

import jax
import jax.numpy as jnp
from jax import lax
from jax.experimental import pallas as pl
from jax.experimental.pallas import tpu as pltpu

F32 = jnp.float32
BF16 = jnp.bfloat16

D_MODEL = 1024
D_SSM = 256
D_ATT = 512
D_GM = 256
SSM_GROUP = 16
SSM_GROUPS = 16
SSM_STATE = 64
ATT_HEADS = 8
ATT_HEAD_DIM = 64
GM_HEADS = 4
GM_HEAD_DIM = 64
CHUNK = 128
N_EXPERTS = 8
EPS = 1e-6

LANES = 128
SUBLANES = 8
SSM_T = 8
SSM_ROW = SSM_T * D_SSM
SSM_NSTATE = SSM_GROUPS * SSM_STATE

C_U = 0
C_Q = C_U + D_SSM
C_K = C_Q + D_ATT
C_V = C_K + D_ATT
C_Z = C_V + D_ATT
C_F = C_Z + 2 * D_GM
D_IN_PAD = C_F + LANES

VMEM_LIMIT = 56 * 1024 * 1024


def _cparams(*sem):
    return pltpu.CompilerParams(dimension_semantics=sem, vmem_limit_bytes=VMEM_LIMIT)


def _resident(shape):
    nd = len(shape)
    return pl.BlockSpec(shape, lambda *_: (0,) * nd, pipeline_mode=pl.Buffered(1))


def _rms(x, g):
    return x * lax.rsqrt(jnp.mean(x * x, axis=-1, keepdims=True) + EPS) * g


def _gelu(x):
    c = math.sqrt(2.0 / math.pi)
    return 0.5 * x * (1.0 + jnp.tanh(c * (x + 0.044715 * (x * x * x))))


def _sigmoid(x):
    return 1.0 / (1.0 + jnp.exp(-x))


def _split3(x):
    p1 = x.astype(BF16)
    r1 = x - p1.astype(F32)
    p2 = r1.astype(BF16)
    r2 = r1 - p2.astype(F32)
    return p1, p2, r2.astype(BF16)


def _dot(a, b):
    return jnp.dot(a, b, preferred_element_type=F32)


def _inproj_body(x_ref, g_ref, w_ref, bf_ref, lng_ref, lnb_ref, ws_ref, bs_ref,
                 u_ref, q_ref, k_ref, v_ref, cum_ref, ygm_ref, carry_ref, *, tiles_per_seq):
    i = pl.program_id(0)

    @pl.when(i % tiles_per_seq == 0)
    def _():
        carry_ref[...] = jnp.zeros_like(carry_ref)

    tm = x_ref.shape[0]
    hb = _rms(x_ref[...], g_ref[...]).astype(BF16)

    def proj(lo, width):
        return _dot(hb, w_ref[:, lo:lo + width])

    u_ref[...] = proj(C_U, D_SSM)
    q_ref[...] = proj(C_Q, D_ATT).astype(BF16)
    k_ref[...] = proj(C_K, D_ATT).astype(BF16)
    v_ref[...] = proj(C_V, D_ATT).astype(BF16)

    f = proj(C_F, LANES) + bf_ref[...]
    logf = jnp.minimum(f, 0.0) - jnp.log(1.0 + jnp.exp(-jnp.abs(f)))
    row = lax.broadcasted_iota(jnp.int32, (tm, tm), 0)
    col = lax.broadcasted_iota(jnp.int32, (tm, tm), 1)
    tri = (col <= row).astype(BF16)
    p1, p2, p3 = _split3(logf)
    cum = _dot(tri, p1) + _dot(tri, p2) + _dot(tri, p3) + carry_ref[...]
    cum_ref[...] = cum
    carry_ref[...] = cum[tm - 1:tm, :]

    zg = _gelu(proj(C_Z, 2 * D_GM))
    ug = zg[:, :D_GM]
    vg = zg[:, D_GM:]
    mean = jnp.mean(vg, axis=-1, keepdims=True)
    cen = vg - mean
    var = jnp.mean(cen * cen, axis=-1, keepdims=True)
    vn = (cen * lax.rsqrt(var + EPS) * lng_ref[...] + lnb_ref[...]).astype(BF16)
    lane_head = lax.broadcasted_iota(jnp.int32, (CHUNK, D_GM), 1) >> 6
    for c in range(tm // CHUNK):
        rows = slice(c * CHUNK, (c + 1) * CHUNK)
        vc = vn[rows, :]
        mixed = _dot(ws_ref[0], vc)
        for g in range(1, GM_HEADS):
            mixed = jnp.where(lane_head == g, _dot(ws_ref[g], vc), mixed)
        mixed = mixed + bs_ref[...]
        ygm_ref[rows, :] = (ug[rows, :] * mixed).astype(BF16)


def _inproj(x2, g, w, bf, lng, lnb, ws, bs, *, seq, tm):
    n = x2.shape[0]
    row = lambda width: pl.BlockSpec((tm, width), lambda i: (i, 0))
    return pl.pallas_call(
        functools.partial(_inproj_body, tiles_per_seq=seq // tm),
        grid=(n // tm,),
        in_specs=[row(D_MODEL), _resident((1, D_MODEL)), _resident((D_MODEL, D_IN_PAD)),
                  _resident((1, LANES)), _resident((1, D_GM)), _resident((1, D_GM)),
                  _resident((GM_HEADS, CHUNK, CHUNK)), _resident((CHUNK, D_GM))],
        out_specs=[row(D_SSM), row(D_ATT), row(D_ATT), row(D_ATT), row(LANES), row(D_GM)],
        out_shape=[jax.ShapeDtypeStruct((n, D_SSM), F32),
                   jax.ShapeDtypeStruct((n, D_ATT), BF16),
                   jax.ShapeDtypeStruct((n, D_ATT), BF16),
                   jax.ShapeDtypeStruct((n, D_ATT), BF16),
                   jax.ShapeDtypeStruct((n, LANES), F32),
                   jax.ShapeDtypeStruct((n, D_GM), BF16)],
        scratch_shapes=[pltpu.VMEM((1, LANES), F32)],
        compiler_params=_cparams("arbitrary"),
        name="inproj",
    )(x2, g, w, bf, lng, lnb, ws, bs)


def _ssm_state_body(u_ref, f_ref, s_ref):
    s_ref[...] = _dot(u_ref[...].astype(BF16), f_ref[...])


def _ssm_state(u2, fmat, *, tr):
    nc = u2.shape[0]
    return pl.pallas_call(
        _ssm_state_body,
        grid=(nc // tr,),
        in_specs=[pl.BlockSpec((tr, SSM_ROW), lambda i: (i, 0)),
                  _resident((SSM_ROW, 2 * SSM_NSTATE))],
        out_specs=pl.BlockSpec((tr, 2 * SSM_NSTATE), lambda i: (i, 0)),
        out_shape=jax.ShapeDtypeStruct((nc, 2 * SSM_NSTATE), F32),
        compiler_params=_cparams("parallel"),
        name="ssm_state",
    )(u2, fmat)


def _ssm_scan_body(s_ref, a_ref, xp_ref, st_ref):
    @pl.when(pl.program_id(1) == 0)
    def _():
        st_ref[...] = jnp.zeros_like(st_ref)

    tc = s_ref.shape[0]
    are = a_ref[:, :SSM_NSTATE]
    aim = a_ref[:, SSM_NSTATE:]

    def step(c8, carry):
        xre, xim = carry
        base = pl.multiple_of(c8 * SUBLANES, SUBLANES)
        blk = s_ref[pl.ds(base, SUBLANES), :]
        prev_re, prev_im = [], []
        for r in range(SUBLANES):
            prev_re.append(xre)
            prev_im.append(xim)
            sre = blk[r:r + 1, :SSM_NSTATE]
            sim = blk[r:r + 1, SSM_NSTATE:]
            xre, xim = are * xre - aim * xim + sre, are * xim + aim * xre + sim
        xp_ref[pl.ds(base, SUBLANES), :SSM_NSTATE] = jnp.concatenate(prev_re, axis=0)
        xp_ref[pl.ds(base, SUBLANES), SSM_NSTATE:] = jnp.concatenate(prev_im, axis=0)
        return xre, xim

    xre, xim = lax.fori_loop(0, tc // SUBLANES, step,
                             (st_ref[:, :SSM_NSTATE], st_ref[:, SSM_NSTATE:]))
    st_ref[:, :SSM_NSTATE] = xre
    st_ref[:, SSM_NSTATE:] = xim


def _ssm_scan(s3, a, *, tc):
    b, ncb, w = s3.shape
    spec = pl.BlockSpec((None, tc, w), lambda bi, ci: (bi, ci, 0))
    return pl.pallas_call(
        _ssm_scan_body,
        grid=(b, ncb // tc),
        in_specs=[spec, pl.BlockSpec((1, w), lambda bi, ci: (0, 0))],
        out_specs=spec,
        out_shape=jax.ShapeDtypeStruct(s3.shape, F32),
        scratch_shapes=[pltpu.VMEM((1, w), F32)],
        compiler_params=_cparams("parallel", "arbitrary"),
        name="ssm_scan",
    )(s3, a)


def _ssm_out_body(u_ref, xp_ref, m_ref, e_ref, y_ref):
    y_ref[...] = (_dot(u_ref[...].astype(BF16), m_ref[...])
                  + _dot(xp_ref[...].astype(BF16), e_ref[...]))


def _ssm_out(u2, xp, mmat, emat, *, tr):
    nc = u2.shape[0]
    return pl.pallas_call(
        _ssm_out_body,
        grid=(nc // tr,),
        in_specs=[pl.BlockSpec((tr, SSM_ROW), lambda i: (i, 0)),
                  pl.BlockSpec((tr, 2 * SSM_NSTATE), lambda i: (i, 0)),
                  _resident((SSM_ROW, SSM_ROW)), _resident((2 * SSM_NSTATE, SSM_ROW))],
        out_specs=pl.BlockSpec((tr, SSM_ROW), lambda i: (i, 0)),
        out_shape=jax.ShapeDtypeStruct((nc, SSM_ROW), F32),
        compiler_params=_cparams("parallel"),
        name="ssm_out",
    )(u2, xp, mmat, emat)


def _ssm_matrices(lam_re, lam_im, log_dt, b_re, b_im, c_re, c_im):
    t, g, p, h = SSM_T, SSM_GROUPS, SSM_STATE, SSM_GROUP
    lam = lax.complex(lam_re.astype(F32), lam_im.astype(F32))
    dt = jnp.exp(log_dt.astype(F32))[:, None]
    lam_bar = jnp.exp(lam * dt)
    b_bar = ((lam_bar - 1.0) / lam)[..., None] * lax.complex(b_re.astype(F32), b_im.astype(F32))
    c = lax.complex(c_re.astype(F32), c_im.astype(F32))
    steps = jnp.arange(t + 1, dtype=F32)
    pw = jnp.exp((lam * dt)[None] * steps[:, None, None])
    eye = jnp.eye(g, dtype=F32)

    kern = jnp.real(jnp.einsum('gop,tgp,gpi->tgio', c, pw[:t], b_bar))
    lag = jnp.arange(t)[None, :] - jnp.arange(t)[:, None]
    k_st = jnp.where((lag >= 0)[:, :, None, None, None], kern[jnp.clip(lag, 0, t - 1)], 0.0)
    mmat = jnp.einsum('stgio,gk->sgitko', k_st, eye).reshape(t * g * h, t * g * h)

    fw = pw[:t][::-1][:, :, :, None] * b_bar[None]
    fmat = jnp.einsum('sgpi,gk->sgikp', fw, eye)
    fmat = jnp.concatenate([jnp.real(fmat).reshape(t * g * h, g * p),
                            jnp.imag(fmat).reshape(t * g * h, g * p)], axis=1)

    ew = c[None] * pw[1:, :, None, :]
    emat = jnp.einsum('tgop,gk->gptko', ew, eye)
    emat = jnp.concatenate([jnp.real(emat).reshape(g * p, t * g * h),
                            -jnp.imag(emat).reshape(g * p, t * g * h)], axis=0)

    a_row = pw[t].reshape(1, g * p)
    a_row = jnp.concatenate([jnp.real(a_row), jnp.imag(a_row)], axis=1)
    return mmat.astype(BF16), fmat.astype(BF16), emat.astype(BF16), a_row


def _attn_body(q_ref, k_ref, v_ref, cq_ref, ck_ref, o_ref, *, tq):
    hp = pl.program_id(1)
    qi = pl.program_id(2)
    q = q_ref[...]
    lane = lax.broadcasted_iota(jnp.int32, (tq, LANES), 1)
    cq_all = cq_ref[...]
    rowpos = lax.broadcasted_iota(jnp.int32, (tq, tq), 0)
    colpos = lax.broadcasted_iota(jnp.int32, (tq, tq), 1)
    outs = []
    upper = lane >= ATT_HEAD_DIM
    for j in range(2):
        qj = jnp.where(upper if j else ~upper, q, jnp.zeros_like(q))
        cq = jnp.sum(jnp.where(lane == 2 * hp + j, cq_all, 0.0), axis=-1, keepdims=True)

        def tile(t, carry, masked):
            m, l, acc = carry
            start = pl.multiple_of(t * tq, tq)
            kt = k_ref[pl.ds(start, tq), :]
            vt = v_ref[pl.ds(start, tq), :]
            s = lax.dot_general(qj, kt, (((1,), (1,)), ((), ())), preferred_element_type=F32)
            s = s - ck_ref[j:j + 1, pl.ds(start, tq)]
            if masked:
                s = jnp.where(colpos <= rowpos, s, -jnp.inf)
            m_new = jnp.maximum(m, jnp.max(s, axis=-1, keepdims=True) + cq)
            alpha = jnp.exp(m - m_new)
            p = jnp.exp(s + (cq - m_new))
            l = alpha * l + jnp.sum(p, axis=-1, keepdims=True)
            acc = alpha * acc + _dot(p.astype(BF16), vt)
            return m_new, l, acc

        init = (jnp.full((tq, 1), -jnp.inf, F32), jnp.zeros((tq, 1), F32),
                jnp.zeros((tq, LANES), F32))
        carry = lax.fori_loop(0, qi, functools.partial(tile, masked=False), init)
        _, l, acc = tile(qi, carry, True)
        outs.append(acc / l)
    o_ref[...] = jnp.where(upper, outs[1], outs[0]).astype(BF16)


def _attention(q3, k3, v3, cq3, ck4, *, tq):
    b, s, _ = q3.shape
    npairs = ATT_HEADS // 2
    return pl.pallas_call(
        functools.partial(_attn_body, tq=tq),
        grid=(b, npairs, s // tq),
        in_specs=[pl.BlockSpec((None, tq, LANES), lambda bi, hp, qi: (bi, qi, hp)),
                  pl.BlockSpec((None, s, LANES), lambda bi, hp, qi: (bi, 0, hp)),
                  pl.BlockSpec((None, s, LANES), lambda bi, hp, qi: (bi, 0, hp)),
                  pl.BlockSpec((None, tq, LANES), lambda bi, hp, qi: (bi, qi, 0)),
                  pl.BlockSpec((None, None, 2, s), lambda bi, hp, qi: (bi, hp, 0, 0))],
        out_specs=pl.BlockSpec((None, tq, LANES), lambda bi, hp, qi: (bi, qi, hp)),
        out_shape=jax.ShapeDtypeStruct((b, s, D_ATT), BF16),
        compiler_params=_cparams("parallel", "parallel", "arbitrary"),
        name="attention",
    )(q3, k3, v3, cq3, ck4)


def _top2_gates(logits):
    lane = lax.broadcasted_iota(jnp.int32, logits.shape, 1).astype(F32)
    lg = jnp.where(lane < N_EXPERTS, logits, -jnp.inf)
    m1 = jnp.max(lg, axis=-1, keepdims=True)
    i1 = jnp.min(jnp.where(lg == m1, lane, float(LANES)), axis=-1, keepdims=True)
    lg2 = jnp.where(lane == i1, -jnp.inf, lg)
    m2 = jnp.max(lg2, axis=-1, keepdims=True)
    i2 = jnp.min(jnp.where(lg2 == m2, lane, float(LANES)), axis=-1, keepdims=True)
    e = jnp.exp(m2 - m1)
    w1 = 1.0 / (1.0 + e)
    w2 = e / (1.0 + e)
    return jnp.where(lane == i1, w1, 0.0) + jnp.where(lane == i2, w2, 0.0)


def _outproj_body(*refs, with_router):
    if with_router:
        (x_ref, ys_ref, us_ref, ya_ref, yg_ref, d_ref, wglu_ref, gn_ref, w_ref, fg_ref, rw_ref,
         xo_ref, h_ref, gates_ref) = refs
    else:
        (x_ref, ys_ref, us_ref, ya_ref, yg_ref, d_ref, wglu_ref, gn_ref, w_ref, fg_ref,
         xo_ref, h_ref) = refs

    y1 = _gelu(ys_ref[...] + d_ref[...] * us_ref[...])
    y_ssm = y1 * _sigmoid(_dot(y1.astype(BF16), wglu_ref[...]))

    o1, o2 = D_SSM, D_SSM + D_ATT
    acc = _dot(_rms(y_ssm, gn_ref[:, :o1]).astype(BF16), w_ref[:o1, :])
    acc += _dot(_rms(ya_ref[...].astype(F32), gn_ref[:, o1:o2]).astype(BF16), w_ref[o1:o2, :])
    acc += _dot(_rms(yg_ref[...].astype(F32), gn_ref[:, o2:]).astype(BF16), w_ref[o2:, :])
    xn = x_ref[...] + acc
    xo_ref[...] = xn
    hn = _rms(xn, fg_ref[...])
    h_ref[...] = hn.astype(BF16)
    if with_router:
        h1, h2, h3 = _split3(hn)
        r1, r2, r3 = _split3(rw_ref[...])
        logits = (_dot(h1, r1) + _dot(h1, r2) + _dot(h2, r1)
                  + _dot(h1, r3) + _dot(h2, r2) + _dot(h3, r1))
        gates_ref[...] = _top2_gates(logits)


def _outproj(x2, ys, us, ya, yg, d, wglu, gn, w, fg, rw, *, tm):
    n = x2.shape[0]
    with_router = rw is not None
    row = lambda width: pl.BlockSpec((tm, width), lambda i: (i, 0))
    in_specs = [row(D_MODEL), row(D_SSM), row(D_SSM), row(D_ATT), row(D_GM),
                _resident((1, D_SSM)), _resident((D_SSM, D_SSM)), _resident((1, D_MODEL)),
                _resident((D_MODEL, D_MODEL)), _resident((1, D_MODEL))]
    out_specs = [row(D_MODEL), row(D_MODEL)]
    out_shape = [jax.ShapeDtypeStruct((n, D_MODEL), F32), jax.ShapeDtypeStruct((n, D_MODEL), BF16)]
    args = [x2, ys, us, ya, yg, d, wglu, gn, w, fg]
    if with_router:
        in_specs.append(_resident((D_MODEL, LANES)))
        out_specs.append(row(LANES))
        out_shape.append(jax.ShapeDtypeStruct((n, LANES), F32))
        args.append(rw)
    return pl.pallas_call(
        functools.partial(_outproj_body, with_router=with_router),
        grid=(n // tm,),
        in_specs=in_specs, out_specs=out_specs, out_shape=out_shape,
        compiler_params=_cparams("parallel"),
        name="outproj_router" if with_router else "outproj",
    )(*args)


def _swiglu_tile(h, wg, wu, wd):
    a = _dot(h, wg)
    return _dot((a * _sigmoid(a) * _dot(h, wu)).astype(BF16), wd)


def _finish(x, acc, fin_ref, o_ref):
    xn = x + acc
    o_ref[...] = xn if fin_ref is None else _rms(xn, fin_ref[...])


def _dense_ffn_body(*refs, final_norm):
    if final_norm:
        h_ref, x_ref, wg_ref, wu_ref, wd_ref, fin_ref, o_ref, acc_ref = refs
    else:
        h_ref, x_ref, wg_ref, wu_ref, wd_ref, o_ref, acc_ref = refs
        fin_ref = None
    f = pl.program_id(1)

    @pl.when(f == 0)
    def _():
        acc_ref[...] = jnp.zeros_like(acc_ref)

    acc_ref[...] += _swiglu_tile(h_ref[...], wg_ref[...], wu_ref[...], wd_ref[...])

    @pl.when(f == pl.num_programs(1) - 1)
    def _():
        _finish(x_ref[...], acc_ref[...], fin_ref, o_ref)


def _dense_ffn(h, x2, wg, wu, wd, fin, *, tm, tf):
    n = x2.shape[0]
    dff = wg.shape[1]
    row = lambda: pl.BlockSpec((tm, D_MODEL), lambda i, f: (i, 0))
    in_specs = [row(), row(),
                pl.BlockSpec((D_MODEL, tf), lambda i, f: (0, f)),
                pl.BlockSpec((D_MODEL, tf), lambda i, f: (0, f)),
                pl.BlockSpec((tf, D_MODEL), lambda i, f: (f, 0))]
    args = [h, x2, wg, wu, wd]
    if fin is not None:
        in_specs.append(pl.BlockSpec((1, D_MODEL), lambda i, f: (0, 0)))
        args.append(fin)
    return pl.pallas_call(
        functools.partial(_dense_ffn_body, final_norm=fin is not None),
        grid=(n // tm, dff // tf),
        in_specs=in_specs, out_specs=row(),
        out_shape=jax.ShapeDtypeStruct((n, D_MODEL), F32),
        scratch_shapes=[pltpu.VMEM((tm, D_MODEL), F32)],
        compiler_params=_cparams("parallel", "arbitrary"),
        name="dense_ffn",
    )(*args)


def _moe_ffn_body(*refs, final_norm):
    if final_norm:
        h_ref, x_ref, gates_ref, wg_ref, wu_ref, wd_ref, fin_ref, o_ref, acc_ref = refs
    else:
        h_ref, x_ref, gates_ref, wg_ref, wu_ref, wd_ref, o_ref, acc_ref = refs
        fin_ref = None
    e = pl.program_id(1)
    f = pl.program_id(2)

    @pl.when((e == 0) & (f == 0))
    def _():
        acc_ref[...] = jnp.zeros_like(acc_ref)

    gates = gates_ref[...]
    lane = lax.broadcasted_iota(jnp.int32, gates.shape, 1)
    gate = jnp.sum(jnp.where(lane == e, gates, 0.0), axis=-1, keepdims=True)
    acc_ref[...] += gate * _swiglu_tile(h_ref[...], wg_ref[...], wu_ref[...], wd_ref[...])

    @pl.when((e == pl.num_programs(1) - 1) & (f == pl.num_programs(2) - 1))
    def _():
        _finish(x_ref[...], acc_ref[...], fin_ref, o_ref)


def _moe_ffn(h, x2, gates, wg, wu, wd, fin, *, tm, tf):
    n = x2.shape[0]
    ne, _, dff = wg.shape
    row = lambda width: pl.BlockSpec((tm, width), lambda i, e, f: (i, 0))
    in_specs = [row(D_MODEL), row(D_MODEL), row(LANES),
                pl.BlockSpec((None, D_MODEL, tf), lambda i, e, f: (e, 0, f)),
                pl.BlockSpec((None, D_MODEL, tf), lambda i, e, f: (e, 0, f)),
                pl.BlockSpec((None, tf, D_MODEL), lambda i, e, f: (e, f, 0))]
    args = [h, x2, gates, wg, wu, wd]
    if fin is not None:
        in_specs.append(pl.BlockSpec((1, D_MODEL), lambda i, e, f: (0, 0)))
        args.append(fin)
    return pl.pallas_call(
        functools.partial(_moe_ffn_body, final_norm=fin is not None),
        grid=(n // tm, ne, dff // tf),
        in_specs=in_specs, out_specs=row(D_MODEL),
        out_shape=jax.ShapeDtypeStruct((n, D_MODEL), F32),
        scratch_shapes=[pltpu.VMEM((tm, D_MODEL), F32)],
        compiler_params=_cparams("parallel", "arbitrary", "arbitrary"),
        name="moe_ffn",
    )(*args)


def _largest_tile(total, cap, mult):
    best = mult
    for t in range(mult, min(total, cap) + 1, mult):
        if total % t == 0:
            best = t
    return best


def _pad_lanes(a):
    return jnp.pad(a, [(0, 0)] * (a.ndim - 1) + [(0, LANES - a.shape[-1])])


def kernel(x, mix_norm_g, w_in, b_forget, ssm_lambda_re, ssm_lambda_im, ssm_log_dt, ssm_b_re, ssm_b_im, ssm_c_re, ssm_c_im, ssm_d, ssm_w_glu, gm_ln_g, gm_ln_b, gm_w_s, gm_b_s, group_norm_g, w_out, ffn_norm_g, dense_w_gate, dense_w_up, dense_w_down, router_w, moe_w_gate, moe_w_up, moe_w_down, final_norm_g):
    bsz, seq, _ = x.shape
    n = bsz * seq
    depth = w_in.shape[0]
    assert seq % CHUNK == 0 and seq % SSM_T == 0
    tm = _largest_tile(seq, 512, CHUNK)
    tq = _largest_tile(seq, 256, LANES)
    nc = n // SSM_T
    ncb = seq // SSM_T
    tr = _largest_tile(nc, 512, 8)
    tc = _largest_tile(ncb, 128, 8)

    x2 = x.reshape(n, D_MODEL).astype(F32)
    tril = jnp.tril(jnp.ones((CHUNK, CHUNK), F32))
    i0, i1, i2, i3, i4 = (D_SSM, D_SSM + D_ATT, D_SSM + 2 * D_ATT, D_SSM + 3 * D_ATT,
                          D_SSM + 3 * D_ATT + ATT_HEADS)
    for layer in range(depth):
        wl = w_in[layer]
        w_re = jnp.concatenate(
            [wl[:, :i0], wl[:, i0:i1] * (ATT_HEAD_DIM ** -0.5), wl[:, i1:i3], wl[:, i4:],
             _pad_lanes(wl[:, i3:i4])], axis=1).astype(BF16)
        ws = (gm_w_s[layer].astype(F32) * tril[None]).astype(BF16)
        bs = jnp.repeat(gm_b_s[layer].astype(F32).T, GM_HEAD_DIM, axis=1)
        u, q, k, v, cum, ygm = _inproj(
            x2, mix_norm_g[layer].reshape(1, D_MODEL), w_re,
            _pad_lanes(b_forget[layer].reshape(1, ATT_HEADS).astype(F32)),
            gm_ln_g[layer].reshape(1, D_GM), gm_ln_b[layer].reshape(1, D_GM), ws, bs,
            seq=seq, tm=tm)

        mmat, fmat, emat, a_row = _ssm_matrices(
            ssm_lambda_re[layer], ssm_lambda_im[layer], ssm_log_dt[layer], ssm_b_re[layer],
            ssm_b_im[layer], ssm_c_re[layer], ssm_c_im[layer])
        u2 = u.reshape(nc, SSM_ROW)
        s_loc = _ssm_state(u2, fmat, tr=tr)
        xprev = _ssm_scan(s_loc.reshape(bsz, ncb, 2 * SSM_NSTATE), a_row, tc=tc)
        y_ssm = _ssm_out(u2, xprev.reshape(nc, 2 * SSM_NSTATE), mmat, emat, tr=tr).reshape(n, D_SSM)

        ck = cum[:, :ATT_HEADS].reshape(bsz, seq, ATT_HEADS // 2, 2).transpose(0, 2, 3, 1)
        y_att = _attention(q.reshape(bsz, seq, D_ATT), k.reshape(bsz, seq, D_ATT),
                           v.reshape(bsz, seq, D_ATT), cum.reshape(bsz, seq, LANES), ck,
                           tq=tq).reshape(n, D_ATT)

        is_moe = layer % 2 == 1
        j = layer // 2
        rw = _pad_lanes(router_w[j].astype(F32)) if is_moe else None
        outs = _outproj(x2, y_ssm, u, y_att, ygm, ssm_d[layer].reshape(1, D_SSM).astype(F32),
                        ssm_w_glu[layer].astype(BF16), group_norm_g[layer].reshape(1, D_MODEL),
                        w_out[layer].astype(BF16), ffn_norm_g[layer].reshape(1, D_MODEL), rw, tm=tm)
        fin = final_norm_g.reshape(1, D_MODEL) if layer == depth - 1 else None
        if is_moe:
            x_mid, h, gates = outs
            dffe = moe_w_gate.shape[-1]
            x2 = _moe_ffn(h, x_mid, gates, moe_w_gate[j].astype(BF16), moe_w_up[j].astype(BF16),
                          moe_w_down[j].astype(BF16), fin, tm=tm,
                          tf=_largest_tile(dffe, 1024, LANES))
        else:
            x_mid, h = outs
            dff = dense_w_gate.shape[-1]
            x2 = _dense_ffn(h, x_mid, dense_w_gate[j].astype(BF16), dense_w_up[j].astype(BF16),
                            dense_w_down[j].astype(BF16), fin, tm=tm,
                            tf=_largest_tile(dff, 1536, LANES))
    return x2.reshape(bsz, seq, D_MODEL).astype(x.dtype)
```

```python
import functools
import math

import jax
import jax.numpy as jnp
from jax import lax
from jax.experimental import pallas as pl
from jax.experimental.pallas import tpu as pltpu

F32 = jnp.float32
BF16 = jnp.bfloat16

D_MODEL = 1024
D_SSM = 256
D_ATT = 512
D_GM = 256
SSM_GROUP = 16
SSM_GROUPS = 16
SSM_STATE = 64
ATT_HEADS = 8
ATT_HEAD_DIM = 64
GM_HEADS = 4
GM_HEAD_DIM = 64
CHUNK = 128
N_EXPERTS = 8
EPS = 1e-6
LOG2E = math.log2(math.e)

LANES = 128
SUBLANES = 8
SSM_T = 8
SSM_ROW = SSM_T * D_SSM
SSM_NSTATE = SSM_GROUPS * SSM_STATE

C_U = 0
C_Q = C_U + D_SSM
C_K = C_Q + D_ATT
C_V = C_K + D_ATT
C_Z = C_V + D_ATT
C_F = C_Z + 2 * D_GM
D_IN_PAD = C_F + LANES

VMEM_LIMIT = 56 * 1024 * 1024


def _cparams(*sem):
    return pltpu.CompilerParams(dimension_semantics=sem, vmem_limit_bytes=VMEM_LIMIT)


def _resident(shape):
    nd = len(shape)
    return pl.BlockSpec(shape, lambda *_: (0,) * nd, pipeline_mode=pl.Buffered(1))


def _rms(x, g):
    return x * lax.rsqrt(jnp.mean(x * x, axis=-1, keepdims=True) + EPS) * g


def _gelu(x):
    c = math.sqrt(2.0 / math.pi)
    return 0.5 * x * (1.0 + jnp.tanh(c * (x + 0.044715 * (x * x * x))))


def _sigmoid(x):
    return 1.0 / (1.0 + jnp.exp(-x))


def _split3(x):
    p1 = x.astype(BF16)
    r1 = x - p1.astype(F32)
    p2 = r1.astype(BF16)
    r2 = r1 - p2.astype(F32)
    return p1, p2, r2.astype(BF16)


def _dot(a, b):
    return jnp.dot(a, b, preferred_element_type=F32)


def _inproj_body(x_ref, g_ref, w_ref, bf_ref, lng_ref, lnb_ref, ws_ref, bs_ref,
                 u_ref, q_ref, k_ref, v_ref, cum_ref, ygm_ref, carry_ref, *, tiles_per_seq):
    i = pl.program_id(0)

    @pl.when(i % tiles_per_seq == 0)
    def _():
        carry_ref[...] = jnp.zeros_like(carry_ref)

    tm = x_ref.shape[0]
    hb = _rms(x_ref[...], g_ref[...]).astype(BF16)

    def proj(lo, width):
        return _dot(hb, w_ref[:, lo:lo + width])

    u_ref[...] = proj(C_U, D_SSM)
    q_ref[...] = proj(C_Q, D_ATT).astype(BF16)
    k_ref[...] = proj(C_K, D_ATT).astype(BF16)
    v_ref[...] = proj(C_V, D_ATT).astype(BF16)

    f = proj(C_F, LANES) + bf_ref[...]
    logf = (jnp.minimum(f, 0.0) - jnp.log(1.0 + jnp.exp(-jnp.abs(f)))) * LOG2E
    row = lax.broadcasted_iota(jnp.int32, (tm, tm), 0)
    col = lax.broadcasted_iota(jnp.int32, (tm, tm), 1)
    tri = (col <= row).astype(BF16)
    p1, p2, p3 = _split3(logf)
    cum = _dot(tri, p1) + _dot(tri, p2) + _dot(tri, p3) + carry_ref[...]
    cum_ref[...] = cum
    carry_ref[...] = cum[tm - 1:tm, :]

    zg = _gelu(proj(C_Z, 2 * D_GM))
    ug = zg[:, :D_GM]
    vg = zg[:, D_GM:]
    mean = jnp.mean(vg, axis=-1, keepdims=True)
    cen = vg - mean
    var = jnp.mean(cen * cen, axis=-1, keepdims=True)
    vn = (cen * lax.rsqrt(var + EPS) * lng_ref[...] + lnb_ref[...]).astype(BF16)
    lane_head = lax.broadcasted_iota(jnp.int32, (CHUNK, D_GM), 1) >> 6
    for c in range(tm // CHUNK):
        rows = slice(c * CHUNK, (c + 1) * CHUNK)
        vc = vn[rows, :]
        mixed = _dot(ws_ref[0], vc)
        for g in range(1, GM_HEADS):
            mixed = jnp.where(lane_head == g, _dot(ws_ref[g], vc), mixed)
        mixed = mixed + bs_ref[...]
        ygm_ref[rows, :] = (ug[rows, :] * mixed).astype(BF16)


def _inproj(x2, g, w, bf, lng, lnb, ws, bs, *, seq, tm):
    n = x2.shape[0]
    row = lambda width: pl.BlockSpec((tm, width), lambda i: (i, 0))
    return pl.pallas_call(
        functools.partial(_inproj_body, tiles_per_seq=seq // tm),
        grid=(n // tm,),
        in_specs=[row(D_MODEL), _resident((1, D_MODEL)), _resident((D_MODEL, D_IN_PAD)),
                  _resident((1, LANES)), _resident((1, D_GM)), _resident((1, D_GM)),
                  _resident((GM_HEADS, CHUNK, CHUNK)), _resident((CHUNK, D_GM))],
        out_specs=[row(D_SSM), row(D_ATT), row(D_ATT), row(D_ATT), row(LANES), row(D_GM)],
        out_shape=[jax.ShapeDtypeStruct((n, D_SSM), F32),
                   jax.ShapeDtypeStruct((n, D_ATT), BF16),
                   jax.ShapeDtypeStruct((n, D_ATT), BF16),
                   jax.ShapeDtypeStruct((n, D_ATT), BF16),
                   jax.ShapeDtypeStruct((n, LANES), F32),
                   jax.ShapeDtypeStruct((n, D_GM), BF16)],
        scratch_shapes=[pltpu.VMEM((1, LANES), F32)],
        compiler_params=_cparams("arbitrary"),
        name="inproj",
    )(x2, g, w, bf, lng, lnb, ws, bs)


def _ssm_state_body(u_ref, f_ref, s_ref):
    s_ref[...] = _dot(u_ref[...].astype(BF16), f_ref[...])


def _ssm_state(u2, fmat, *, tr):
    nc = u2.shape[0]
    return pl.pallas_call(
        _ssm_state_body,
        grid=(nc // tr,),
        in_specs=[pl.BlockSpec((tr, SSM_ROW), lambda i: (i, 0)),
                  _resident((SSM_ROW, 2 * SSM_NSTATE))],
        out_specs=pl.BlockSpec((tr, 2 * SSM_NSTATE), lambda i: (i, 0)),
        out_shape=jax.ShapeDtypeStruct((nc, 2 * SSM_NSTATE), F32),
        compiler_params=_cparams("parallel"),
        name="ssm_state",
    )(u2, fmat)


def _ssm_scan_body(s_ref, a_ref, xp_ref, st_ref):
    @pl.when(pl.program_id(1) == 0)
    def _():
        st_ref[...] = jnp.zeros_like(st_ref)

    tc = s_ref.shape[0]
    are = a_ref[:, :SSM_NSTATE]
    aim = a_ref[:, SSM_NSTATE:]

    def step(c8, carry):
        xre, xim = carry
        base = pl.multiple_of(c8 * SUBLANES, SUBLANES)
        blk = s_ref[pl.ds(base, SUBLANES), :]
        prev_re, prev_im = [], []
        for r in range(SUBLANES):
            prev_re.append(xre)
            prev_im.append(xim)
            sre = blk[r:r + 1, :SSM_NSTATE]
            sim = blk[r:r + 1, SSM_NSTATE:]
            xre, xim = are * xre - aim * xim + sre, are * xim + aim * xre + sim
        xp_ref[pl.ds(base, SUBLANES), :SSM_NSTATE] = jnp.concatenate(prev_re, axis=0)
        xp_ref[pl.ds(base, SUBLANES), SSM_NSTATE:] = jnp.concatenate(prev_im, axis=0)
        return xre, xim

    xre, xim = lax.fori_loop(0, tc // SUBLANES, step,
                             (st_ref[:, :SSM_NSTATE], st_ref[:, SSM_NSTATE:]))
    st_ref[:, :SSM_NSTATE] = xre
    st_ref[:, SSM_NSTATE:] = xim


def _ssm_scan(s3, a, *, tc):
    b, ncb, w = s3.shape
    spec = pl.BlockSpec((None, tc, w), lambda bi, ci: (bi, ci, 0))
    return pl.pallas_call(
        _ssm_scan_body,
        grid=(b, ncb // tc),
        in_specs=[spec, pl.BlockSpec((1, w), lambda bi, ci: (0, 0))],
        out_specs=spec,
        out_shape=jax.ShapeDtypeStruct(s3.shape, F32),
        scratch_shapes=[pltpu.VMEM((1, w), F32)],
        compiler_params=_cparams("parallel", "arbitrary"),
        name="ssm_scan",
    )(s3, a)


def _ssm_out_body(u_ref, xp_ref, m_ref, e_ref, y_ref):
    y_ref[...] = (_dot(u_ref[...].astype(BF16), m_ref[...])
                  + _dot(xp_ref[...].astype(BF16), e_ref[...]))


def _ssm_out(u2, xp, mmat, emat, *, tr):
    nc = u2.shape[0]
    return pl.pallas_call(
        _ssm_out_body,
        grid=(nc // tr,),
        in_specs=[pl.BlockSpec((tr, SSM_ROW), lambda i: (i, 0)),
                  pl.BlockSpec((tr, 2 * SSM_NSTATE), lambda i: (i, 0)),
                  _resident((SSM_ROW, SSM_ROW)), _resident((2 * SSM_NSTATE, SSM_ROW))],
        out_specs=pl.BlockSpec((tr, SSM_ROW), lambda i: (i, 0)),
        out_shape=jax.ShapeDtypeStruct((nc, SSM_ROW), F32),
        compiler_params=_cparams("parallel"),
        name="ssm_out",
    )(u2, xp, mmat, emat)


def _ssm_matrices(lam_re, lam_im, log_dt, b_re, b_im, c_re, c_im):
    t, g, p, h = SSM_T, SSM_GROUPS, SSM_STATE, SSM_GROUP
    hp = lax.Precision.HIGHEST
    lr, li = lam_re.astype(F32), lam_im.astype(F32)
    dt = jnp.exp(log_dt.astype(F32))[:, None]
    steps = jnp.arange(t + 1, dtype=F32)[:, None, None]
    mag = jnp.exp((lr * dt)[None] * steps)
    ang = (li * dt)[None] * steps
    pr, pi = mag * jnp.cos(ang), mag * jnp.sin(ang)
    nr, ni = pr[1] - 1.0, pi[1]
    den = lr * lr + li * li
    zr, zi = (nr * lr + ni * li) / den, (ni * lr - nr * li) / den
    br, bi = b_re.astype(F32), b_im.astype(F32)
    bbr = zr[..., None] * br - zi[..., None] * bi
    bbi = zr[..., None] * bi + zi[..., None] * br
    cr, ci = c_re.astype(F32)[None], c_im.astype(F32)[None]
    cpr = cr * pr[:, :, None, :] - ci * pi[:, :, None, :]
    cpi = cr * pi[:, :, None, :] + ci * pr[:, :, None, :]
    eye = jnp.eye(g, dtype=F32)

    kern = (jnp.einsum('tgop,gpi->tgio', cpr[:t], bbr, precision=hp)
            - jnp.einsum('tgop,gpi->tgio', cpi[:t], bbi, precision=hp))
    lag = jnp.arange(t)[None, :] - jnp.arange(t)[:, None]
    k_st = jnp.where((lag >= 0)[:, :, None, None, None], kern[jnp.clip(lag, 0, t - 1)], 0.0)
    mmat = (k_st[:, :, :, :, None, :] * eye[None, None, :, None, :, None])
    mmat = mmat.transpose(0, 2, 3, 1, 4, 5).reshape(t * g * h, t * g * h)

    qr, qi = pr[:t][::-1][..., None], pi[:t][::-1][..., None]
    fr = qr * bbr[None] - qi * bbi[None]
    fi = qr * bbi[None] + qi * bbr[None]

    def rows_to_state(w):
        w = w.transpose(0, 1, 3, 2)[:, :, :, None, :] * eye[None, :, None, :, None]
        return w.reshape(t * g * h, g * p)

    fmat = jnp.concatenate([rows_to_state(fr), rows_to_state(fi)], axis=1)

    def state_to_rows(w):
        w = w.transpose(1, 3, 0, 2)[:, :, :, None, :] * eye[:, None, None, :, None]
        return w.reshape(g * p, t * g * h)

    emat = jnp.concatenate([state_to_rows(cpr[1:]), -state_to_rows(cpi[1:])], axis=0)

    a_row = jnp.concatenate([pr[t].reshape(1, g * p), pi[t].reshape(1, g * p)], axis=1)
    return mmat.astype(BF16), fmat.astype(BF16), emat.astype(BF16), a_row


def _attn_body(q_ref, k_ref, v_ref, cq_ref, ck_ref, o_ref, m_ref, acc_ref, *, tq):
    hp = pl.program_id(1)
    qi = pl.program_id(2)
    q = q_ref[...]
    lane = lax.broadcasted_iota(jnp.int32, (tq, LANES), 1)
    upper = lane >= ATT_HEAD_DIM
    own = [~upper, upper]
    cq_all = cq_ref[...]
    qs = [jnp.where(own[j], q, jnp.zeros_like(q)) for j in range(2)]
    cqs = [jnp.sum(jnp.where(lane == 2 * hp + j, cq_all, 0.0), axis=-1, keepdims=True)
           for j in range(2)]
    rowpos = lax.broadcasted_iota(jnp.int32, (tq, tq), 0)
    colpos = lax.broadcasted_iota(jnp.int32, (tq, tq), 1)
    for j in range(2):
        m_ref[j] = jnp.full((tq, 1), -jnp.inf, F32)
        acc_ref[j] = jnp.zeros((tq, LANES), F32)

    def tile(t, masked):
        start = pl.multiple_of(t * tq, tq)
        kt = k_ref[pl.ds(start, tq), :]
        vt = v_ref[pl.ds(start, tq), :]
        for j in range(2):
            s = lax.dot_general(qs[j], kt, (((1,), (1,)), ((), ())), preferred_element_type=F32)
            s = s - ck_ref[j:j + 1, pl.ds(start, tq)]
            if masked:
                s = jnp.where(colpos <= rowpos, s, -jnp.inf)
            m = m_ref[j]
            m_new = jnp.maximum(m, jnp.max(s, axis=-1, keepdims=True) + cqs[j])
            p = jnp.exp2(s + (cqs[j] - m_new)).astype(BF16)
            vj = jnp.where(own[j], vt, jnp.ones_like(vt))
            acc_ref[j] = jnp.exp2(m - m_new) * acc_ref[j] + _dot(p, vj)
            m_ref[j] = m_new

    def loop_body(t, carry):
        tile(t, False)
        return carry

    lax.fori_loop(0, qi, loop_body, 0)
    tile(qi, True)
    outs = []
    for j in range(2):
        acc = acc_ref[j]
        outs.append(acc / pltpu.roll(acc, ATT_HEAD_DIM, axis=1))
    o_ref[...] = jnp.where(upper, outs[1], outs[0]).astype(BF16)


def _attention(q3, k3, v3, cq3, ck4, *, tq):
    b, s, _ = q3.shape
    npairs = ATT_HEADS // 2
    return pl.pallas_call(
        functools.partial(_attn_body, tq=tq),
        scratch_shapes=[pltpu.VMEM((2, tq, 1), F32), pltpu.VMEM((2, tq, LANES), F32)],
        grid=(b, npairs, s // tq),
        in_specs=[pl.BlockSpec((None, tq, LANES), lambda bi, hp, qi: (bi, qi, hp)),
                  pl.BlockSpec((None, s, LANES), lambda bi, hp, qi: (bi, 0, hp)),
                  pl.BlockSpec((None, s, LANES), lambda bi, hp, qi: (bi, 0, hp)),
                  pl.BlockSpec((None, tq, LANES), lambda bi, hp, qi: (bi, qi, 0)),
                  pl.BlockSpec((None, None, 2, s), lambda bi, hp, qi: (bi, hp, 0, 0))],
        out_specs=pl.BlockSpec((None, tq, LANES), lambda bi, hp, qi: (bi, qi, hp)),
        out_shape=jax.ShapeDtypeStruct((b, s, D_ATT), BF16),
        compiler_params=_cparams("parallel", "parallel", "arbitrary"),
        name="attention",
    )(q3, k3, v3, cq3, ck4)


def _top2_gates(logits):
    lane = lax.broadcasted_iota(jnp.int32, logits.shape, 1).astype(F32)
    lg = jnp.where(lane < N_EXPERTS, logits, -jnp.inf)
    m1 = jnp.max(lg, axis=-1, keepdims=True)
    i1 = jnp.min(jnp.where(lg == m1, lane, float(LANES)), axis=-1, keepdims=True)
    lg2 = jnp.where(lane == i1, -jnp.inf, lg)
    m2 = jnp.max(lg2, axis=-1, keepdims=True)
    i2 = jnp.min(jnp.where(lg2 == m2, lane, float(LANES)), axis=-1, keepdims=True)
    e = jnp.exp(m2 - m1)
    w1 = 1.0 / (1.0 + e)
    w2 = e / (1.0 + e)
    return jnp.where(lane == i1, w1, 0.0) + jnp.where(lane == i2, w2, 0.0)


def _outproj_body(*refs, with_router):
    if with_router:
        (x_ref, ys_ref, us_ref, ya_ref, yg_ref, d_ref, wglu_ref, gn_ref, w_ref, fg_ref, rw_ref,
         xo_ref, h_ref, gates_ref) = refs
    else:
        (x_ref, ys_ref, us_ref, ya_ref, yg_ref, d_ref, wglu_ref, gn_ref, w_ref, fg_ref,
         xo_ref, h_ref) = refs

    y1 = _gelu(ys_ref[...] + d_ref[...] * us_ref[...])
    y_ssm = y1 * _sigmoid(_dot(y1.astype(BF16), wglu_ref[...]))

    o1, o2 = D_SSM, D_SSM + D_ATT
    acc = _dot(_rms(y_ssm, gn_ref[:, :o1]).astype(BF16), w_ref[:o1, :])
    acc += _dot(_rms(ya_ref[...].astype(F32), gn_ref[:, o1:o2]).astype(BF16), w_ref[o1:o2, :])
    acc += _dot(_rms(yg_ref[...].astype(F32), gn_ref[:, o2:]).astype(BF16), w_ref[o2:, :])
    xn = x_ref[...] + acc
    xo_ref[...] = xn
    hn = _rms(xn, fg_ref[...])
    h_ref[...] = hn.astype(BF16)
    if with_router:
        h1, h2, h3 = _split3(hn)
        r1, r2, r3 = _split3(rw_ref[...])
        logits = (_dot(h1, r1) + _dot(h1, r2) + _dot(h2, r1)
                  + _dot(h1, r3) + _dot(h2, r2) + _dot(h3, r1))
        gates_ref[...] = _top2_gates(logits)


def _outproj(x2, ys, us, ya, yg, d, wglu, gn, w, fg, rw, *, tm):
    n = x2.shape[0]
    with_router = rw is not None
    row = lambda width: pl.BlockSpec((tm, width), lambda i: (i, 0))
    in_specs = [row(D_MODEL), row(D_SSM), row(D_SSM), row(D_ATT), row(D_GM),
                _resident((1, D_SSM)), _resident((D_SSM, D_SSM)), _resident((1, D_MODEL)),
                _resident((D_MODEL, D_MODEL)), _resident((1, D_MODEL))]
    out_specs = [row(D_MODEL), row(D_MODEL)]
    out_shape = [jax.ShapeDtypeStruct((n, D_MODEL), F32), jax.ShapeDtypeStruct((n, D_MODEL), BF16)]
    args = [x2, ys, us, ya, yg, d, wglu, gn, w, fg]
    if with_router:
        in_specs.append(_resident((D_MODEL, LANES)))
        out_specs.append(row(LANES))
        out_shape.append(jax.ShapeDtypeStruct((n, LANES), F32))
        args.append(rw)
    return pl.pallas_call(
        functools.partial(_outproj_body, with_router=with_router),
        grid=(n // tm,),
        in_specs=in_specs, out_specs=out_specs, out_shape=out_shape,
        compiler_params=_cparams("parallel"),
        name="outproj_router" if with_router else "outproj",
    )(*args)


def _swiglu_tile(h, wg, wu, wd):
    a = _dot(h, wg)
    return _dot((a * _sigmoid(a) * _dot(h, wu)).astype(BF16), wd)


def _finish(x, acc, fin_ref, o_ref):
    xn = x + acc
    o_ref[...] = xn if fin_ref is None else _rms(xn, fin_ref[...])


def _dense_ffn_body(*refs, final_norm):
    if final_norm:
        h_ref, x_ref, wg_ref, wu_ref, wd_ref, fin_ref, o_ref, acc_ref = refs
    else:
        h_ref, x_ref, wg_ref, wu_ref, wd_ref, o_ref, acc_ref = refs
        fin_ref = None
    f = pl.program_id(1)

    @pl.when(f == 0)
    def _():
        acc_ref[...] = jnp.zeros_like(acc_ref)

    acc_ref[...] += _swiglu_tile(h_ref[...], wg_ref[...], wu_ref[...], wd_ref[...])

    @pl.when(f == pl.num_programs(1) - 1)
    def _():
        _finish(x_ref[...], acc_ref[...], fin_ref, o_ref)


def _dense_ffn(h, x2, wg, wu, wd, fin, *, tm, tf):
    n = x2.shape[0]
    dff = wg.shape[1]
    row = lambda: pl.BlockSpec((tm, D_MODEL), lambda i, f: (i, 0))
    in_specs = [row(), row(),
                pl.BlockSpec((D_MODEL, tf), lambda i, f: (0, f)),
                pl.BlockSpec((D_MODEL, tf), lambda i, f: (0, f)),
                pl.BlockSpec((tf, D_MODEL), lambda i, f: (f, 0))]
    args = [h, x2, wg, wu, wd]
    if fin is not None:
        in_specs.append(pl.BlockSpec((1, D_MODEL), lambda i, f: (0, 0)))
        args.append(fin)
    return pl.pallas_call(
        functools.partial(_dense_ffn_body, final_norm=fin is not None),
        grid=(n // tm, dff // tf),
        in_specs=in_specs, out_specs=row(),
        out_shape=jax.ShapeDtypeStruct((n, D_MODEL), F32),
        scratch_shapes=[pltpu.VMEM((tm, D_MODEL), F32)],
        compiler_params=_cparams("parallel", "arbitrary"),
        name="dense_ffn",
    )(*args)


def _moe_ffn_body(*refs, final_norm):
    if final_norm:
        h_ref, x_ref, gates_ref, wg_ref, wu_ref, wd_ref, fin_ref, o_ref, acc_ref = refs
    else:
        h_ref, x_ref, gates_ref, wg_ref, wu_ref, wd_ref, o_ref, acc_ref = refs
        fin_ref = None
    e = pl.program_id(1)
    f = pl.program_id(2)

    @pl.when((e == 0) & (f == 0))
    def _():
        acc_ref[...] = jnp.zeros_like(acc_ref)

    gates = gates_ref[...]
    lane = lax.broadcasted_iota(jnp.int32, gates.shape, 1)
    gate = jnp.sum(jnp.where(lane == e, gates, 0.0), axis=-1, keepdims=True)
    acc_ref[...] += gate * _swiglu_tile(h_ref[...], wg_ref[...], wu_ref[...], wd_ref[...])

    @pl.when((e == pl.num_programs(1) - 1) & (f == pl.num_programs(2) - 1))
    def _():
        _finish(x_ref[...], acc_ref[...], fin_ref, o_ref)


def _moe_ffn(h, x2, gates, wg, wu, wd, fin, *, tm, tf):
    n = x2.shape[0]
    ne, _, dff = wg.shape
    row = lambda width: pl.BlockSpec((tm, width), lambda i, e, f: (i, 0))
    in_specs = [row(D_MODEL), row(D_MODEL), row(LANES),
                pl.BlockSpec((None, D_MODEL, tf), lambda i, e, f: (e, 0, f)),
                pl.BlockSpec((None, D_MODEL, tf), lambda i, e, f: (e, 0, f)),
                pl.BlockSpec((None, tf, D_MODEL), lambda i, e, f: (e, f, 0))]
    args = [h, x2, gates, wg, wu, wd]
    if fin is not None:
        in_specs.append(pl.BlockSpec((1, D_MODEL), lambda i, e, f: (0, 0)))
        args.append(fin)
    return pl.pallas_call(
        functools.partial(_moe_ffn_body, final_norm=fin is not None),
        grid=(n // tm, ne, dff // tf),
        in_specs=in_specs, out_specs=row(D_MODEL),
        out_shape=jax.ShapeDtypeStruct((n, D_MODEL), F32),
        scratch_shapes=[pltpu.VMEM((tm, D_MODEL), F32)],
        compiler_params=_cparams("parallel", "arbitrary", "arbitrary"),
        name="moe_ffn",
    )(*args)


def _largest_tile(total, cap, mult):
    best = mult
    for t in range(mult, min(total, cap) + 1, mult):
        if total % t == 0:
            best = t
    return best


def _pad_lanes(a):
    return jnp.pad(a, [(0, 0)] * (a.ndim - 1) + [(0, LANES - a.shape[-1])])


def kernel(x, mix_norm_g, w_in, b_forget, ssm_lambda_re, ssm_lambda_im, ssm_log_dt, ssm_b_re, ssm_b_im, ssm_c_re, ssm_c_im, ssm_d, ssm_w_glu, gm_ln_g, gm_ln_b, gm_w_s, gm_b_s, group_norm_g, w_out, ffn_norm_g, dense_w_gate, dense_w_up, dense_w_down, router_w, moe_w_gate, moe_w_up, moe_w_down, final_norm_g):
    bsz, seq, _ = x.shape
    n = bsz * seq
    depth = w_in.shape[0]
    assert seq % CHUNK == 0 and seq % SSM_T == 0
    tm = _largest_tile(seq, 512, CHUNK)
    tq = _largest_tile(seq, 512, LANES)
    nc = n // SSM_T
    ncb = seq // SSM_T
    tr = _largest_tile(nc, 512, 8)
    tc = _largest_tile(ncb, 128, 8)

    x2 = x.reshape(n, D_MODEL).astype(F32)
    tril = jnp.tril(jnp.ones((CHUNK, CHUNK), F32))
    i0, i1, i2, i3, i4 = (D_SSM, D_SSM + D_ATT, D_SSM + 2 * D_ATT, D_SSM + 3 * D_ATT,
                          D_SSM + 3 * D_ATT + ATT_HEADS)
    for layer in range(depth):
        wl = w_in[layer]
        w_re = jnp.concatenate(
            [wl[:, :i0], wl[:, i0:i1] * (ATT_HEAD_DIM ** -0.5 * LOG2E), wl[:, i1:i3], wl[:, i4:],
             _pad_lanes(wl[:, i3:i4])], axis=1).astype(BF16)
        ws = (gm_w_s[layer].astype(F32) * tril[None]).astype(BF16)
        bs = jnp.repeat(gm_b_s[layer].astype(F32).T, GM_HEAD_DIM, axis=1)
        u, q, k, v, cum, ygm = _inproj(
            x2, mix_norm_g[layer].reshape(1, D_MODEL), w_re,
            _pad_lanes(b_forget[layer].reshape(1, ATT_HEADS).astype(F32)),
            gm_ln_g[layer].reshape(1, D_GM), gm_ln_b[layer].reshape(1, D_GM), ws, bs,
            seq=seq, tm=tm)

        mmat, fmat, emat, a_row = _ssm_matrices(
            ssm_lambda_re[layer], ssm_lambda_im[layer], ssm_log_dt[layer], ssm_b_re[layer],
            ssm_b_im[layer], ssm_c_re[layer], ssm_c_im[layer])
        u2 = u.reshape(nc, SSM_ROW)
        s_loc = _ssm_state(u2, fmat, tr=tr)
        xprev = _ssm_scan(s_loc.reshape(bsz, ncb, 2 * SSM_NSTATE), a_row, tc=tc)
        y_ssm = _ssm_out(u2, xprev.reshape(nc, 2 * SSM_NSTATE), mmat, emat, tr=tr).reshape(n, D_SSM)

        ck = cum[:, :ATT_HEADS].reshape(bsz, seq, ATT_HEADS // 2, 2).transpose(0, 2, 3, 1)
        y_att = _attention(q.reshape(bsz, seq, D_ATT), k.reshape(bsz, seq, D_ATT),
                           v.reshape(bsz, seq, D_ATT), cum.reshape(bsz, seq, LANES), ck,
                           tq=tq).reshape(n, D_ATT)

        is_moe = layer % 2 == 1
        j = layer // 2
        rw = _pad_lanes(router_w[j].astype(F32)) if is_moe else None
        outs = _outproj(x2, y_ssm, u, y_att, ygm, ssm_d[layer].reshape(1, D_SSM).astype(F32),
                        ssm_w_glu[layer].astype(BF16), group_norm_g[layer].reshape(1, D_MODEL),
                        w_out[layer].astype(BF16), ffn_norm_g[layer].reshape(1, D_MODEL), rw, tm=tm)
        fin = final_norm_g.reshape(1, D_MODEL) if layer == depth - 1 else None
        if is_moe:
            x_mid, h, gates = outs
            dffe = moe_w_gate.shape[-1]
            x2 = _moe_ffn(h, x_mid, gates, moe_w_gate[j].astype(BF16), moe_w_up[j].astype(BF16),
                          moe_w_down[j].astype(BF16), fin, tm=tm,
                          tf=_largest_tile(dffe, 1024, LANES))
        else:
            x_mid, h = outs
            dff = dense_w_gate.shape[-1]
            x2 = _dense_ffn(h, x_mid, dense_w_gate[j].astype(BF16), dense_w_up[j].astype(BF16),
                            dense_w_down[j].astype(BF16), fin, tm=tm,
                            tf=_largest_tile(dff, 1536, LANES))
    return x2.reshape(bsz, seq, D_MODEL).astype(x.dtype)
```

```python
import functools
import math

import jax
import jax.numpy as jnp
from jax import lax
from jax.experimental import pallas as pl
from jax.experimental.pallas import tpu as pltpu

F32 = jnp.float32
BF16 = jnp.bfloat16

D_MODEL = 1024
D_SSM = 256
D_ATT = 512
D_GM = 256
SSM_GROUP = 16
SSM_GROUPS = 16
SSM_STATE = 64
ATT_HEADS = 8
ATT_HEAD_DIM = 64
GM_HEADS = 4
GM_HEAD_DIM = 64
CHUNK = 128
N_EXPERTS = 8
EPS = 1e-6
LOG2E = math.log2(math.e)

LANES = 128
SUBLANES = 8
MOE_ROW_TILE = 512
SSM_T = 8
SSM_ROW = SSM_T * D_SSM
SSM_NSTATE = SSM_GROUPS * SSM_STATE

C_U = 0
C_Q = C_U + D_SSM
C_K = C_Q + D_ATT
C_V = C_K + D_ATT
C_Z = C_V + D_ATT
C_F = C_Z + 2 * D_GM
D_IN_PAD = C_F + LANES

VMEM_LIMIT = 56 * 1024 * 1024


def _cparams(*sem):
    return pltpu.CompilerParams(dimension_semantics=sem, vmem_limit_bytes=VMEM_LIMIT)


def _resident(shape):
    nd = len(shape)
    return pl.BlockSpec(shape, lambda *_: (0,) * nd, pipeline_mode=pl.Buffered(1))


def _rms(x, g):
    return x * lax.rsqrt(jnp.mean(x * x, axis=-1, keepdims=True) + EPS) * g


def _gelu(x):
    c = math.sqrt(2.0 / math.pi)
    return 0.5 * x * (1.0 + jnp.tanh(c * (x + 0.044715 * (x * x * x))))


def _sigmoid(x):
    return 1.0 / (1.0 + jnp.exp(-x))


def _split3(x):
    p1 = x.astype(BF16)
    r1 = x - p1.astype(F32)
    p2 = r1.astype(BF16)
    r2 = r1 - p2.astype(F32)
    return p1, p2, r2.astype(BF16)


def _dot(a, b):
    return jnp.dot(a, b, preferred_element_type=F32)


def _inproj_body(x_ref, g_ref, w_ref, bf_ref, lng_ref, lnb_ref, ws_ref, bs_ref,
                 u_ref, q_ref, k_ref, v_ref, cum_ref, ygm_ref, carry_ref, *, tiles_per_seq):
    i = pl.program_id(0)

    @pl.when(i % tiles_per_seq == 0)
    def _():
        carry_ref[...] = jnp.zeros_like(carry_ref)

    tm = x_ref.shape[0]
    hb = _rms(x_ref[...], g_ref[...]).astype(BF16)

    def proj(lo, width):
        return _dot(hb, w_ref[:, lo:lo + width])

    u_ref[...] = proj(C_U, D_SSM)
    q_ref[...] = proj(C_Q, D_ATT).astype(BF16)
    k_ref[...] = proj(C_K, D_ATT).astype(BF16)
    v_ref[...] = proj(C_V, D_ATT).astype(BF16)

    f = proj(C_F, LANES) + bf_ref[...]
    logf = (jnp.minimum(f, 0.0) - jnp.log(1.0 + jnp.exp(-jnp.abs(f)))) * LOG2E
    row = lax.broadcasted_iota(jnp.int32, (tm, tm), 0)
    col = lax.broadcasted_iota(jnp.int32, (tm, tm), 1)
    tri = (col <= row).astype(BF16)
    p1, p2, p3 = _split3(logf)
    cum = _dot(tri, p1) + _dot(tri, p2) + _dot(tri, p3) + carry_ref[...]
    cum_ref[...] = cum
    carry_ref[...] = cum[tm - 1:tm, :]

    zg = _gelu(proj(C_Z, 2 * D_GM))
    ug = zg[:, :D_GM]
    vg = zg[:, D_GM:]
    mean = jnp.mean(vg, axis=-1, keepdims=True)
    cen = vg - mean
    var = jnp.mean(cen * cen, axis=-1, keepdims=True)
    vn = (cen * lax.rsqrt(var + EPS) * lng_ref[...] + lnb_ref[...]).astype(BF16)
    lane_head = lax.broadcasted_iota(jnp.int32, (CHUNK, D_GM), 1) >> 6
    for c in range(tm // CHUNK):
        rows = slice(c * CHUNK, (c + 1) * CHUNK)
        vc = vn[rows, :]
        mixed = _dot(ws_ref[0], vc)
        for g in range(1, GM_HEADS):
            mixed = jnp.where(lane_head == g, _dot(ws_ref[g], vc), mixed)
        mixed = mixed + bs_ref[...]
        ygm_ref[rows, :] = (ug[rows, :] * mixed).astype(BF16)


def _inproj(x2, g, w, bf, lng, lnb, ws, bs, *, seq, tm):
    n = x2.shape[0]
    row = lambda width: pl.BlockSpec((tm, width), lambda i: (i, 0))
    return pl.pallas_call(
        functools.partial(_inproj_body, tiles_per_seq=seq // tm),
        grid=(n // tm,),
        in_specs=[row(D_MODEL), _resident((1, D_MODEL)), _resident((D_MODEL, D_IN_PAD)),
                  _resident((1, LANES)), _resident((1, D_GM)), _resident((1, D_GM)),
                  _resident((GM_HEADS, CHUNK, CHUNK)), _resident((CHUNK, D_GM))],
        out_specs=[row(D_SSM), row(D_ATT), row(D_ATT), row(D_ATT), row(LANES), row(D_GM)],
        out_shape=[jax.ShapeDtypeStruct((n, D_SSM), F32),
                   jax.ShapeDtypeStruct((n, D_ATT), BF16),
                   jax.ShapeDtypeStruct((n, D_ATT), BF16),
                   jax.ShapeDtypeStruct((n, D_ATT), BF16),
                   jax.ShapeDtypeStruct((n, LANES), F32),
                   jax.ShapeDtypeStruct((n, D_GM), BF16)],
        scratch_shapes=[pltpu.VMEM((1, LANES), F32)],
        compiler_params=_cparams("arbitrary"),
        name="inproj",
    )(x2, g, w, bf, lng, lnb, ws, bs)


def _ssm_state_body(u_ref, f_ref, s_ref):
    s_ref[...] = _dot(u_ref[...].astype(BF16), f_ref[...])


def _ssm_state(u2, fmat, *, tr):
    nc = u2.shape[0]
    return pl.pallas_call(
        _ssm_state_body,
        grid=(nc // tr,),
        in_specs=[pl.BlockSpec((tr, SSM_ROW), lambda i: (i, 0)),
                  _resident((SSM_ROW, 2 * SSM_NSTATE))],
        out_specs=pl.BlockSpec((tr, 2 * SSM_NSTATE), lambda i: (i, 0)),
        out_shape=jax.ShapeDtypeStruct((nc, 2 * SSM_NSTATE), F32),
        compiler_params=_cparams("parallel"),
        name="ssm_state",
    )(u2, fmat)


def _ssm_scan_body(s_ref, a_ref, xp_ref, st_ref):
    @pl.when(pl.program_id(1) == 0)
    def _():
        st_ref[...] = jnp.zeros_like(st_ref)

    tc = s_ref.shape[0]
    are = a_ref[:, :SSM_NSTATE]
    aim = a_ref[:, SSM_NSTATE:]

    def step(c8, carry):
        xre, xim = carry
        base = pl.multiple_of(c8 * SUBLANES, SUBLANES)
        blk = s_ref[pl.ds(base, SUBLANES), :]
        prev_re, prev_im = [], []
        for r in range(SUBLANES):
            prev_re.append(xre)
            prev_im.append(xim)
            sre = blk[r:r + 1, :SSM_NSTATE]
            sim = blk[r:r + 1, SSM_NSTATE:]
            xre, xim = are * xre - aim * xim + sre, are * xim + aim * xre + sim
        xp_ref[pl.ds(base, SUBLANES), :SSM_NSTATE] = jnp.concatenate(prev_re, axis=0)
        xp_ref[pl.ds(base, SUBLANES), SSM_NSTATE:] = jnp.concatenate(prev_im, axis=0)
        return xre, xim

    xre, xim = lax.fori_loop(0, tc // SUBLANES, step,
                             (st_ref[:, :SSM_NSTATE], st_ref[:, SSM_NSTATE:]))
    st_ref[:, :SSM_NSTATE] = xre
    st_ref[:, SSM_NSTATE:] = xim


def _ssm_scan(s3, a, *, tc):
    b, ncb, w = s3.shape
    spec = pl.BlockSpec((None, tc, w), lambda bi, ci: (bi, ci, 0))
    return pl.pallas_call(
        _ssm_scan_body,
        grid=(b, ncb // tc),
        in_specs=[spec, pl.BlockSpec((1, w), lambda bi, ci: (0, 0))],
        out_specs=spec,
        out_shape=jax.ShapeDtypeStruct(s3.shape, F32),
        scratch_shapes=[pltpu.VMEM((1, w), F32)],
        compiler_params=_cparams("parallel", "arbitrary"),
        name="ssm_scan",
    )(s3, a)


def _ssm_out_body(u_ref, xp_ref, m_ref, e_ref, y_ref):
    y_ref[...] = (_dot(u_ref[...].astype(BF16), m_ref[...])
                  + _dot(xp_ref[...].astype(BF16), e_ref[...]))


def _ssm_out(u2, xp, mmat, emat, *, tr):
    nc = u2.shape[0]
    return pl.pallas_call(
        _ssm_out_body,
        grid=(nc // tr,),
        in_specs=[pl.BlockSpec((tr, SSM_ROW), lambda i: (i, 0)),
                  pl.BlockSpec((tr, 2 * SSM_NSTATE), lambda i: (i, 0)),
                  _resident((SSM_ROW, SSM_ROW)), _resident((2 * SSM_NSTATE, SSM_ROW))],
        out_specs=pl.BlockSpec((tr, SSM_ROW), lambda i: (i, 0)),
        out_shape=jax.ShapeDtypeStruct((nc, SSM_ROW), F32),
        compiler_params=_cparams("parallel"),
        name="ssm_out",
    )(u2, xp, mmat, emat)


def _ssm_matrices(lam_re, lam_im, log_dt, b_re, b_im, c_re, c_im):
    t, g, p, h = SSM_T, SSM_GROUPS, SSM_STATE, SSM_GROUP
    hp = lax.Precision.HIGHEST
    lr, li = lam_re.astype(F32), lam_im.astype(F32)
    dt = jnp.exp(log_dt.astype(F32))[:, None]
    steps = jnp.arange(t + 1, dtype=F32)[:, None, None]
    mag = jnp.exp((lr * dt)[None] * steps)
    ang = (li * dt)[None] * steps
    pr, pi = mag * jnp.cos(ang), mag * jnp.sin(ang)
    nr, ni = pr[1] - 1.0, pi[1]
    den = lr * lr + li * li
    zr, zi = (nr * lr + ni * li) / den, (ni * lr - nr * li) / den
    br, bi = b_re.astype(F32), b_im.astype(F32)
    bbr = zr[..., None] * br - zi[..., None] * bi
    bbi = zr[..., None] * bi + zi[..., None] * br
    cr, ci = c_re.astype(F32)[None], c_im.astype(F32)[None]
    cpr = cr * pr[:, :, None, :] - ci * pi[:, :, None, :]
    cpi = cr * pi[:, :, None, :] + ci * pr[:, :, None, :]
    eye = jnp.eye(g, dtype=F32)

    kern = (jnp.einsum('tgop,gpi->tgio', cpr[:t], bbr, precision=hp)
            - jnp.einsum('tgop,gpi->tgio', cpi[:t], bbi, precision=hp))
    lag = jnp.arange(t)[None, :] - jnp.arange(t)[:, None]
    k_st = jnp.where((lag >= 0)[:, :, None, None, None], kern[jnp.clip(lag, 0, t - 1)], 0.0)
    mmat = (k_st[:, :, :, :, None, :] * eye[None, None, :, None, :, None])
    mmat = mmat.transpose(0, 2, 3, 1, 4, 5).reshape(t * g * h, t * g * h)

    qr, qi = pr[:t][::-1][..., None], pi[:t][::-1][..., None]
    fr = qr * bbr[None] - qi * bbi[None]
    fi = qr * bbi[None] + qi * bbr[None]

    def rows_to_state(w):
        w = w.transpose(0, 1, 3, 2)[:, :, :, None, :] * eye[None, :, None, :, None]
        return w.reshape(t * g * h, g * p)

    fmat = jnp.concatenate([rows_to_state(fr), rows_to_state(fi)], axis=1)

    def state_to_rows(w):
        w = w.transpose(1, 3, 0, 2)[:, :, :, None, :] * eye[:, None, None, :, None]
        return w.reshape(g * p, t * g * h)

    emat = jnp.concatenate([state_to_rows(cpr[1:]), -state_to_rows(cpi[1:])], axis=0)

    a_row = jnp.concatenate([pr[t].reshape(1, g * p), pi[t].reshape(1, g * p)], axis=1)
    return mmat.astype(BF16), fmat.astype(BF16), emat.astype(BF16), a_row


def _attn_body(q_ref, k_ref, v_ref, cq_ref, ck_ref, o_ref, m_ref, acc_ref, *, tq):
    hp = pl.program_id(1)
    qi = pl.program_id(2)
    q = q_ref[...]
    lane = lax.broadcasted_iota(jnp.int32, (tq, LANES), 1)
    upper = lane >= ATT_HEAD_DIM
    own = [~upper, upper]
    cq_all = cq_ref[...]
    qs = [jnp.where(own[j], q, jnp.zeros_like(q)) for j in range(2)]
    cqs = [jnp.sum(jnp.where(lane == 2 * hp + j, cq_all, 0.0), axis=-1, keepdims=True)
           for j in range(2)]
    rowpos = lax.broadcasted_iota(jnp.int32, (tq, tq), 0)
    colpos = lax.broadcasted_iota(jnp.int32, (tq, tq), 1)
    for j in range(2):
        m_ref[j] = jnp.full((tq, 1), -jnp.inf, F32)
        acc_ref[j] = jnp.zeros((tq, LANES), F32)

    def tile(t, masked):
        start = pl.multiple_of(t * tq, tq)
        kt = k_ref[pl.ds(start, tq), :]
        vt = v_ref[pl.ds(start, tq), :]
        for j in range(2):
            s = lax.dot_general(qs[j], kt, (((1,), (1,)), ((), ())), preferred_element_type=F32)
            s = s - ck_ref[j:j + 1, pl.ds(start, tq)]
            if masked:
                s = jnp.where(colpos <= rowpos, s, -jnp.inf)
            m = m_ref[j]
            m_new = jnp.maximum(m, jnp.max(s, axis=-1, keepdims=True) + cqs[j])
            p = jnp.exp2(s + (cqs[j] - m_new)).astype(BF16)
            vj = jnp.where(own[j], vt, jnp.ones_like(vt))
            acc_ref[j] = jnp.exp2(m - m_new) * acc_ref[j] + _dot(p, vj)
            m_ref[j] = m_new

    def loop_body(t, carry):
        tile(t, False)
        return carry

    lax.fori_loop(0, qi, loop_body, 0)
    tile(qi, True)
    outs = []
    for j in range(2):
        acc = acc_ref[j]
        outs.append(acc / pltpu.roll(acc, ATT_HEAD_DIM, axis=1))
    o_ref[...] = jnp.where(upper, outs[1], outs[0]).astype(BF16)


def _attention(q3, k3, v3, cq3, ck4, *, tq):
    b, s, _ = q3.shape
    npairs = ATT_HEADS // 2
    return pl.pallas_call(
        functools.partial(_attn_body, tq=tq),
        scratch_shapes=[pltpu.VMEM((2, tq, 1), F32), pltpu.VMEM((2, tq, LANES), F32)],
        grid=(b, npairs, s // tq),
        in_specs=[pl.BlockSpec((None, tq, LANES), lambda bi, hp, qi: (bi, qi, hp)),
                  pl.BlockSpec((None, s, LANES), lambda bi, hp, qi: (bi, 0, hp)),
                  pl.BlockSpec((None, s, LANES), lambda bi, hp, qi: (bi, 0, hp)),
                  pl.BlockSpec((None, tq, LANES), lambda bi, hp, qi: (bi, qi, 0)),
                  pl.BlockSpec((None, None, 2, s), lambda bi, hp, qi: (bi, hp, 0, 0))],
        out_specs=pl.BlockSpec((None, tq, LANES), lambda bi, hp, qi: (bi, qi, hp)),
        out_shape=jax.ShapeDtypeStruct((b, s, D_ATT), BF16),
        compiler_params=_cparams("parallel", "parallel", "arbitrary"),
        name="attention",
    )(q3, k3, v3, cq3, ck4)


R_E1, R_E2, R_W1, R_W2, R_RANK1, R_RANK2 = range(6)


def _route(logits, cnt_ref):
    tm = logits.shape[0]
    lane = lax.broadcasted_iota(jnp.int32, logits.shape, 1).astype(F32)
    lg = jnp.where(lane < N_EXPERTS, logits, -jnp.inf)
    m1 = jnp.max(lg, axis=-1, keepdims=True)
    i1 = jnp.min(jnp.where(lg == m1, lane, float(LANES)), axis=-1, keepdims=True)
    lg2 = jnp.where(lane == i1, -jnp.inf, lg)
    m2 = jnp.max(lg2, axis=-1, keepdims=True)
    i2 = jnp.min(jnp.where(lg2 == m2, lane, float(LANES)), axis=-1, keepdims=True)
    e = jnp.exp(m2 - m1)
    w1 = 1.0 / (1.0 + e)
    w2 = e / (1.0 + e)

    hit1, hit2 = lane == i1, lane == i2
    oh1, oh2 = hit1.astype(BF16), hit2.astype(BF16)
    row = lax.broadcasted_iota(jnp.int32, (tm, tm), 0)
    col = lax.broadcasted_iota(jnp.int32, (tm, tm), 1)
    before = (col < row).astype(BF16)
    carry = cnt_ref[...]
    tot1 = jnp.sum(oh1.astype(F32), axis=0, keepdims=True)
    tot2 = jnp.sum(oh2.astype(F32), axis=0, keepdims=True)
    pos1 = _dot(before, oh1) + carry
    pos2 = _dot(before, oh2) + (carry + tot1)
    rank1 = jnp.sum(jnp.where(hit1, pos1, 0.0), axis=-1, keepdims=True)
    rank2 = jnp.sum(jnp.where(hit2, pos2, 0.0), axis=-1, keepdims=True)
    cnt_ref[...] = carry + tot1 + tot2

    rec = jnp.zeros(logits.shape, F32)
    for idx, val in ((R_E1, i1), (R_E2, i2), (R_W1, w1), (R_W2, w2), (R_RANK1, rank1), (R_RANK2, rank2)):
        rec = jnp.where(lane == idx, val, rec)
    return rec


def _outproj_body(*refs, with_router):
    if with_router:
        (x_ref, ys_ref, us_ref, ya_ref, yg_ref, d_ref, wglu_ref, gn_ref, w_ref, fg_ref, rw_ref,
         xo_ref, h_ref, route_ref, cnt_ref) = refs

        @pl.when(pl.program_id(0) == 0)
        def _():
            cnt_ref[...] = jnp.zeros_like(cnt_ref)
    else:
        (x_ref, ys_ref, us_ref, ya_ref, yg_ref, d_ref, wglu_ref, gn_ref, w_ref, fg_ref,
         xo_ref, h_ref) = refs

    y1 = _gelu(ys_ref[...] + d_ref[...] * us_ref[...])
    y_ssm = y1 * _sigmoid(_dot(y1.astype(BF16), wglu_ref[...]))

    o1, o2 = D_SSM, D_SSM + D_ATT
    acc = _dot(_rms(y_ssm, gn_ref[:, :o1]).astype(BF16), w_ref[:o1, :])
    acc += _dot(_rms(ya_ref[...].astype(F32), gn_ref[:, o1:o2]).astype(BF16), w_ref[o1:o2, :])
    acc += _dot(_rms(yg_ref[...].astype(F32), gn_ref[:, o2:]).astype(BF16), w_ref[o2:, :])
    xn = x_ref[...] + acc
    xo_ref[...] = xn
    hn = _rms(xn, fg_ref[...])
    h_ref[...] = hn.astype(h_ref.dtype)
    if with_router:
        h1, h2, h3 = _split3(hn)
        r1, r2, r3 = _split3(rw_ref[...])
        logits = (_dot(h1, r1) + _dot(h1, r2) + _dot(h2, r1)
                  + _dot(h1, r3) + _dot(h2, r2) + _dot(h3, r1))
        route_ref[...] = _route(logits, cnt_ref)


def _outproj(x2, ys, us, ya, yg, d, wglu, gn, w, fg, rw, *, tm):
    n = x2.shape[0]
    with_router = rw is not None
    row = lambda width: pl.BlockSpec((tm, width), lambda i: (i, 0))
    in_specs = [row(D_MODEL), row(D_SSM), row(D_SSM), row(D_ATT), row(D_GM),
                _resident((1, D_SSM)), _resident((D_SSM, D_SSM)), _resident((1, D_MODEL)),
                _resident((D_MODEL, D_MODEL)), _resident((1, D_MODEL))]
    out_specs = [row(D_MODEL), row(D_MODEL)]
    out_shape = [jax.ShapeDtypeStruct((n, D_MODEL), F32),
                 jax.ShapeDtypeStruct((n, D_MODEL), F32 if with_router else BF16)]
    args = [x2, ys, us, ya, yg, d, wglu, gn, w, fg]
    if with_router:
        in_specs.append(_resident((D_MODEL, LANES)))
        out_specs += [row(LANES), pl.BlockSpec((1, LANES), lambda i: (0, 0))]
        out_shape += [jax.ShapeDtypeStruct((n, LANES), F32), jax.ShapeDtypeStruct((1, LANES), F32)]
        args.append(rw)
    return pl.pallas_call(
        functools.partial(_outproj_body, with_router=with_router),
        grid=(n // tm,),
        in_specs=in_specs, out_specs=out_specs, out_shape=out_shape,
        compiler_params=_cparams("arbitrary" if with_router else "parallel"),
        name="outproj_router" if with_router else "outproj",
    )(*args)


def _swiglu_tile(h, wg, wu, wd):
    a = _dot(h, wg)
    return _dot((a * _sigmoid(a) * _dot(h, wu)).astype(BF16), wd)


def _finish(x, acc, fin_ref, o_ref):
    xn = x + acc
    o_ref[...] = xn if fin_ref is None else _rms(xn, fin_ref[...])


def _dense_ffn_body(*refs, final_norm):
    if final_norm:
        h_ref, x_ref, wg_ref, wu_ref, wd_ref, fin_ref, o_ref, acc_ref = refs
    else:
        h_ref, x_ref, wg_ref, wu_ref, wd_ref, o_ref, acc_ref = refs
        fin_ref = None
    f = pl.program_id(1)

    @pl.when(f == 0)
    def _():
        acc_ref[...] = jnp.zeros_like(acc_ref)

    acc_ref[...] += _swiglu_tile(h_ref[...], wg_ref[...], wu_ref[...], wd_ref[...])

    @pl.when(f == pl.num_programs(1) - 1)
    def _():
        _finish(x_ref[...], acc_ref[...], fin_ref, o_ref)


def _dense_ffn(h, x2, wg, wu, wd, fin, *, tm, tf):
    n = x2.shape[0]
    dff = wg.shape[1]
    row = lambda: pl.BlockSpec((tm, D_MODEL), lambda i, f: (i, 0))
    in_specs = [row(), row(),
                pl.BlockSpec((D_MODEL, tf), lambda i, f: (0, f)),
                pl.BlockSpec((D_MODEL, tf), lambda i, f: (0, f)),
                pl.BlockSpec((tf, D_MODEL), lambda i, f: (f, 0))]
    args = [h, x2, wg, wu, wd]
    if fin is not None:
        in_specs.append(pl.BlockSpec((1, D_MODEL), lambda i, f: (0, 0)))
        args.append(fin)
    return pl.pallas_call(
        functools.partial(_dense_ffn_body, final_norm=fin is not None),
        grid=(n // tm, dff // tf),
        in_specs=in_specs, out_specs=row(),
        out_shape=jax.ShapeDtypeStruct((n, D_MODEL), F32),
        scratch_shapes=[pltpu.VMEM((tm, D_MODEL), F32)],
        compiler_params=_cparams("parallel", "arbitrary"),
        name="dense_ffn",
    )(*args)


def _row_copy(src, src_row, dst, dst_row, sem):
    return pltpu.make_async_copy(src.at[pl.ds(src_row, 1)], dst.at[pl.ds(dst_row, 1)], sem)


def _moe_dispatch_body(dest_ref, h_ref, xs_in_ref, xs_ref, sem):
    del xs_in_ref
    tm = h_ref.shape[0]

    def issue(r, carry):
        for k in range(2):
            _row_copy(h_ref, r, xs_ref, dest_ref[k, r], sem).start()
        return carry

    lax.fori_loop(0, tm, issue, 0)
    for k in range(2):
        pltpu.make_async_copy(h_ref, xs_ref.at[pl.ds(0, tm)], sem).wait()


def _moe_dispatch(dest, h, xs_zero, *, tm):
    n = h.shape[0]
    return pl.pallas_call(
        _moe_dispatch_body,
        grid=(n // tm,),
        in_specs=[pl.BlockSpec((None, 2, tm), lambda i: (i, 0, 0), memory_space=pltpu.SMEM),
                  pl.BlockSpec((tm, D_MODEL), lambda i: (i, 0)),
                  pl.BlockSpec(memory_space=pl.ANY)],
        out_specs=pl.BlockSpec(memory_space=pl.ANY),
        out_shape=jax.ShapeDtypeStruct(xs_zero.shape, xs_zero.dtype),
        scratch_shapes=[pltpu.SemaphoreType.DMA(())],
        input_output_aliases={2: 0},
        compiler_params=pltpu.CompilerParams(dimension_semantics=("arbitrary",),
                                             vmem_limit_bytes=VMEM_LIMIT, has_side_effects=True),
        name="moe_dispatch",
    )(dest, h, xs_zero)


def _moe_ffn_body(te_ref, nu_ref, xs_ref, wg_ref, wu_ref, wd_ref, ys_ref, acc_ref):
    del te_ref
    i = pl.program_id(0)
    f = pl.program_id(1)

    @pl.when(i < nu_ref[0])
    def _():
        @pl.when(f == 0)
        def _():
            acc_ref[...] = jnp.zeros_like(acc_ref)

        acc_ref[...] += _swiglu_tile(xs_ref[...].astype(BF16), wg_ref[...], wu_ref[...], wd_ref[...])

        @pl.when(f == pl.num_programs(1) - 1)
        def _():
            ys_ref[...] = acc_ref[...]

    @pl.when((i >= nu_ref[0]) & (f == pl.num_programs(1) - 1))
    def _():
        ys_ref[...] = jnp.zeros_like(ys_ref)


def _moe_ffn(tile_expert, n_used, xs, wg, wu, wd, *, tme, tf):
    rows = xs.shape[0]
    dff = wg.shape[-1]
    nf = dff // tf
    row_idx = lambda i, f, te, nu: (jnp.minimum(i, nu[0] - 1), 0)
    col = lambda i, f, nu: jnp.where(i < nu[0], f, nf - 1)
    grid_spec = pltpu.PrefetchScalarGridSpec(
        num_scalar_prefetch=2,
        grid=(rows // tme, nf),
        in_specs=[pl.BlockSpec((tme, D_MODEL), row_idx),
                  pl.BlockSpec((None, D_MODEL, tf), lambda i, f, te, nu: (te[i], 0, col(i, f, nu))),
                  pl.BlockSpec((None, D_MODEL, tf), lambda i, f, te, nu: (te[i], 0, col(i, f, nu))),
                  pl.BlockSpec((None, tf, D_MODEL), lambda i, f, te, nu: (te[i], col(i, f, nu), 0))],
        out_specs=pl.BlockSpec((tme, D_MODEL), lambda i, f, te, nu: (i, 0)),
        scratch_shapes=[pltpu.VMEM((tme, D_MODEL), F32)])
    return pl.pallas_call(
        _moe_ffn_body,
        grid_spec=grid_spec,
        out_shape=jax.ShapeDtypeStruct((rows, D_MODEL), F32),
        compiler_params=_cparams("arbitrary", "arbitrary"),
        name="moe_ffn",
    )(tile_expert, n_used, xs, wg, wu, wd)


def _moe_combine_body(*refs, final_norm):
    if final_norm:
        dest_ref, x_ref, route_ref, ys_ref, fin_ref, o_ref, buf_ref, sem = refs
    else:
        dest_ref, x_ref, route_ref, ys_ref, o_ref, buf_ref, sem = refs
        fin_ref = None
    tm = x_ref.shape[0]

    def issue(r, carry):
        for k in range(2):
            _row_copy(ys_ref, dest_ref[k, r], buf_ref.at[k], r, sem).start()
        return carry

    lax.fori_loop(0, tm, issue, 0)
    for k in range(2):
        pltpu.make_async_copy(ys_ref.at[pl.ds(0, tm)], buf_ref.at[k], sem).wait()

    route = route_ref[...]
    lane = lax.broadcasted_iota(jnp.int32, route.shape, 1)
    w1 = jnp.sum(jnp.where(lane == R_W1, route, 0.0), axis=-1, keepdims=True)
    w2 = jnp.sum(jnp.where(lane == R_W2, route, 0.0), axis=-1, keepdims=True)
    _finish(x_ref[...], w1 * buf_ref[0] + w2 * buf_ref[1], fin_ref, o_ref)


def _moe_combine(dest, x2, route, ys, fin, *, tm):
    n = x2.shape[0]
    row = lambda width: pl.BlockSpec((tm, width), lambda i: (i, 0))
    in_specs = [pl.BlockSpec((None, 2, tm), lambda i: (i, 0, 0), memory_space=pltpu.SMEM),
                row(D_MODEL), row(LANES), pl.BlockSpec(memory_space=pl.ANY)]
    args = [dest, x2, route, ys]
    if fin is not None:
        in_specs.append(pl.BlockSpec((1, D_MODEL), lambda i: (0, 0)))
        args.append(fin)
    return pl.pallas_call(
        functools.partial(_moe_combine_body, final_norm=fin is not None),
        grid=(n // tm,),
        in_specs=in_specs, out_specs=row(D_MODEL),
        out_shape=jax.ShapeDtypeStruct((n, D_MODEL), F32),
        scratch_shapes=[pltpu.VMEM((2, tm, D_MODEL), F32), pltpu.SemaphoreType.DMA(())],
        compiler_params=_cparams("arbitrary"),
        name="moe_combine",
    )(*args)


def _moe_layout(route, counts, *, tm, tme):
    n = route.shape[0]
    cnt = counts[0, :N_EXPERTS].astype(jnp.int32)
    padded = (cnt + (tme - 1)) // tme * tme
    ends = jnp.cumsum(padded)
    starts = ends - padded
    e1 = route[:, R_E1].astype(jnp.int32)
    e2 = route[:, R_E2].astype(jnp.int32)
    dest = jnp.stack([starts[e1] + route[:, R_RANK1].astype(jnp.int32),
                      starts[e2] + route[:, R_RANK2].astype(jnp.int32)])
    dest = dest.reshape(2, n // tm, tm).transpose(1, 0, 2)
    n_tiles = (2 * n) // tme + N_EXPERTS
    tile_start = jnp.arange(n_tiles, dtype=jnp.int32) * tme
    tile_expert = jnp.minimum(jnp.sum(tile_start[:, None] >= ends[None, :], axis=1), N_EXPERTS - 1)
    n_used = (ends[-1] // tme).reshape(1)
    tile_expert = jnp.where(tile_start < ends[-1], tile_expert, tile_expert[jnp.maximum(n_used[0] - 1, 0)])
    return dest, tile_expert.astype(jnp.int32), n_used.astype(jnp.int32), n_tiles * tme


def _largest_tile(total, cap, mult):
    best = mult
    for t in range(mult, min(total, cap) + 1, mult):
        if total % t == 0:
            best = t
    return best


def _pad_lanes(a):
    return jnp.pad(a, [(0, 0)] * (a.ndim - 1) + [(0, LANES - a.shape[-1])])


def kernel(x, mix_norm_g, w_in, b_forget, ssm_lambda_re, ssm_lambda_im, ssm_log_dt, ssm_b_re, ssm_b_im, ssm_c_re, ssm_c_im, ssm_d, ssm_w_glu, gm_ln_g, gm_ln_b, gm_w_s, gm_b_s, group_norm_g, w_out, ffn_norm_g, dense_w_gate, dense_w_up, dense_w_down, router_w, moe_w_gate, moe_w_up, moe_w_down, final_norm_g):
    bsz, seq, _ = x.shape
    n = bsz * seq
    depth = w_in.shape[0]
    assert seq % CHUNK == 0 and seq % SSM_T == 0
    tm = _largest_tile(seq, 512, CHUNK)
    tq = _largest_tile(seq, 512, LANES)
    nc = n // SSM_T
    ncb = seq // SSM_T
    tr = _largest_tile(nc, 512, 8)
    tc = _largest_tile(ncb, 128, 8)

    x2 = x.reshape(n, D_MODEL).astype(F32)
    tril = jnp.tril(jnp.ones((CHUNK, CHUNK), F32))
    i0, i1, i2, i3, i4 = (D_SSM, D_SSM + D_ATT, D_SSM + 2 * D_ATT, D_SSM + 3 * D_ATT,
                          D_SSM + 3 * D_ATT + ATT_HEADS)
    for layer in range(depth):
        wl = w_in[layer]
        w_re = jnp.concatenate(
            [wl[:, :i0], wl[:, i0:i1] * (ATT_HEAD_DIM ** -0.5 * LOG2E), wl[:, i1:i3], wl[:, i4:],
             _pad_lanes(wl[:, i3:i4])], axis=1).astype(BF16)
        ws = (gm_w_s[layer].astype(F32) * tril[None]).astype(BF16)
        bs = jnp.repeat(gm_b_s[layer].astype(F32).T, GM_HEAD_DIM, axis=1)
        u, q, k, v, cum, ygm = _inproj(
            x2, mix_norm_g[layer].reshape(1, D_MODEL), w_re,
            _pad_lanes(b_forget[layer].reshape(1, ATT_HEADS).astype(F32)),
            gm_ln_g[layer].reshape(1, D_GM), gm_ln_b[layer].reshape(1, D_GM), ws, bs,
            seq=seq, tm=tm)

        mmat, fmat, emat, a_row = _ssm_matrices(
            ssm_lambda_re[layer], ssm_lambda_im[layer], ssm_log_dt[layer], ssm_b_re[layer],
            ssm_b_im[layer], ssm_c_re[layer], ssm_c_im[layer])
        u2 = u.reshape(nc, SSM_ROW)
        s_loc = _ssm_state(u2, fmat, tr=tr)
        xprev = _ssm_scan(s_loc.reshape(bsz, ncb, 2 * SSM_NSTATE), a_row, tc=tc)
        y_ssm = _ssm_out(u2, xprev.reshape(nc, 2 * SSM_NSTATE), mmat, emat, tr=tr).reshape(n, D_SSM)

        ck = cum[:, :ATT_HEADS].reshape(bsz, seq, ATT_HEADS // 2, 2).transpose(0, 2, 3, 1)
        y_att = _attention(q.reshape(bsz, seq, D_ATT), k.reshape(bsz, seq, D_ATT),
                           v.reshape(bsz, seq, D_ATT), cum.reshape(bsz, seq, LANES), ck,
                           tq=tq).reshape(n, D_ATT)

        is_moe = layer % 2 == 1
        j = layer // 2
        rw = _pad_lanes(router_w[j].astype(F32)) if is_moe else None
        outs = _outproj(x2, y_ssm, u, y_att, ygm, ssm_d[layer].reshape(1, D_SSM).astype(F32),
                        ssm_w_glu[layer].astype(BF16), group_norm_g[layer].reshape(1, D_MODEL),
                        w_out[layer].astype(BF16), ffn_norm_g[layer].reshape(1, D_MODEL), rw, tm=tm)
        fin = final_norm_g.reshape(1, D_MODEL) if layer == depth - 1 else None
        if is_moe:
            x_mid, h, route, counts = outs
            dffe = moe_w_gate.shape[-1]
            tme = min(MOE_ROW_TILE, 2 * n)
            dest, tile_expert, n_used, rows = _moe_layout(route, counts, tm=tm, tme=tme)
            xs = _moe_dispatch(dest, h, jnp.zeros((rows, D_MODEL), F32), tm=tm)
            ys = _moe_ffn(tile_expert, n_used, xs, moe_w_gate[j].astype(BF16),
                          moe_w_up[j].astype(BF16), moe_w_down[j].astype(BF16), tme=tme,
                          tf=_largest_tile(dffe, 1024, LANES))
            x2 = _moe_combine(dest, x_mid, route, ys, fin, tm=tm)
        else:
            x_mid, h = outs
            dff = dense_w_gate.shape[-1]
            x2 = _dense_ffn(h, x_mid, dense_w_gate[j].astype(BF16), dense_w_up[j].astype(BF16),
                            dense_w_down[j].astype(BF16), fin, tm=tm,
                            tf=_largest_tile(dff, 1536, LANES))
    return x2.reshape(bsz, seq, D_MODEL).astype(x.dtype)
```

```python
import functools
import math

import jax
import jax.numpy as jnp
from jax import lax
from jax.experimental import pallas as pl
from jax.experimental.pallas import tpu as pltpu

F32 = jnp.float32
BF16 = jnp.bfloat16

D_MODEL = 1024
D_SSM = 256
D_ATT = 512
D_GM = 256
SSM_GROUP = 16
SSM_GROUPS = 16
SSM_STATE = 64
ATT_HEADS = 8
ATT_HEAD_DIM = 64
GM_HEADS = 4
GM_HEAD_DIM = 64
CHUNK = 128
N_EXPERTS = 8
EPS = 1e-6
LOG2E = math.log2(math.e)

LANES = 128
SUBLANES = 8
MOE_ROW_TILE = 512
SSM_T = 8
SSM_ROW = SSM_T * D_SSM
SSM_NSTATE = SSM_GROUPS * SSM_STATE

C_U = 0
C_Q = C_U + D_SSM
C_K = C_Q + D_ATT
C_V = C_K + D_ATT
C_Z = C_V + D_ATT
C_F = C_Z + 2 * D_GM
D_IN_PAD = C_F + LANES

VMEM_LIMIT = 56 * 1024 * 1024


def _cparams(*sem):
    return pltpu.CompilerParams(dimension_semantics=sem, vmem_limit_bytes=VMEM_LIMIT)


def _resident(shape):
    nd = len(shape)
    return pl.BlockSpec(shape, lambda *_: (0,) * nd, pipeline_mode=pl.Buffered(1))


def _rms(x, g):
    return x * lax.rsqrt(jnp.mean(x * x, axis=-1, keepdims=True) + EPS) * g


def _gelu(x):
    c = math.sqrt(2.0 / math.pi)
    return 0.5 * x * (1.0 + jnp.tanh(c * (x + 0.044715 * (x * x * x))))


def _sigmoid(x):
    return 1.0 / (1.0 + jnp.exp(-x))


def _split3(x):
    p1 = x.astype(BF16)
    r1 = x - p1.astype(F32)
    p2 = r1.astype(BF16)
    r2 = r1 - p2.astype(F32)
    return p1, p2, r2.astype(BF16)


def _dot(a, b):
    return jnp.dot(a, b, preferred_element_type=F32)


def _inproj_body(x_ref, g_ref, w_ref, bf_ref, lng_ref, lnb_ref, ws_ref, bs_ref,
                 u_ref, q_ref, k_ref, v_ref, cum_ref, ygm_ref, carry_ref, *, tiles_per_seq):
    i = pl.program_id(0)

    @pl.when(i % tiles_per_seq == 0)
    def _():
        carry_ref[...] = jnp.zeros_like(carry_ref)

    tm = x_ref.shape[0]
    hb = _rms(x_ref[...], g_ref[...]).astype(BF16)

    def proj(lo, width):
        return _dot(hb, w_ref[:, lo:lo + width])

    u_ref[...] = proj(C_U, D_SSM)
    q_ref[...] = proj(C_Q, D_ATT).astype(BF16)
    k_ref[...] = proj(C_K, D_ATT).astype(BF16)
    v_ref[...] = proj(C_V, D_ATT).astype(BF16)

    f = proj(C_F, LANES) + bf_ref[...]
    logf = (jnp.minimum(f, 0.0) - jnp.log(1.0 + jnp.exp(-jnp.abs(f)))) * LOG2E
    row = lax.broadcasted_iota(jnp.int32, (tm, tm), 0)
    col = lax.broadcasted_iota(jnp.int32, (tm, tm), 1)
    tri = (col <= row).astype(BF16)
    p1, p2, p3 = _split3(logf)
    cum = _dot(tri, p1) + _dot(tri, p2) + _dot(tri, p3) + carry_ref[...]
    cum_ref[...] = cum
    carry_ref[...] = cum[tm - 1:tm, :]

    zg = _gelu(proj(C_Z, 2 * D_GM))
    ug = zg[:, :D_GM]
    vg = zg[:, D_GM:]
    mean = jnp.mean(vg, axis=-1, keepdims=True)
    cen = vg - mean
    var = jnp.mean(cen * cen, axis=-1, keepdims=True)
    vn = (cen * lax.rsqrt(var + EPS) * lng_ref[...] + lnb_ref[...]).astype(BF16)
    lane_head = lax.broadcasted_iota(jnp.int32, (CHUNK, D_GM), 1) >> 6
    for c in range(tm // CHUNK):
        rows = slice(c * CHUNK, (c + 1) * CHUNK)
        vc = vn[rows, :]
        mixed = _dot(ws_ref[0], vc)
        for g in range(1, GM_HEADS):
            mixed = jnp.where(lane_head == g, _dot(ws_ref[g], vc), mixed)
        mixed = mixed + bs_ref[...]
        ygm_ref[rows, :] = (ug[rows, :] * mixed).astype(BF16)


def _inproj(x2, g, w, bf, lng, lnb, ws, bs, *, seq, tm):
    n = x2.shape[0]
    row = lambda width: pl.BlockSpec((tm, width), lambda i: (i, 0))
    return pl.pallas_call(
        functools.partial(_inproj_body, tiles_per_seq=seq // tm),
        grid=(n // tm,),
        in_specs=[row(D_MODEL), _resident((1, D_MODEL)), _resident((D_MODEL, D_IN_PAD)),
                  _resident((1, LANES)), _resident((1, D_GM)), _resident((1, D_GM)),
                  _resident((GM_HEADS, CHUNK, CHUNK)), _resident((CHUNK, D_GM))],
        out_specs=[row(D_SSM), row(D_ATT), row(D_ATT), row(D_ATT), row(LANES), row(D_GM)],
        out_shape=[jax.ShapeDtypeStruct((n, D_SSM), F32),
                   jax.ShapeDtypeStruct((n, D_ATT), BF16),
                   jax.ShapeDtypeStruct((n, D_ATT), BF16),
                   jax.ShapeDtypeStruct((n, D_ATT), BF16),
                   jax.ShapeDtypeStruct((n, LANES), F32),
                   jax.ShapeDtypeStruct((n, D_GM), BF16)],
        scratch_shapes=[pltpu.VMEM((1, LANES), F32)],
        compiler_params=_cparams("arbitrary"),
        name="inproj",
    )(x2, g, w, bf, lng, lnb, ws, bs)


def _ssm_state_body(u_ref, f_ref, s_ref):
    s_ref[...] = _dot(u_ref[...].astype(BF16), f_ref[...])


def _ssm_state(u2, fmat, *, tr):
    nc = u2.shape[0]
    return pl.pallas_call(
        _ssm_state_body,
        grid=(nc // tr,),
        in_specs=[pl.BlockSpec((tr, SSM_ROW), lambda i: (i, 0)),
                  _resident((SSM_ROW, 2 * SSM_NSTATE))],
        out_specs=pl.BlockSpec((tr, 2 * SSM_NSTATE), lambda i: (i, 0)),
        out_shape=jax.ShapeDtypeStruct((nc, 2 * SSM_NSTATE), F32),
        compiler_params=_cparams("parallel"),
        name="ssm_state",
    )(u2, fmat)


def _ssm_scan_body(s_ref, a_ref, xp_ref, st_ref):
    @pl.when(pl.program_id(1) == 0)
    def _():
        st_ref[...] = jnp.zeros_like(st_ref)

    tc = s_ref.shape[0]
    are = a_ref[:, :SSM_NSTATE]
    aim = a_ref[:, SSM_NSTATE:]

    def step(c8, carry):
        xre, xim = carry
        base = pl.multiple_of(c8 * SUBLANES, SUBLANES)
        blk = s_ref[pl.ds(base, SUBLANES), :]
        prev_re, prev_im = [], []
        for r in range(SUBLANES):
            prev_re.append(xre)
            prev_im.append(xim)
            sre = blk[r:r + 1, :SSM_NSTATE]
            sim = blk[r:r + 1, SSM_NSTATE:]
            xre, xim = are * xre - aim * xim + sre, are * xim + aim * xre + sim
        xp_ref[pl.ds(base, SUBLANES), :SSM_NSTATE] = jnp.concatenate(prev_re, axis=0)
        xp_ref[pl.ds(base, SUBLANES), SSM_NSTATE:] = jnp.concatenate(prev_im, axis=0)
        return xre, xim

    xre, xim = lax.fori_loop(0, tc // SUBLANES, step,
                             (st_ref[:, :SSM_NSTATE], st_ref[:, SSM_NSTATE:]))
    st_ref[:, :SSM_NSTATE] = xre
    st_ref[:, SSM_NSTATE:] = xim


def _ssm_scan(s3, a, *, tc):
    b, ncb, w = s3.shape
    spec = pl.BlockSpec((None, tc, w), lambda bi, ci: (bi, ci, 0))
    return pl.pallas_call(
        _ssm_scan_body,
        grid=(b, ncb // tc),
        in_specs=[spec, pl.BlockSpec((1, w), lambda bi, ci: (0, 0))],
        out_specs=spec,
        out_shape=jax.ShapeDtypeStruct(s3.shape, F32),
        scratch_shapes=[pltpu.VMEM((1, w), F32)],
        compiler_params=_cparams("parallel", "arbitrary"),
        name="ssm_scan",
    )(s3, a)


def _ssm_out_body(u_ref, xp_ref, m_ref, e_ref, y_ref):
    y_ref[...] = (_dot(u_ref[...].astype(BF16), m_ref[...])
                  + _dot(xp_ref[...].astype(BF16), e_ref[...]))


def _ssm_out(u2, xp, mmat, emat, *, tr):
    nc = u2.shape[0]
    return pl.pallas_call(
        _ssm_out_body,
        grid=(nc // tr,),
        in_specs=[pl.BlockSpec((tr, SSM_ROW), lambda i: (i, 0)),
                  pl.BlockSpec((tr, 2 * SSM_NSTATE), lambda i: (i, 0)),
                  _resident((SSM_ROW, SSM_ROW)), _resident((2 * SSM_NSTATE, SSM_ROW))],
        out_specs=pl.BlockSpec((tr, SSM_ROW), lambda i: (i, 0)),
        out_shape=jax.ShapeDtypeStruct((nc, SSM_ROW), F32),
        compiler_params=_cparams("parallel"),
        name="ssm_out",
    )(u2, xp, mmat, emat)


def _ssm_matrices(lam_re, lam_im, log_dt, b_re, b_im, c_re, c_im):
    t, g, p, h = SSM_T, SSM_GROUPS, SSM_STATE, SSM_GROUP
    hp = lax.Precision.HIGHEST
    lr, li = lam_re.astype(F32), lam_im.astype(F32)
    dt = jnp.exp(log_dt.astype(F32))[:, None]
    steps = jnp.arange(t + 1, dtype=F32)[:, None, None]
    mag = jnp.exp((lr * dt)[None] * steps)
    ang = (li * dt)[None] * steps
    pr, pi = mag * jnp.cos(ang), mag * jnp.sin(ang)
    nr, ni = pr[1] - 1.0, pi[1]
    den = lr * lr + li * li
    zr, zi = (nr * lr + ni * li) / den, (ni * lr - nr * li) / den
    br, bi = b_re.astype(F32), b_im.astype(F32)
    bbr = zr[..., None] * br - zi[..., None] * bi
    bbi = zr[..., None] * bi + zi[..., None] * br
    cr, ci = c_re.astype(F32)[None], c_im.astype(F32)[None]
    cpr = cr * pr[:, :, None, :] - ci * pi[:, :, None, :]
    cpi = cr * pi[:, :, None, :] + ci * pr[:, :, None, :]
    eye = jnp.eye(g, dtype=F32)

    kern = (jnp.einsum('tgop,gpi->tgio', cpr[:t], bbr, precision=hp)
            - jnp.einsum('tgop,gpi->tgio', cpi[:t], bbi, precision=hp))
    lag = jnp.arange(t)[None, :] - jnp.arange(t)[:, None]
    k_st = jnp.where((lag >= 0)[:, :, None, None, None], kern[jnp.clip(lag, 0, t - 1)], 0.0)
    mmat = (k_st[:, :, :, :, None, :] * eye[None, None, :, None, :, None])
    mmat = mmat.transpose(0, 2, 3, 1, 4, 5).reshape(t * g * h, t * g * h)

    qr, qi = pr[:t][::-1][..., None], pi[:t][::-1][..., None]
    fr = qr * bbr[None] - qi * bbi[None]
    fi = qr * bbi[None] + qi * bbr[None]

    def rows_to_state(w):
        w = w.transpose(0, 1, 3, 2)[:, :, :, None, :] * eye[None, :, None, :, None]
        return w.reshape(t * g * h, g * p)

    fmat = jnp.concatenate([rows_to_state(fr), rows_to_state(fi)], axis=1)

    def state_to_rows(w):
        w = w.transpose(1, 3, 0, 2)[:, :, :, None, :] * eye[:, None, None, :, None]
        return w.reshape(g * p, t * g * h)

    emat = jnp.concatenate([state_to_rows(cpr[1:]), -state_to_rows(cpi[1:])], axis=0)

    a_row = jnp.concatenate([pr[t].reshape(1, g * p), pi[t].reshape(1, g * p)], axis=1)
    return mmat.astype(BF16), fmat.astype(BF16), emat.astype(BF16), a_row


def _attn_body(q_ref, k_ref, v_ref, cq_ref, ck_ref, o_ref,
               m_ref, acc_ref, s_ref, mt_ref, p_ref, al_ref, *, tq):
    hp = pl.program_id(1)
    qi = pl.program_id(2)
    nchunk = tq // LANES
    q = q_ref[...]
    lane = lax.broadcasted_iota(jnp.int32, (tq, LANES), 1)
    upper = lane >= ATT_HEAD_DIM
    own = [~upper, upper]
    cq_all = cq_ref[...]
    qs = [jnp.where(own[j], q, jnp.zeros_like(q)) for j in range(2)]
    cqs = [jnp.broadcast_to(
        jnp.sum(jnp.where(lane == 2 * hp + j, cq_all, 0.0), axis=-1, keepdims=True), (tq, LANES))
        for j in range(2)]
    rowpos = lax.broadcasted_iota(jnp.int32, (tq, LANES), 0)

    def chunk(c):
        return slice(c * LANES, (c + 1) * LANES)

    def scores(t, slot, masked):
        start = pl.multiple_of(t * tq, tq)
        kt = k_ref[pl.ds(start, tq), :]
        for j in range(2):
            s = lax.dot_general(qs[j], kt, (((1,), (1,)), ((), ())), preferred_element_type=F32)
            ck = ck_ref[j:j + 1, pl.ds(start, tq)]
            mt = None
            for c in range(nchunk):
                sc = s[:, chunk(c)] - ck[:, chunk(c)]
                if masked:
                    sc = jnp.where(lane + c * LANES <= rowpos, sc, -jnp.inf)
                s_ref[slot, j, :, chunk(c)] = sc
                mt = sc if mt is None else jnp.maximum(mt, sc)
            mt_ref[slot, j] = jnp.broadcast_to(jnp.max(mt, axis=-1, keepdims=True), (tq, LANES))

    def probs(slot):
        for j in range(2):
            m = m_ref[j]
            m_new = jnp.maximum(m, mt_ref[slot, j] + cqs[j])
            shift = cqs[j] - m_new
            for c in range(nchunk):
                p_ref[slot, j, :, chunk(c)] = jnp.exp2(s_ref[slot, j, :, chunk(c)] + shift).astype(BF16)
            al_ref[slot, j] = jnp.exp2(m - m_new)
            m_ref[j] = m_new

    def values(t, slot):
        start = pl.multiple_of(t * tq, tq)
        vt = v_ref[pl.ds(start, tq), :]
        for j in range(2):
            vj = jnp.where(own[j], vt, jnp.ones_like(vt))
            acc_ref[j] = al_ref[slot, j] * acc_ref[j] + _dot(p_ref[slot, j], vj)

    def trip(t, slot, masked=False):
        values(t - 1, 1 - slot)
        probs(slot)
        scores(t + 1, 1 - slot, masked)

    for j in range(2):
        m_ref[j] = jnp.full((tq, LANES), -jnp.inf, F32)
        acc_ref[j] = jnp.zeros((tq, LANES), F32)

    @pl.when(qi == 0)
    def _():
        scores(0, 0, True)
        probs(0)
        values(0, 0)

    @pl.when(qi == 1)
    def _():
        scores(0, 0, False)
        probs(0)
        scores(1, 1, True)
        values(0, 0)
        probs(1)
        values(1, 1)

    @pl.when(qi >= 2)
    def _():
        scores(0, 0, False)
        probs(0)
        scores(1, 1, False)

        def pair(u, carry):
            t = 2 * u + 1
            trip(t, 1)
            trip(t + 1, 0)
            return carry

        lax.fori_loop(0, (qi - 2) // 2, pair, 0)

        @pl.when(qi % 2 == 0)
        def _():
            trip(qi - 1, 1, masked=True)
            values(qi - 1, 1)
            probs(0)
            values(qi, 0)

        @pl.when(qi % 2 == 1)
        def _():
            trip(qi - 2, 1)
            trip(qi - 1, 0, masked=True)
            values(qi - 1, 0)
            probs(1)
            values(qi, 1)
    outs = []
    for j in range(2):
        acc = acc_ref[j]
        outs.append(acc / pltpu.roll(acc, ATT_HEAD_DIM, axis=1))
    o_ref[...] = jnp.where(upper, outs[1], outs[0]).astype(BF16)


def _attention(q3, k3, v3, cq3, ck4, *, tq):
    b, s, _ = q3.shape
    npairs = ATT_HEADS // 2
    return pl.pallas_call(
        functools.partial(_attn_body, tq=tq),
        scratch_shapes=[pltpu.VMEM((2, tq, LANES), F32), pltpu.VMEM((2, tq, LANES), F32),
                        pltpu.VMEM((2, 2, tq, tq), F32), pltpu.VMEM((2, 2, tq, LANES), F32),
                        pltpu.VMEM((2, 2, tq, tq), BF16), pltpu.VMEM((2, 2, tq, LANES), F32)],
        grid=(b, npairs, s // tq),
        in_specs=[pl.BlockSpec((None, tq, LANES), lambda bi, hp, qi: (bi, qi, hp)),
                  pl.BlockSpec((None, s, LANES), lambda bi, hp, qi: (bi, 0, hp)),
                  pl.BlockSpec((None, s, LANES), lambda bi, hp, qi: (bi, 0, hp)),
                  pl.BlockSpec((None, tq, LANES), lambda bi, hp, qi: (bi, qi, 0)),
                  pl.BlockSpec((None, None, 2, s), lambda bi, hp, qi: (bi, hp, 0, 0))],
        out_specs=pl.BlockSpec((None, tq, LANES), lambda bi, hp, qi: (bi, qi, hp)),
        out_shape=jax.ShapeDtypeStruct((b, s, D_ATT), BF16),
        compiler_params=_cparams("parallel", "parallel", "arbitrary"),
        name="attention",
    )(q3, k3, v3, cq3, ck4)


R_E1, R_E2, R_W1, R_W2, R_RANK1, R_RANK2 = range(6)


def _route(logits, cnt_ref):
    tm = logits.shape[0]
    lane = lax.broadcasted_iota(jnp.int32, logits.shape, 1).astype(F32)
    lg = jnp.where(lane < N_EXPERTS, logits, -jnp.inf)
    m1 = jnp.max(lg, axis=-1, keepdims=True)
    i1 = jnp.min(jnp.where(lg == m1, lane, float(LANES)), axis=-1, keepdims=True)
    lg2 = jnp.where(lane == i1, -jnp.inf, lg)
    m2 = jnp.max(lg2, axis=-1, keepdims=True)
    i2 = jnp.min(jnp.where(lg2 == m2, lane, float(LANES)), axis=-1, keepdims=True)
    e = jnp.exp(m2 - m1)
    w1 = 1.0 / (1.0 + e)
    w2 = e / (1.0 + e)

    hit1, hit2 = lane == i1, lane == i2
    oh1, oh2 = hit1.astype(BF16), hit2.astype(BF16)
    row = lax.broadcasted_iota(jnp.int32, (tm, tm), 0)
    col = lax.broadcasted_iota(jnp.int32, (tm, tm), 1)
    before = (col < row).astype(BF16)
    carry = cnt_ref[...]
    tot1 = jnp.sum(oh1.astype(F32), axis=0, keepdims=True)
    tot2 = jnp.sum(oh2.astype(F32), axis=0, keepdims=True)
    pos1 = _dot(before, oh1) + carry
    pos2 = _dot(before, oh2) + (carry + tot1)
    rank1 = jnp.sum(jnp.where(hit1, pos1, 0.0), axis=-1, keepdims=True)
    rank2 = jnp.sum(jnp.where(hit2, pos2, 0.0), axis=-1, keepdims=True)
    cnt_ref[...] = carry + tot1 + tot2

    rec = jnp.zeros(logits.shape, F32)
    for idx, val in ((R_E1, i1), (R_E2, i2), (R_W1, w1), (R_W2, w2), (R_RANK1, rank1), (R_RANK2, rank2)):
        rec = jnp.where(lane == idx, val, rec)
    return rec


def _outproj_body(*refs, with_router):
    if with_router:
        (x_ref, ys_ref, us_ref, ya_ref, yg_ref, d_ref, wglu_ref, gn_ref, w_ref, fg_ref, rw_ref,
         xo_ref, h_ref, route_ref, cnt_ref) = refs

        @pl.when(pl.program_id(0) == 0)
        def _():
            cnt_ref[...] = jnp.zeros_like(cnt_ref)
    else:
        (x_ref, ys_ref, us_ref, ya_ref, yg_ref, d_ref, wglu_ref, gn_ref, w_ref, fg_ref,
         xo_ref, h_ref) = refs

    y1 = _gelu(ys_ref[...] + d_ref[...] * us_ref[...])
    y_ssm = y1 * _sigmoid(_dot(y1.astype(BF16), wglu_ref[...]))

    o1, o2 = D_SSM, D_SSM + D_ATT
    acc = _dot(_rms(y_ssm, gn_ref[:, :o1]).astype(BF16), w_ref[:o1, :])
    acc += _dot(_rms(ya_ref[...].astype(F32), gn_ref[:, o1:o2]).astype(BF16), w_ref[o1:o2, :])
    acc += _dot(_rms(yg_ref[...].astype(F32), gn_ref[:, o2:]).astype(BF16), w_ref[o2:, :])
    xn = x_ref[...] + acc
    xo_ref[...] = xn
    hn = _rms(xn, fg_ref[...])
    h_ref[...] = hn.astype(h_ref.dtype)
    if with_router:
        h1, h2, h3 = _split3(hn)
        r1, r2, r3 = _split3(rw_ref[...])
        logits = (_dot(h1, r1) + _dot(h1, r2) + _dot(h2, r1)
                  + _dot(h1, r3) + _dot(h2, r2) + _dot(h3, r1))
        route_ref[...] = _route(logits, cnt_ref)


def _outproj(x2, ys, us, ya, yg, d, wglu, gn, w, fg, rw, *, tm):
    n = x2.shape[0]
    with_router = rw is not None
    row = lambda width: pl.BlockSpec((tm, width), lambda i: (i, 0))
    in_specs = [row(D_MODEL), row(D_SSM), row(D_SSM), row(D_ATT), row(D_GM),
                _resident((1, D_SSM)), _resident((D_SSM, D_SSM)), _resident((1, D_MODEL)),
                _resident((D_MODEL, D_MODEL)), _resident((1, D_MODEL))]
    out_specs = [row(D_MODEL), row(D_MODEL)]
    out_shape = [jax.ShapeDtypeStruct((n, D_MODEL), F32),
                 jax.ShapeDtypeStruct((n, D_MODEL), F32 if with_router else BF16)]
    args = [x2, ys, us, ya, yg, d, wglu, gn, w, fg]
    if with_router:
        in_specs.append(_resident((D_MODEL, LANES)))
        out_specs += [row(LANES), pl.BlockSpec((1, LANES), lambda i: (0, 0))]
        out_shape += [jax.ShapeDtypeStruct((n, LANES), F32), jax.ShapeDtypeStruct((1, LANES), F32)]
        args.append(rw)
    return pl.pallas_call(
        functools.partial(_outproj_body, with_router=with_router),
        grid=(n // tm,),
        in_specs=in_specs, out_specs=out_specs, out_shape=out_shape,
        compiler_params=_cparams("arbitrary" if with_router else "parallel"),
        name="outproj_router" if with_router else "outproj",
    )(*args)


def _swiglu_tile(h, wg, wu, wd):
    a = _dot(h, wg)
    return _dot((a * _sigmoid(a) * _dot(h, wu)).astype(BF16), wd)


def _finish(x, acc, fin_ref, o_ref):
    xn = x + acc
    o_ref[...] = xn if fin_ref is None else _rms(xn, fin_ref[...])


def _dense_ffn_body(*refs, final_norm):
    if final_norm:
        h_ref, x_ref, wg_ref, wu_ref, wd_ref, fin_ref, o_ref, acc_ref = refs
    else:
        h_ref, x_ref, wg_ref, wu_ref, wd_ref, o_ref, acc_ref = refs
        fin_ref = None
    f = pl.program_id(1)

    @pl.when(f == 0)
    def _():
        acc_ref[...] = jnp.zeros_like(acc_ref)

    acc_ref[...] += _swiglu_tile(h_ref[...], wg_ref[...], wu_ref[...], wd_ref[...])

    @pl.when(f == pl.num_programs(1) - 1)
    def _():
        _finish(x_ref[...], acc_ref[...], fin_ref, o_ref)


def _dense_ffn(h, x2, wg, wu, wd, fin, *, tm, tf):
    n = x2.shape[0]
    dff = wg.shape[1]
    row = lambda: pl.BlockSpec((tm, D_MODEL), lambda i, f: (i, 0))
    in_specs = [row(), row(),
                pl.BlockSpec((D_MODEL, tf), lambda i, f: (0, f)),
                pl.BlockSpec((D_MODEL, tf), lambda i, f: (0, f)),
                pl.BlockSpec((tf, D_MODEL), lambda i, f: (f, 0))]
    args = [h, x2, wg, wu, wd]
    if fin is not None:
        in_specs.append(pl.BlockSpec((1, D_MODEL), lambda i, f: (0, 0)))
        args.append(fin)
    return pl.pallas_call(
        functools.partial(_dense_ffn_body, final_norm=fin is not None),
        grid=(n // tm, dff // tf),
        in_specs=in_specs, out_specs=row(),
        out_shape=jax.ShapeDtypeStruct((n, D_MODEL), F32),
        scratch_shapes=[pltpu.VMEM((tm, D_MODEL), F32)],
        compiler_params=_cparams("parallel", "arbitrary"),
        name="dense_ffn",
    )(*args)


def _row_copy(src, src_row, dst, dst_row, sem):
    return pltpu.make_async_copy(src.at[pl.ds(src_row, 1)], dst.at[pl.ds(dst_row, 1)], sem)


def _moe_dispatch_body(dest_ref, h_ref, xs_in_ref, xs_ref, sem):
    del xs_in_ref
    tm = h_ref.shape[0]

    def issue(r, carry):
        for k in range(2):
            _row_copy(h_ref, r, xs_ref, dest_ref[k, r], sem).start()
        return carry

    lax.fori_loop(0, tm, issue, 0)
    for k in range(2):
        pltpu.make_async_copy(h_ref, xs_ref.at[pl.ds(0, tm)], sem).wait()


def _moe_dispatch(dest, h, xs_zero, *, tm):
    n = h.shape[0]
    return pl.pallas_call(
        _moe_dispatch_body,
        grid=(n // tm,),
        in_specs=[pl.BlockSpec((None, 2, tm), lambda i: (i, 0, 0), memory_space=pltpu.SMEM),
                  pl.BlockSpec((tm, D_MODEL), lambda i: (i, 0)),
                  pl.BlockSpec(memory_space=pl.ANY)],
        out_specs=pl.BlockSpec(memory_space=pl.ANY),
        out_shape=jax.ShapeDtypeStruct(xs_zero.shape, xs_zero.dtype),
        scratch_shapes=[pltpu.SemaphoreType.DMA(())],
        input_output_aliases={2: 0},
        compiler_params=pltpu.CompilerParams(dimension_semantics=("arbitrary",),
                                             vmem_limit_bytes=VMEM_LIMIT, has_side_effects=True),
        name="moe_dispatch",
    )(dest, h, xs_zero)


def _moe_ffn_body(te_ref, nu_ref, xs_ref, wg_ref, wu_ref, wd_ref, ys_ref, acc_ref):
    del te_ref
    i = pl.program_id(0)
    f = pl.program_id(1)

    @pl.when(i < nu_ref[0])
    def _():
        @pl.when(f == 0)
        def _():
            acc_ref[...] = jnp.zeros_like(acc_ref)

        acc_ref[...] += _swiglu_tile(xs_ref[...].astype(BF16), wg_ref[...], wu_ref[...], wd_ref[...])

        @pl.when(f == pl.num_programs(1) - 1)
        def _():
            ys_ref[...] = acc_ref[...]

    @pl.when((i >= nu_ref[0]) & (f == pl.num_programs(1) - 1))
    def _():
        ys_ref[...] = jnp.zeros_like(ys_ref)


def _moe_ffn(tile_expert, n_used, xs, wg, wu, wd, *, tme, tf):
    rows = xs.shape[0]
    dff = wg.shape[-1]
    nf = dff // tf
    row_idx = lambda i, f, te, nu: (jnp.minimum(i, nu[0] - 1), 0)
    col = lambda i, f, nu: jnp.where(i < nu[0], f, nf - 1)
    grid_spec = pltpu.PrefetchScalarGridSpec(
        num_scalar_prefetch=2,
        grid=(rows // tme, nf),
        in_specs=[pl.BlockSpec((tme, D_MODEL), row_idx),
                  pl.BlockSpec((None, D_MODEL, tf), lambda i, f, te, nu: (te[i], 0, col(i, f, nu))),
                  pl.BlockSpec((None, D_MODEL, tf), lambda i, f, te, nu: (te[i], 0, col(i, f, nu))),
                  pl.BlockSpec((None, tf, D_MODEL), lambda i, f, te, nu: (te[i], col(i, f, nu), 0))],
        out_specs=pl.BlockSpec((tme, D_MODEL), lambda i, f, te, nu: (i, 0)),
        scratch_shapes=[pltpu.VMEM((tme, D_MODEL), F32)])
    return pl.pallas_call(
        _moe_ffn_body,
        grid_spec=grid_spec,
        out_shape=jax.ShapeDtypeStruct((rows, D_MODEL), F32),
        compiler_params=_cparams("arbitrary", "arbitrary"),
        name="moe_ffn",
    )(tile_expert, n_used, xs, wg, wu, wd)


def _moe_combine_body(*refs, final_norm):
    if final_norm:
        dest_ref, x_ref, route_ref, ys_ref, fin_ref, o_ref, buf_ref, sem = refs
    else:
        dest_ref, x_ref, route_ref, ys_ref, o_ref, buf_ref, sem = refs
        fin_ref = None
    tm = x_ref.shape[0]

    def issue(r, carry):
        for k in range(2):
            _row_copy(ys_ref, dest_ref[k, r], buf_ref.at[k], r, sem).start()
        return carry

    lax.fori_loop(0, tm, issue, 0)
    for k in range(2):
        pltpu.make_async_copy(ys_ref.at[pl.ds(0, tm)], buf_ref.at[k], sem).wait()

    route = route_ref[...]
    lane = lax.broadcasted_iota(jnp.int32, route.shape, 1)
    w1 = jnp.sum(jnp.where(lane == R_W1, route, 0.0), axis=-1, keepdims=True)
    w2 = jnp.sum(jnp.where(lane == R_W2, route, 0.0), axis=-1, keepdims=True)
    _finish(x_ref[...], w1 * buf_ref[0] + w2 * buf_ref[1], fin_ref, o_ref)


def _moe_combine(dest, x2, route, ys, fin, *, tm):
    n = x2.shape[0]
    row = lambda width: pl.BlockSpec((tm, width), lambda i: (i, 0))
    in_specs = [pl.BlockSpec((None, 2, tm), lambda i: (i, 0, 0), memory_space=pltpu.SMEM),
                row(D_MODEL), row(LANES), pl.BlockSpec(memory_space=pl.ANY)]
    args = [dest, x2, route, ys]
    if fin is not None:
        in_specs.append(pl.BlockSpec((1, D_MODEL), lambda i: (0, 0)))
        args.append(fin)
    return pl.pallas_call(
        functools.partial(_moe_combine_body, final_norm=fin is not None),
        grid=(n // tm,),
        in_specs=in_specs, out_specs=row(D_MODEL),
        out_shape=jax.ShapeDtypeStruct((n, D_MODEL), F32),
        scratch_shapes=[pltpu.VMEM((2, tm, D_MODEL), F32), pltpu.SemaphoreType.DMA(())],
        compiler_params=_cparams("arbitrary"),
        name="moe_combine",
    )(*args)


def _moe_layout(route, counts, *, tm, tme):
    n = route.shape[0]
    cnt = counts[0, :N_EXPERTS].astype(jnp.int32)
    padded = (cnt + (tme - 1)) // tme * tme
    ends = jnp.cumsum(padded)
    starts = ends - padded
    e1 = route[:, R_E1].astype(jnp.int32)
    e2 = route[:, R_E2].astype(jnp.int32)
    dest = jnp.stack([starts[e1] + route[:, R_RANK1].astype(jnp.int32),
                      starts[e2] + route[:, R_RANK2].astype(jnp.int32)])
    dest = dest.reshape(2, n // tm, tm).transpose(1, 0, 2)
    n_tiles = (2 * n) // tme + N_EXPERTS
    tile_start = jnp.arange(n_tiles, dtype=jnp.int32) * tme
    tile_expert = jnp.minimum(jnp.sum(tile_start[:, None] >= ends[None, :], axis=1), N_EXPERTS - 1)
    n_used = (ends[-1] // tme).reshape(1)
    tile_expert = jnp.where(tile_start < ends[-1], tile_expert, tile_expert[jnp.maximum(n_used[0] - 1, 0)])
    return dest, tile_expert.astype(jnp.int32), n_used.astype(jnp.int32), n_tiles * tme


def _largest_tile(total, cap, mult):
    best = mult
    for t in range(mult, min(total, cap) + 1, mult):
        if total % t == 0:
            best = t
    return best


def _pad_lanes(a):
    return jnp.pad(a, [(0, 0)] * (a.ndim - 1) + [(0, LANES - a.shape[-1])])


def kernel(x, mix_norm_g, w_in, b_forget, ssm_lambda_re, ssm_lambda_im, ssm_log_dt, ssm_b_re, ssm_b_im, ssm_c_re, ssm_c_im, ssm_d, ssm_w_glu, gm_ln_g, gm_ln_b, gm_w_s, gm_b_s, group_norm_g, w_out, ffn_norm_g, dense_w_gate, dense_w_up, dense_w_down, router_w, moe_w_gate, moe_w_up, moe_w_down, final_norm_g):
    bsz, seq, _ = x.shape
    n = bsz * seq
    depth = w_in.shape[0]
    assert seq % CHUNK == 0 and seq % SSM_T == 0
    tm = _largest_tile(seq, 512, CHUNK)
    tq = _largest_tile(seq, 512, LANES)
    nc = n // SSM_T
    ncb = seq // SSM_T
    tr = _largest_tile(nc, 512, 8)
    tc = _largest_tile(ncb, 128, 8)

    x2 = x.reshape(n, D_MODEL).astype(F32)
    tril = jnp.tril(jnp.ones((CHUNK, CHUNK), F32))
    i0, i1, i2, i3, i4 = (D_SSM, D_SSM + D_ATT, D_SSM + 2 * D_ATT, D_SSM + 3 * D_ATT,
                          D_SSM + 3 * D_ATT + ATT_HEADS)
    for layer in range(depth):
        wl = w_in[layer]
        w_re = jnp.concatenate(
            [wl[:, :i0], wl[:, i0:i1] * (ATT_HEAD_DIM ** -0.5 * LOG2E), wl[:, i1:i3], wl[:, i4:],
             _pad_lanes(wl[:, i3:i4])], axis=1).astype(BF16)
        ws = (gm_w_s[layer].astype(F32) * tril[None]).astype(BF16)
        bs = jnp.repeat(gm_b_s[layer].astype(F32).T, GM_HEAD_DIM, axis=1)
        u, q, k, v, cum, ygm = _inproj(
            x2, mix_norm_g[layer].reshape(1, D_MODEL), w_re,
            _pad_lanes(b_forget[layer].reshape(1, ATT_HEADS).astype(F32)),
            gm_ln_g[layer].reshape(1, D_GM), gm_ln_b[layer].reshape(1, D_GM), ws, bs,
            seq=seq, tm=tm)

        mmat, fmat, emat, a_row = _ssm_matrices(
            ssm_lambda_re[layer], ssm_lambda_im[layer], ssm_log_dt[layer], ssm_b_re[layer],
            ssm_b_im[layer], ssm_c_re[layer], ssm_c_im[layer])
        u2 = u.reshape(nc, SSM_ROW)
        s_loc = _ssm_state(u2, fmat, tr=tr)
        xprev = _ssm_scan(s_loc.reshape(bsz, ncb, 2 * SSM_NSTATE), a_row, tc=tc)
        y_ssm = _ssm_out(u2, xprev.reshape(nc, 2 * SSM_NSTATE), mmat, emat, tr=tr).reshape(n, D_SSM)

        ck = cum[:, :ATT_HEADS].reshape(bsz, seq, ATT_HEADS // 2, 2).transpose(0, 2, 3, 1)
        y_att = _attention(q.reshape(bsz, seq, D_ATT), k.reshape(bsz, seq, D_ATT),
                           v.reshape(bsz, seq, D_ATT), cum.reshape(bsz, seq, LANES), ck,
                           tq=tq).reshape(n, D_ATT)

        is_moe = layer % 2 == 1
        j = layer // 2
        rw = _pad_lanes(router_w[j].astype(F32)) if is_moe else None
        outs = _outproj(x2, y_ssm, u, y_att, ygm, ssm_d[layer].reshape(1, D_SSM).astype(F32),
                        ssm_w_glu[layer].astype(BF16), group_norm_g[layer].reshape(1, D_MODEL),
                        w_out[layer].astype(BF16), ffn_norm_g[layer].reshape(1, D_MODEL), rw, tm=tm)
        fin = final_norm_g.reshape(1, D_MODEL) if layer == depth - 1 else None
        if is_moe:
            x_mid, h, route, counts = outs
            dffe = moe_w_gate.shape[-1]
            tme = min(MOE_ROW_TILE, 2 * n)
            dest, tile_expert, n_used, rows = _moe_layout(route, counts, tm=tm, tme=tme)
            xs = _moe_dispatch(dest, h, jnp.zeros((rows, D_MODEL), F32), tm=tm)
            ys = _moe_ffn(tile_expert, n_used, xs, moe_w_gate[j].astype(BF16),
                          moe_w_up[j].astype(BF16), moe_w_down[j].astype(BF16), tme=tme,
                          tf=_largest_tile(dffe, 1024, LANES))
            x2 = _moe_combine(dest, x_mid, route, ys, fin, tm=tm)
        else:
            x_mid, h = outs
            dff = dense_w_gate.shape[-1]
            x2 = _dense_ffn(h, x_mid, dense_w_gate[j].astype(BF16), dense_w_up[j].astype(BF16),
                            dense_w_down[j].astype(BF16), fin, tm=tm,
                            tf=_largest_tile(dff, 1536, LANES))
    return x2.reshape(bsz, seq, D_MODEL).astype(x.dtype)
```

```python
import functools
import math

import jax
import jax.numpy as jnp
from jax import lax
from jax.experimental import pallas as pl
from jax.experimental.pallas import tpu as pltpu

F32 = jnp.float32
BF16 = jnp.bfloat16

D_MODEL = 1024
D_SSM = 256
D_ATT = 512
D_GM = 256
SSM_GROUP = 16
SSM_GROUPS = 16
SSM_STATE = 64
ATT_HEADS = 8
ATT_HEAD_DIM = 64
GM_HEADS = 4
GM_HEAD_DIM = 64
CHUNK = 128
N_EXPERTS = 8
EPS = 1e-6
LOG2E = math.log2(math.e)

LANES = 128
SUBLANES = 8
MOE_ROW_TILE = 512
ATT_STEP_HEADS = 2
SSM_T = 8
SSM_ROW = SSM_T * D_SSM
SSM_NSTATE = SSM_GROUPS * SSM_STATE

C_U = 0
C_Q = C_U + D_SSM
C_K = C_Q + D_ATT
C_V = C_K + D_ATT
C_Z = C_V + D_ATT
C_F = C_Z + 2 * D_GM
D_IN_PAD = C_F + LANES

VMEM_LIMIT = 56 * 1024 * 1024


def _cparams(*sem):
    return pltpu.CompilerParams(dimension_semantics=sem, vmem_limit_bytes=VMEM_LIMIT)


def _resident(shape):
    nd = len(shape)
    return pl.BlockSpec(shape, lambda *_: (0,) * nd, pipeline_mode=pl.Buffered(1))


def _rms(x, g):
    return x * lax.rsqrt(jnp.mean(x * x, axis=-1, keepdims=True) + EPS) * g


def _gelu(x):
    c = math.sqrt(2.0 / math.pi)
    return 0.5 * x * (1.0 + jnp.tanh(c * (x + 0.044715 * (x * x * x))))


def _sigmoid(x):
    return 1.0 / (1.0 + jnp.exp(-x))


def _split3(x):
    p1 = x.astype(BF16)
    r1 = x - p1.astype(F32)
    p2 = r1.astype(BF16)
    r2 = r1 - p2.astype(F32)
    return p1, p2, r2.astype(BF16)


def _dot(a, b):
    return jnp.dot(a, b, preferred_element_type=F32)


def _inproj_body(x_ref, g_ref, w_ref, bf_ref, lng_ref, lnb_ref, ws_ref, bs_ref,
                 u_ref, q_ref, k_ref, v_ref, cum_ref, ygm_ref, carry_ref, *, tiles_per_seq):
    i = pl.program_id(0)

    @pl.when(i % tiles_per_seq == 0)
    def _():
        carry_ref[...] = jnp.zeros_like(carry_ref)

    tm = x_ref.shape[0]
    hb = _rms(x_ref[...], g_ref[...]).astype(BF16)

    def proj(lo, width):
        return _dot(hb, w_ref[:, lo:lo + width])

    u_ref[...] = proj(C_U, D_SSM)
    q_ref[...] = proj(C_Q, D_ATT).astype(BF16)
    k_ref[...] = proj(C_K, D_ATT).astype(BF16)
    v_ref[...] = proj(C_V, D_ATT).astype(BF16)

    f = proj(C_F, LANES) + bf_ref[...]
    logf = (jnp.minimum(f, 0.0) - jnp.log(1.0 + jnp.exp(-jnp.abs(f)))) * LOG2E
    row = lax.broadcasted_iota(jnp.int32, (tm, tm), 0)
    col = lax.broadcasted_iota(jnp.int32, (tm, tm), 1)
    tri = (col <= row).astype(BF16)
    p1, p2, p3 = _split3(logf)
    cum = _dot(tri, p1) + _dot(tri, p2) + _dot(tri, p3) + carry_ref[...]
    cum_ref[...] = cum
    carry_ref[...] = cum[tm - 1:tm, :]

    zg = _gelu(proj(C_Z, 2 * D_GM))
    ug = zg[:, :D_GM]
    vg = zg[:, D_GM:]
    mean = jnp.mean(vg, axis=-1, keepdims=True)
    cen = vg - mean
    var = jnp.mean(cen * cen, axis=-1, keepdims=True)
    vn = (cen * lax.rsqrt(var + EPS) * lng_ref[...] + lnb_ref[...]).astype(BF16)
    lane_head = lax.broadcasted_iota(jnp.int32, (CHUNK, D_GM), 1) >> 6
    for c in range(tm // CHUNK):
        rows = slice(c * CHUNK, (c + 1) * CHUNK)
        vc = vn[rows, :]
        mixed = _dot(ws_ref[0], vc)
        for g in range(1, GM_HEADS):
            mixed = jnp.where(lane_head == g, _dot(ws_ref[g], vc), mixed)
        mixed = mixed + bs_ref[...]
        ygm_ref[rows, :] = (ug[rows, :] * mixed).astype(BF16)


def _inproj(x2, g, w, bf, lng, lnb, ws, bs, *, seq, tm):
    n = x2.shape[0]
    row = lambda width: pl.BlockSpec((tm, width), lambda i: (i, 0))
    return pl.pallas_call(
        functools.partial(_inproj_body, tiles_per_seq=seq // tm),
        grid=(n // tm,),
        in_specs=[row(D_MODEL), _resident((1, D_MODEL)), _resident((D_MODEL, D_IN_PAD)),
                  _resident((1, LANES)), _resident((1, D_GM)), _resident((1, D_GM)),
                  _resident((GM_HEADS, CHUNK, CHUNK)), _resident((CHUNK, D_GM))],
        out_specs=[row(D_SSM), row(D_ATT), row(D_ATT), row(D_ATT), row(LANES), row(D_GM)],
        out_shape=[jax.ShapeDtypeStruct((n, D_SSM), F32),
                   jax.ShapeDtypeStruct((n, D_ATT), BF16),
                   jax.ShapeDtypeStruct((n, D_ATT), BF16),
                   jax.ShapeDtypeStruct((n, D_ATT), BF16),
                   jax.ShapeDtypeStruct((n, LANES), F32),
                   jax.ShapeDtypeStruct((n, D_GM), BF16)],
        scratch_shapes=[pltpu.VMEM((1, LANES), F32)],
        compiler_params=_cparams("arbitrary"),
        name="inproj",
    )(x2, g, w, bf, lng, lnb, ws, bs)


def _ssm_state_body(u_ref, f_ref, s_ref):
    s_ref[...] = _dot(u_ref[...].astype(BF16), f_ref[...])


def _ssm_state(u2, fmat, *, tr):
    nc = u2.shape[0]
    return pl.pallas_call(
        _ssm_state_body,
        grid=(nc // tr,),
        in_specs=[pl.BlockSpec((tr, SSM_ROW), lambda i: (i, 0)),
                  _resident((SSM_ROW, 2 * SSM_NSTATE))],
        out_specs=pl.BlockSpec((tr, 2 * SSM_NSTATE), lambda i: (i, 0)),
        out_shape=jax.ShapeDtypeStruct((nc, 2 * SSM_NSTATE), F32),
        compiler_params=_cparams("parallel"),
        name="ssm_state",
    )(u2, fmat)


def _ssm_scan_body(s_ref, a_ref, xp_ref, st_ref):
    @pl.when(pl.program_id(1) == 0)
    def _():
        st_ref[...] = jnp.zeros_like(st_ref)

    tc = s_ref.shape[0]
    are = a_ref[:, :SSM_NSTATE]
    aim = a_ref[:, SSM_NSTATE:]

    def step(c8, carry):
        xre, xim = carry
        base = pl.multiple_of(c8 * SUBLANES, SUBLANES)
        blk = s_ref[pl.ds(base, SUBLANES), :]
        prev_re, prev_im = [], []
        for r in range(SUBLANES):
            prev_re.append(xre)
            prev_im.append(xim)
            sre = blk[r:r + 1, :SSM_NSTATE]
            sim = blk[r:r + 1, SSM_NSTATE:]
            xre, xim = are * xre - aim * xim + sre, are * xim + aim * xre + sim
        xp_ref[pl.ds(base, SUBLANES), :SSM_NSTATE] = jnp.concatenate(prev_re, axis=0)
        xp_ref[pl.ds(base, SUBLANES), SSM_NSTATE:] = jnp.concatenate(prev_im, axis=0)
        return xre, xim

    xre, xim = lax.fori_loop(0, tc // SUBLANES, step,
                             (st_ref[:, :SSM_NSTATE], st_ref[:, SSM_NSTATE:]))
    st_ref[:, :SSM_NSTATE] = xre
    st_ref[:, SSM_NSTATE:] = xim


def _ssm_scan(s3, a, *, tc):
    b, ncb, w = s3.shape
    spec = pl.BlockSpec((None, tc, w), lambda bi, ci: (bi, ci, 0))
    return pl.pallas_call(
        _ssm_scan_body,
        grid=(b, ncb // tc),
        in_specs=[spec, pl.BlockSpec((1, w), lambda bi, ci: (0, 0))],
        out_specs=spec,
        out_shape=jax.ShapeDtypeStruct(s3.shape, F32),
        scratch_shapes=[pltpu.VMEM((1, w), F32)],
        compiler_params=_cparams("parallel", "arbitrary"),
        name="ssm_scan",
    )(s3, a)


def _ssm_out_body(u_ref, xp_ref, m_ref, e_ref, y_ref):
    y_ref[...] = (_dot(u_ref[...].astype(BF16), m_ref[...])
                  + _dot(xp_ref[...].astype(BF16), e_ref[...]))


def _ssm_out(u2, xp, mmat, emat, *, tr):
    nc = u2.shape[0]
    return pl.pallas_call(
        _ssm_out_body,
        grid=(nc // tr,),
        in_specs=[pl.BlockSpec((tr, SSM_ROW), lambda i: (i, 0)),
                  pl.BlockSpec((tr, 2 * SSM_NSTATE), lambda i: (i, 0)),
                  _resident((SSM_ROW, SSM_ROW)), _resident((2 * SSM_NSTATE, SSM_ROW))],
        out_specs=pl.BlockSpec((tr, SSM_ROW), lambda i: (i, 0)),
        out_shape=jax.ShapeDtypeStruct((nc, SSM_ROW), F32),
        compiler_params=_cparams("parallel"),
        name="ssm_out",
    )(u2, xp, mmat, emat)


def _ssm_matrices(lam_re, lam_im, log_dt, b_re, b_im, c_re, c_im):
    t, g, p, h = SSM_T, SSM_GROUPS, SSM_STATE, SSM_GROUP
    ns, nch = g * p, g * h
    hp = lax.Precision.HIGHEST
    lr, li = lam_re.astype(F32).reshape(ns), lam_im.astype(F32).reshape(ns)
    dt = jnp.repeat(jnp.exp(log_dt.astype(F32)), p)
    steps = jnp.arange(t + 1, dtype=F32)[:, None]
    mag = jnp.exp((lr * dt)[None] * steps)
    ang = (li * dt)[None] * steps
    pr, pi = mag * jnp.cos(ang), mag * jnp.sin(ang)
    nr, ni = pr[1] - 1.0, pi[1]
    den = lr * lr + li * li
    zr, zi = ((nr * lr + ni * li) / den)[:, None], ((ni * lr - nr * li) / den)[:, None]
    br, bi = b_re.astype(F32).reshape(ns, h), b_im.astype(F32).reshape(ns, h)
    cr = c_re.astype(F32).transpose(0, 2, 1).reshape(ns, h)
    ci = c_im.astype(F32).transpose(0, 2, 1).reshape(ns, h)
    same = jnp.repeat(jnp.repeat(jnp.eye(g, dtype=F32), p, axis=0), h, axis=1)
    spread = lambda w: jnp.tile(w, (1, g)) * same
    bhr, bhi = spread(zr * br - zi * bi), spread(zr * bi + zi * br)
    chr_, chi = spread(cr), spread(ci)
    er = jnp.concatenate([pr[k][:, None] * chr_ - pi[k][:, None] * chi for k in range(t + 1)], axis=1)
    ei = jnp.concatenate([pr[k][:, None] * chi + pi[k][:, None] * chr_ for k in range(t + 1)], axis=1)

    k_all = (jnp.dot(bhr.T, er[:, :t * nch], precision=hp)
             - jnp.dot(bhi.T, ei[:, :t * nch], precision=hp))
    mmat = jnp.concatenate(
        [jnp.pad(k_all[:, :(t - s) * nch], ((0, 0), (s * nch, 0))) for s in range(t)], axis=0)

    fmat = jnp.concatenate(
        [jnp.concatenate([bhr.T * pr[t - 1 - s][None] - bhi.T * pi[t - 1 - s][None],
                          bhr.T * pi[t - 1 - s][None] + bhi.T * pr[t - 1 - s][None]], axis=1)
         for s in range(t)], axis=0)

    emat = jnp.concatenate([er[:, nch:], -ei[:, nch:]], axis=0)

    a_row = jnp.concatenate([pr[t][None], pi[t][None]], axis=1)
    return mmat.astype(BF16), fmat.astype(BF16), emat.astype(BF16), a_row


def _attn_body(q_ref, k_ref, v_ref, cq_ref, ck_ref, o_ref,
               m_ref, acc_ref, s_ref, mt_ref, p_ref, al_ref, *, tq):
    hg = pl.program_id(1)
    qi = pl.program_id(2)
    nh = ATT_STEP_HEADS
    nchunk = tq // LANES
    lane = lax.broadcasted_iota(jnp.int32, (tq, LANES), 1)
    upper = lane >= ATT_HEAD_DIM

    def chunk(c):
        return slice(c * LANES, (c + 1) * LANES)

    def own(j):
        return upper if j % 2 else ~upper

    cq_all = cq_ref[...]
    qs = [jnp.where(own(j), q_ref[:, chunk(j // 2)], jnp.zeros((tq, LANES), BF16)) for j in range(nh)]
    cqs = [jnp.broadcast_to(
        jnp.sum(jnp.where(lane == nh * hg + j, cq_all, 0.0), axis=-1, keepdims=True), (tq, LANES))
        for j in range(nh)]
    rowpos = lax.broadcasted_iota(jnp.int32, (tq, LANES), 0)

    def scores(t, slot, masked):
        start = pl.multiple_of(t * tq, tq)
        for j in range(nh):
            kt = k_ref[pl.ds(start, tq), chunk(j // 2)]
            s = lax.dot_general(qs[j], kt, (((1,), (1,)), ((), ())), preferred_element_type=F32)
            ck = ck_ref[j // 2, j % 2:j % 2 + 1, pl.ds(start, tq)]
            mt = None
            for c in range(nchunk):
                sc = s[:, chunk(c)] - ck[:, chunk(c)]
                if masked:
                    sc = jnp.where(lane + c * LANES <= rowpos, sc, -jnp.inf)
                s_ref[slot, j, :, chunk(c)] = sc
                mt = sc if mt is None else jnp.maximum(mt, sc)
            mt_ref[slot, j] = mt

    def probs(slot):
        for j in range(nh):
            m = m_ref[j]
            mt = jnp.broadcast_to(jnp.max(mt_ref[slot, j], axis=-1, keepdims=True), (tq, LANES))
            m_new = jnp.maximum(m, mt + cqs[j])
            shift = cqs[j] - m_new
            for c in range(nchunk):
                p_ref[slot, j, :, chunk(c)] = jnp.exp2(s_ref[slot, j, :, chunk(c)] + shift).astype(BF16)
            al_ref[slot, j] = jnp.exp2(m - m_new)
            m_ref[j] = m_new

    def values(t, slot):
        start = pl.multiple_of(t * tq, tq)
        for j in range(nh):
            vt = v_ref[pl.ds(start, tq), chunk(j // 2)]
            vj = jnp.where(own(j), vt, jnp.ones_like(vt))
            acc_ref[j] = al_ref[slot, j] * acc_ref[j] + _dot(p_ref[slot, j], vj)

    def trip(t, slot, masked=False):
        values(t - 1, 1 - slot)
        probs(slot)
        scores(t + 1, 1 - slot, masked)

    for j in range(nh):
        m_ref[j] = jnp.full((tq, LANES), -jnp.inf, F32)
        acc_ref[j] = jnp.zeros((tq, LANES), F32)

    @pl.when(qi == 0)
    def _():
        scores(0, 0, True)
        probs(0)
        values(0, 0)

    @pl.when(qi == 1)
    def _():
        scores(0, 0, False)
        probs(0)
        scores(1, 1, True)
        values(0, 0)
        probs(1)
        values(1, 1)

    @pl.when(qi >= 2)
    def _():
        scores(0, 0, False)
        probs(0)
        scores(1, 1, False)

        def pair(u, carry):
            t = 2 * u + 1
            trip(t, 1)
            trip(t + 1, 0)
            return carry

        lax.fori_loop(0, (qi - 2) // 2, pair, 0)

        @pl.when(qi % 2 == 0)
        def _():
            trip(qi - 1, 1, masked=True)
            values(qi - 1, 1)
            probs(0)
            values(qi, 0)

        @pl.when(qi % 2 == 1)
        def _():
            trip(qi - 2, 1)
            trip(qi - 1, 0, masked=True)
            values(qi - 1, 0)
            probs(1)
            values(qi, 1)
    for pair in range(nh // 2):
        outs = []
        for j in (2 * pair, 2 * pair + 1):
            acc = acc_ref[j]
            outs.append(acc / pltpu.roll(acc, ATT_HEAD_DIM, axis=1))
        o_ref[:, chunk(pair)] = jnp.where(upper, outs[1], outs[0]).astype(BF16)


def _attention(q3, k3, v3, cq3, ck4, *, tq):
    b, s, _ = q3.shape
    nh = ATT_STEP_HEADS
    width = nh * ATT_HEAD_DIM
    return pl.pallas_call(
        functools.partial(_attn_body, tq=tq),
        scratch_shapes=[pltpu.VMEM((nh, tq, LANES), F32), pltpu.VMEM((nh, tq, LANES), F32),
                        pltpu.VMEM((2, nh, tq, tq), F32), pltpu.VMEM((2, nh, tq, LANES), F32),
                        pltpu.VMEM((2, nh, tq, tq), BF16), pltpu.VMEM((2, nh, tq, LANES), F32)],
        grid=(b, ATT_HEADS // nh, s // tq),
        in_specs=[pl.BlockSpec((None, tq, width), lambda bi, hg, qi: (bi, qi, hg)),
                  pl.BlockSpec((None, s, width), lambda bi, hg, qi: (bi, 0, hg)),
                  pl.BlockSpec((None, s, width), lambda bi, hg, qi: (bi, 0, hg)),
                  pl.BlockSpec((None, tq, LANES), lambda bi, hg, qi: (bi, qi, 0)),
                  pl.BlockSpec((None, nh // 2, 2, s), lambda bi, hg, qi: (bi, hg, 0, 0))],
        out_specs=pl.BlockSpec((None, tq, width), lambda bi, hg, qi: (bi, qi, hg)),
        out_shape=jax.ShapeDtypeStruct((b, s, D_ATT), BF16),
        compiler_params=_cparams("parallel", "parallel", "arbitrary"),
        name="attention",
    )(q3, k3, v3, cq3, ck4)


R_E1, R_E2, R_W1, R_W2, R_RANK1, R_RANK2 = range(6)


def _route(logits, cnt_ref):
    tm = logits.shape[0]
    lane = lax.broadcasted_iota(jnp.int32, logits.shape, 1).astype(F32)
    lg = jnp.where(lane < N_EXPERTS, logits, -jnp.inf)
    m1 = jnp.max(lg, axis=-1, keepdims=True)
    i1 = jnp.min(jnp.where(lg == m1, lane, float(LANES)), axis=-1, keepdims=True)
    lg2 = jnp.where(lane == i1, -jnp.inf, lg)
    m2 = jnp.max(lg2, axis=-1, keepdims=True)
    i2 = jnp.min(jnp.where(lg2 == m2, lane, float(LANES)), axis=-1, keepdims=True)
    e = jnp.exp(m2 - m1)
    w1 = 1.0 / (1.0 + e)
    w2 = e / (1.0 + e)

    hit1, hit2 = lane == i1, lane == i2
    oh1, oh2 = hit1.astype(BF16), hit2.astype(BF16)
    row = lax.broadcasted_iota(jnp.int32, (tm, tm), 0)
    col = lax.broadcasted_iota(jnp.int32, (tm, tm), 1)
    before = (col < row).astype(BF16)
    carry = cnt_ref[...]
    tot1 = jnp.sum(oh1.astype(F32), axis=0, keepdims=True)
    tot2 = jnp.sum(oh2.astype(F32), axis=0, keepdims=True)
    pos1 = _dot(before, oh1) + carry
    pos2 = _dot(before, oh2) + (carry + tot1)
    rank1 = jnp.sum(jnp.where(hit1, pos1, 0.0), axis=-1, keepdims=True)
    rank2 = jnp.sum(jnp.where(hit2, pos2, 0.0), axis=-1, keepdims=True)
    cnt_ref[...] = carry + tot1 + tot2

    rec = jnp.zeros(logits.shape, F32)
    for idx, val in ((R_E1, i1), (R_E2, i2), (R_W1, w1), (R_W2, w2), (R_RANK1, rank1), (R_RANK2, rank2)):
        rec = jnp.where(lane == idx, val, rec)
    return rec


def _outproj_body(*refs, with_router):
    if with_router:
        (x_ref, ys_ref, us_ref, ya_ref, yg_ref, d_ref, wglu_ref, gn_ref, w_ref, fg_ref, rw_ref,
         xo_ref, h_ref, route_ref, cnt_ref) = refs

        @pl.when(pl.program_id(0) == 0)
        def _():
            cnt_ref[...] = jnp.zeros_like(cnt_ref)
    else:
        (x_ref, ys_ref, us_ref, ya_ref, yg_ref, d_ref, wglu_ref, gn_ref, w_ref, fg_ref,
         xo_ref, h_ref) = refs

    y1 = _gelu(ys_ref[...] + d_ref[...] * us_ref[...])
    y_ssm = y1 * _sigmoid(_dot(y1.astype(BF16), wglu_ref[...]))

    o1, o2 = D_SSM, D_SSM + D_ATT
    acc = _dot(_rms(y_ssm, gn_ref[:, :o1]).astype(BF16), w_ref[:o1, :])
    acc += _dot(_rms(ya_ref[...].astype(F32), gn_ref[:, o1:o2]).astype(BF16), w_ref[o1:o2, :])
    acc += _dot(_rms(yg_ref[...].astype(F32), gn_ref[:, o2:]).astype(BF16), w_ref[o2:, :])
    xn = x_ref[...] + acc
    xo_ref[...] = xn
    hn = _rms(xn, fg_ref[...])
    h_ref[...] = hn.astype(h_ref.dtype)
    if with_router:
        h1, h2, h3 = _split3(hn)
        r1, r2, r3 = _split3(rw_ref[...])
        logits = (_dot(h1, r1) + _dot(h1, r2) + _dot(h2, r1)
                  + _dot(h1, r3) + _dot(h2, r2) + _dot(h3, r1))
        route_ref[...] = _route(logits, cnt_ref)


def _outproj(x2, ys, us, ya, yg, d, wglu, gn, w, fg, rw, *, tm):
    n = x2.shape[0]
    with_router = rw is not None
    row = lambda width: pl.BlockSpec((tm, width), lambda i: (i, 0))
    in_specs = [row(D_MODEL), row(D_SSM), row(D_SSM), row(D_ATT), row(D_GM),
                _resident((1, D_SSM)), _resident((D_SSM, D_SSM)), _resident((1, D_MODEL)),
                _resident((D_MODEL, D_MODEL)), _resident((1, D_MODEL))]
    out_specs = [row(D_MODEL), row(D_MODEL)]
    out_shape = [jax.ShapeDtypeStruct((n, D_MODEL), F32),
                 jax.ShapeDtypeStruct((n, D_MODEL), F32 if with_router else BF16)]
    args = [x2, ys, us, ya, yg, d, wglu, gn, w, fg]
    if with_router:
        in_specs.append(_resident((D_MODEL, LANES)))
        out_specs += [row(LANES), pl.BlockSpec((1, LANES), lambda i: (0, 0))]
        out_shape += [jax.ShapeDtypeStruct((n, LANES), F32), jax.ShapeDtypeStruct((1, LANES), F32)]
        args.append(rw)
    return pl.pallas_call(
        functools.partial(_outproj_body, with_router=with_router),
        grid=(n // tm,),
        in_specs=in_specs, out_specs=out_specs, out_shape=out_shape,
        compiler_params=_cparams("arbitrary" if with_router else "parallel"),
        name="outproj_router" if with_router else "outproj",
    )(*args)


def _swiglu_tile(h, wg, wu, wd):
    a = _dot(h, wg)
    return _dot((a * _sigmoid(a) * _dot(h, wu)).astype(BF16), wd)


def _finish(x, acc, fin_ref, o_ref):
    xn = x + acc
    o_ref[...] = xn if fin_ref is None else _rms(xn, fin_ref[...])


def _dense_ffn_body(*refs, final_norm):
    if final_norm:
        h_ref, x_ref, wg_ref, wu_ref, wd_ref, fin_ref, o_ref, acc_ref = refs
    else:
        h_ref, x_ref, wg_ref, wu_ref, wd_ref, o_ref, acc_ref = refs
        fin_ref = None
    f = pl.program_id(1)

    @pl.when(f == 0)
    def _():
        acc_ref[...] = jnp.zeros_like(acc_ref)

    acc_ref[...] += _swiglu_tile(h_ref[...], wg_ref[...], wu_ref[...], wd_ref[...])

    @pl.when(f == pl.num_programs(1) - 1)
    def _():
        _finish(x_ref[...], acc_ref[...], fin_ref, o_ref)


def _dense_ffn(h, x2, wg, wu, wd, fin, *, tm, tf):
    n = x2.shape[0]
    dff = wg.shape[1]
    row = lambda: pl.BlockSpec((tm, D_MODEL), lambda i, f: (i, 0))
    in_specs = [row(), row(),
                pl.BlockSpec((D_MODEL, tf), lambda i, f: (0, f)),
                pl.BlockSpec((D_MODEL, tf), lambda i, f: (0, f)),
                pl.BlockSpec((tf, D_MODEL), lambda i, f: (f, 0))]
    args = [h, x2, wg, wu, wd]
    if fin is not None:
        in_specs.append(pl.BlockSpec((1, D_MODEL), lambda i, f: (0, 0)))
        args.append(fin)
    return pl.pallas_call(
        functools.partial(_dense_ffn_body, final_norm=fin is not None),
        grid=(n // tm, dff // tf),
        in_specs=in_specs, out_specs=row(),
        out_shape=jax.ShapeDtypeStruct((n, D_MODEL), F32),
        scratch_shapes=[pltpu.VMEM((tm, D_MODEL), F32)],
        compiler_params=_cparams("parallel", "arbitrary"),
        name="dense_ffn",
    )(*args)


def _row_copy(src, src_row, dst, dst_row, sem):
    return pltpu.make_async_copy(src.at[pl.ds(src_row, 1)], dst.at[pl.ds(dst_row, 1)], sem)


def _moe_dispatch_body(dest_ref, h_ref, xs_in_ref, xs_ref, sem):
    del xs_in_ref
    tm = h_ref.shape[0]

    def issue(r, carry):
        for k in range(2):
            _row_copy(h_ref, r, xs_ref, dest_ref[k, r], sem).start()
        return carry

    lax.fori_loop(0, tm, issue, 0, unroll=8)
    for k in range(2):
        pltpu.make_async_copy(h_ref, xs_ref.at[pl.ds(0, tm)], sem).wait()


def _moe_dispatch(dest, h, xs_zero, *, tm):
    n = h.shape[0]
    return pl.pallas_call(
        _moe_dispatch_body,
        grid=(n // tm,),
        in_specs=[pl.BlockSpec((None, 2, tm), lambda i: (i, 0, 0), memory_space=pltpu.SMEM),
                  pl.BlockSpec((tm, D_MODEL), lambda i: (i, 0)),
                  pl.BlockSpec(memory_space=pl.ANY)],
        out_specs=pl.BlockSpec(memory_space=pl.ANY),
        out_shape=jax.ShapeDtypeStruct(xs_zero.shape, xs_zero.dtype),
        scratch_shapes=[pltpu.SemaphoreType.DMA(())],
        input_output_aliases={2: 0},
        compiler_params=pltpu.CompilerParams(dimension_semantics=("arbitrary",),
                                             vmem_limit_bytes=VMEM_LIMIT, has_side_effects=True),
        name="moe_dispatch",
    )(dest, h, xs_zero)


def _moe_ffn_body(te_ref, nu_ref, xs_ref, wg_ref, wu_ref, wd_ref, ys_ref, acc_ref):
    del te_ref
    i = pl.program_id(0)
    f = pl.program_id(1)

    @pl.when(i < nu_ref[0])
    def _():
        @pl.when(f == 0)
        def _():
            acc_ref[...] = jnp.zeros_like(acc_ref)

        acc_ref[...] += _swiglu_tile(xs_ref[...].astype(BF16), wg_ref[...], wu_ref[...], wd_ref[...])

        @pl.when(f == pl.num_programs(1) - 1)
        def _():
            ys_ref[...] = acc_ref[...]

    @pl.when((i >= nu_ref[0]) & (f == pl.num_programs(1) - 1))
    def _():
        ys_ref[...] = jnp.zeros_like(ys_ref)


def _moe_ffn(tile_expert, n_used, xs, wg, wu, wd, *, tme, tf):
    rows = xs.shape[0]
    dff = wg.shape[-1]
    nf = dff // tf
    row_idx = lambda i, f, te, nu: (jnp.minimum(i, nu[0] - 1), 0)
    col = lambda i, f, nu: jnp.where(i < nu[0], f, nf - 1)
    grid_spec = pltpu.PrefetchScalarGridSpec(
        num_scalar_prefetch=2,
        grid=(rows // tme, nf),
        in_specs=[pl.BlockSpec((tme, D_MODEL), row_idx),
                  pl.BlockSpec((None, D_MODEL, tf), lambda i, f, te, nu: (te[i], 0, col(i, f, nu))),
                  pl.BlockSpec((None, D_MODEL, tf), lambda i, f, te, nu: (te[i], 0, col(i, f, nu))),
                  pl.BlockSpec((None, tf, D_MODEL), lambda i, f, te, nu: (te[i], col(i, f, nu), 0))],
        out_specs=pl.BlockSpec((tme, D_MODEL), lambda i, f, te, nu: (i, 0)),
        scratch_shapes=[pltpu.VMEM((tme, D_MODEL), F32)])
    return pl.pallas_call(
        _moe_ffn_body,
        grid_spec=grid_spec,
        out_shape=jax.ShapeDtypeStruct((rows, D_MODEL), F32),
        compiler_params=_cparams("arbitrary", "arbitrary"),
        name="moe_ffn",
    )(tile_expert, n_used, xs, wg, wu, wd)


def _moe_combine_body(*refs, final_norm):
    if final_norm:
        dest_ref, x_ref, route_ref, ys_ref, fin_ref, o_ref, buf_ref, sem = refs
    else:
        dest_ref, x_ref, route_ref, ys_ref, o_ref, buf_ref, sem = refs
        fin_ref = None
    tm = x_ref.shape[0]

    def issue(r, carry):
        for k in range(2):
            _row_copy(ys_ref, dest_ref[k, r], buf_ref.at[k], r, sem).start()
        return carry

    lax.fori_loop(0, tm, issue, 0, unroll=8)
    for k in range(2):
        pltpu.make_async_copy(ys_ref.at[pl.ds(0, tm)], buf_ref.at[k], sem).wait()

    route = route_ref[...]
    lane = lax.broadcasted_iota(jnp.int32, route.shape, 1)
    w1 = jnp.sum(jnp.where(lane == R_W1, route, 0.0), axis=-1, keepdims=True)
    w2 = jnp.sum(jnp.where(lane == R_W2, route, 0.0), axis=-1, keepdims=True)
    _finish(x_ref[...], w1 * buf_ref[0] + w2 * buf_ref[1], fin_ref, o_ref)


def _moe_combine(dest, x2, route, ys, fin, *, tm):
    n = x2.shape[0]
    row = lambda width: pl.BlockSpec((tm, width), lambda i: (i, 0))
    in_specs = [pl.BlockSpec((None, 2, tm), lambda i: (i, 0, 0), memory_space=pltpu.SMEM),
                row(D_MODEL), row(LANES), pl.BlockSpec(memory_space=pl.ANY)]
    args = [dest, x2, route, ys]
    if fin is not None:
        in_specs.append(pl.BlockSpec((1, D_MODEL), lambda i: (0, 0)))
        args.append(fin)
    return pl.pallas_call(
        functools.partial(_moe_combine_body, final_norm=fin is not None),
        grid=(n // tm,),
        in_specs=in_specs, out_specs=row(D_MODEL),
        out_shape=jax.ShapeDtypeStruct((n, D_MODEL), F32),
        scratch_shapes=[pltpu.VMEM((2, tm, D_MODEL), F32), pltpu.SemaphoreType.DMA(())],
        compiler_params=_cparams("arbitrary"),
        name="moe_combine",
    )(*args)


def _moe_layout(route, counts, *, tm, tme):
    n = route.shape[0]
    cnt = counts[0, :N_EXPERTS].astype(jnp.int32)
    padded = (cnt + (tme - 1)) // tme * tme
    ends = jnp.cumsum(padded)
    starts = ends - padded
    experts = jnp.arange(N_EXPERTS, dtype=F32)

    def rows(e_lane, rank_lane):
        start = jnp.sum(jnp.where(route[:, e_lane, None] == experts[None], starts[None], 0), axis=1)
        return start + route[:, rank_lane].astype(jnp.int32)

    dest = jnp.stack([rows(R_E1, R_RANK1), rows(R_E2, R_RANK2)])
    dest = dest.reshape(2, n // tm, tm).transpose(1, 0, 2)
    n_tiles = (2 * n) // tme + N_EXPERTS
    tile_start = jnp.arange(n_tiles, dtype=jnp.int32) * tme
    tile_expert = jnp.minimum(jnp.sum(tile_start[:, None] >= ends[None, :], axis=1), N_EXPERTS - 1)
    n_used = (ends[-1] // tme).reshape(1)
    tile_expert = jnp.where(tile_start < ends[-1], tile_expert, tile_expert[jnp.maximum(n_used[0] - 1, 0)])
    return dest, tile_expert.astype(jnp.int32), n_used.astype(jnp.int32), n_tiles * tme


def _largest_tile(total, cap, mult):
    best = mult
    for t in range(mult, min(total, cap) + 1, mult):
        if total % t == 0:
            best = t
    return best


def _pad_lanes(a):
    return jnp.pad(a, [(0, 0)] * (a.ndim - 1) + [(0, LANES - a.shape[-1])])


def kernel(x, mix_norm_g, w_in, b_forget, ssm_lambda_re, ssm_lambda_im, ssm_log_dt, ssm_b_re, ssm_b_im, ssm_c_re, ssm_c_im, ssm_d, ssm_w_glu, gm_ln_g, gm_ln_b, gm_w_s, gm_b_s, group_norm_g, w_out, ffn_norm_g, dense_w_gate, dense_w_up, dense_w_down, router_w, moe_w_gate, moe_w_up, moe_w_down, final_norm_g):
    bsz, seq, _ = x.shape
    n = bsz * seq
    depth = w_in.shape[0]
    assert seq % CHUNK == 0 and seq % SSM_T == 0
    tm = _largest_tile(seq, 512, CHUNK)
    tq = _largest_tile(seq, 512, LANES)
    nc = n // SSM_T
    ncb = seq // SSM_T
    tr = _largest_tile(nc, 512, 8)
    tc = _largest_tile(ncb, 128, 8)

    x2 = x.reshape(n, D_MODEL).astype(F32)
    tril = jnp.tril(jnp.ones((CHUNK, CHUNK), F32))
    i0, i1, i2, i3, i4 = (D_SSM, D_SSM + D_ATT, D_SSM + 2 * D_ATT, D_SSM + 3 * D_ATT,
                          D_SSM + 3 * D_ATT + ATT_HEADS)
    for layer in range(depth):
        wl = w_in[layer]
        w_re = jnp.concatenate(
            [wl[:, :i0], wl[:, i0:i1] * (ATT_HEAD_DIM ** -0.5 * LOG2E), wl[:, i1:i3], wl[:, i4:],
             _pad_lanes(wl[:, i3:i4])], axis=1).astype(BF16)
        ws = (gm_w_s[layer].astype(F32) * tril[None]).astype(BF16)
        bs = jnp.repeat(gm_b_s[layer].astype(F32).T, GM_HEAD_DIM, axis=1)
        u, q, k, v, cum, ygm = _inproj(
            x2, mix_norm_g[layer].reshape(1, D_MODEL), w_re,
            _pad_lanes(b_forget[layer].reshape(1, ATT_HEADS).astype(F32)),
            gm_ln_g[layer].reshape(1, D_GM), gm_ln_b[layer].reshape(1, D_GM), ws, bs,
            seq=seq, tm=tm)

        mmat, fmat, emat, a_row = _ssm_matrices(
            ssm_lambda_re[layer], ssm_lambda_im[layer], ssm_log_dt[layer], ssm_b_re[layer],
            ssm_b_im[layer], ssm_c_re[layer], ssm_c_im[layer])
        u2 = u.reshape(nc, SSM_ROW)
        s_loc = _ssm_state(u2, fmat, tr=tr)
        xprev = _ssm_scan(s_loc.reshape(bsz, ncb, 2 * SSM_NSTATE), a_row, tc=tc)
        y_ssm = _ssm_out(u2, xprev.reshape(nc, 2 * SSM_NSTATE), mmat, emat, tr=tr).reshape(n, D_SSM)

        ck = cum[:, :ATT_HEADS].reshape(bsz, seq, ATT_HEADS // 2, 2).transpose(0, 2, 3, 1)
        y_att = _attention(q.reshape(bsz, seq, D_ATT), k.reshape(bsz, seq, D_ATT),
                           v.reshape(bsz, seq, D_ATT), cum.reshape(bsz, seq, LANES), ck,
                           tq=tq).reshape(n, D_ATT)

        is_moe = layer % 2 == 1
        j = layer // 2
        rw = _pad_lanes(router_w[j].astype(F32)) if is_moe else None
        outs = _outproj(x2, y_ssm, u, y_att, ygm, ssm_d[layer].reshape(1, D_SSM).astype(F32),
                        ssm_w_glu[layer].astype(BF16), group_norm_g[layer].reshape(1, D_MODEL),
                        w_out[layer].astype(BF16), ffn_norm_g[layer].reshape(1, D_MODEL), rw, tm=tm)
        fin = final_norm_g.reshape(1, D_MODEL) if layer == depth - 1 else None
        if is_moe:
            x_mid, h, route, counts = outs
            dffe = moe_w_gate.shape[-1]
            tme = min(MOE_ROW_TILE, 2 * n)
            dest, tile_expert, n_used, rows = _moe_layout(route, counts, tm=tm, tme=tme)
            xs = _moe_dispatch(dest, h, jnp.zeros((rows, D_MODEL), F32), tm=tm)
            ys = _moe_ffn(tile_expert, n_used, xs, moe_w_gate[j].astype(BF16),
                          moe_w_up[j].astype(BF16), moe_w_down[j].astype(BF16), tme=tme,
                          tf=_largest_tile(dffe, 1024, LANES))
            x2 = _moe_combine(dest, x_mid, route, ys, fin, tm=tm)
        else:
            x_mid, h = outs
            dff = dense_w_gate.shape[-1]
            x2 = _dense_ffn(h, x_mid, dense_w_gate[j].astype(BF16), dense_w_up[j].astype(BF16),
                            dense_w_down[j].astype(BF16), fin, tm=tm,
                            tf=_largest_tile(dff, 1536, LANES))
    return x2.reshape(bsz, seq, D_MODEL).astype(x.dtype)
```

```python
import functools
import math

import jax
import jax.numpy as jnp
from jax import lax
from jax.experimental import pallas as pl
from jax.experimental.pallas import tpu as pltpu

F32 = jnp.float32
BF16 = jnp.bfloat16

D_MODEL = 1024
D_SSM = 256
D_ATT = 512
D_GM = 256
SSM_GROUP = 16
SSM_GROUPS = 16
SSM_STATE = 64
ATT_HEADS = 8
ATT_HEAD_DIM = 64
GM_HEADS = 4
GM_HEAD_DIM = 64
CHUNK = 128
N_EXPERTS = 8
EPS = 1e-6
LOG2E = math.log2(math.e)

LANES = 128
SUBLANES = 8
MXU_WIDTH = 256
MOE_ROW_TILE = 512
OUTPROJ_PARTS = 2
ATT_STEP_HEADS = 2
SSM_T = 8
SSM_ROW = SSM_T * D_SSM
SSM_NSTATE = SSM_GROUPS * SSM_STATE

C_U = 0
C_Q = C_U + D_SSM
C_K = C_Q + D_ATT
C_V = C_K + D_ATT
C_Z = C_V + D_ATT
C_F = C_Z + 2 * D_GM
D_IN_PAD = C_F + LANES

VMEM_LIMIT = 56 * 1024 * 1024


def _cparams(*sem):
    return pltpu.CompilerParams(dimension_semantics=sem, vmem_limit_bytes=VMEM_LIMIT)


def _resident(shape):
    nd = len(shape)
    return pl.BlockSpec(shape, lambda *_: (0,) * nd, pipeline_mode=pl.Buffered(1))


def _rms(x, g):
    return x * lax.rsqrt(jnp.mean(x * x, axis=-1, keepdims=True) + EPS) * g


def _gelu(x):
    c = math.sqrt(2.0 / math.pi)
    return 0.5 * x * (1.0 + jnp.tanh(c * (x + 0.044715 * (x * x * x))))


def _sigmoid(x):
    return 1.0 / (1.0 + jnp.exp(-x))


def _split3(x):
    p1 = x.astype(BF16)
    r1 = x - p1.astype(F32)
    p2 = r1.astype(BF16)
    r2 = r1 - p2.astype(F32)
    return p1, p2, r2.astype(BF16)


def _dot(a, b):
    return jnp.dot(a, b, preferred_element_type=F32)


def _split_spec(rows):
    return pl.BlockSpec((D_SSM // LANES, rows, LANES), lambda i: (0, i, 0))


def _split_store(ref, value, rows=slice(None)):
    for h in range(D_SSM // LANES):
        ref[h, rows, :] = value[:, h * LANES:(h + 1) * LANES]


def _split_load(ref, rows=slice(None)):
    return jnp.concatenate([ref[h, rows, :] for h in range(D_SSM // LANES)], axis=1)


def _inproj_body(x_ref, g_ref, w_ref, bf_ref, lng_ref, lnb_ref, ws_ref, bs_ref,
                 u_ref, q_ref, k_ref, v_ref, cum_ref, ygm_ref, carry_ref, *, tiles_per_seq):
    i = pl.program_id(0)

    @pl.when(i % tiles_per_seq == 0)
    def _():
        carry_ref[...] = jnp.zeros_like(carry_ref)

    tm = x_ref.shape[0]
    hb = _rms(x_ref[...], g_ref[...]).astype(BF16)

    def proj(lo, width):
        return _dot(hb, w_ref[:, lo:lo + width])

    _split_store(u_ref, proj(C_U, D_SSM))
    q_ref[...] = proj(C_Q, D_ATT).astype(BF16)
    k_ref[...] = proj(C_K, D_ATT).astype(BF16)
    v_ref[...] = proj(C_V, D_ATT).astype(BF16)

    f = proj(C_F, LANES) + bf_ref[...]
    logf = (jnp.minimum(f, 0.0) - jnp.log(1.0 + jnp.exp(-jnp.abs(f)))) * LOG2E
    row = lax.broadcasted_iota(jnp.int32, (tm, tm), 0)
    col = lax.broadcasted_iota(jnp.int32, (tm, tm), 1)
    tri = (col <= row).astype(BF16)
    p1, p2, p3 = _split3(logf)
    cum = _dot(tri, p1) + _dot(tri, p2) + _dot(tri, p3) + carry_ref[...]
    cum_ref[...] = cum
    carry_ref[...] = cum[tm - 1:tm, :]

    zg = _gelu(proj(C_Z, 2 * D_GM))
    ug = zg[:, :D_GM]
    vg = zg[:, D_GM:]
    mean = jnp.mean(vg, axis=-1, keepdims=True)
    cen = vg - mean
    var = jnp.mean(cen * cen, axis=-1, keepdims=True)
    vn = (cen * lax.rsqrt(var + EPS) * lng_ref[...] + lnb_ref[...]).astype(BF16)
    lane_head = lax.broadcasted_iota(jnp.int32, (CHUNK, D_GM), 1) >> 6
    for c in range(tm // CHUNK):
        rows = slice(c * CHUNK, (c + 1) * CHUNK)
        vc = vn[rows, :]
        mixed = _dot(ws_ref[0], vc)
        for g in range(1, GM_HEADS):
            mixed = jnp.where(lane_head == g, _dot(ws_ref[g], vc), mixed)
        mixed = mixed + bs_ref[...]
        ygm_ref[rows, :] = (ug[rows, :] * mixed).astype(BF16)


def _inproj(x2, g, w, bf, lng, lnb, ws, bs, *, seq, tm):
    n = x2.shape[0]
    row = lambda width: pl.BlockSpec((tm, width), lambda i: (i, 0))
    return pl.pallas_call(
        functools.partial(_inproj_body, tiles_per_seq=seq // tm),
        grid=(n // tm,),
        in_specs=[row(D_MODEL), _resident((1, D_MODEL)), _resident((D_MODEL, D_IN_PAD)),
                  _resident((1, LANES)), _resident((1, D_GM)), _resident((1, D_GM)),
                  _resident((GM_HEADS, CHUNK, CHUNK)), _resident((CHUNK, D_GM))],
        out_specs=[_split_spec(tm), row(D_ATT), row(D_ATT), row(D_ATT), row(LANES), row(D_GM)],
        out_shape=[jax.ShapeDtypeStruct((D_SSM // LANES, n, LANES), F32),
                   jax.ShapeDtypeStruct((n, D_ATT), BF16),
                   jax.ShapeDtypeStruct((n, D_ATT), BF16),
                   jax.ShapeDtypeStruct((n, D_ATT), BF16),
                   jax.ShapeDtypeStruct((n, LANES), F32),
                   jax.ShapeDtypeStruct((n, D_GM), BF16)],
        scratch_shapes=[pltpu.VMEM((1, LANES), F32)],
        compiler_params=_cparams("arbitrary"),
        name="inproj",
    )(x2, g, w, bf, lng, lnb, ws, bs)


def _ssm_tokens(u_ref, s, tr):
    return _split_load(u_ref, pl.ds(s, tr, stride=SSM_T)).astype(BF16)


def _ssm_state_body(u_ref, f_ref, s_ref):
    tr = s_ref.shape[0]
    acc = _dot(_ssm_tokens(u_ref, 0, tr), f_ref[:D_SSM, :])
    for s in range(1, SSM_T):
        acc += _dot(_ssm_tokens(u_ref, s, tr), f_ref[s * D_SSM:(s + 1) * D_SSM, :])
    s_ref[...] = acc


def _ssm_state(u, fmat, *, tr):
    nc = u.shape[1] // SSM_T
    return pl.pallas_call(
        _ssm_state_body,
        grid=(nc // tr,),
        in_specs=[_split_spec(tr * SSM_T), _resident((SSM_ROW, 2 * SSM_NSTATE))],
        out_specs=pl.BlockSpec((tr, 2 * SSM_NSTATE), lambda i: (i, 0)),
        out_shape=jax.ShapeDtypeStruct((nc, 2 * SSM_NSTATE), F32),
        compiler_params=_cparams("parallel"),
        name="ssm_state",
    )(u, fmat)


def _ssm_scan_body(s_ref, a_ref, xp_ref, st_ref):
    @pl.when(pl.program_id(1) == 0)
    def _():
        st_ref[...] = jnp.zeros_like(st_ref)

    tc = s_ref.shape[0]
    are = a_ref[:, :SSM_NSTATE]
    aim = a_ref[:, SSM_NSTATE:]

    def step(c8, carry):
        xre, xim = carry
        base = pl.multiple_of(c8 * SUBLANES, SUBLANES)
        blk = s_ref[pl.ds(base, SUBLANES), :]
        prev_re, prev_im = [], []
        for r in range(SUBLANES):
            prev_re.append(xre)
            prev_im.append(xim)
            sre = blk[r:r + 1, :SSM_NSTATE]
            sim = blk[r:r + 1, SSM_NSTATE:]
            xre, xim = are * xre - aim * xim + sre, are * xim + aim * xre + sim
        xp_ref[pl.ds(base, SUBLANES), :SSM_NSTATE] = jnp.concatenate(prev_re, axis=0)
        xp_ref[pl.ds(base, SUBLANES), SSM_NSTATE:] = jnp.concatenate(prev_im, axis=0)
        return xre, xim

    xre, xim = lax.fori_loop(0, tc // SUBLANES, step,
                             (st_ref[:, :SSM_NSTATE], st_ref[:, SSM_NSTATE:]))
    st_ref[:, :SSM_NSTATE] = xre
    st_ref[:, SSM_NSTATE:] = xim


def _ssm_scan(s3, a, *, tc):
    b, ncb, w = s3.shape
    spec = pl.BlockSpec((None, tc, w), lambda bi, ci: (bi, ci, 0))
    return pl.pallas_call(
        _ssm_scan_body,
        grid=(b, ncb // tc),
        in_specs=[spec, pl.BlockSpec((1, w), lambda bi, ci: (0, 0))],
        out_specs=spec,
        out_shape=jax.ShapeDtypeStruct(s3.shape, F32),
        scratch_shapes=[pltpu.VMEM((1, w), F32)],
        compiler_params=_cparams("parallel", "arbitrary"),
        name="ssm_scan",
    )(s3, a)


def _ssm_out_body(u_ref, xp_ref, m_ref, e_ref, y_ref):
    tr = xp_ref.shape[0]
    us = [_ssm_tokens(u_ref, s, tr) for s in range(SSM_T)]
    xp = xp_ref[...].astype(BF16)
    for t in range(SSM_T):
        cols = slice(t * D_SSM, (t + 1) * D_SSM)
        acc = _dot(xp, e_ref[:, cols])
        for s in range(t + 1):
            acc += _dot(us[s], m_ref[s * D_SSM:(s + 1) * D_SSM, cols])
        _split_store(y_ref, acc, pl.ds(t, tr, stride=SSM_T))


def _ssm_out(u, xp, mmat, emat, *, tr):
    nc = xp.shape[0]
    return pl.pallas_call(
        _ssm_out_body,
        grid=(nc // tr,),
        in_specs=[_split_spec(tr * SSM_T),
                  pl.BlockSpec((tr, 2 * SSM_NSTATE), lambda i: (i, 0)),
                  _resident((SSM_ROW, SSM_ROW)), _resident((2 * SSM_NSTATE, SSM_ROW))],
        out_specs=_split_spec(tr * SSM_T),
        out_shape=jax.ShapeDtypeStruct((D_SSM // LANES, nc * SSM_T, LANES), F32),
        compiler_params=_cparams("parallel"),
        name="ssm_out",
    )(u, xp, mmat, emat)


def _ssm_matrices(lam_re, lam_im, log_dt, b_re, b_im, c_re, c_im):
    t, g, p, h = SSM_T, SSM_GROUPS, SSM_STATE, SSM_GROUP
    ns, nch = g * p, g * h
    hp = lax.Precision.HIGHEST
    lr, li = lam_re.astype(F32).reshape(ns), lam_im.astype(F32).reshape(ns)
    dt = jnp.repeat(jnp.exp(log_dt.astype(F32)), p)
    steps = jnp.arange(t + 1, dtype=F32)[:, None]
    mag = jnp.exp((lr * dt)[None] * steps)
    ang = (li * dt)[None] * steps
    pr, pi = mag * jnp.cos(ang), mag * jnp.sin(ang)
    nr, ni = pr[1] - 1.0, pi[1]
    den = lr * lr + li * li
    zr, zi = ((nr * lr + ni * li) / den)[:, None], ((ni * lr - nr * li) / den)[:, None]
    br, bi = b_re.astype(F32).reshape(ns, h), b_im.astype(F32).reshape(ns, h)
    cr = c_re.astype(F32).transpose(0, 2, 1).reshape(ns, h)
    ci = c_im.astype(F32).transpose(0, 2, 1).reshape(ns, h)
    same = jnp.repeat(jnp.repeat(jnp.eye(g, dtype=F32), p, axis=0), h, axis=1)
    spread = lambda w: jnp.tile(w, (1, g)) * same
    bhr, bhi = spread(zr * br - zi * bi), spread(zr * bi + zi * br)
    chr_, chi = spread(cr), spread(ci)
    er = jnp.concatenate([pr[k][:, None] * chr_ - pi[k][:, None] * chi for k in range(t + 1)], axis=1)
    ei = jnp.concatenate([pr[k][:, None] * chi + pi[k][:, None] * chr_ for k in range(t + 1)], axis=1)

    k_all = (jnp.dot(bhr.T, er[:, :t * nch], precision=hp)
             - jnp.dot(bhi.T, ei[:, :t * nch], precision=hp))
    mmat = jnp.concatenate(
        [jnp.pad(k_all[:, :(t - s) * nch], ((0, 0), (s * nch, 0))) for s in range(t)], axis=0)

    fmat = jnp.concatenate(
        [jnp.concatenate([bhr.T * pr[t - 1 - s][None] - bhi.T * pi[t - 1 - s][None],
                          bhr.T * pi[t - 1 - s][None] + bhi.T * pr[t - 1 - s][None]], axis=1)
         for s in range(t)], axis=0)

    emat = jnp.concatenate([er[:, nch:], -ei[:, nch:]], axis=0)

    a_row = jnp.concatenate([pr[t][None], pi[t][None]], axis=1)
    return mmat.astype(BF16), fmat.astype(BF16), emat.astype(BF16), a_row


def _attn_body(q_ref, k_ref, v_ref, cq_ref, ck_ref, o_ref,
               m_ref, acc_ref, s_ref, mt_ref, p_ref, al_ref, *, tq):
    hg = pl.program_id(1)
    qi = pl.program_id(2)
    nh = ATT_STEP_HEADS
    nchunk = tq // LANES
    lane = lax.broadcasted_iota(jnp.int32, (tq, LANES), 1)
    upper = lane >= ATT_HEAD_DIM

    def chunk(c):
        return slice(c * LANES, (c + 1) * LANES)

    def own(j):
        return upper if j % 2 else ~upper

    cq_all = cq_ref[...]
    qs = [jnp.where(own(j), q_ref[:, chunk(j // 2)], jnp.zeros((tq, LANES), BF16)) for j in range(nh)]
    cqs = [jnp.broadcast_to(
        jnp.sum(jnp.where(lane == nh * hg + j, cq_all, 0.0), axis=-1, keepdims=True), (tq, LANES))
        for j in range(nh)]
    rowpos = lax.broadcasted_iota(jnp.int32, (tq, LANES), 0)

    def scores(t, slot, masked):
        start = pl.multiple_of(t * tq, tq)
        for j in range(nh):
            kt = k_ref[pl.ds(start, tq), chunk(j // 2)]
            s = lax.dot_general(qs[j], kt, (((1,), (1,)), ((), ())), preferred_element_type=F32)
            ck = ck_ref[j // 2, j % 2:j % 2 + 1, pl.ds(start, tq)]
            mt = None
            for c in range(nchunk):
                sc = s[:, chunk(c)] - ck[:, chunk(c)]
                if masked:
                    sc = jnp.where(lane + c * LANES <= rowpos, sc, -jnp.inf)
                s_ref[slot, j, :, chunk(c)] = sc
                mt = sc if mt is None else jnp.maximum(mt, sc)
            mt_ref[slot, j] = mt

    def probs(slot):
        for j in range(nh):
            m = m_ref[j]
            mt = jnp.broadcast_to(jnp.max(mt_ref[slot, j], axis=-1, keepdims=True), (tq, LANES))
            m_new = jnp.maximum(m, mt + cqs[j])
            shift = cqs[j] - m_new
            for c in range(nchunk):
                p_ref[slot, j, :, chunk(c)] = jnp.exp2(s_ref[slot, j, :, chunk(c)] + shift).astype(BF16)
            al_ref[slot, j] = jnp.exp2(m - m_new)
            m_ref[j] = m_new

    def values(t, slot):
        start = pl.multiple_of(t * tq, tq)
        for j in range(nh):
            vt = v_ref[pl.ds(start, tq), chunk(j // 2)]
            vj = jnp.where(own(j), vt, jnp.ones_like(vt))
            acc_ref[j] = al_ref[slot, j] * acc_ref[j] + _dot(p_ref[slot, j], vj)

    def trip(t, slot, masked=False):
        values(t - 1, 1 - slot)
        probs(slot)
        scores(t + 1, 1 - slot, masked)

    for j in range(nh):
        m_ref[j] = jnp.full((tq, LANES), -jnp.inf, F32)
        acc_ref[j] = jnp.zeros((tq, LANES), F32)

    @pl.when(qi == 0)
    def _():
        scores(0, 0, True)
        probs(0)
        values(0, 0)

    @pl.when(qi == 1)
    def _():
        scores(0, 0, False)
        probs(0)
        scores(1, 1, True)
        values(0, 0)
        probs(1)
        values(1, 1)

    @pl.when(qi >= 2)
    def _():
        scores(0, 0, False)
        probs(0)
        scores(1, 1, False)

        def pair(u, carry):
            t = 2 * u + 1
            trip(t, 1)
            trip(t + 1, 0)
            return carry

        lax.fori_loop(0, (qi - 2) // 2, pair, 0)

        @pl.when(qi % 2 == 0)
        def _():
            trip(qi - 1, 1, masked=True)
            values(qi - 1, 1)
            probs(0)
            values(qi, 0)

        @pl.when(qi % 2 == 1)
        def _():
            trip(qi - 2, 1)
            trip(qi - 1, 0, masked=True)
            values(qi - 1, 0)
            probs(1)
            values(qi, 1)
    for pair in range(nh // 2):
        outs = []
        for j in (2 * pair, 2 * pair + 1):
            acc = acc_ref[j]
            outs.append(acc / pltpu.roll(acc, ATT_HEAD_DIM, axis=1))
        o_ref[:, chunk(pair)] = jnp.where(upper, outs[1], outs[0]).astype(BF16)


def _attention(q3, k3, v3, cq3, ck4, *, tq):
    b, s, _ = q3.shape
    nh = ATT_STEP_HEADS
    width = nh * ATT_HEAD_DIM
    return pl.pallas_call(
        functools.partial(_attn_body, tq=tq),
        scratch_shapes=[pltpu.VMEM((nh, tq, LANES), F32), pltpu.VMEM((nh, tq, LANES), F32),
                        pltpu.VMEM((2, nh, tq, tq), F32), pltpu.VMEM((2, nh, tq, LANES), F32),
                        pltpu.VMEM((2, nh, tq, tq), BF16), pltpu.VMEM((2, nh, tq, LANES), F32)],
        grid=(b, ATT_HEADS // nh, s // tq),
        in_specs=[pl.BlockSpec((None, tq, width), lambda bi, hg, qi: (bi, qi, hg)),
                  pl.BlockSpec((None, s, width), lambda bi, hg, qi: (bi, 0, hg)),
                  pl.BlockSpec((None, s, width), lambda bi, hg, qi: (bi, 0, hg)),
                  pl.BlockSpec((None, tq, LANES), lambda bi, hg, qi: (bi, qi, 0)),
                  pl.BlockSpec((None, nh // 2, 2, s), lambda bi, hg, qi: (bi, hg, 0, 0))],
        out_specs=pl.BlockSpec((None, tq, width), lambda bi, hg, qi: (bi, qi, hg)),
        out_shape=jax.ShapeDtypeStruct((b, s, D_ATT), BF16),
        compiler_params=_cparams("parallel", "parallel", "arbitrary"),
        name="attention",
    )(q3, k3, v3, cq3, ck4)


R_E1, R_E2, R_W1, R_W2, R_RANK1, R_RANK2 = range(6)


def _route(logits, cnt_ref):
    tm = logits.shape[0]
    lane = lax.broadcasted_iota(jnp.int32, logits.shape, 1).astype(F32)
    lg = jnp.where(lane < N_EXPERTS, logits, -jnp.inf)
    m1 = jnp.max(lg, axis=-1, keepdims=True)
    i1 = jnp.min(jnp.where(lg == m1, lane, float(LANES)), axis=-1, keepdims=True)
    lg2 = jnp.where(lane == i1, -jnp.inf, lg)
    m2 = jnp.max(lg2, axis=-1, keepdims=True)
    i2 = jnp.min(jnp.where(lg2 == m2, lane, float(LANES)), axis=-1, keepdims=True)
    e = jnp.exp(m2 - m1)
    w1 = 1.0 / (1.0 + e)
    w2 = e / (1.0 + e)

    hit1, hit2 = lane == i1, lane == i2
    oh1, oh2 = hit1.astype(BF16), hit2.astype(BF16)
    row = lax.broadcasted_iota(jnp.int32, (tm, tm), 0)
    col = lax.broadcasted_iota(jnp.int32, (tm, tm), 1)
    before = (col < row).astype(BF16)
    carry = cnt_ref[...]
    tot1 = jnp.sum(oh1.astype(F32), axis=0, keepdims=True)
    tot2 = jnp.sum(oh2.astype(F32), axis=0, keepdims=True)
    pos1 = _dot(before, oh1) + carry
    pos2 = _dot(before, oh2) + (carry + tot1)
    rank1 = jnp.sum(jnp.where(hit1, pos1, 0.0), axis=-1, keepdims=True)
    rank2 = jnp.sum(jnp.where(hit2, pos2, 0.0), axis=-1, keepdims=True)
    cnt_ref[...] = carry + tot1 + tot2

    rec = jnp.zeros(logits.shape, F32)
    for idx, val in ((R_E1, i1), (R_E2, i2), (R_W1, w1), (R_W2, w2), (R_RANK1, rank1), (R_RANK2, rank2)):
        rec = jnp.where(lane == idx, val, rec)
    return rec


def _outproj_body(*refs, with_router):
    if with_router:
        (x_ref, ys_ref, us_ref, ya_ref, yg_ref, d_ref, wglu_ref, gn_ref, w_ref, fg_ref, rw_ref,
         xo_ref, h_ref, route_ref, cnt_ref) = refs

        @pl.when(pl.program_id(0) == 0)
        def _():
            cnt_ref[...] = jnp.zeros_like(cnt_ref)
    else:
        (x_ref, ys_ref, us_ref, ya_ref, yg_ref, d_ref, wglu_ref, gn_ref, w_ref, fg_ref,
         xo_ref, h_ref) = refs

    tm = x_ref.shape[0]
    part = tm // OUTPROJ_PARTS
    o1, o2 = D_SSM, D_SSM + D_ATT
    for k in range(OUTPROJ_PARTS):
        rows = slice(k * part, (k + 1) * part)
        y1 = _gelu(_split_load(ys_ref, rows) + d_ref[...] * _split_load(us_ref, rows))
        y_ssm = y1 * _sigmoid(_dot(y1.astype(BF16), wglu_ref[...]))

        acc = _dot(_rms(y_ssm, gn_ref[:, :o1]).astype(BF16), w_ref[:o1, :])
        acc += _dot(_rms(ya_ref[rows, :].astype(F32), gn_ref[:, o1:o2]).astype(BF16), w_ref[o1:o2, :])
        acc += _dot(_rms(yg_ref[rows, :].astype(F32), gn_ref[:, o2:]).astype(BF16), w_ref[o2:, :])
        xn = x_ref[rows, :] + acc
        xo_ref[rows, :] = xn
        hn = _rms(xn, fg_ref[...])
        h_ref[rows, :] = hn.astype(h_ref.dtype)
        if with_router:
            h1, h2, h3 = _split3(hn)
            r1, r2, r3 = _split3(rw_ref[...])
            logits = (_dot(h1, r1) + _dot(h1, r2) + _dot(h2, r1)
                      + _dot(h1, r3) + _dot(h2, r2) + _dot(h3, r1))
            route_ref[rows, :] = _route(logits, cnt_ref)


def _outproj(x2, ys, us, ya, yg, d, wglu, gn, w, fg, rw, *, tm):
    n = x2.shape[0]
    with_router = rw is not None
    row = lambda width: pl.BlockSpec((tm, width), lambda i: (i, 0))
    in_specs = [row(D_MODEL), _split_spec(tm), _split_spec(tm), row(D_ATT), row(D_GM),
                _resident((1, D_SSM)), _resident((D_SSM, D_SSM)), _resident((1, D_MODEL)),
                _resident((D_MODEL, D_MODEL)), _resident((1, D_MODEL))]
    out_specs = [row(D_MODEL), row(D_MODEL)]
    out_shape = [jax.ShapeDtypeStruct((n, D_MODEL), F32),
                 jax.ShapeDtypeStruct((n, D_MODEL), F32 if with_router else BF16)]
    args = [x2, ys, us, ya, yg, d, wglu, gn, w, fg]
    if with_router:
        in_specs.append(_resident((D_MODEL, LANES)))
        out_specs += [row(LANES), pl.BlockSpec((1, LANES), lambda i: (0, 0))]
        out_shape += [jax.ShapeDtypeStruct((n, LANES), F32), jax.ShapeDtypeStruct((1, LANES), F32)]
        args.append(rw)
    return pl.pallas_call(
        functools.partial(_outproj_body, with_router=with_router),
        grid=(n // tm,),
        in_specs=in_specs, out_specs=out_specs, out_shape=out_shape,
        compiler_params=_cparams("arbitrary" if with_router else "parallel"),
        name="outproj_router" if with_router else "outproj",
    )(*args)


def _swiglu_tile(h, wg, wu, wd):
    a = _dot(h, wg)
    return _dot((a * _sigmoid(a) * _dot(h, wu)).astype(BF16), wd)


def _finish(x, acc, fin_ref, o_ref):
    xn = x + acc
    o_ref[...] = xn if fin_ref is None else _rms(xn, fin_ref[...])


def _dense_ffn_body(*refs, final_norm):
    if final_norm:
        h_ref, x_ref, wg_ref, wu_ref, wd_ref, fin_ref, o_ref, acc_ref = refs
    else:
        h_ref, x_ref, wg_ref, wu_ref, wd_ref, o_ref, acc_ref = refs
        fin_ref = None
    f = pl.program_id(1)

    @pl.when(f == 0)
    def _():
        acc_ref[...] = jnp.zeros_like(acc_ref)

    acc_ref[...] += _swiglu_tile(h_ref[...], wg_ref[...], wu_ref[...], wd_ref[...])

    @pl.when(f == pl.num_programs(1) - 1)
    def _():
        _finish(x_ref[...], acc_ref[...], fin_ref, o_ref)


def _dense_ffn(h, x2, wg, wu, wd, fin, *, tm, tf):
    n = x2.shape[0]
    dff = wg.shape[1]
    row = lambda: pl.BlockSpec((tm, D_MODEL), lambda i, f: (i, 0))
    mode = dict(pipeline_mode=pl.Buffered(1)) if tf == dff else {}
    in_specs = [row(), row(),
                pl.BlockSpec((D_MODEL, tf), lambda i, f: (0, f), **mode),
                pl.BlockSpec((D_MODEL, tf), lambda i, f: (0, f), **mode),
                pl.BlockSpec((tf, D_MODEL), lambda i, f: (f, 0), **mode)]
    args = [h, x2, wg, wu, wd]
    if fin is not None:
        in_specs.append(pl.BlockSpec((1, D_MODEL), lambda i, f: (0, 0)))
        args.append(fin)
    return pl.pallas_call(
        functools.partial(_dense_ffn_body, final_norm=fin is not None),
        grid=(n // tm, dff // tf),
        in_specs=in_specs, out_specs=row(),
        out_shape=jax.ShapeDtypeStruct((n, D_MODEL), F32),
        scratch_shapes=[pltpu.VMEM((tm, D_MODEL), F32)],
        compiler_params=_cparams("parallel", "arbitrary"),
        name="dense_ffn",
    )(*args)


def _row_copy(src, src_row, dst, dst_row, sem):
    return pltpu.make_async_copy(src.at[pl.ds(src_row, 1)], dst.at[pl.ds(dst_row, 1)], sem)


def _moe_dispatch_body(dest_ref, h_ref, xs_in_ref, xs_ref, sem):
    del xs_in_ref
    tm = h_ref.shape[0]

    def issue(r, carry):
        for k in range(2):
            _row_copy(h_ref, r, xs_ref, dest_ref[k, r], sem).start()
        return carry

    lax.fori_loop(0, tm, issue, 0, unroll=8)
    for k in range(2):
        pltpu.make_async_copy(h_ref, xs_ref.at[pl.ds(0, tm)], sem).wait()


def _moe_dispatch(dest, h, xs_zero, *, tm):
    n = h.shape[0]
    return pl.pallas_call(
        _moe_dispatch_body,
        grid=(n // tm,),
        in_specs=[pl.BlockSpec((None, 2, tm), lambda i: (i, 0, 0), memory_space=pltpu.SMEM),
                  pl.BlockSpec((tm, D_MODEL), lambda i: (i, 0)),
                  pl.BlockSpec(memory_space=pl.ANY)],
        out_specs=pl.BlockSpec(memory_space=pl.ANY),
        out_shape=jax.ShapeDtypeStruct(xs_zero.shape, xs_zero.dtype),
        scratch_shapes=[pltpu.SemaphoreType.DMA(())],
        input_output_aliases={2: 0},
        compiler_params=pltpu.CompilerParams(dimension_semantics=("arbitrary",),
                                             vmem_limit_bytes=VMEM_LIMIT, has_side_effects=True),
        name="moe_dispatch",
    )(dest, h, xs_zero)


def _moe_ffn_body(te_ref, nu_ref, xs_ref, wg_ref, wu_ref, wd_ref, ys_ref, acc_ref):
    del te_ref
    i = pl.program_id(0)
    f = pl.program_id(1)

    @pl.when(i < nu_ref[0])
    def _():
        @pl.when(f == 0)
        def _():
            acc_ref[...] = jnp.zeros_like(acc_ref)

        acc_ref[...] += _swiglu_tile(xs_ref[...].astype(BF16), wg_ref[...], wu_ref[...], wd_ref[...])

        @pl.when(f == pl.num_programs(1) - 1)
        def _():
            ys_ref[...] = acc_ref[...]

    @pl.when((i >= nu_ref[0]) & (f == pl.num_programs(1) - 1))
    def _():
        ys_ref[...] = jnp.zeros_like(ys_ref)


def _moe_ffn(tile_expert, n_used, xs, wg, wu, wd, *, tme, tf):
    rows = xs.shape[0]
    dff = wg.shape[-1]
    nf = dff // tf
    row_idx = lambda i, f, te, nu: (jnp.minimum(i, nu[0] - 1), 0)
    col = lambda i, f, nu: jnp.where(i < nu[0], f, nf - 1)
    grid_spec = pltpu.PrefetchScalarGridSpec(
        num_scalar_prefetch=2,
        grid=(rows // tme, nf),
        in_specs=[pl.BlockSpec((tme, D_MODEL), row_idx),
                  pl.BlockSpec((None, D_MODEL, tf), lambda i, f, te, nu: (te[i], 0, col(i, f, nu))),
                  pl.BlockSpec((None, D_MODEL, tf), lambda i, f, te, nu: (te[i], 0, col(i, f, nu))),
                  pl.BlockSpec((None, tf, D_MODEL), lambda i, f, te, nu: (te[i], col(i, f, nu), 0))],
        out_specs=pl.BlockSpec((tme, D_MODEL), lambda i, f, te, nu: (i, 0)),
        scratch_shapes=[pltpu.VMEM((tme, D_MODEL), F32)])
    return pl.pallas_call(
        _moe_ffn_body,
        grid_spec=grid_spec,
        out_shape=jax.ShapeDtypeStruct((rows, D_MODEL), F32),
        compiler_params=_cparams("arbitrary", "arbitrary"),
        name="moe_ffn",
    )(tile_expert, n_used, xs, wg, wu, wd)


def _moe_combine_body(*refs, final_norm):
    if final_norm:
        dest_ref, x_ref, route_ref, ys_ref, fin_ref, o_ref, buf_ref, sem = refs
    else:
        dest_ref, x_ref, route_ref, ys_ref, o_ref, buf_ref, sem = refs
        fin_ref = None
    tm = x_ref.shape[0]

    def issue(r, carry):
        for k in range(2):
            _row_copy(ys_ref, dest_ref[k, r], buf_ref.at[k], r, sem).start()
        return carry

    lax.fori_loop(0, tm, issue, 0, unroll=8)
    for k in range(2):
        pltpu.make_async_copy(ys_ref.at[pl.ds(0, tm)], buf_ref.at[k], sem).wait()

    route = route_ref[...]
    lane = lax.broadcasted_iota(jnp.int32, route.shape, 1)
    w1 = jnp.sum(jnp.where(lane == R_W1, route, 0.0), axis=-1, keepdims=True)
    w2 = jnp.sum(jnp.where(lane == R_W2, route, 0.0), axis=-1, keepdims=True)
    _finish(x_ref[...], w1 * buf_ref[0] + w2 * buf_ref[1], fin_ref, o_ref)


def _moe_combine(dest, x2, route, ys, fin, *, tm):
    n = x2.shape[0]
    row = lambda width: pl.BlockSpec((tm, width), lambda i: (i, 0))
    in_specs = [pl.BlockSpec((None, 2, tm), lambda i: (i, 0, 0), memory_space=pltpu.SMEM),
                row(D_MODEL), row(LANES), pl.BlockSpec(memory_space=pl.ANY)]
    args = [dest, x2, route, ys]
    if fin is not None:
        in_specs.append(pl.BlockSpec((1, D_MODEL), lambda i: (0, 0)))
        args.append(fin)
    return pl.pallas_call(
        functools.partial(_moe_combine_body, final_norm=fin is not None),
        grid=(n // tm,),
        in_specs=in_specs, out_specs=row(D_MODEL),
        out_shape=jax.ShapeDtypeStruct((n, D_MODEL), F32),
        scratch_shapes=[pltpu.VMEM((2, tm, D_MODEL), F32), pltpu.SemaphoreType.DMA(())],
        compiler_params=_cparams("arbitrary"),
        name="moe_combine",
    )(*args)


def _moe_layout(route, counts, *, tm, tme):
    n = route.shape[0]
    cnt = counts[0, :N_EXPERTS].astype(jnp.int32)
    padded = (cnt + (tme - 1)) // tme * tme
    ends = jnp.cumsum(padded)
    starts = ends - padded
    experts = jnp.arange(N_EXPERTS, dtype=F32)

    def rows(e_lane, rank_lane):
        start = jnp.sum(jnp.where(route[:, e_lane, None] == experts[None], starts[None], 0), axis=1)
        return start + route[:, rank_lane].astype(jnp.int32)

    dest = jnp.stack([rows(R_E1, R_RANK1), rows(R_E2, R_RANK2)])
    dest = dest.reshape(2, n // tm, tm).transpose(1, 0, 2)
    n_tiles = (2 * n) // tme + N_EXPERTS
    tile_start = jnp.arange(n_tiles, dtype=jnp.int32) * tme
    tile_expert = jnp.minimum(jnp.sum(tile_start[:, None] >= ends[None, :], axis=1), N_EXPERTS - 1)
    n_used = (ends[-1] // tme).reshape(1)
    tile_expert = jnp.where(tile_start < ends[-1], tile_expert, tile_expert[jnp.maximum(n_used[0] - 1, 0)])
    return dest, tile_expert.astype(jnp.int32), n_used.astype(jnp.int32), n_tiles * tme


def _largest_tile(total, cap, mult):
    best = mult
    for t in range(mult, min(total, cap) + 1, mult):
        if total % t == 0:
            best = t
    return best


def _mxu_tile(total, cap):
    if total % MXU_WIDTH == 0:
        return _largest_tile(total, cap, MXU_WIDTH)
    return _largest_tile(total, cap, LANES)


def _pad_lanes(a):
    return jnp.pad(a, [(0, 0)] * (a.ndim - 1) + [(0, LANES - a.shape[-1])])


def kernel(x, mix_norm_g, w_in, b_forget, ssm_lambda_re, ssm_lambda_im, ssm_log_dt, ssm_b_re, ssm_b_im, ssm_c_re, ssm_c_im, ssm_d, ssm_w_glu, gm_ln_g, gm_ln_b, gm_w_s, gm_b_s, group_norm_g, w_out, ffn_norm_g, dense_w_gate, dense_w_up, dense_w_down, router_w, moe_w_gate, moe_w_up, moe_w_down, final_norm_g):
    bsz, seq, _ = x.shape
    n = bsz * seq
    depth = w_in.shape[0]
    assert seq % CHUNK == 0 and seq % SSM_T == 0
    tm = _largest_tile(seq, 512, CHUNK)
    tq = _largest_tile(seq, 512, LANES)
    nc = n // SSM_T
    ncb = seq // SSM_T
    tr = _largest_tile(nc, 512, 8)
    tc = _largest_tile(ncb, 128, 8)

    x2 = x.reshape(n, D_MODEL).astype(F32)
    tril = jnp.tril(jnp.ones((CHUNK, CHUNK), F32))
    i0, i1, i2, i3, i4 = (D_SSM, D_SSM + D_ATT, D_SSM + 2 * D_ATT, D_SSM + 3 * D_ATT,
                          D_SSM + 3 * D_ATT + ATT_HEADS)
    for layer in range(depth):
        wl = w_in[layer]
        w_re = jnp.concatenate(
            [wl[:, :i0], wl[:, i0:i1] * (ATT_HEAD_DIM ** -0.5 * LOG2E), wl[:, i1:i3], wl[:, i4:],
             _pad_lanes(wl[:, i3:i4])], axis=1).astype(BF16)
        ws = (gm_w_s[layer].astype(F32) * tril[None]).astype(BF16)
        bs = jnp.repeat(gm_b_s[layer].astype(F32).T, GM_HEAD_DIM, axis=1)
        u, q, k, v, cum, ygm = _inproj(
            x2, mix_norm_g[layer].reshape(1, D_MODEL), w_re,
            _pad_lanes(b_forget[layer].reshape(1, ATT_HEADS).astype(F32)),
            gm_ln_g[layer].reshape(1, D_GM), gm_ln_b[layer].reshape(1, D_GM), ws, bs,
            seq=seq, tm=tm)

        mmat, fmat, emat, a_row = _ssm_matrices(
            ssm_lambda_re[layer], ssm_lambda_im[layer], ssm_log_dt[layer], ssm_b_re[layer],
            ssm_b_im[layer], ssm_c_re[layer], ssm_c_im[layer])
        s_loc = _ssm_state(u, fmat, tr=tr)
        xprev = _ssm_scan(s_loc.reshape(bsz, ncb, 2 * SSM_NSTATE), a_row, tc=tc)
        y_ssm = _ssm_out(u, xprev.reshape(nc, 2 * SSM_NSTATE), mmat, emat, tr=tr)

        ck = cum[:, :ATT_HEADS].reshape(bsz, seq, ATT_HEADS // 2, 2).transpose(0, 2, 3, 1)
        y_att = _attention(q.reshape(bsz, seq, D_ATT), k.reshape(bsz, seq, D_ATT),
                           v.reshape(bsz, seq, D_ATT), cum.reshape(bsz, seq, LANES), ck,
                           tq=tq).reshape(n, D_ATT)

        is_moe = layer % 2 == 1
        j = layer // 2
        rw = _pad_lanes(router_w[j].astype(F32)) if is_moe else None
        outs = _outproj(x2, y_ssm, u, y_att, ygm, ssm_d[layer].reshape(1, D_SSM).astype(F32),
                        ssm_w_glu[layer].astype(BF16), group_norm_g[layer].reshape(1, D_MODEL),
                        w_out[layer].astype(BF16), ffn_norm_g[layer].reshape(1, D_MODEL), rw, tm=tm)
        fin = final_norm_g.reshape(1, D_MODEL) if layer == depth - 1 else None
        if is_moe:
            x_mid, h, route, counts = outs
            dffe = moe_w_gate.shape[-1]
            tme = min(MOE_ROW_TILE, 2 * n)
            dest, tile_expert, n_used, rows = _moe_layout(route, counts, tm=tm, tme=tme)
            xs = _moe_dispatch(dest, h, jnp.zeros((rows, D_MODEL), F32), tm=tm)
            ys = _moe_ffn(tile_expert, n_used, xs, moe_w_gate[j].astype(BF16),
                          moe_w_up[j].astype(BF16), moe_w_down[j].astype(BF16), tme=tme,
                          tf=_mxu_tile(dffe, 1792))
            x2 = _moe_combine(dest, x_mid, route, ys, fin, tm=tm)
        else:
            x_mid, h = outs
            dff = dense_w_gate.shape[-1]
            x2 = _dense_ffn(h, x_mid, dense_w_gate[j].astype(BF16), dense_w_up[j].astype(BF16),
                            dense_w_down[j].astype(BF16), fin, tm=tm,
                            tf=_mxu_tile(dff, 2816))
    return x2.reshape(bsz, seq, D_MODEL).astype(x.dtype)
```

```python
import functools
import math

import jax
import jax.numpy as jnp
from jax import lax
from jax.experimental import pallas as pl
from jax.experimental.pallas import tpu as pltpu

F32 = jnp.float32
BF16 = jnp.bfloat16

D_MODEL = 1024
D_SSM = 256
D_ATT = 512
D_GM = 256
SSM_GROUP = 16
SSM_GROUPS = 16
SSM_STATE = 64
ATT_HEADS = 8
ATT_HEAD_DIM = 64
ATT_SLOTS = ATT_HEADS * 128
GM_HEADS = 4
GM_HEAD_DIM = 64
CHUNK = 128
N_EXPERTS = 8
EPS = 1e-6
LOG2E = math.log2(math.e)

LANES = 128
SUBLANES = 8
MXU_WIDTH = 256
MOE_ROW_TILE = 512
OUTPROJ_PARTS = 2
ATT_STEP_HEADS = 2
SSM_T = 8
SSM_ROW = SSM_T * D_SSM
SSM_NSTATE = SSM_GROUPS * SSM_STATE

C_U = 0
C_Q = C_U + D_SSM
C_K = C_Q + D_ATT
C_V = C_K + D_ATT
C_Z = C_V + D_ATT
C_F = C_Z + 2 * D_GM
D_IN_PAD = C_F + LANES

VMEM_LIMIT = 56 * 1024 * 1024


def _cparams(*sem):
    return pltpu.CompilerParams(dimension_semantics=sem, vmem_limit_bytes=VMEM_LIMIT)


def _resident(shape):
    nd = len(shape)
    return pl.BlockSpec(shape, lambda *_: (0,) * nd, pipeline_mode=pl.Buffered(1))


def _rms(x, g):
    return x * lax.rsqrt(jnp.mean(x * x, axis=-1, keepdims=True) + EPS) * g


def _gelu(x):
    c = math.sqrt(2.0 / math.pi)
    return 0.5 * x * (1.0 + jnp.tanh(c * (x + 0.044715 * (x * x * x))))


def _sigmoid(x):
    return 1.0 / (1.0 + jnp.exp(-x))


def _split3(x):
    p1 = x.astype(BF16)
    r1 = x - p1.astype(F32)
    p2 = r1.astype(BF16)
    r2 = r1 - p2.astype(F32)
    return p1, p2, r2.astype(BF16)


def _dot(a, b):
    return jnp.dot(a, b, preferred_element_type=F32)


def _split_spec(rows):
    return pl.BlockSpec((D_SSM // LANES, rows, LANES), lambda i: (0, i, 0))


def _split_store(ref, value, rows=slice(None)):
    for h in range(D_SSM // LANES):
        ref[h, rows, :] = value[:, h * LANES:(h + 1) * LANES]


def _split_load(ref, rows=slice(None)):
    return jnp.concatenate([ref[h, rows, :] for h in range(D_SSM // LANES)], axis=1)


def _bias_lane(head):
    return 0 if head % 2 else ATT_HEAD_DIM


def _inproj_body(x_ref, g_ref, w_ref, bf_ref, lng_ref, lnb_ref, ws_ref, bs_ref, place_ref, qc_ref,
                 u_ref, q_ref, k_ref, v_ref, cum_ref, ygm_ref, carry_ref, *, tiles_per_seq):
    i = pl.program_id(0)

    @pl.when(i % tiles_per_seq == 0)
    def _():
        carry_ref[...] = jnp.zeros_like(carry_ref)

    tm = x_ref.shape[0]
    hb = _rms(x_ref[...], g_ref[...]).astype(BF16)

    def proj(lo, width):
        return _dot(hb, w_ref[:, lo:lo + width])

    _split_store(u_ref, proj(C_U, D_SSM))
    v_ref[...] = proj(C_V, D_ATT).astype(BF16)

    f = proj(C_F, LANES) + bf_ref[...]
    logf = (jnp.minimum(f, 0.0) - jnp.log(1.0 + jnp.exp(-jnp.abs(f)))) * LOG2E
    row = lax.broadcasted_iota(jnp.int32, (tm, tm), 0)
    col = lax.broadcasted_iota(jnp.int32, (tm, tm), 1)
    tri = (col <= row).astype(BF16)
    p1, p2, p3 = _split3(logf)
    cum = _dot(tri, p1) + _dot(tri, p2) + _dot(tri, p3) + carry_ref[...]
    cum_ref[...] = cum
    carry_ref[...] = cum[tm - 1:tm, :]

    placed = _dot(jnp.concatenate(_split3(cum), axis=1), place_ref[...])
    upper = lax.broadcasted_iota(jnp.int32, (tm, LANES), 1) >= ATT_HEAD_DIM
    qf = proj(C_Q, D_ATT)
    kf = proj(C_K, D_ATT)
    for h in range(ATT_HEADS):
        pair = slice((h // 2) * LANES, (h // 2 + 1) * LANES)
        slot = slice(h * LANES, (h + 1) * LANES)
        own = upper if h % 2 else ~upper
        q_ref[:, slot] = jnp.where(own, qf[:, pair], qc_ref[:, slot]).astype(BF16)
        k_ref[:, slot] = jnp.where(own, kf[:, pair], placed[:, slot]).astype(BF16)

    zg = _gelu(proj(C_Z, 2 * D_GM))
    ug = zg[:, :D_GM]
    vg = zg[:, D_GM:]
    mean = jnp.mean(vg, axis=-1, keepdims=True)
    cen = vg - mean
    var = jnp.mean(cen * cen, axis=-1, keepdims=True)
    vn = (cen * lax.rsqrt(var + EPS) * lng_ref[...] + lnb_ref[...]).astype(BF16)
    lane_head = lax.broadcasted_iota(jnp.int32, (CHUNK, D_GM), 1) >> 6
    for c in range(tm // CHUNK):
        rows = slice(c * CHUNK, (c + 1) * CHUNK)
        vc = vn[rows, :]
        mixed = _dot(ws_ref[0], vc)
        for g in range(1, GM_HEADS):
            mixed = jnp.where(lane_head == g, _dot(ws_ref[g], vc), mixed)
        mixed = mixed + bs_ref[...]
        ygm_ref[rows, :] = (ug[rows, :] * mixed).astype(BF16)


def _attn_slot_constants():
    place = jnp.zeros((3 * LANES, ATT_SLOTS), F32)
    qconst = jnp.zeros((1, ATT_SLOTS), F32)
    for h in range(ATT_HEADS):
        for piece in range(3):
            lane = h * LANES + _bias_lane(h) + piece
            place = place.at[piece * LANES + h, lane].set(1.0)
            qconst = qconst.at[0, lane].set(-1.0)
    return place.astype(BF16), qconst


def _inproj(x2, g, w, bf, lng, lnb, ws, bs, *, seq, tm):
    n = x2.shape[0]
    place, qconst = _attn_slot_constants()
    row = lambda width: pl.BlockSpec((tm, width), lambda i: (i, 0))
    return pl.pallas_call(
        functools.partial(_inproj_body, tiles_per_seq=seq // tm),
        grid=(n // tm,),
        in_specs=[row(D_MODEL), _resident((1, D_MODEL)), _resident((D_MODEL, D_IN_PAD)),
                  _resident((1, LANES)), _resident((1, D_GM)), _resident((1, D_GM)),
                  _resident((GM_HEADS, CHUNK, CHUNK)), _resident((CHUNK, D_GM)),
                  _resident((3 * LANES, ATT_SLOTS)), _resident((1, ATT_SLOTS))],
        out_specs=[_split_spec(tm), row(ATT_SLOTS), row(ATT_SLOTS), row(D_ATT), row(LANES), row(D_GM)],
        out_shape=[jax.ShapeDtypeStruct((D_SSM // LANES, n, LANES), F32),
                   jax.ShapeDtypeStruct((n, ATT_SLOTS), BF16),
                   jax.ShapeDtypeStruct((n, ATT_SLOTS), BF16),
                   jax.ShapeDtypeStruct((n, D_ATT), BF16),
                   jax.ShapeDtypeStruct((n, LANES), F32),
                   jax.ShapeDtypeStruct((n, D_GM), BF16)],
        scratch_shapes=[pltpu.VMEM((1, LANES), F32)],
        compiler_params=_cparams("arbitrary"),
        name="inproj",
    )(x2, g, w, bf, lng, lnb, ws, bs, place, qconst)


def _ssm_tokens(u_ref, s, tr):
    return _split_load(u_ref, pl.ds(s, tr, stride=SSM_T)).astype(BF16)


def _ssm_state_body(u_ref, f_ref, s_ref):
    tr = s_ref.shape[0]
    acc = _dot(_ssm_tokens(u_ref, 0, tr), f_ref[:D_SSM, :])
    for s in range(1, SSM_T):
        acc += _dot(_ssm_tokens(u_ref, s, tr), f_ref[s * D_SSM:(s + 1) * D_SSM, :])
    s_ref[...] = acc


def _ssm_state(u, fmat, *, tr):
    nc = u.shape[1] // SSM_T
    return pl.pallas_call(
        _ssm_state_body,
        grid=(nc // tr,),
        in_specs=[_split_spec(tr * SSM_T), _resident((SSM_ROW, 2 * SSM_NSTATE))],
        out_specs=pl.BlockSpec((tr, 2 * SSM_NSTATE), lambda i: (i, 0)),
        out_shape=jax.ShapeDtypeStruct((nc, 2 * SSM_NSTATE), F32),
        compiler_params=_cparams("parallel"),
        name="ssm_state",
    )(u, fmat)


def _ssm_scan_body(s_ref, a_ref, xp_ref, st_ref):
    @pl.when(pl.program_id(1) == 0)
    def _():
        st_ref[...] = jnp.zeros_like(st_ref)

    tc = s_ref.shape[0]
    are = a_ref[:, :SSM_NSTATE]
    aim = a_ref[:, SSM_NSTATE:]

    def step(c8, carry):
        xre, xim = carry
        base = pl.multiple_of(c8 * SUBLANES, SUBLANES)
        blk = s_ref[pl.ds(base, SUBLANES), :]
        prev_re, prev_im = [], []
        for r in range(SUBLANES):
            prev_re.append(xre)
            prev_im.append(xim)
            sre = blk[r:r + 1, :SSM_NSTATE]
            sim = blk[r:r + 1, SSM_NSTATE:]
            xre, xim = are * xre - aim * xim + sre, are * xim + aim * xre + sim
        xp_ref[pl.ds(base, SUBLANES), :SSM_NSTATE] = jnp.concatenate(prev_re, axis=0)
        xp_ref[pl.ds(base, SUBLANES), SSM_NSTATE:] = jnp.concatenate(prev_im, axis=0)
        return xre, xim

    xre, xim = lax.fori_loop(0, tc // SUBLANES, step,
                             (st_ref[:, :SSM_NSTATE], st_ref[:, SSM_NSTATE:]))
    st_ref[:, :SSM_NSTATE] = xre
    st_ref[:, SSM_NSTATE:] = xim


def _ssm_scan(s3, a, *, tc):
    b, ncb, w = s3.shape
    spec = pl.BlockSpec((None, tc, w), lambda bi, ci: (bi, ci, 0))
    return pl.pallas_call(
        _ssm_scan_body,
        grid=(b, ncb // tc),
        in_specs=[spec, pl.BlockSpec((1, w), lambda bi, ci: (0, 0))],
        out_specs=spec,
        out_shape=jax.ShapeDtypeStruct(s3.shape, F32),
        scratch_shapes=[pltpu.VMEM((1, w), F32)],
        compiler_params=_cparams("parallel", "arbitrary"),
        name="ssm_scan",
    )(s3, a)


def _ssm_out_body(u_ref, xp_ref, m_ref, e_ref, y_ref):
    tr = xp_ref.shape[0]
    us = [_ssm_tokens(u_ref, s, tr) for s in range(SSM_T)]
    xp = xp_ref[...].astype(BF16)
    for t in range(SSM_T):
        cols = slice(t * D_SSM, (t + 1) * D_SSM)
        acc = _dot(xp, e_ref[:, cols])
        for s in range(t + 1):
            acc += _dot(us[s], m_ref[s * D_SSM:(s + 1) * D_SSM, cols])
        _split_store(y_ref, acc, pl.ds(t, tr, stride=SSM_T))


def _ssm_out(u, xp, mmat, emat, *, tr):
    nc = xp.shape[0]
    return pl.pallas_call(
        _ssm_out_body,
        grid=(nc // tr,),
        in_specs=[_split_spec(tr * SSM_T),
                  pl.BlockSpec((tr, 2 * SSM_NSTATE), lambda i: (i, 0)),
                  _resident((SSM_ROW, SSM_ROW)), _resident((2 * SSM_NSTATE, SSM_ROW))],
        out_specs=_split_spec(tr * SSM_T),
        out_shape=jax.ShapeDtypeStruct((D_SSM // LANES, nc * SSM_T, LANES), F32),
        compiler_params=_cparams("parallel"),
        name="ssm_out",
    )(u, xp, mmat, emat)


def _ssm_matrices(lam_re, lam_im, log_dt, b_re, b_im, c_re, c_im):
    t, g, p, h = SSM_T, SSM_GROUPS, SSM_STATE, SSM_GROUP
    ns, nch = g * p, g * h
    hp = lax.Precision.HIGHEST
    lr, li = lam_re.astype(F32).reshape(ns), lam_im.astype(F32).reshape(ns)
    dt = jnp.repeat(jnp.exp(log_dt.astype(F32)), p)
    steps = jnp.arange(t + 1, dtype=F32)[:, None]
    mag = jnp.exp((lr * dt)[None] * steps)
    ang = (li * dt)[None] * steps
    pr, pi = mag * jnp.cos(ang), mag * jnp.sin(ang)
    nr, ni = pr[1] - 1.0, pi[1]
    den = lr * lr + li * li
    zr, zi = ((nr * lr + ni * li) / den)[:, None], ((ni * lr - nr * li) / den)[:, None]
    br, bi = b_re.astype(F32).reshape(ns, h), b_im.astype(F32).reshape(ns, h)
    cr = c_re.astype(F32).transpose(0, 2, 1).reshape(ns, h)
    ci = c_im.astype(F32).transpose(0, 2, 1).reshape(ns, h)
    same = jnp.repeat(jnp.repeat(jnp.eye(g, dtype=F32), p, axis=0), h, axis=1)
    spread = lambda w: jnp.tile(w, (1, g)) * same
    bhr, bhi = spread(zr * br - zi * bi), spread(zr * bi + zi * br)
    chr_, chi = spread(cr), spread(ci)
    er = jnp.concatenate([pr[k][:, None] * chr_ - pi[k][:, None] * chi for k in range(t + 1)], axis=1)
    ei = jnp.concatenate([pr[k][:, None] * chi + pi[k][:, None] * chr_ for k in range(t + 1)], axis=1)

    k_all = (jnp.dot(bhr.T, er[:, :t * nch], precision=hp)
             - jnp.dot(bhi.T, ei[:, :t * nch], precision=hp))
    mmat = jnp.concatenate(
        [jnp.pad(k_all[:, :(t - s) * nch], ((0, 0), (s * nch, 0))) for s in range(t)], axis=0)

    fmat = jnp.concatenate(
        [jnp.concatenate([bhr.T * pr[t - 1 - s][None] - bhi.T * pi[t - 1 - s][None],
                          bhr.T * pi[t - 1 - s][None] + bhi.T * pr[t - 1 - s][None]], axis=1)
         for s in range(t)], axis=0)

    emat = jnp.concatenate([er[:, nch:], -ei[:, nch:]], axis=0)

    a_row = jnp.concatenate([pr[t][None], pi[t][None]], axis=1)
    return mmat.astype(BF16), fmat.astype(BF16), emat.astype(BF16), a_row


def _attn_body(q_ref, k_ref, v_ref, cq_ref, o_ref,
               m_ref, acc_ref, s_ref, mt_ref, p_ref, al_ref, *, tq):
    qi = pl.program_id(2)
    nh = ATT_STEP_HEADS
    keypos = lax.broadcasted_iota(jnp.int32, (tq, tq), 0)
    qpos = lax.broadcasted_iota(jnp.int32, (tq, tq), 1)

    def slot_lanes(j):
        return slice(j * LANES, (j + 1) * LANES)

    def scores(t, slot, masked):
        start = pl.multiple_of(t * tq, tq)
        for j in range(nh):
            kt = k_ref[pl.ds(start, tq), slot_lanes(j)]
            s = lax.dot_general(kt, q_ref[:, slot_lanes(j)], (((1,), (1,)), ((), ())),
                                preferred_element_type=F32)
            if masked:
                s = jnp.where(keypos <= qpos, s, -jnp.inf)
            s_ref[slot, j] = s
            mt_ref[slot, j] = jnp.max(s, axis=0, keepdims=True)

    def probs(slot):
        for j in range(nh):
            m = m_ref[j]
            cq = cq_ref[j:j + 1, :]
            m_new = jnp.maximum(m, mt_ref[slot, j] + cq)
            p_ref[slot, j] = jnp.exp2(s_ref[slot, j] + (cq - m_new)).astype(BF16)
            al_ref[slot, j] = jnp.exp2(m - m_new)
            m_ref[j] = m_new

    def values(t, slot):
        start = pl.multiple_of(t * tq, tq)
        for j in range(nh):
            vt = v_ref[j, :, pl.ds(start, tq)]
            acc_ref[j] = al_ref[slot, j] * acc_ref[j] + _dot(vt, p_ref[slot, j])

    def trip(t, slot, masked=False):
        values(t - 1, 1 - slot)
        probs(slot)
        scores(t + 1, 1 - slot, masked)

    for j in range(nh):
        m_ref[j] = jnp.full((1, tq), -jnp.inf, F32)
        acc_ref[j] = jnp.zeros((LANES, tq), F32)

    @pl.when(qi == 0)
    def _():
        scores(0, 0, True)
        probs(0)
        values(0, 0)

    @pl.when(qi == 1)
    def _():
        scores(0, 0, False)
        probs(0)
        scores(1, 1, True)
        values(0, 0)
        probs(1)
        values(1, 1)

    @pl.when(qi >= 2)
    def _():
        scores(0, 0, False)
        probs(0)
        scores(1, 1, False)

        def pair(u, carry):
            t = 2 * u + 1
            trip(t, 1)
            trip(t + 1, 0)
            return carry

        lax.fori_loop(0, (qi - 2) // 2, pair, 0)

        @pl.when(qi % 2 == 0)
        def _():
            trip(qi - 1, 1, masked=True)
            values(qi - 1, 1)
            probs(0)
            values(qi, 0)

        @pl.when(qi % 2 == 1)
        def _():
            trip(qi - 2, 1)
            trip(qi - 1, 0, masked=True)
            values(qi - 1, 0)
            probs(1)
            values(qi, 1)
    for pair in range(nh // 2):
        outs = []
        for j in (2 * pair, 2 * pair + 1):
            acc = acc_ref[j]
            outs.append(acc[:ATT_HEAD_DIM, :] / acc[ATT_HEAD_DIM:, :])
        o_ref[:, slot_lanes(pair)] = jnp.concatenate(outs, axis=0).T.astype(BF16)


def _attention(q3, k3, vt4, cum3, *, tq):
    b, s, _ = q3.shape
    nh = ATT_STEP_HEADS
    return pl.pallas_call(
        functools.partial(_attn_body, tq=tq),
        scratch_shapes=[pltpu.VMEM((nh, 1, tq), F32), pltpu.VMEM((nh, LANES, tq), F32),
                        pltpu.VMEM((2, nh, tq, tq), F32), pltpu.VMEM((2, nh, 1, tq), F32),
                        pltpu.VMEM((2, nh, tq, tq), BF16), pltpu.VMEM((2, nh, 1, tq), F32)],
        grid=(b, ATT_HEADS // nh, s // tq),
        in_specs=[pl.BlockSpec((None, tq, nh * LANES), lambda bi, hg, qi: (bi, qi, hg)),
                  pl.BlockSpec((None, s, nh * LANES), lambda bi, hg, qi: (bi, 0, hg)),
                  pl.BlockSpec((None, nh, LANES, s), lambda bi, hg, qi: (bi, hg, 0, 0)),
                  pl.BlockSpec((None, None, nh, tq), lambda bi, hg, qi: (bi, hg, 0, qi))],
        out_specs=pl.BlockSpec((None, tq, nh * ATT_HEAD_DIM), lambda bi, hg, qi: (bi, qi, hg)),
        out_shape=jax.ShapeDtypeStruct((b, s, D_ATT), BF16),
        compiler_params=_cparams("parallel", "parallel", "arbitrary"),
        name="attention",
    )(q3, k3, vt4, cum3)


R_E1, R_E2, R_W1, R_W2, R_RANK1, R_RANK2 = range(6)


def _route(logits, cnt_ref):
    tm = logits.shape[0]
    lane = lax.broadcasted_iota(jnp.int32, logits.shape, 1).astype(F32)
    lg = jnp.where(lane < N_EXPERTS, logits, -jnp.inf)
    m1 = jnp.max(lg, axis=-1, keepdims=True)
    i1 = jnp.min(jnp.where(lg == m1, lane, float(LANES)), axis=-1, keepdims=True)
    lg2 = jnp.where(lane == i1, -jnp.inf, lg)
    m2 = jnp.max(lg2, axis=-1, keepdims=True)
    i2 = jnp.min(jnp.where(lg2 == m2, lane, float(LANES)), axis=-1, keepdims=True)
    e = jnp.exp(m2 - m1)
    w1 = 1.0 / (1.0 + e)
    w2 = e / (1.0 + e)

    hit1, hit2 = lane == i1, lane == i2
    oh1, oh2 = hit1.astype(BF16), hit2.astype(BF16)
    row = lax.broadcasted_iota(jnp.int32, (tm, tm), 0)
    col = lax.broadcasted_iota(jnp.int32, (tm, tm), 1)
    before = (col < row).astype(BF16)
    carry = cnt_ref[...]
    tot1 = jnp.sum(oh1.astype(F32), axis=0, keepdims=True)
    tot2 = jnp.sum(oh2.astype(F32), axis=0, keepdims=True)
    pos1 = _dot(before, oh1) + carry
    pos2 = _dot(before, oh2) + (carry + tot1)
    rank1 = jnp.sum(jnp.where(hit1, pos1, 0.0), axis=-1, keepdims=True)
    rank2 = jnp.sum(jnp.where(hit2, pos2, 0.0), axis=-1, keepdims=True)
    cnt_ref[...] = carry + tot1 + tot2

    rec = jnp.zeros(logits.shape, F32)
    for idx, val in ((R_E1, i1), (R_E2, i2), (R_W1, w1), (R_W2, w2), (R_RANK1, rank1), (R_RANK2, rank2)):
        rec = jnp.where(lane == idx, val, rec)
    return rec


def _outproj_body(*refs, with_router):
    if with_router:
        (x_ref, ys_ref, us_ref, ya_ref, yg_ref, d_ref, wglu_ref, gn_ref, w_ref, fg_ref, rw_ref,
         xo_ref, h_ref, route_ref, cnt_ref) = refs

        @pl.when(pl.program_id(0) == 0)
        def _():
            cnt_ref[...] = jnp.zeros_like(cnt_ref)
    else:
        (x_ref, ys_ref, us_ref, ya_ref, yg_ref, d_ref, wglu_ref, gn_ref, w_ref, fg_ref,
         xo_ref, h_ref) = refs

    tm = x_ref.shape[0]
    part = tm // OUTPROJ_PARTS
    o1, o2 = D_SSM, D_SSM + D_ATT
    for k in range(OUTPROJ_PARTS):
        rows = slice(k * part, (k + 1) * part)
        y1 = _gelu(_split_load(ys_ref, rows) + d_ref[...] * _split_load(us_ref, rows))
        y_ssm = y1 * _sigmoid(_dot(y1.astype(BF16), wglu_ref[...]))

        acc = _dot(_rms(y_ssm, gn_ref[:, :o1]).astype(BF16), w_ref[:o1, :])
        acc += _dot(_rms(ya_ref[rows, :].astype(F32), gn_ref[:, o1:o2]).astype(BF16), w_ref[o1:o2, :])
        acc += _dot(_rms(yg_ref[rows, :].astype(F32), gn_ref[:, o2:]).astype(BF16), w_ref[o2:, :])
        xn = x_ref[rows, :] + acc
        xo_ref[rows, :] = xn
        hn = _rms(xn, fg_ref[...])
        h_ref[rows, :] = hn.astype(h_ref.dtype)
        if with_router:
            h1, h2, h3 = _split3(hn)
            r1, r2, r3 = _split3(rw_ref[...])
            logits = (_dot(h1, r1) + _dot(h1, r2) + _dot(h2, r1)
                      + _dot(h1, r3) + _dot(h2, r2) + _dot(h3, r1))
            route_ref[rows, :] = _route(logits, cnt_ref)


def _outproj(x2, ys, us, ya, yg, d, wglu, gn, w, fg, rw, *, tm):
    n = x2.shape[0]
    with_router = rw is not None
    row = lambda width: pl.BlockSpec((tm, width), lambda i: (i, 0))
    in_specs = [row(D_MODEL), _split_spec(tm), _split_spec(tm), row(D_ATT), row(D_GM),
                _resident((1, D_SSM)), _resident((D_SSM, D_SSM)), _resident((1, D_MODEL)),
                _resident((D_MODEL, D_MODEL)), _resident((1, D_MODEL))]
    out_specs = [row(D_MODEL), row(D_MODEL)]
    out_shape = [jax.ShapeDtypeStruct((n, D_MODEL), F32),
                 jax.ShapeDtypeStruct((n, D_MODEL), F32 if with_router else BF16)]
    args = [x2, ys, us, ya, yg, d, wglu, gn, w, fg]
    if with_router:
        in_specs.append(_resident((D_MODEL, LANES)))
        out_specs += [row(LANES), pl.BlockSpec((1, LANES), lambda i: (0, 0))]
        out_shape += [jax.ShapeDtypeStruct((n, LANES), F32), jax.ShapeDtypeStruct((1, LANES), F32)]
        args.append(rw)
    return pl.pallas_call(
        functools.partial(_outproj_body, with_router=with_router),
        grid=(n // tm,),
        in_specs=in_specs, out_specs=out_specs, out_shape=out_shape,
        compiler_params=_cparams("arbitrary" if with_router else "parallel"),
        name="outproj_router" if with_router else "outproj",
    )(*args)


def _swiglu_tile(h, wg, wu, wd):
    a = _dot(h, wg)
    return _dot((a * _sigmoid(a) * _dot(h, wu)).astype(BF16), wd)


def _finish(x, acc, fin_ref, o_ref):
    xn = x + acc
    o_ref[...] = xn if fin_ref is None else _rms(xn, fin_ref[...])


def _dense_ffn_body(*refs, final_norm):
    if final_norm:
        h_ref, x_ref, wg_ref, wu_ref, wd_ref, fin_ref, o_ref, acc_ref = refs
    else:
        h_ref, x_ref, wg_ref, wu_ref, wd_ref, o_ref, acc_ref = refs
        fin_ref = None
    f = pl.program_id(1)

    @pl.when(f == 0)
    def _():
        acc_ref[...] = jnp.zeros_like(acc_ref)

    acc_ref[...] += _swiglu_tile(h_ref[...], wg_ref[...], wu_ref[...], wd_ref[...])

    @pl.when(f == pl.num_programs(1) - 1)
    def _():
        _finish(x_ref[...], acc_ref[...], fin_ref, o_ref)


def _dense_ffn(h, x2, wg, wu, wd, fin, *, tm, tf):
    n = x2.shape[0]
    dff = wg.shape[1]
    row = lambda: pl.BlockSpec((tm, D_MODEL), lambda i, f: (i, 0))
    mode = dict(pipeline_mode=pl.Buffered(1)) if tf == dff else {}
    in_specs = [row(), row(),
                pl.BlockSpec((D_MODEL, tf), lambda i, f: (0, f), **mode),
                pl.BlockSpec((D_MODEL, tf), lambda i, f: (0, f), **mode),
                pl.BlockSpec((tf, D_MODEL), lambda i, f: (f, 0), **mode)]
    args = [h, x2, wg, wu, wd]
    if fin is not None:
        in_specs.append(pl.BlockSpec((1, D_MODEL), lambda i, f: (0, 0)))
        args.append(fin)
    return pl.pallas_call(
        functools.partial(_dense_ffn_body, final_norm=fin is not None),
        grid=(n // tm, dff // tf),
        in_specs=in_specs, out_specs=row(),
        out_shape=jax.ShapeDtypeStruct((n, D_MODEL), F32),
        scratch_shapes=[pltpu.VMEM((tm, D_MODEL), F32)],
        compiler_params=_cparams("parallel", "arbitrary"),
        name="dense_ffn",
    )(*args)


def _row_copy(src, src_row, dst, dst_row, sem):
    return pltpu.make_async_copy(src.at[pl.ds(src_row, 1)], dst.at[pl.ds(dst_row, 1)], sem)


def _moe_dispatch_body(dest_ref, h_ref, xs_in_ref, xs_ref, sem):
    del xs_in_ref
    tm = h_ref.shape[0]

    def issue(r, carry):
        for k in range(2):
            _row_copy(h_ref, r, xs_ref, dest_ref[k, r], sem).start()
        return carry

    lax.fori_loop(0, tm, issue, 0, unroll=8)
    for k in range(2):
        pltpu.make_async_copy(h_ref, xs_ref.at[pl.ds(0, tm)], sem).wait()


def _moe_dispatch(dest, h, xs_zero, *, tm):
    n = h.shape[0]
    return pl.pallas_call(
        _moe_dispatch_body,
        grid=(n // tm,),
        in_specs=[pl.BlockSpec((None, 2, tm), lambda i: (i, 0, 0), memory_space=pltpu.SMEM),
                  pl.BlockSpec((tm, D_MODEL), lambda i: (i, 0)),
                  pl.BlockSpec(memory_space=pl.ANY)],
        out_specs=pl.BlockSpec(memory_space=pl.ANY),
        out_shape=jax.ShapeDtypeStruct(xs_zero.shape, xs_zero.dtype),
        scratch_shapes=[pltpu.SemaphoreType.DMA(())],
        input_output_aliases={2: 0},
        compiler_params=pltpu.CompilerParams(dimension_semantics=("arbitrary",),
                                             vmem_limit_bytes=VMEM_LIMIT, has_side_effects=True),
        name="moe_dispatch",
    )(dest, h, xs_zero)


def _moe_ffn_body(te_ref, nu_ref, xs_ref, wg_ref, wu_ref, wd_ref, ys_ref, acc_ref):
    del te_ref
    i = pl.program_id(0)
    f = pl.program_id(1)

    @pl.when(i < nu_ref[0])
    def _():
        @pl.when(f == 0)
        def _():
            acc_ref[...] = jnp.zeros_like(acc_ref)

        acc_ref[...] += _swiglu_tile(xs_ref[...].astype(BF16), wg_ref[...], wu_ref[...], wd_ref[...])

        @pl.when(f == pl.num_programs(1) - 1)
        def _():
            ys_ref[...] = acc_ref[...]

    @pl.when((i >= nu_ref[0]) & (f == pl.num_programs(1) - 1))
    def _():
        ys_ref[...] = jnp.zeros_like(ys_ref)


def _moe_ffn(tile_expert, n_used, xs, wg, wu, wd, *, tme, tf):
    rows = xs.shape[0]
    dff = wg.shape[-1]
    nf = dff // tf
    row_idx = lambda i, f, te, nu: (jnp.minimum(i, nu[0] - 1), 0)
    col = lambda i, f, nu: jnp.where(i < nu[0], f, nf - 1)
    grid_spec = pltpu.PrefetchScalarGridSpec(
        num_scalar_prefetch=2,
        grid=(rows // tme, nf),
        in_specs=[pl.BlockSpec((tme, D_MODEL), row_idx),
                  pl.BlockSpec((None, D_MODEL, tf), lambda i, f, te, nu: (te[i], 0, col(i, f, nu))),
                  pl.BlockSpec((None, D_MODEL, tf), lambda i, f, te, nu: (te[i], 0, col(i, f, nu))),
                  pl.BlockSpec((None, tf, D_MODEL), lambda i, f, te, nu: (te[i], col(i, f, nu), 0))],
        out_specs=pl.BlockSpec((tme, D_MODEL), lambda i, f, te, nu: (i, 0)),
        scratch_shapes=[pltpu.VMEM((tme, D_MODEL), F32)])
    return pl.pallas_call(
        _moe_ffn_body,
        grid_spec=grid_spec,
        out_shape=jax.ShapeDtypeStruct((rows, D_MODEL), F32),
        compiler_params=_cparams("arbitrary", "arbitrary"),
        name="moe_ffn",
    )(tile_expert, n_used, xs, wg, wu, wd)


def _moe_combine_body(*refs, final_norm):
    if final_norm:
        dest_ref, x_ref, route_ref, ys_ref, fin_ref, o_ref, buf_ref, sem = refs
    else:
        dest_ref, x_ref, route_ref, ys_ref, o_ref, buf_ref, sem = refs
        fin_ref = None
    tm = x_ref.shape[0]

    def issue(r, carry):
        for k in range(2):
            _row_copy(ys_ref, dest_ref[k, r], buf_ref.at[k], r, sem).start()
        return carry

    lax.fori_loop(0, tm, issue, 0, unroll=8)
    for k in range(2):
        pltpu.make_async_copy(ys_ref.at[pl.ds(0, tm)], buf_ref.at[k], sem).wait()

    route = route_ref[...]
    lane = lax.broadcasted_iota(jnp.int32, route.shape, 1)
    w1 = jnp.sum(jnp.where(lane == R_W1, route, 0.0), axis=-1, keepdims=True)
    w2 = jnp.sum(jnp.where(lane == R_W2, route, 0.0), axis=-1, keepdims=True)
    _finish(x_ref[...], w1 * buf_ref[0] + w2 * buf_ref[1], fin_ref, o_ref)


def _moe_combine(dest, x2, route, ys, fin, *, tm):
    n = x2.shape[0]
    row = lambda width: pl.BlockSpec((tm, width), lambda i: (i, 0))
    in_specs = [pl.BlockSpec((None, 2, tm), lambda i: (i, 0, 0), memory_space=pltpu.SMEM),
                row(D_MODEL), row(LANES), pl.BlockSpec(memory_space=pl.ANY)]
    args = [dest, x2, route, ys]
    if fin is not None:
        in_specs.append(pl.BlockSpec((1, D_MODEL), lambda i: (0, 0)))
        args.append(fin)
    return pl.pallas_call(
        functools.partial(_moe_combine_body, final_norm=fin is not None),
        grid=(n // tm,),
        in_specs=in_specs, out_specs=row(D_MODEL),
        out_shape=jax.ShapeDtypeStruct((n, D_MODEL), F32),
        scratch_shapes=[pltpu.VMEM((2, tm, D_MODEL), F32), pltpu.SemaphoreType.DMA(())],
        compiler_params=_cparams("arbitrary"),
        name="moe_combine",
    )(*args)


def _moe_layout(route, counts, *, tm, tme):
    n = route.shape[0]
    cnt = counts[0, :N_EXPERTS].astype(jnp.int32)
    padded = (cnt + (tme - 1)) // tme * tme
    ends = jnp.cumsum(padded)
    starts = ends - padded
    experts = jnp.arange(N_EXPERTS, dtype=F32)

    def rows(e_lane, rank_lane):
        start = jnp.sum(jnp.where(route[:, e_lane, None] == experts[None], starts[None], 0), axis=1)
        return start + route[:, rank_lane].astype(jnp.int32)

    dest = jnp.stack([rows(R_E1, R_RANK1), rows(R_E2, R_RANK2)])
    dest = dest.reshape(2, n // tm, tm).transpose(1, 0, 2)
    n_tiles = (2 * n) // tme + N_EXPERTS
    tile_start = jnp.arange(n_tiles, dtype=jnp.int32) * tme
    tile_expert = jnp.minimum(jnp.sum(tile_start[:, None] >= ends[None, :], axis=1), N_EXPERTS - 1)
    n_used = (ends[-1] // tme).reshape(1)
    tile_expert = jnp.where(tile_start < ends[-1], tile_expert, tile_expert[jnp.maximum(n_used[0] - 1, 0)])
    return dest, tile_expert.astype(jnp.int32), n_used.astype(jnp.int32), n_tiles * tme


def _largest_tile(total, cap, mult):
    best = mult
    for t in range(mult, min(total, cap) + 1, mult):
        if total % t == 0:
            best = t
    return best


def _mxu_tile(total, cap):
    if total % MXU_WIDTH == 0:
        return _largest_tile(total, cap, MXU_WIDTH)
    return _largest_tile(total, cap, LANES)


def _pad_lanes(a):
    return jnp.pad(a, [(0, 0)] * (a.ndim - 1) + [(0, LANES - a.shape[-1])])


def kernel(x, mix_norm_g, w_in, b_forget, ssm_lambda_re, ssm_lambda_im, ssm_log_dt, ssm_b_re, ssm_b_im, ssm_c_re, ssm_c_im, ssm_d, ssm_w_glu, gm_ln_g, gm_ln_b, gm_w_s, gm_b_s, group_norm_g, w_out, ffn_norm_g, dense_w_gate, dense_w_up, dense_w_down, router_w, moe_w_gate, moe_w_up, moe_w_down, final_norm_g):
    bsz, seq, _ = x.shape
    n = bsz * seq
    depth = w_in.shape[0]
    assert seq % CHUNK == 0 and seq % SSM_T == 0
    tm = _largest_tile(seq, 512, CHUNK)
    tq = _largest_tile(seq, 512, LANES)
    nc = n // SSM_T
    ncb = seq // SSM_T
    tr = _largest_tile(nc, 512, 8)
    tc = _largest_tile(ncb, 128, 8)

    x2 = x.reshape(n, D_MODEL).astype(F32)
    tril = jnp.tril(jnp.ones((CHUNK, CHUNK), F32))
    i0, i1, i2, i3, i4 = (D_SSM, D_SSM + D_ATT, D_SSM + 2 * D_ATT, D_SSM + 3 * D_ATT,
                          D_SSM + 3 * D_ATT + ATT_HEADS)
    for layer in range(depth):
        wl = w_in[layer]
        w_re = jnp.concatenate(
            [wl[:, :i0], wl[:, i0:i1] * (ATT_HEAD_DIM ** -0.5 * LOG2E), wl[:, i1:i3], wl[:, i4:],
             _pad_lanes(wl[:, i3:i4])], axis=1).astype(BF16)
        ws = (gm_w_s[layer].astype(F32) * tril[None]).astype(BF16)
        bs = jnp.repeat(gm_b_s[layer].astype(F32).T, GM_HEAD_DIM, axis=1)
        u, q, k, v, cum, ygm = _inproj(
            x2, mix_norm_g[layer].reshape(1, D_MODEL), w_re,
            _pad_lanes(b_forget[layer].reshape(1, ATT_HEADS).astype(F32)),
            gm_ln_g[layer].reshape(1, D_GM), gm_ln_b[layer].reshape(1, D_GM), ws, bs,
            seq=seq, tm=tm)

        mmat, fmat, emat, a_row = _ssm_matrices(
            ssm_lambda_re[layer], ssm_lambda_im[layer], ssm_log_dt[layer], ssm_b_re[layer],
            ssm_b_im[layer], ssm_c_re[layer], ssm_c_im[layer])
        s_loc = _ssm_state(u, fmat, tr=tr)
        xprev = _ssm_scan(s_loc.reshape(bsz, ncb, 2 * SSM_NSTATE), a_row, tc=tc)
        y_ssm = _ssm_out(u, xprev.reshape(nc, 2 * SSM_NSTATE), mmat, emat, tr=tr)

        cum_rows = cum[:, :ATT_HEADS].reshape(bsz, seq, ATT_HEADS // ATT_STEP_HEADS, ATT_STEP_HEADS)
        cum_rows = cum_rows.transpose(0, 2, 3, 1)
        vt = v.reshape(bsz, seq, ATT_HEADS, ATT_HEAD_DIM).transpose(0, 2, 3, 1)
        vt = jnp.concatenate([vt, jnp.ones_like(vt)], axis=2)
        y_att = _attention(q.reshape(bsz, seq, ATT_SLOTS), k.reshape(bsz, seq, ATT_SLOTS), vt,
                           cum_rows, tq=tq).reshape(n, D_ATT)

        is_moe = layer % 2 == 1
        j = layer // 2
        rw = _pad_lanes(router_w[j].astype(F32)) if is_moe else None
        outs = _outproj(x2, y_ssm, u, y_att, ygm, ssm_d[layer].reshape(1, D_SSM).astype(F32),
                        ssm_w_glu[layer].astype(BF16), group_norm_g[layer].reshape(1, D_MODEL),
                        w_out[layer].astype(BF16), ffn_norm_g[layer].reshape(1, D_MODEL), rw, tm=tm)
        fin = final_norm_g.reshape(1, D_MODEL) if layer == depth - 1 else None
        if is_moe:
            x_mid, h, route, counts = outs
            dffe = moe_w_gate.shape[-1]
            tme = min(MOE_ROW_TILE, 2 * n)
            dest, tile_expert, n_used, rows = _moe_layout(route, counts, tm=tm, tme=tme)
            xs = _moe_dispatch(dest, h, jnp.zeros((rows, D_MODEL), F32), tm=tm)
            ys = _moe_ffn(tile_expert, n_used, xs, moe_w_gate[j].astype(BF16),
                          moe_w_up[j].astype(BF16), moe_w_down[j].astype(BF16), tme=tme,
                          tf=_mxu_tile(dffe, 1792))
            x2 = _moe_combine(dest, x_mid, route, ys, fin, tm=tm)
        else:
            x_mid, h = outs
            dff = dense_w_gate.shape[-1]
            x2 = _dense_ffn(h, x_mid, dense_w_gate[j].astype(BF16), dense_w_up[j].astype(BF16),
                            dense_w_down[j].astype(BF16), fin, tm=tm,
                            tf=_mxu_tile(dff, 2816))
    return x2.reshape(bsz, seq, D_MODEL).astype(x.dtype)
```

```python
import functools
import math

import jax
import jax.numpy as jnp
from jax import lax
from jax.experimental import pallas as pl
from jax.experimental.pallas import tpu as pltpu

F32 = jnp.float32
BF16 = jnp.bfloat16

D_MODEL = 1024
D_SSM = 256
D_ATT = 512
D_GM = 256
SSM_GROUP = 16
SSM_GROUPS = 16
SSM_STATE = 64
ATT_HEADS = 8
ATT_HEAD_DIM = 64
ATT_SLOTS = ATT_HEADS * 128
GM_HEADS = 4
GM_HEAD_DIM = 64
CHUNK = 128
N_EXPERTS = 8
EPS = 1e-6
LOG2E = math.log2(math.e)

LANES = 128
SUBLANES = 8
MXU_WIDTH = 256
MOE_ROW_TILE = 512
OUTPROJ_PARTS = 2
ATT_STEP_HEADS = 2
SSM_T = 8
SSM_ROW = SSM_T * D_SSM
SSM_NSTATE = SSM_GROUPS * SSM_STATE

C_U = 0
C_Q = C_U + D_SSM
C_K = C_Q + D_ATT
C_V = C_K + D_ATT
C_Z = C_V + D_ATT
C_F = C_Z + 2 * D_GM
D_IN_PAD = C_F + LANES

VMEM_LIMIT = 56 * 1024 * 1024


def _cparams(*sem):
    return pltpu.CompilerParams(dimension_semantics=sem, vmem_limit_bytes=VMEM_LIMIT)


def _resident(shape):
    nd = len(shape)
    return pl.BlockSpec(shape, lambda *_: (0,) * nd, pipeline_mode=pl.Buffered(1))


def _rms(x, g):
    return x * lax.rsqrt(jnp.mean(x * x, axis=-1, keepdims=True) + EPS) * g


def _gelu(x):
    c = math.sqrt(2.0 / math.pi)
    return 0.5 * x * (1.0 + jnp.tanh(c * (x + 0.044715 * (x * x * x))))


def _sigmoid(x):
    return 1.0 / (1.0 + jnp.exp(-x))


def _split3(x):
    p1 = x.astype(BF16)
    r1 = x - p1.astype(F32)
    p2 = r1.astype(BF16)
    r2 = r1 - p2.astype(F32)
    return p1, p2, r2.astype(BF16)


def _dot(a, b):
    return jnp.dot(a, b, preferred_element_type=F32)


def _split_spec(rows):
    return pl.BlockSpec((D_SSM // LANES, rows, LANES), lambda i: (0, i, 0))


def _split_store(ref, value, rows=slice(None)):
    for h in range(D_SSM // LANES):
        ref[h, rows, :] = value[:, h * LANES:(h + 1) * LANES]


def _split_load(ref, rows=slice(None)):
    return jnp.concatenate([ref[h, rows, :] for h in range(D_SSM // LANES)], axis=1)


def _bias_lane(head):
    return 0 if head % 2 else ATT_HEAD_DIM


def _inproj_body(x_ref, g_ref, w_ref, wvt_ref, bf_ref, lng_ref, lnb_ref, ws_ref, bs_ref, place_ref, qc_ref,
                 u_ref, q_ref, k_ref, v_ref, cum_ref, ygm_ref, carry_ref, *, tiles_per_seq):
    i = pl.program_id(0)

    @pl.when(i % tiles_per_seq == 0)
    def _():
        carry_ref[...] = jnp.zeros_like(carry_ref)

    tm = x_ref.shape[0]
    hb = _rms(x_ref[...], g_ref[...]).astype(BF16)

    def proj(lo, width):
        return _dot(hb, w_ref[:, lo:lo + width])

    _split_store(u_ref, proj(C_U, D_SSM))
    v_ref[...] = lax.dot_general(wvt_ref[...], hb, (((1,), (1,)), ((), ())),
                                 preferred_element_type=F32).astype(BF16)

    f = proj(C_F, LANES) + bf_ref[...]
    logf = (jnp.minimum(f, 0.0) - jnp.log(1.0 + jnp.exp(-jnp.abs(f)))) * LOG2E
    row = lax.broadcasted_iota(jnp.int32, (tm, tm), 0)
    col = lax.broadcasted_iota(jnp.int32, (tm, tm), 1)
    tri = (col <= row).astype(BF16)
    sums = _dot(tri, jnp.concatenate(_split3(logf), axis=1))
    cum = sums[:, :LANES] + sums[:, LANES:2 * LANES] + sums[:, 2 * LANES:] + carry_ref[...]
    cum_ref[...] = cum
    carry_ref[...] = cum[tm - 1:tm, :]

    lane = lax.broadcasted_iota(jnp.int32, (tm, LANES), 1)
    c1, c2, c3 = [jnp.where(lane < ATT_HEADS, piece.astype(F32), 0.0) for piece in _split3(cum)]
    packed = c1 + pltpu.roll(c2, ATT_HEADS, axis=1) + pltpu.roll(c3, 2 * ATT_HEADS, axis=1)
    placed = _dot(packed.astype(BF16), place_ref[...])
    upper = lax.broadcasted_iota(jnp.int32, (tm, LANES), 1) >= ATT_HEAD_DIM
    qf = proj(C_Q, D_ATT)
    kf = proj(C_K, D_ATT)
    for h in range(ATT_HEADS):
        pair = slice((h // 2) * LANES, (h // 2 + 1) * LANES)
        slot = slice(h * LANES, (h + 1) * LANES)
        own = upper if h % 2 else ~upper
        q_ref[:, slot] = jnp.where(own, qf[:, pair], qc_ref[:, slot]).astype(BF16)
        k_ref[:, slot] = jnp.where(own, kf[:, pair], placed[:, slot]).astype(BF16)

    zg = _gelu(proj(C_Z, 2 * D_GM))
    ug = zg[:, :D_GM]
    vg = zg[:, D_GM:]
    mean = jnp.mean(vg, axis=-1, keepdims=True)
    cen = vg - mean
    var = jnp.mean(cen * cen, axis=-1, keepdims=True)
    vn = (cen * lax.rsqrt(var + EPS) * lng_ref[...] + lnb_ref[...]).astype(BF16)
    lane_head = lax.broadcasted_iota(jnp.int32, (CHUNK, D_GM), 1) >> 6
    for c in range(tm // CHUNK):
        rows = slice(c * CHUNK, (c + 1) * CHUNK)
        vc = vn[rows, :]
        mixed = _dot(ws_ref[0], vc)
        for g in range(1, GM_HEADS):
            mixed = jnp.where(lane_head == g, _dot(ws_ref[g], vc), mixed)
        mixed = mixed + bs_ref[...]
        ygm_ref[rows, :] = (ug[rows, :] * mixed).astype(BF16)


def _attn_slot_constants():
    place = jnp.zeros((LANES, ATT_SLOTS), F32)
    qconst = jnp.zeros((1, ATT_SLOTS), F32)
    for h in range(ATT_HEADS):
        for piece in range(3):
            lane = h * LANES + _bias_lane(h) + piece
            place = place.at[piece * ATT_HEADS + h, lane].set(1.0)
            qconst = qconst.at[0, lane].set(-1.0)
    return place.astype(BF16), qconst


def _inproj(x2, g, w, bf, lng, lnb, ws, bs, *, seq, tm):
    n = x2.shape[0]
    tps = seq // tm
    place, qconst = _attn_slot_constants()
    wvt = w[:, C_V:C_V + D_ATT].T
    row = lambda width: pl.BlockSpec((tm, width), lambda i: (i, 0))
    return pl.pallas_call(
        functools.partial(_inproj_body, tiles_per_seq=tps),
        grid=(n // tm,),
        in_specs=[row(D_MODEL), _resident((1, D_MODEL)), _resident((D_MODEL, D_IN_PAD)),
                  _resident((D_ATT, D_MODEL)),
                  _resident((1, LANES)), _resident((1, D_GM)), _resident((1, D_GM)),
                  _resident((GM_HEADS, CHUNK, CHUNK)), _resident((CHUNK, D_GM)),
                  _resident((LANES, ATT_SLOTS)), _resident((1, ATT_SLOTS))],
        out_specs=[_split_spec(tm), row(ATT_SLOTS), row(ATT_SLOTS),
                   pl.BlockSpec((None, D_ATT, tm), lambda i: (i // tps, 0, i % tps)),
                   row(LANES), row(D_GM)],
        out_shape=[jax.ShapeDtypeStruct((D_SSM // LANES, n, LANES), F32),
                   jax.ShapeDtypeStruct((n, ATT_SLOTS), BF16),
                   jax.ShapeDtypeStruct((n, ATT_SLOTS), BF16),
                   jax.ShapeDtypeStruct((n // seq, D_ATT, seq), BF16),
                   jax.ShapeDtypeStruct((n, LANES), F32),
                   jax.ShapeDtypeStruct((n, D_GM), BF16)],
        scratch_shapes=[pltpu.VMEM((1, LANES), F32)],
        compiler_params=_cparams("arbitrary"),
        name="inproj",
    )(x2, g, w, wvt, bf, lng, lnb, ws, bs, place, qconst)


def _ssm_tokens(u_ref, s, tr):
    return _split_load(u_ref, pl.ds(s, tr, stride=SSM_T)).astype(BF16)


def _ssm_state_body(u_ref, f_ref, s_ref):
    tr = s_ref.shape[0]
    acc = _dot(_ssm_tokens(u_ref, 0, tr), f_ref[:D_SSM, :])
    for s in range(1, SSM_T):
        acc += _dot(_ssm_tokens(u_ref, s, tr), f_ref[s * D_SSM:(s + 1) * D_SSM, :])
    s_ref[...] = acc


def _ssm_state(u, fmat, *, tr):
    nc = u.shape[1] // SSM_T
    return pl.pallas_call(
        _ssm_state_body,
        grid=(nc // tr,),
        in_specs=[_split_spec(tr * SSM_T), _resident((SSM_ROW, 2 * SSM_NSTATE))],
        out_specs=pl.BlockSpec((tr, 2 * SSM_NSTATE), lambda i: (i, 0)),
        out_shape=jax.ShapeDtypeStruct((nc, 2 * SSM_NSTATE), F32),
        compiler_params=_cparams("parallel"),
        name="ssm_state",
    )(u, fmat)


def _ssm_scan_body(s_ref, a_ref, xp_ref, st_ref):
    @pl.when(pl.program_id(1) == 0)
    def _():
        st_ref[...] = jnp.zeros_like(st_ref)

    tc = s_ref.shape[0]
    are = a_ref[:, :SSM_NSTATE]
    aim = a_ref[:, SSM_NSTATE:]

    def step(c8, carry):
        xre, xim = carry
        base = pl.multiple_of(c8 * SUBLANES, SUBLANES)
        blk = s_ref[pl.ds(base, SUBLANES), :]
        prev_re, prev_im = [], []
        for r in range(SUBLANES):
            prev_re.append(xre)
            prev_im.append(xim)
            sre = blk[r:r + 1, :SSM_NSTATE]
            sim = blk[r:r + 1, SSM_NSTATE:]
            xre, xim = are * xre - aim * xim + sre, are * xim + aim * xre + sim
        xp_ref[pl.ds(base, SUBLANES), :SSM_NSTATE] = jnp.concatenate(prev_re, axis=0)
        xp_ref[pl.ds(base, SUBLANES), SSM_NSTATE:] = jnp.concatenate(prev_im, axis=0)
        return xre, xim

    xre, xim = lax.fori_loop(0, tc // SUBLANES, step,
                             (st_ref[:, :SSM_NSTATE], st_ref[:, SSM_NSTATE:]))
    st_ref[:, :SSM_NSTATE] = xre
    st_ref[:, SSM_NSTATE:] = xim


def _ssm_scan(s3, a, *, tc):
    b, ncb, w = s3.shape
    spec = pl.BlockSpec((None, tc, w), lambda bi, ci: (bi, ci, 0))
    return pl.pallas_call(
        _ssm_scan_body,
        grid=(b, ncb // tc),
        in_specs=[spec, pl.BlockSpec((1, w), lambda bi, ci: (0, 0))],
        out_specs=spec,
        out_shape=jax.ShapeDtypeStruct(s3.shape, F32),
        scratch_shapes=[pltpu.VMEM((1, w), F32)],
        compiler_params=_cparams("parallel", "arbitrary"),
        name="ssm_scan",
    )(s3, a)


def _ssm_out_body(u_ref, xp_ref, m_ref, e_ref, y_ref):
    tr = xp_ref.shape[0]
    us = [_ssm_tokens(u_ref, s, tr) for s in range(SSM_T)]
    xp = xp_ref[...].astype(BF16)
    for t in range(SSM_T):
        cols = slice(t * D_SSM, (t + 1) * D_SSM)
        acc = _dot(xp, e_ref[:, cols])
        for s in range(t + 1):
            acc += _dot(us[s], m_ref[s * D_SSM:(s + 1) * D_SSM, cols])
        _split_store(y_ref, acc, pl.ds(t, tr, stride=SSM_T))


def _ssm_out(u, xp, mmat, emat, *, tr):
    nc = xp.shape[0]
    return pl.pallas_call(
        _ssm_out_body,
        grid=(nc // tr,),
        in_specs=[_split_spec(tr * SSM_T),
                  pl.BlockSpec((tr, 2 * SSM_NSTATE), lambda i: (i, 0)),
                  _resident((SSM_ROW, SSM_ROW)), _resident((2 * SSM_NSTATE, SSM_ROW))],
        out_specs=_split_spec(tr * SSM_T),
        out_shape=jax.ShapeDtypeStruct((D_SSM // LANES, nc * SSM_T, LANES), F32),
        compiler_params=_cparams("parallel"),
        name="ssm_out",
    )(u, xp, mmat, emat)


def _ssm_matrices(lam_re, lam_im, log_dt, b_re, b_im, c_re, c_im):
    t, g, p, h = SSM_T, SSM_GROUPS, SSM_STATE, SSM_GROUP
    ns, nch = g * p, g * h
    hp = lax.Precision.HIGHEST
    lr, li = lam_re.astype(F32).reshape(ns), lam_im.astype(F32).reshape(ns)
    dt = jnp.repeat(jnp.exp(log_dt.astype(F32)), p)
    steps = jnp.arange(t + 1, dtype=F32)[:, None]
    mag = jnp.exp((lr * dt)[None] * steps)
    ang = (li * dt)[None] * steps
    pr, pi = mag * jnp.cos(ang), mag * jnp.sin(ang)
    nr, ni = pr[1] - 1.0, pi[1]
    den = lr * lr + li * li
    zr, zi = ((nr * lr + ni * li) / den)[:, None], ((ni * lr - nr * li) / den)[:, None]
    br, bi = b_re.astype(F32).reshape(ns, h), b_im.astype(F32).reshape(ns, h)
    cr = c_re.astype(F32).transpose(0, 2, 1).reshape(ns, h)
    ci = c_im.astype(F32).transpose(0, 2, 1).reshape(ns, h)
    same = jnp.repeat(jnp.repeat(jnp.eye(g, dtype=F32), p, axis=0), h, axis=1)
    spread = lambda w: jnp.tile(w, (1, g)) * same
    bhr, bhi = spread(zr * br - zi * bi), spread(zr * bi + zi * br)
    chr_, chi = spread(cr), spread(ci)
    er = jnp.concatenate([pr[k][:, None] * chr_ - pi[k][:, None] * chi for k in range(t + 1)], axis=1)
    ei = jnp.concatenate([pr[k][:, None] * chi + pi[k][:, None] * chr_ for k in range(t + 1)], axis=1)

    k_all = (jnp.dot(bhr.T, er[:, :t * nch], precision=hp)
             - jnp.dot(bhi.T, ei[:, :t * nch], precision=hp))
    mmat = jnp.concatenate(
        [jnp.pad(k_all[:, :(t - s) * nch], ((0, 0), (s * nch, 0))) for s in range(t)], axis=0)

    fmat = jnp.concatenate(
        [jnp.concatenate([bhr.T * pr[t - 1 - s][None] - bhi.T * pi[t - 1 - s][None],
                          bhr.T * pi[t - 1 - s][None] + bhi.T * pr[t - 1 - s][None]], axis=1)
         for s in range(t)], axis=0)

    emat = jnp.concatenate([er[:, nch:], -ei[:, nch:]], axis=0)

    a_row = jnp.concatenate([pr[t][None], pi[t][None]], axis=1)
    return mmat.astype(BF16), fmat.astype(BF16), emat.astype(BF16), a_row


def _attn_body(q_ref, k_ref, v_ref, cq_ref, o_ref,
               m_ref, acc_ref, s_ref, mt_ref, p_ref, al_ref, *, tq):
    qi = pl.program_id(2)
    nh = ATT_STEP_HEADS
    keypos = lax.broadcasted_iota(jnp.int32, (tq, tq), 0)
    qpos = lax.broadcasted_iota(jnp.int32, (tq, tq), 1)

    def slot_lanes(j):
        return slice(j * LANES, (j + 1) * LANES)

    def scores(t, slot, masked):
        start = pl.multiple_of(t * tq, tq)
        for j in range(nh):
            kt = k_ref[pl.ds(start, tq), slot_lanes(j)]
            s = lax.dot_general(kt, q_ref[:, slot_lanes(j)], (((1,), (1,)), ((), ())),
                                preferred_element_type=F32)
            if masked:
                s = jnp.where(keypos <= qpos, s, -jnp.inf)
            s_ref[slot, j] = s
            mt_ref[slot, j] = jnp.max(s, axis=0, keepdims=True)

    def probs(slot):
        for j in range(nh):
            m = m_ref[j]
            cq = cq_ref[j:j + 1, :]
            m_new = jnp.maximum(m, mt_ref[slot, j] + cq)
            p_ref[slot, j] = jnp.exp2(s_ref[slot, j] + (cq - m_new)).astype(BF16)
            al_ref[slot, j] = jnp.exp2(m - m_new)
            m_ref[j] = m_new

    def values(t, slot):
        start = pl.multiple_of(t * tq, tq)
        for j in range(nh):
            vt = v_ref[j * ATT_HEAD_DIM:(j + 1) * ATT_HEAD_DIM, pl.ds(start, tq)]
            vt = jnp.concatenate([vt, jnp.ones_like(vt)], axis=0)
            acc_ref[j] = al_ref[slot, j] * acc_ref[j] + _dot(vt, p_ref[slot, j])

    def trip(t, slot, masked=False):
        values(t - 1, 1 - slot)
        probs(slot)
        scores(t + 1, 1 - slot, masked)

    for j in range(nh):
        m_ref[j] = jnp.full((1, tq), -jnp.inf, F32)
        acc_ref[j] = jnp.zeros((LANES, tq), F32)

    @pl.when(qi == 0)
    def _():
        scores(0, 0, True)
        probs(0)
        values(0, 0)

    @pl.when(qi == 1)
    def _():
        scores(0, 0, False)
        probs(0)
        scores(1, 1, True)
        values(0, 0)
        probs(1)
        values(1, 1)

    @pl.when(qi >= 2)
    def _():
        scores(0, 0, False)
        probs(0)
        scores(1, 1, False)

        def pair(u, carry):
            t = 2 * u + 1
            trip(t, 1)
            trip(t + 1, 0)
            return carry

        lax.fori_loop(0, (qi - 2) // 2, pair, 0)

        @pl.when(qi % 2 == 0)
        def _():
            trip(qi - 1, 1, masked=True)
            values(qi - 1, 1)
            probs(0)
            values(qi, 0)

        @pl.when(qi % 2 == 1)
        def _():
            trip(qi - 2, 1)
            trip(qi - 1, 0, masked=True)
            values(qi - 1, 0)
            probs(1)
            values(qi, 1)
    for pair in range(nh // 2):
        outs = []
        for j in (2 * pair, 2 * pair + 1):
            acc = acc_ref[j]
            outs.append(acc[:ATT_HEAD_DIM, :] / acc[ATT_HEAD_DIM:, :])
        o_ref[:, slot_lanes(pair)] = jnp.concatenate(outs, axis=0).T.astype(BF16)


def _attention(q3, k3, vt4, cum3, *, tq):
    b, s, _ = q3.shape
    nh = ATT_STEP_HEADS
    return pl.pallas_call(
        functools.partial(_attn_body, tq=tq),
        scratch_shapes=[pltpu.VMEM((nh, 1, tq), F32), pltpu.VMEM((nh, LANES, tq), F32),
                        pltpu.VMEM((2, nh, tq, tq), F32), pltpu.VMEM((2, nh, 1, tq), F32),
                        pltpu.VMEM((2, nh, tq, tq), BF16), pltpu.VMEM((2, nh, 1, tq), F32)],
        grid=(b, ATT_HEADS // nh, s // tq),
        in_specs=[pl.BlockSpec((None, tq, nh * LANES), lambda bi, hg, qi: (bi, qi, hg)),
                  pl.BlockSpec((None, s, nh * LANES), lambda bi, hg, qi: (bi, 0, hg)),
                  pl.BlockSpec((None, nh * ATT_HEAD_DIM, s), lambda bi, hg, qi: (bi, hg, 0)),
                  pl.BlockSpec((None, None, nh, tq), lambda bi, hg, qi: (bi, hg, 0, qi))],
        out_specs=pl.BlockSpec((None, tq, nh * ATT_HEAD_DIM), lambda bi, hg, qi: (bi, qi, hg)),
        out_shape=jax.ShapeDtypeStruct((b, s, D_ATT), BF16),
        compiler_params=_cparams("parallel", "parallel", "arbitrary"),
        name="attention",
    )(q3, k3, vt4, cum3)


R_E1, R_E2, R_W1, R_W2, R_RANK1, R_RANK2 = range(6)


def _route(logits, cnt_ref):
    tm = logits.shape[0]
    lane = lax.broadcasted_iota(jnp.int32, logits.shape, 1).astype(F32)
    lg = jnp.where(lane < N_EXPERTS, logits, -jnp.inf)
    m1 = jnp.max(lg, axis=-1, keepdims=True)
    i1 = jnp.min(jnp.where(lg == m1, lane, float(LANES)), axis=-1, keepdims=True)
    lg2 = jnp.where(lane == i1, -jnp.inf, lg)
    m2 = jnp.max(lg2, axis=-1, keepdims=True)
    i2 = jnp.min(jnp.where(lg2 == m2, lane, float(LANES)), axis=-1, keepdims=True)
    e = jnp.exp(m2 - m1)
    w1 = 1.0 / (1.0 + e)
    w2 = e / (1.0 + e)

    hit1, hit2 = lane == i1, lane == i2
    oh1, oh2 = hit1.astype(BF16), hit2.astype(BF16)
    row = lax.broadcasted_iota(jnp.int32, (tm, tm), 0)
    col = lax.broadcasted_iota(jnp.int32, (tm, tm), 1)
    before = (col < row).astype(BF16)
    carry = cnt_ref[...]
    tot1 = jnp.sum(oh1.astype(F32), axis=0, keepdims=True)
    tot2 = jnp.sum(oh2.astype(F32), axis=0, keepdims=True)
    pos1 = _dot(before, oh1) + carry
    pos2 = _dot(before, oh2) + (carry + tot1)
    rank1 = jnp.sum(jnp.where(hit1, pos1, 0.0), axis=-1, keepdims=True)
    rank2 = jnp.sum(jnp.where(hit2, pos2, 0.0), axis=-1, keepdims=True)
    cnt_ref[...] = carry + tot1 + tot2

    rec = jnp.zeros(logits.shape, F32)
    for idx, val in ((R_E1, i1), (R_E2, i2), (R_W1, w1), (R_W2, w2), (R_RANK1, rank1), (R_RANK2, rank2)):
        rec = jnp.where(lane == idx, val, rec)
    return rec


def _outproj_body(*refs, with_router):
    if with_router:
        (x_ref, ys_ref, us_ref, ya_ref, yg_ref, d_ref, wglu_ref, gn_ref, w_ref, fg_ref, rw_ref,
         xo_ref, h_ref, route_ref, cnt_ref) = refs

        @pl.when(pl.program_id(0) == 0)
        def _():
            cnt_ref[...] = jnp.zeros_like(cnt_ref)
    else:
        (x_ref, ys_ref, us_ref, ya_ref, yg_ref, d_ref, wglu_ref, gn_ref, w_ref, fg_ref,
         xo_ref, h_ref) = refs

    tm = x_ref.shape[0]
    part = tm // OUTPROJ_PARTS
    o1, o2 = D_SSM, D_SSM + D_ATT
    for k in range(OUTPROJ_PARTS):
        rows = slice(k * part, (k + 1) * part)
        y1 = _gelu(_split_load(ys_ref, rows) + d_ref[...] * _split_load(us_ref, rows))
        y_ssm = y1 * _sigmoid(_dot(y1.astype(BF16), wglu_ref[...]))

        acc = _dot(_rms(y_ssm, gn_ref[:, :o1]).astype(BF16), w_ref[:o1, :])
        acc += _dot(_rms(ya_ref[rows, :].astype(F32), gn_ref[:, o1:o2]).astype(BF16), w_ref[o1:o2, :])
        acc += _dot(_rms(yg_ref[rows, :].astype(F32), gn_ref[:, o2:]).astype(BF16), w_ref[o2:, :])
        xn = x_ref[rows, :] + acc
        xo_ref[rows, :] = xn
        hn = _rms(xn, fg_ref[...])
        h_ref[rows, :] = hn.astype(h_ref.dtype)
        if with_router:
            h1, h2, h3 = _split3(hn)
            r1, r2, r3 = _split3(rw_ref[...])
            logits = (_dot(h1, r1) + _dot(h1, r2) + _dot(h2, r1)
                      + _dot(h1, r3) + _dot(h2, r2) + _dot(h3, r1))
            route_ref[rows, :] = _route(logits, cnt_ref)


def _outproj(x2, ys, us, ya, yg, d, wglu, gn, w, fg, rw, *, tm):
    n = x2.shape[0]
    with_router = rw is not None
    row = lambda width: pl.BlockSpec((tm, width), lambda i: (i, 0))
    in_specs = [row(D_MODEL), _split_spec(tm), _split_spec(tm), row(D_ATT), row(D_GM),
                _resident((1, D_SSM)), _resident((D_SSM, D_SSM)), _resident((1, D_MODEL)),
                _resident((D_MODEL, D_MODEL)), _resident((1, D_MODEL))]
    out_specs = [row(D_MODEL), row(D_MODEL)]
    out_shape = [jax.ShapeDtypeStruct((n, D_MODEL), F32),
                 jax.ShapeDtypeStruct((n, D_MODEL), F32 if with_router else BF16)]
    args = [x2, ys, us, ya, yg, d, wglu, gn, w, fg]
    if with_router:
        in_specs.append(_resident((D_MODEL, LANES)))
        out_specs += [row(LANES), pl.BlockSpec((1, LANES), lambda i: (0, 0))]
        out_shape += [jax.ShapeDtypeStruct((n, LANES), F32), jax.ShapeDtypeStruct((1, LANES), F32)]
        args.append(rw)
    return pl.pallas_call(
        functools.partial(_outproj_body, with_router=with_router),
        grid=(n // tm,),
        in_specs=in_specs, out_specs=out_specs, out_shape=out_shape,
        compiler_params=_cparams("arbitrary" if with_router else "parallel"),
        name="outproj_router" if with_router else "outproj",
    )(*args)


def _swiglu_tile(h, wg, wu, wd):
    a = _dot(h, wg)
    return _dot((a * _sigmoid(a) * _dot(h, wu)).astype(BF16), wd)


def _finish(x, acc, fin_ref, o_ref):
    xn = x + acc
    o_ref[...] = xn if fin_ref is None else _rms(xn, fin_ref[...])


def _dense_ffn_body(*refs, final_norm):
    if final_norm:
        h_ref, x_ref, wg_ref, wu_ref, wd_ref, fin_ref, o_ref, acc_ref = refs
    else:
        h_ref, x_ref, wg_ref, wu_ref, wd_ref, o_ref, acc_ref = refs
        fin_ref = None
    f = pl.program_id(1)

    @pl.when(f == 0)
    def _():
        acc_ref[...] = jnp.zeros_like(acc_ref)

    acc_ref[...] += _swiglu_tile(h_ref[...], wg_ref[...], wu_ref[...], wd_ref[...])

    @pl.when(f == pl.num_programs(1) - 1)
    def _():
        _finish(x_ref[...], acc_ref[...], fin_ref, o_ref)


def _dense_ffn(h, x2, wg, wu, wd, fin, *, tm, tf):
    n = x2.shape[0]
    dff = wg.shape[1]
    row = lambda: pl.BlockSpec((tm, D_MODEL), lambda i, f: (i, 0))
    mode = dict(pipeline_mode=pl.Buffered(1)) if tf == dff else {}
    in_specs = [row(), row(),
                pl.BlockSpec((D_MODEL, tf), lambda i, f: (0, f), **mode),
                pl.BlockSpec((D_MODEL, tf), lambda i, f: (0, f), **mode),
                pl.BlockSpec((tf, D_MODEL), lambda i, f: (f, 0), **mode)]
    args = [h, x2, wg, wu, wd]
    if fin is not None:
        in_specs.append(pl.BlockSpec((1, D_MODEL), lambda i, f: (0, 0)))
        args.append(fin)
    return pl.pallas_call(
        functools.partial(_dense_ffn_body, final_norm=fin is not None),
        grid=(n // tm, dff // tf),
        in_specs=in_specs, out_specs=row(),
        out_shape=jax.ShapeDtypeStruct((n, D_MODEL), F32),
        scratch_shapes=[pltpu.VMEM((tm, D_MODEL), F32)],
        compiler_params=_cparams("parallel", "arbitrary"),
        name="dense_ffn",
    )(*args)


def _row_copy(src, src_row, dst, dst_row, sem):
    return pltpu.make_async_copy(src.at[pl.ds(src_row, 1)], dst.at[pl.ds(dst_row, 1)], sem)


def _moe_dispatch_body(dest_ref, h_ref, xs_in_ref, xs_ref, sem):
    del xs_in_ref
    tm = h_ref.shape[0]

    def issue(r, carry):
        for k in range(2):
            _row_copy(h_ref, r, xs_ref, dest_ref[k, r], sem).start()
        return carry

    lax.fori_loop(0, tm, issue, 0, unroll=8)
    for k in range(2):
        pltpu.make_async_copy(h_ref, xs_ref.at[pl.ds(0, tm)], sem).wait()


def _moe_dispatch(dest, h, xs_zero, *, tm):
    n = h.shape[0]
    return pl.pallas_call(
        _moe_dispatch_body,
        grid=(n // tm,),
        in_specs=[pl.BlockSpec((None, 2, tm), lambda i: (i, 0, 0), memory_space=pltpu.SMEM),
                  pl.BlockSpec((tm, D_MODEL), lambda i: (i, 0)),
                  pl.BlockSpec(memory_space=pl.ANY)],
        out_specs=pl.BlockSpec(memory_space=pl.ANY),
        out_shape=jax.ShapeDtypeStruct(xs_zero.shape, xs_zero.dtype),
        scratch_shapes=[pltpu.SemaphoreType.DMA(())],
        input_output_aliases={2: 0},
        compiler_params=pltpu.CompilerParams(dimension_semantics=("arbitrary",),
                                             vmem_limit_bytes=VMEM_LIMIT, has_side_effects=True),
        name="moe_dispatch",
    )(dest, h, xs_zero)


def _moe_ffn_body(te_ref, nu_ref, xs_ref, wg_ref, wu_ref, wd_ref, ys_ref, acc_ref):
    del te_ref
    i = pl.program_id(0)
    f = pl.program_id(1)

    @pl.when(i < nu_ref[0])
    def _():
        @pl.when(f == 0)
        def _():
            acc_ref[...] = jnp.zeros_like(acc_ref)

        acc_ref[...] += _swiglu_tile(xs_ref[...].astype(BF16), wg_ref[...], wu_ref[...], wd_ref[...])

        @pl.when(f == pl.num_programs(1) - 1)
        def _():
            ys_ref[...] = acc_ref[...]

    @pl.when((i >= nu_ref[0]) & (f == pl.num_programs(1) - 1))
    def _():
        ys_ref[...] = jnp.zeros_like(ys_ref)


def _moe_ffn(tile_expert, n_used, xs, wg, wu, wd, *, tme, tf):
    rows = xs.shape[0]
    dff = wg.shape[-1]
    nf = dff // tf
    row_idx = lambda i, f, te, nu: (jnp.minimum(i, nu[0] - 1), 0)
    col = lambda i, f, nu: jnp.where(i < nu[0], f, nf - 1)
    grid_spec = pltpu.PrefetchScalarGridSpec(
        num_scalar_prefetch=2,
        grid=(rows // tme, nf),
        in_specs=[pl.BlockSpec((tme, D_MODEL), row_idx),
                  pl.BlockSpec((None, D_MODEL, tf), lambda i, f, te, nu: (te[i], 0, col(i, f, nu))),
                  pl.BlockSpec((None, D_MODEL, tf), lambda i, f, te, nu: (te[i], 0, col(i, f, nu))),
                  pl.BlockSpec((None, tf, D_MODEL), lambda i, f, te, nu: (te[i], col(i, f, nu), 0))],
        out_specs=pl.BlockSpec((tme, D_MODEL), lambda i, f, te, nu: (i, 0)),
        scratch_shapes=[pltpu.VMEM((tme, D_MODEL), F32)])
    return pl.pallas_call(
        _moe_ffn_body,
        grid_spec=grid_spec,
        out_shape=jax.ShapeDtypeStruct((rows, D_MODEL), F32),
        compiler_params=_cparams("arbitrary", "arbitrary"),
        name="moe_ffn",
    )(tile_expert, n_used, xs, wg, wu, wd)


def _moe_combine_body(*refs, final_norm):
    if final_norm:
        dest_ref, x_ref, route_ref, ys_ref, fin_ref, o_ref, buf_ref, sem = refs
    else:
        dest_ref, x_ref, route_ref, ys_ref, o_ref, buf_ref, sem = refs
        fin_ref = None
    tm = x_ref.shape[0]

    def issue(r, carry):
        for k in range(2):
            _row_copy(ys_ref, dest_ref[k, r], buf_ref.at[k], r, sem).start()
        return carry

    lax.fori_loop(0, tm, issue, 0, unroll=8)
    for k in range(2):
        pltpu.make_async_copy(ys_ref.at[pl.ds(0, tm)], buf_ref.at[k], sem).wait()

    route = route_ref[...]
    lane = lax.broadcasted_iota(jnp.int32, route.shape, 1)
    w1 = jnp.sum(jnp.where(lane == R_W1, route, 0.0), axis=-1, keepdims=True)
    w2 = jnp.sum(jnp.where(lane == R_W2, route, 0.0), axis=-1, keepdims=True)
    _finish(x_ref[...], w1 * buf_ref[0] + w2 * buf_ref[1], fin_ref, o_ref)


def _moe_combine(dest, x2, route, ys, fin, *, tm):
    n = x2.shape[0]
    row = lambda width: pl.BlockSpec((tm, width), lambda i: (i, 0))
    in_specs = [pl.BlockSpec((None, 2, tm), lambda i: (i, 0, 0), memory_space=pltpu.SMEM),
                row(D_MODEL), row(LANES), pl.BlockSpec(memory_space=pl.ANY)]
    args = [dest, x2, route, ys]
    if fin is not None:
        in_specs.append(pl.BlockSpec((1, D_MODEL), lambda i: (0, 0)))
        args.append(fin)
    return pl.pallas_call(
        functools.partial(_moe_combine_body, final_norm=fin is not None),
        grid=(n // tm,),
        in_specs=in_specs, out_specs=row(D_MODEL),
        out_shape=jax.ShapeDtypeStruct((n, D_MODEL), F32),
        scratch_shapes=[pltpu.VMEM((2, tm, D_MODEL), F32), pltpu.SemaphoreType.DMA(())],
        compiler_params=_cparams("arbitrary"),
        name="moe_combine",
    )(*args)


def _moe_layout(route, counts, *, tm, tme):
    n = route.shape[0]
    cnt = counts[0, :N_EXPERTS].astype(jnp.int32)
    padded = (cnt + (tme - 1)) // tme * tme
    ends = jnp.cumsum(padded)
    starts = ends - padded
    experts = jnp.arange(N_EXPERTS, dtype=F32)

    def rows(e_lane, rank_lane):
        start = jnp.sum(jnp.where(route[:, e_lane, None] == experts[None], starts[None], 0), axis=1)
        return start + route[:, rank_lane].astype(jnp.int32)

    dest = jnp.stack([rows(R_E1, R_RANK1), rows(R_E2, R_RANK2)])
    dest = dest.reshape(2, n // tm, tm).transpose(1, 0, 2)
    n_tiles = (2 * n) // tme + N_EXPERTS
    tile_start = jnp.arange(n_tiles, dtype=jnp.int32) * tme
    tile_expert = jnp.minimum(jnp.sum(tile_start[:, None] >= ends[None, :], axis=1), N_EXPERTS - 1)
    n_used = (ends[-1] // tme).reshape(1)
    tile_expert = jnp.where(tile_start < ends[-1], tile_expert, tile_expert[jnp.maximum(n_used[0] - 1, 0)])
    return dest, tile_expert.astype(jnp.int32), n_used.astype(jnp.int32), n_tiles * tme


def _largest_tile(total, cap, mult):
    best = mult
    for t in range(mult, min(total, cap) + 1, mult):
        if total % t == 0:
            best = t
    return best


def _mxu_tile(total, cap):
    if total % MXU_WIDTH == 0:
        return _largest_tile(total, cap, MXU_WIDTH)
    return _largest_tile(total, cap, LANES)


def _pad_lanes(a):
    return jnp.pad(a, [(0, 0)] * (a.ndim - 1) + [(0, LANES - a.shape[-1])])


def kernel(x, mix_norm_g, w_in, b_forget, ssm_lambda_re, ssm_lambda_im, ssm_log_dt, ssm_b_re, ssm_b_im, ssm_c_re, ssm_c_im, ssm_d, ssm_w_glu, gm_ln_g, gm_ln_b, gm_w_s, gm_b_s, group_norm_g, w_out, ffn_norm_g, dense_w_gate, dense_w_up, dense_w_down, router_w, moe_w_gate, moe_w_up, moe_w_down, final_norm_g):
    bsz, seq, _ = x.shape
    n = bsz * seq
    depth = w_in.shape[0]
    assert seq % CHUNK == 0 and seq % SSM_T == 0
    tm = _largest_tile(seq, 512, CHUNK)
    tq = _largest_tile(seq, 512, LANES)
    nc = n // SSM_T
    ncb = seq // SSM_T
    tr = _largest_tile(nc, 512, 8)
    tc = _largest_tile(ncb, 128, 8)

    x2 = x.reshape(n, D_MODEL).astype(F32)
    tril = jnp.tril(jnp.ones((CHUNK, CHUNK), F32))
    i0, i1, i2, i3, i4 = (D_SSM, D_SSM + D_ATT, D_SSM + 2 * D_ATT, D_SSM + 3 * D_ATT,
                          D_SSM + 3 * D_ATT + ATT_HEADS)
    for layer in range(depth):
        wl = w_in[layer]
        w_re = jnp.concatenate(
            [wl[:, :i0], wl[:, i0:i1] * (ATT_HEAD_DIM ** -0.5 * LOG2E), wl[:, i1:i3], wl[:, i4:],
             _pad_lanes(wl[:, i3:i4])], axis=1).astype(BF16)
        ws = (gm_w_s[layer].astype(F32) * tril[None]).astype(BF16)
        bs = jnp.repeat(gm_b_s[layer].astype(F32).T, GM_HEAD_DIM, axis=1)
        u, q, k, v, cum, ygm = _inproj(
            x2, mix_norm_g[layer].reshape(1, D_MODEL), w_re,
            _pad_lanes(b_forget[layer].reshape(1, ATT_HEADS).astype(F32)),
            gm_ln_g[layer].reshape(1, D_GM), gm_ln_b[layer].reshape(1, D_GM), ws, bs,
            seq=seq, tm=tm)

        mmat, fmat, emat, a_row = _ssm_matrices(
            ssm_lambda_re[layer], ssm_lambda_im[layer], ssm_log_dt[layer], ssm_b_re[layer],
            ssm_b_im[layer], ssm_c_re[layer], ssm_c_im[layer])
        s_loc = _ssm_state(u, fmat, tr=tr)
        xprev = _ssm_scan(s_loc.reshape(bsz, ncb, 2 * SSM_NSTATE), a_row, tc=tc)
        y_ssm = _ssm_out(u, xprev.reshape(nc, 2 * SSM_NSTATE), mmat, emat, tr=tr)

        cum_rows = cum[:, :ATT_HEADS].reshape(bsz, seq, ATT_HEADS // ATT_STEP_HEADS, ATT_STEP_HEADS)
        cum_rows = cum_rows.transpose(0, 2, 3, 1)
        y_att = _attention(q.reshape(bsz, seq, ATT_SLOTS), k.reshape(bsz, seq, ATT_SLOTS), v,
                           cum_rows, tq=tq).reshape(n, D_ATT)

        is_moe = layer % 2 == 1
        j = layer // 2
        rw = _pad_lanes(router_w[j].astype(F32)) if is_moe else None
        outs = _outproj(x2, y_ssm, u, y_att, ygm, ssm_d[layer].reshape(1, D_SSM).astype(F32),
                        ssm_w_glu[layer].astype(BF16), group_norm_g[layer].reshape(1, D_MODEL),
                        w_out[layer].astype(BF16), ffn_norm_g[layer].reshape(1, D_MODEL), rw, tm=tm)
        fin = final_norm_g.reshape(1, D_MODEL) if layer == depth - 1 else None
        if is_moe:
            x_mid, h, route, counts = outs
            dffe = moe_w_gate.shape[-1]
            tme = min(MOE_ROW_TILE, 2 * n)
            dest, tile_expert, n_used, rows = _moe_layout(route, counts, tm=tm, tme=tme)
            xs = _moe_dispatch(dest, h, jnp.zeros((rows, D_MODEL), F32), tm=tm)
            ys = _moe_ffn(tile_expert, n_used, xs, moe_w_gate[j].astype(BF16),
                          moe_w_up[j].astype(BF16), moe_w_down[j].astype(BF16), tme=tme,
                          tf=_mxu_tile(dffe, 1792))
            x2 = _moe_combine(dest, x_mid, route, ys, fin, tm=tm)
        else:
            x_mid, h = outs
            dff = dense_w_gate.shape[-1]
            x2 = _dense_ffn(h, x_mid, dense_w_gate[j].astype(BF16), dense_w_up[j].astype(BF16),
                            dense_w_down[j].astype(BF16), fin, tm=tm,
                            tf=_mxu_tile(dff, 2816))
    return x2.reshape(bsz, seq, D_MODEL).astype(x.dtype)
```

```python
import functools
import math

import jax
import jax.numpy as jnp
import numpy as np
from jax import lax
from jax.experimental import pallas as pl
from jax.experimental.pallas import tpu as pltpu

F32 = jnp.float32
BF16 = jnp.bfloat16

D_MODEL = 1024
D_SSM = 256
D_ATT = 512
D_GM = 256
SSM_GROUP = 16
SSM_GROUPS = 16
SSM_STATE = 64
ATT_HEADS = 8
ATT_HEAD_DIM = 64
ATT_SLOTS = ATT_HEADS * 128
GM_HEADS = 4
GM_HEAD_DIM = 64
CHUNK = 128
N_EXPERTS = 8
EPS = 1e-6
LOG2E = math.log2(math.e)

LANES = 128
SUBLANES = 8
MXU_WIDTH = 256
MOE_ROW_TILE = 512
OUTPROJ_PARTS = 2
ATT_STEP_HEADS = 2
SSM_T = 8
SSM_ROW = SSM_T * D_SSM
SSM_NSTATE = SSM_GROUPS * SSM_STATE

C_U = 0
C_Q = C_U + D_SSM
C_K = C_Q + D_ATT
C_V = C_K + D_ATT
C_Z = C_V + D_ATT
C_F = C_Z + 2 * D_GM
D_IN_PAD = C_F + LANES

VMEM_LIMIT = 56 * 1024 * 1024


def _cparams(*sem):
    return pltpu.CompilerParams(dimension_semantics=sem, vmem_limit_bytes=VMEM_LIMIT)


def _resident(shape):
    nd = len(shape)
    return pl.BlockSpec(shape, lambda *_: (0,) * nd, pipeline_mode=pl.Buffered(1))


def _rms(x, g):
    return x * lax.rsqrt(jnp.mean(x * x, axis=-1, keepdims=True) + EPS) * g


def _gelu(x):
    c = math.sqrt(2.0 / math.pi)
    return 0.5 * x * (1.0 + jnp.tanh(c * (x + 0.044715 * (x * x * x))))


def _sigmoid(x):
    return 1.0 / (1.0 + jnp.exp(-x))


def _split3(x):
    p1 = x.astype(BF16)
    r1 = x - p1.astype(F32)
    p2 = r1.astype(BF16)
    r2 = r1 - p2.astype(F32)
    return p1, p2, r2.astype(BF16)


def _dot(a, b):
    return jnp.dot(a, b, preferred_element_type=F32)


def _split_spec(rows):
    return pl.BlockSpec((D_SSM // LANES, rows, LANES), lambda i: (0, i, 0))


def _split_store(ref, value, rows=slice(None)):
    for h in range(D_SSM // LANES):
        ref[h, rows, :] = value[:, h * LANES:(h + 1) * LANES]


def _split_load(ref, rows=slice(None)):
    return jnp.concatenate([ref[h, rows, :] for h in range(D_SSM // LANES)], axis=1)


def _bias_lane(head):
    return 0 if head % 2 else ATT_HEAD_DIM


def _inproj_body(x_ref, g_ref, w_ref, wvt_ref, bf_ref, lng_ref, lnb_ref, ws_ref, bs_ref, place_ref, qc_ref,
                 u_ref, q_ref, k_ref, v_ref, cum_ref, ygm_ref, carry_ref, *, tiles_per_seq):
    i = pl.program_id(0)

    @pl.when(i % tiles_per_seq == 0)
    def _():
        carry_ref[...] = jnp.zeros_like(carry_ref)

    tm = x_ref.shape[0]
    hb = _rms(x_ref[...], g_ref[...]).astype(BF16)

    def proj(lo, width):
        return _dot(hb, w_ref[:, lo:lo + width])

    _split_store(u_ref, proj(C_U, D_SSM))
    v_ref[...] = lax.dot_general(wvt_ref[...], hb, (((1,), (1,)), ((), ())),
                                 preferred_element_type=F32).astype(BF16)

    f = proj(C_F, LANES) + bf_ref[...]
    logf = (jnp.minimum(f, 0.0) - jnp.log(1.0 + jnp.exp(-jnp.abs(f)))) * LOG2E
    row = lax.broadcasted_iota(jnp.int32, (tm, tm), 0)
    col = lax.broadcasted_iota(jnp.int32, (tm, tm), 1)
    tri = (col <= row).astype(BF16)
    sums = _dot(tri, jnp.concatenate(_split3(logf), axis=1))
    cum = sums[:, :LANES] + sums[:, LANES:2 * LANES] + sums[:, 2 * LANES:] + carry_ref[...]
    cum_ref[...] = cum
    carry_ref[...] = cum[tm - 1:tm, :]

    lane = lax.broadcasted_iota(jnp.int32, (tm, LANES), 1)
    c1, c2, c3 = [jnp.where(lane < ATT_HEADS, piece.astype(F32), 0.0) for piece in _split3(cum)]
    packed = c1 + pltpu.roll(c2, ATT_HEADS, axis=1) + pltpu.roll(c3, 2 * ATT_HEADS, axis=1)
    placed = _dot(packed.astype(BF16), place_ref[...])
    upper = lax.broadcasted_iota(jnp.int32, (tm, LANES), 1) >= ATT_HEAD_DIM
    qf = proj(C_Q, D_ATT)
    kf = proj(C_K, D_ATT)
    for h in range(ATT_HEADS):
        pair = slice((h // 2) * LANES, (h // 2 + 1) * LANES)
        slot = slice(h * LANES, (h + 1) * LANES)
        own = upper if h % 2 else ~upper
        q_ref[:, slot] = jnp.where(own, qf[:, pair], qc_ref[:, slot]).astype(BF16)
        k_ref[:, slot] = jnp.where(own, kf[:, pair], placed[:, slot]).astype(BF16)

    zg = _gelu(proj(C_Z, 2 * D_GM))
    ug = zg[:, :D_GM]
    vg = zg[:, D_GM:]
    mean = jnp.mean(vg, axis=-1, keepdims=True)
    cen = vg - mean
    var = jnp.mean(cen * cen, axis=-1, keepdims=True)
    vn = (cen * lax.rsqrt(var + EPS) * lng_ref[...] + lnb_ref[...]).astype(BF16)
    lane_head = lax.broadcasted_iota(jnp.int32, (CHUNK, D_GM), 1) >> 6
    for c in range(tm // CHUNK):
        rows = slice(c * CHUNK, (c + 1) * CHUNK)
        vc = vn[rows, :]
        mixed = _dot(ws_ref[0], vc)
        for g in range(1, GM_HEADS):
            mixed = jnp.where(lane_head == g, _dot(ws_ref[g], vc), mixed)
        mixed = mixed + bs_ref[...]
        ygm_ref[rows, :] = (ug[rows, :] * mixed).astype(BF16)


def _attn_slot_constants():
    place = np.zeros((LANES, ATT_SLOTS), np.float32)
    qconst = np.zeros((1, ATT_SLOTS), np.float32)
    for h in range(ATT_HEADS):
        for piece in range(3):
            lane = h * LANES + _bias_lane(h) + piece
            place[piece * ATT_HEADS + h, lane] = 1.0
            qconst[0, lane] = -1.0
    return jnp.asarray(place, BF16), jnp.asarray(qconst)


def _inproj(x2, g, w, bf, lng, lnb, ws, bs, *, seq, tm):
    n = x2.shape[0]
    tps = seq // tm
    place, qconst = _attn_slot_constants()
    wvt = w[:, C_V:C_V + D_ATT].T
    row = lambda width: pl.BlockSpec((tm, width), lambda i: (i, 0))
    return pl.pallas_call(
        functools.partial(_inproj_body, tiles_per_seq=tps),
        grid=(n // tm,),
        in_specs=[row(D_MODEL), _resident((1, D_MODEL)), _resident((D_MODEL, D_IN_PAD)),
                  _resident((D_ATT, D_MODEL)),
                  _resident((1, LANES)), _resident((1, D_GM)), _resident((1, D_GM)),
                  _resident((GM_HEADS, CHUNK, CHUNK)), _resident((CHUNK, D_GM)),
                  _resident((LANES, ATT_SLOTS)), _resident((1, ATT_SLOTS))],
        out_specs=[_split_spec(tm), row(ATT_SLOTS), row(ATT_SLOTS),
                   pl.BlockSpec((None, D_ATT, tm), lambda i: (i // tps, 0, i % tps)),
                   row(LANES), row(D_GM)],
        out_shape=[jax.ShapeDtypeStruct((D_SSM // LANES, n, LANES), F32),
                   jax.ShapeDtypeStruct((n, ATT_SLOTS), BF16),
                   jax.ShapeDtypeStruct((n, ATT_SLOTS), BF16),
                   jax.ShapeDtypeStruct((n // seq, D_ATT, seq), BF16),
                   jax.ShapeDtypeStruct((n, LANES), F32),
                   jax.ShapeDtypeStruct((n, D_GM), BF16)],
        scratch_shapes=[pltpu.VMEM((1, LANES), F32)],
        compiler_params=_cparams("arbitrary"),
        name="inproj",
    )(x2, g, w, wvt, bf, lng, lnb, ws, bs, place, qconst)


def _ssm_tokens(u_ref, s, tr):
    return _split_load(u_ref, pl.ds(s, tr, stride=SSM_T)).astype(BF16)


def _ssm_state_body(u_ref, f_ref, s_ref):
    tr = s_ref.shape[0]
    acc = _dot(_ssm_tokens(u_ref, 0, tr), f_ref[:D_SSM, :])
    for s in range(1, SSM_T):
        acc += _dot(_ssm_tokens(u_ref, s, tr), f_ref[s * D_SSM:(s + 1) * D_SSM, :])
    s_ref[...] = acc


def _ssm_state(u, fmat, *, tr):
    nc = u.shape[1] // SSM_T
    return pl.pallas_call(
        _ssm_state_body,
        grid=(nc // tr,),
        in_specs=[_split_spec(tr * SSM_T), _resident((SSM_ROW, 2 * SSM_NSTATE))],
        out_specs=pl.BlockSpec((tr, 2 * SSM_NSTATE), lambda i: (i, 0)),
        out_shape=jax.ShapeDtypeStruct((nc, 2 * SSM_NSTATE), F32),
        compiler_params=_cparams("parallel"),
        name="ssm_state",
    )(u, fmat)


def _ssm_scan_body(s_ref, a_ref, xp_ref, st_ref):
    @pl.when(pl.program_id(1) == 0)
    def _():
        st_ref[...] = jnp.zeros_like(st_ref)

    tc = s_ref.shape[0]
    are = a_ref[:, :SSM_NSTATE]
    aim = a_ref[:, SSM_NSTATE:]

    def step(c8, carry):
        xre, xim = carry
        base = pl.multiple_of(c8 * SUBLANES, SUBLANES)
        blk = s_ref[pl.ds(base, SUBLANES), :]
        prev_re, prev_im = [], []
        for r in range(SUBLANES):
            prev_re.append(xre)
            prev_im.append(xim)
            sre = blk[r:r + 1, :SSM_NSTATE]
            sim = blk[r:r + 1, SSM_NSTATE:]
            xre, xim = are * xre - aim * xim + sre, are * xim + aim * xre + sim
        xp_ref[pl.ds(base, SUBLANES), :SSM_NSTATE] = jnp.concatenate(prev_re, axis=0)
        xp_ref[pl.ds(base, SUBLANES), SSM_NSTATE:] = jnp.concatenate(prev_im, axis=0)
        return xre, xim

    xre, xim = lax.fori_loop(0, tc // SUBLANES, step,
                             (st_ref[:, :SSM_NSTATE], st_ref[:, SSM_NSTATE:]))
    st_ref[:, :SSM_NSTATE] = xre
    st_ref[:, SSM_NSTATE:] = xim


def _ssm_scan(s3, a, *, tc):
    b, ncb, w = s3.shape
    spec = pl.BlockSpec((None, tc, w), lambda bi, ci: (bi, ci, 0))
    return pl.pallas_call(
        _ssm_scan_body,
        grid=(b, ncb // tc),
        in_specs=[spec, pl.BlockSpec((1, w), lambda bi, ci: (0, 0))],
        out_specs=spec,
        out_shape=jax.ShapeDtypeStruct(s3.shape, F32),
        scratch_shapes=[pltpu.VMEM((1, w), F32)],
        compiler_params=_cparams("parallel", "arbitrary"),
        name="ssm_scan",
    )(s3, a)


def _ssm_out_body(u_ref, xp_ref, m_ref, e_ref, y_ref):
    tr = xp_ref.shape[0]
    us = [_ssm_tokens(u_ref, s, tr) for s in range(SSM_T)]
    xp = xp_ref[...].astype(BF16)
    for t in range(SSM_T):
        cols = slice(t * D_SSM, (t + 1) * D_SSM)
        acc = _dot(xp, e_ref[:, cols])
        for s in range(t + 1):
            acc += _dot(us[s], m_ref[s * D_SSM:(s + 1) * D_SSM, cols])
        _split_store(y_ref, acc, pl.ds(t, tr, stride=SSM_T))


def _ssm_out(u, xp, mmat, emat, *, tr):
    nc = xp.shape[0]
    return pl.pallas_call(
        _ssm_out_body,
        grid=(nc // tr,),
        in_specs=[_split_spec(tr * SSM_T),
                  pl.BlockSpec((tr, 2 * SSM_NSTATE), lambda i: (i, 0)),
                  _resident((SSM_ROW, SSM_ROW)), _resident((2 * SSM_NSTATE, SSM_ROW))],
        out_specs=_split_spec(tr * SSM_T),
        out_shape=jax.ShapeDtypeStruct((D_SSM // LANES, nc * SSM_T, LANES), F32),
        compiler_params=_cparams("parallel"),
        name="ssm_out",
    )(u, xp, mmat, emat)


def _ssm_matrices(lam_re, lam_im, log_dt, b_re, b_im, c_re, c_im):
    t, g, p, h = SSM_T, SSM_GROUPS, SSM_STATE, SSM_GROUP
    ns, nch = g * p, g * h
    hp = lax.Precision.HIGHEST
    lr, li = lam_re.astype(F32).reshape(ns), lam_im.astype(F32).reshape(ns)
    dt = jnp.repeat(jnp.exp(log_dt.astype(F32)), p)
    steps = jnp.arange(t + 1, dtype=F32)[:, None]
    mag = jnp.exp((lr * dt)[None] * steps)
    ang = (li * dt)[None] * steps
    pr, pi = mag * jnp.cos(ang), mag * jnp.sin(ang)
    nr, ni = pr[1] - 1.0, pi[1]
    den = lr * lr + li * li
    zr, zi = ((nr * lr + ni * li) / den)[:, None], ((ni * lr - nr * li) / den)[:, None]
    br, bi = b_re.astype(F32).reshape(ns, h), b_im.astype(F32).reshape(ns, h)
    cr = c_re.astype(F32).transpose(0, 2, 1).reshape(ns, h)
    ci = c_im.astype(F32).transpose(0, 2, 1).reshape(ns, h)
    same = jnp.asarray(np.repeat(np.repeat(np.eye(g, dtype=np.float32), p, axis=0), h, axis=1))
    spread = lambda w: jnp.tile(w, (1, g)) * same
    bhr, bhi = spread(zr * br - zi * bi), spread(zr * bi + zi * br)
    chr_, chi = spread(cr), spread(ci)
    prc, pic = pr.T[:, :, None], pi.T[:, :, None]
    er = (prc * chr_[:, None, :] - pic * chi[:, None, :]).reshape(ns, (t + 1) * nch)
    ei = (prc * chi[:, None, :] + pic * chr_[:, None, :]).reshape(ns, (t + 1) * nch)

    k_all = (jnp.dot(bhr.T, er[:, :t * nch], precision=hp)
             - jnp.dot(bhi.T, ei[:, :t * nch], precision=hp))
    mmat = jnp.concatenate(
        [jnp.pad(k_all[:, :(t - s) * nch], ((0, 0), (s * nch, 0))) for s in range(t)], axis=0)

    prr, pir = pr[:t][::-1][:, None, :], pi[:t][::-1][:, None, :]
    bhrt, bhit = bhr.T[None], bhi.T[None]
    fmat = jnp.concatenate([(bhrt * prr - bhit * pir).reshape(t * nch, ns),
                            (bhrt * pir + bhit * prr).reshape(t * nch, ns)], axis=1)

    emat = jnp.concatenate([er[:, nch:], -ei[:, nch:]], axis=0)

    a_row = jnp.concatenate([pr[t][None], pi[t][None]], axis=1)
    return mmat.astype(BF16), fmat.astype(BF16), emat.astype(BF16), a_row


def _attn_body(q_ref, k_ref, v_ref, cq_ref, o_ref,
               m_ref, acc_ref, s_ref, mt_ref, p_ref, al_ref, *, tq):
    qi = pl.program_id(2)
    nh = ATT_STEP_HEADS
    keypos = lax.broadcasted_iota(jnp.int32, (tq, tq), 0)
    qpos = lax.broadcasted_iota(jnp.int32, (tq, tq), 1)

    def slot_lanes(j):
        return slice(j * LANES, (j + 1) * LANES)

    def scores(t, slot, masked):
        start = pl.multiple_of(t * tq, tq)
        for j in range(nh):
            kt = k_ref[pl.ds(start, tq), slot_lanes(j)]
            s = lax.dot_general(kt, q_ref[:, slot_lanes(j)], (((1,), (1,)), ((), ())),
                                preferred_element_type=F32)
            if masked:
                s = jnp.where(keypos <= qpos, s, -jnp.inf)
            s_ref[slot, j] = s
            mt_ref[slot, j] = jnp.max(s, axis=0, keepdims=True)

    def probs(slot):
        for j in range(nh):
            m = m_ref[j]
            cq = cq_ref[j:j + 1, :]
            m_new = jnp.maximum(m, mt_ref[slot, j] + cq)
            p_ref[slot, j] = jnp.exp2(s_ref[slot, j] + (cq - m_new)).astype(BF16)
            al_ref[slot, j] = jnp.exp2(m - m_new)
            m_ref[j] = m_new

    def values(t, slot):
        start = pl.multiple_of(t * tq, tq)
        for j in range(nh):
            vt = v_ref[j * ATT_HEAD_DIM:(j + 1) * ATT_HEAD_DIM, pl.ds(start, tq)]
            vt = jnp.concatenate([vt, jnp.ones_like(vt)], axis=0)
            acc_ref[j] = al_ref[slot, j] * acc_ref[j] + _dot(vt, p_ref[slot, j])

    def trip(t, slot, masked=False):
        values(t - 1, 1 - slot)
        probs(slot)
        scores(t + 1, 1 - slot, masked)

    for j in range(nh):
        m_ref[j] = jnp.full((1, tq), -jnp.inf, F32)
        acc_ref[j] = jnp.zeros((LANES, tq), F32)

    @pl.when(qi == 0)
    def _():
        scores(0, 0, True)
        probs(0)
        values(0, 0)

    @pl.when(qi == 1)
    def _():
        scores(0, 0, False)
        probs(0)
        scores(1, 1, True)
        values(0, 0)
        probs(1)
        values(1, 1)

    @pl.when(qi >= 2)
    def _():
        scores(0, 0, False)
        probs(0)
        scores(1, 1, False)

        def pair(u, carry):
            t = 2 * u + 1
            trip(t, 1)
            trip(t + 1, 0)
            return carry

        lax.fori_loop(0, (qi - 2) // 2, pair, 0)

        @pl.when(qi % 2 == 0)
        def _():
            trip(qi - 1, 1, masked=True)
            values(qi - 1, 1)
            probs(0)
            values(qi, 0)

        @pl.when(qi % 2 == 1)
        def _():
            trip(qi - 2, 1)
            trip(qi - 1, 0, masked=True)
            values(qi - 1, 0)
            probs(1)
            values(qi, 1)
    for pair in range(nh // 2):
        outs = []
        for j in (2 * pair, 2 * pair + 1):
            acc = acc_ref[j]
            outs.append(acc[:ATT_HEAD_DIM, :] / acc[ATT_HEAD_DIM:, :])
        o_ref[:, slot_lanes(pair)] = jnp.concatenate(outs, axis=0).T.astype(BF16)


def _attention(q3, k3, vt4, cum3, *, tq):
    b, s, _ = q3.shape
    nh = ATT_STEP_HEADS
    return pl.pallas_call(
        functools.partial(_attn_body, tq=tq),
        scratch_shapes=[pltpu.VMEM((nh, 1, tq), F32), pltpu.VMEM((nh, LANES, tq), F32),
                        pltpu.VMEM((2, nh, tq, tq), F32), pltpu.VMEM((2, nh, 1, tq), F32),
                        pltpu.VMEM((2, nh, tq, tq), BF16), pltpu.VMEM((2, nh, 1, tq), F32)],
        grid=(b, ATT_HEADS // nh, s // tq),
        in_specs=[pl.BlockSpec((None, tq, nh * LANES), lambda bi, hg, qi: (bi, qi, hg)),
                  pl.BlockSpec((None, s, nh * LANES), lambda bi, hg, qi: (bi, 0, hg)),
                  pl.BlockSpec((None, nh * ATT_HEAD_DIM, s), lambda bi, hg, qi: (bi, hg, 0)),
                  pl.BlockSpec((None, None, nh, tq), lambda bi, hg, qi: (bi, hg, 0, qi))],
        out_specs=pl.BlockSpec((None, tq, nh * ATT_HEAD_DIM), lambda bi, hg, qi: (bi, qi, hg)),
        out_shape=jax.ShapeDtypeStruct((b, s, D_ATT), BF16),
        compiler_params=_cparams("parallel", "parallel", "arbitrary"),
        name="attention",
    )(q3, k3, vt4, cum3)


R_E1, R_E2, R_W1, R_W2, R_RANK1, R_RANK2 = range(6)


def _route(logits, cnt_ref):
    tm = logits.shape[0]
    lane = lax.broadcasted_iota(jnp.int32, logits.shape, 1).astype(F32)
    lg = jnp.where(lane < N_EXPERTS, logits, -jnp.inf)
    m1 = jnp.max(lg, axis=-1, keepdims=True)
    i1 = jnp.min(jnp.where(lg == m1, lane, float(LANES)), axis=-1, keepdims=True)
    lg2 = jnp.where(lane == i1, -jnp.inf, lg)
    m2 = jnp.max(lg2, axis=-1, keepdims=True)
    i2 = jnp.min(jnp.where(lg2 == m2, lane, float(LANES)), axis=-1, keepdims=True)
    e = jnp.exp(m2 - m1)
    w1 = 1.0 / (1.0 + e)
    w2 = e / (1.0 + e)

    hit1, hit2 = lane == i1, lane == i2
    oh1, oh2 = hit1.astype(BF16), hit2.astype(BF16)
    row = lax.broadcasted_iota(jnp.int32, (tm, tm), 0)
    col = lax.broadcasted_iota(jnp.int32, (tm, tm), 1)
    before = (col < row).astype(BF16)
    carry = cnt_ref[...]
    tot1 = jnp.sum(oh1.astype(F32), axis=0, keepdims=True)
    tot2 = jnp.sum(oh2.astype(F32), axis=0, keepdims=True)
    pos1 = _dot(before, oh1) + carry
    pos2 = _dot(before, oh2) + (carry + tot1)
    rank1 = jnp.sum(jnp.where(hit1, pos1, 0.0), axis=-1, keepdims=True)
    rank2 = jnp.sum(jnp.where(hit2, pos2, 0.0), axis=-1, keepdims=True)
    cnt_ref[...] = carry + tot1 + tot2

    rec = jnp.zeros(logits.shape, F32)
    for idx, val in ((R_E1, i1), (R_E2, i2), (R_W1, w1), (R_W2, w2), (R_RANK1, rank1), (R_RANK2, rank2)):
        rec = jnp.where(lane == idx, val, rec)
    return rec


def _outproj_body(*refs, with_router):
    if with_router:
        (x_ref, ys_ref, us_ref, ya_ref, yg_ref, d_ref, wglu_ref, gn_ref, w_ref, fg_ref, rw_ref,
         xo_ref, h_ref, route_ref, cnt_ref) = refs

        @pl.when(pl.program_id(0) == 0)
        def _():
            cnt_ref[...] = jnp.zeros_like(cnt_ref)
    else:
        (x_ref, ys_ref, us_ref, ya_ref, yg_ref, d_ref, wglu_ref, gn_ref, w_ref, fg_ref,
         xo_ref, h_ref) = refs

    tm = x_ref.shape[0]
    part = tm // OUTPROJ_PARTS
    o1, o2 = D_SSM, D_SSM + D_ATT
    for k in range(OUTPROJ_PARTS):
        rows = slice(k * part, (k + 1) * part)
        y1 = _gelu(_split_load(ys_ref, rows) + d_ref[...] * _split_load(us_ref, rows))
        y_ssm = y1 * _sigmoid(_dot(y1.astype(BF16), wglu_ref[...]))

        acc = _dot(_rms(y_ssm, gn_ref[:, :o1]).astype(BF16), w_ref[:o1, :])
        acc += _dot(_rms(ya_ref[rows, :].astype(F32), gn_ref[:, o1:o2]).astype(BF16), w_ref[o1:o2, :])
        acc += _dot(_rms(yg_ref[rows, :].astype(F32), gn_ref[:, o2:]).astype(BF16), w_ref[o2:, :])
        xn = x_ref[rows, :] + acc
        xo_ref[rows, :] = xn
        hn = _rms(xn, fg_ref[...])
        h_ref[rows, :] = hn.astype(h_ref.dtype)
        if with_router:
            h1, h2, h3 = _split3(hn)
            r1, r2, r3 = _split3(rw_ref[...])
            logits = (_dot(h1, r1) + _dot(h1, r2) + _dot(h2, r1)
                      + _dot(h1, r3) + _dot(h2, r2) + _dot(h3, r1))
            route_ref[rows, :] = _route(logits, cnt_ref)


def _outproj(x2, ys, us, ya, yg, d, wglu, gn, w, fg, rw, *, tm):
    n = x2.shape[0]
    with_router = rw is not None
    row = lambda width: pl.BlockSpec((tm, width), lambda i: (i, 0))
    in_specs = [row(D_MODEL), _split_spec(tm), _split_spec(tm), row(D_ATT), row(D_GM),
                _resident((1, D_SSM)), _resident((D_SSM, D_SSM)), _resident((1, D_MODEL)),
                _resident((D_MODEL, D_MODEL)), _resident((1, D_MODEL))]
    out_specs = [row(D_MODEL), row(D_MODEL)]
    out_shape = [jax.ShapeDtypeStruct((n, D_MODEL), F32),
                 jax.ShapeDtypeStruct((n, D_MODEL), F32 if with_router else BF16)]
    args = [x2, ys, us, ya, yg, d, wglu, gn, w, fg]
    if with_router:
        in_specs.append(_resident((D_MODEL, LANES)))
        out_specs += [row(LANES), pl.BlockSpec((1, LANES), lambda i: (0, 0))]
        out_shape += [jax.ShapeDtypeStruct((n, LANES), F32), jax.ShapeDtypeStruct((1, LANES), F32)]
        args.append(rw)
    return pl.pallas_call(
        functools.partial(_outproj_body, with_router=with_router),
        grid=(n // tm,),
        in_specs=in_specs, out_specs=out_specs, out_shape=out_shape,
        compiler_params=_cparams("arbitrary" if with_router else "parallel"),
        name="outproj_router" if with_router else "outproj",
    )(*args)


def _swiglu_tile(h, wg, wu, wd):
    a = _dot(h, wg)
    return _dot((a * _sigmoid(a) * _dot(h, wu)).astype(BF16), wd)


def _finish(x, acc, fin_ref, o_ref):
    xn = x + acc
    o_ref[...] = xn if fin_ref is None else _rms(xn, fin_ref[...])


def _dense_ffn_body(*refs, final_norm):
    if final_norm:
        h_ref, x_ref, wg_ref, wu_ref, wd_ref, fin_ref, o_ref, acc_ref = refs
    else:
        h_ref, x_ref, wg_ref, wu_ref, wd_ref, o_ref, acc_ref = refs
        fin_ref = None
    f = pl.program_id(1)

    @pl.when(f == 0)
    def _():
        acc_ref[...] = jnp.zeros_like(acc_ref)

    acc_ref[...] += _swiglu_tile(h_ref[...], wg_ref[...], wu_ref[...], wd_ref[...])

    @pl.when(f == pl.num_programs(1) - 1)
    def _():
        _finish(x_ref[...], acc_ref[...], fin_ref, o_ref)


def _dense_ffn(h, x2, wg, wu, wd, fin, *, tm, tf):
    n = x2.shape[0]
    dff = wg.shape[1]
    row = lambda: pl.BlockSpec((tm, D_MODEL), lambda i, f: (i, 0))
    mode = dict(pipeline_mode=pl.Buffered(1)) if tf == dff else {}
    in_specs = [row(), row(),
                pl.BlockSpec((D_MODEL, tf), lambda i, f: (0, f), **mode),
                pl.BlockSpec((D_MODEL, tf), lambda i, f: (0, f), **mode),
                pl.BlockSpec((tf, D_MODEL), lambda i, f: (f, 0), **mode)]
    args = [h, x2, wg, wu, wd]
    if fin is not None:
        in_specs.append(pl.BlockSpec((1, D_MODEL), lambda i, f: (0, 0)))
        args.append(fin)
    return pl.pallas_call(
        functools.partial(_dense_ffn_body, final_norm=fin is not None),
        grid=(n // tm, dff // tf),
        in_specs=in_specs, out_specs=row(),
        out_shape=jax.ShapeDtypeStruct((n, D_MODEL), F32),
        scratch_shapes=[pltpu.VMEM((tm, D_MODEL), F32)],
        compiler_params=_cparams("parallel", "arbitrary"),
        name="dense_ffn",
    )(*args)


def _row_copy(src, src_row, dst, dst_row, sem):
    return pltpu.make_async_copy(src.at[pl.ds(src_row, 1)], dst.at[pl.ds(dst_row, 1)], sem)


def _moe_dispatch_body(dest_ref, h_ref, xs_in_ref, xs_ref, sem):
    del xs_in_ref
    tm = h_ref.shape[0]

    def issue(r, carry):
        for k in range(2):
            _row_copy(h_ref, r, xs_ref, dest_ref[k, r], sem).start()
        return carry

    lax.fori_loop(0, tm, issue, 0, unroll=8)
    for k in range(2):
        pltpu.make_async_copy(h_ref, xs_ref.at[pl.ds(0, tm)], sem).wait()


def _moe_dispatch(dest, h, xs_zero, *, tm):
    n = h.shape[0]
    return pl.pallas_call(
        _moe_dispatch_body,
        grid=(n // tm,),
        in_specs=[pl.BlockSpec((None, 2, tm), lambda i: (i, 0, 0), memory_space=pltpu.SMEM),
                  pl.BlockSpec((tm, D_MODEL), lambda i: (i, 0)),
                  pl.BlockSpec(memory_space=pl.ANY)],
        out_specs=pl.BlockSpec(memory_space=pl.ANY),
        out_shape=jax.ShapeDtypeStruct(xs_zero.shape, xs_zero.dtype),
        scratch_shapes=[pltpu.SemaphoreType.DMA(())],
        input_output_aliases={2: 0},
        compiler_params=pltpu.CompilerParams(dimension_semantics=("arbitrary",),
                                             vmem_limit_bytes=VMEM_LIMIT, has_side_effects=True),
        name="moe_dispatch",
    )(dest, h, xs_zero)


def _moe_ffn_body(te_ref, nu_ref, xs_ref, wg_ref, wu_ref, wd_ref, ys_ref, acc_ref):
    del te_ref
    i = pl.program_id(0)
    f = pl.program_id(1)

    @pl.when(i < nu_ref[0])
    def _():
        @pl.when(f == 0)
        def _():
            acc_ref[...] = jnp.zeros_like(acc_ref)

        acc_ref[...] += _swiglu_tile(xs_ref[...].astype(BF16), wg_ref[...], wu_ref[...], wd_ref[...])

        @pl.when(f == pl.num_programs(1) - 1)
        def _():
            ys_ref[...] = acc_ref[...]

    @pl.when((i >= nu_ref[0]) & (f == pl.num_programs(1) - 1))
    def _():
        ys_ref[...] = jnp.zeros_like(ys_ref)


def _moe_ffn(tile_expert, n_used, xs, wg, wu, wd, *, tme, tf):
    rows = xs.shape[0]
    dff = wg.shape[-1]
    nf = dff // tf
    row_idx = lambda i, f, te, nu: (jnp.minimum(i, nu[0] - 1), 0)
    col = lambda i, f, nu: jnp.where(i < nu[0], f, nf - 1)
    grid_spec = pltpu.PrefetchScalarGridSpec(
        num_scalar_prefetch=2,
        grid=(rows // tme, nf),
        in_specs=[pl.BlockSpec((tme, D_MODEL), row_idx),
                  pl.BlockSpec((None, D_MODEL, tf), lambda i, f, te, nu: (te[i], 0, col(i, f, nu))),
                  pl.BlockSpec((None, D_MODEL, tf), lambda i, f, te, nu: (te[i], 0, col(i, f, nu))),
                  pl.BlockSpec((None, tf, D_MODEL), lambda i, f, te, nu: (te[i], col(i, f, nu), 0))],
        out_specs=pl.BlockSpec((tme, D_MODEL), lambda i, f, te, nu: (i, 0)),
        scratch_shapes=[pltpu.VMEM((tme, D_MODEL), F32)])
    return pl.pallas_call(
        _moe_ffn_body,
        grid_spec=grid_spec,
        out_shape=jax.ShapeDtypeStruct((rows, D_MODEL), F32),
        compiler_params=_cparams("arbitrary", "arbitrary"),
        name="moe_ffn",
    )(tile_expert, n_used, xs, wg, wu, wd)


def _moe_combine_body(*refs, final_norm):
    if final_norm:
        dest_ref, x_ref, route_ref, ys_ref, fin_ref, o_ref, buf_ref, sem = refs
    else:
        dest_ref, x_ref, route_ref, ys_ref, o_ref, buf_ref, sem = refs
        fin_ref = None
    tm = x_ref.shape[0]

    def issue(r, carry):
        for k in range(2):
            _row_copy(ys_ref, dest_ref[k, r], buf_ref.at[k], r, sem).start()
        return carry

    lax.fori_loop(0, tm, issue, 0, unroll=8)
    for k in range(2):
        pltpu.make_async_copy(ys_ref.at[pl.ds(0, tm)], buf_ref.at[k], sem).wait()

    route = route_ref[...]
    lane = lax.broadcasted_iota(jnp.int32, route.shape, 1)
    w1 = jnp.sum(jnp.where(lane == R_W1, route, 0.0), axis=-1, keepdims=True)
    w2 = jnp.sum(jnp.where(lane == R_W2, route, 0.0), axis=-1, keepdims=True)
    _finish(x_ref[...], w1 * buf_ref[0] + w2 * buf_ref[1], fin_ref, o_ref)


def _moe_combine(dest, x2, route, ys, fin, *, tm):
    n = x2.shape[0]
    row = lambda width: pl.BlockSpec((tm, width), lambda i: (i, 0))
    in_specs = [pl.BlockSpec((None, 2, tm), lambda i: (i, 0, 0), memory_space=pltpu.SMEM),
                row(D_MODEL), row(LANES), pl.BlockSpec(memory_space=pl.ANY)]
    args = [dest, x2, route, ys]
    if fin is not None:
        in_specs.append(pl.BlockSpec((1, D_MODEL), lambda i: (0, 0)))
        args.append(fin)
    return pl.pallas_call(
        functools.partial(_moe_combine_body, final_norm=fin is not None),
        grid=(n // tm,),
        in_specs=in_specs, out_specs=row(D_MODEL),
        out_shape=jax.ShapeDtypeStruct((n, D_MODEL), F32),
        scratch_shapes=[pltpu.VMEM((2, tm, D_MODEL), F32), pltpu.SemaphoreType.DMA(())],
        compiler_params=_cparams("arbitrary"),
        name="moe_combine",
    )(*args)


def _moe_layout(route, counts, *, tm, tme):
    n = route.shape[0]
    cnt = counts[0, :N_EXPERTS].astype(jnp.int32)
    padded = (cnt + (tme - 1)) // tme * tme
    ends = jnp.cumsum(padded)
    starts = ends - padded
    experts = jnp.arange(N_EXPERTS, dtype=F32)

    def rows(e_lane, rank_lane):
        start = jnp.sum(jnp.where(route[:, e_lane, None] == experts[None], starts[None], 0), axis=1)
        return start + route[:, rank_lane].astype(jnp.int32)

    dest = jnp.stack([rows(R_E1, R_RANK1), rows(R_E2, R_RANK2)])
    dest = dest.reshape(2, n // tm, tm).transpose(1, 0, 2)
    n_tiles = (2 * n) // tme + N_EXPERTS
    tile_start = jnp.arange(n_tiles, dtype=jnp.int32) * tme
    tile_expert = jnp.minimum(jnp.sum(tile_start[:, None] >= ends[None, :], axis=1), N_EXPERTS - 1)
    n_used = (ends[-1] // tme).reshape(1)
    tile_expert = jnp.where(tile_start < ends[-1], tile_expert, tile_expert[jnp.maximum(n_used[0] - 1, 0)])
    return dest, tile_expert.astype(jnp.int32), n_used.astype(jnp.int32), n_tiles * tme


def _largest_tile(total, cap, mult):
    best = mult
    for t in range(mult, min(total, cap) + 1, mult):
        if total % t == 0:
            best = t
    return best


def _mxu_tile(total, cap):
    if total % MXU_WIDTH == 0:
        return _largest_tile(total, cap, MXU_WIDTH)
    return _largest_tile(total, cap, LANES)


def _pad_lanes(a):
    return jnp.pad(a, [(0, 0)] * (a.ndim - 1) + [(0, LANES - a.shape[-1])])


def kernel(x, mix_norm_g, w_in, b_forget, ssm_lambda_re, ssm_lambda_im, ssm_log_dt, ssm_b_re, ssm_b_im, ssm_c_re, ssm_c_im, ssm_d, ssm_w_glu, gm_ln_g, gm_ln_b, gm_w_s, gm_b_s, group_norm_g, w_out, ffn_norm_g, dense_w_gate, dense_w_up, dense_w_down, router_w, moe_w_gate, moe_w_up, moe_w_down, final_norm_g):
    bsz, seq, _ = x.shape
    n = bsz * seq
    depth = w_in.shape[0]
    assert seq % CHUNK == 0 and seq % SSM_T == 0
    tm = _largest_tile(seq, 512, CHUNK)
    tq = _largest_tile(seq, 512, LANES)
    nc = n // SSM_T
    ncb = seq // SSM_T
    tr = _largest_tile(nc, 512, 8)
    tc = _largest_tile(ncb, 128, 8)

    x2 = x.reshape(n, D_MODEL).astype(F32)
    tril = jnp.tril(jnp.ones((CHUNK, CHUNK), F32))
    i0, i1, i2, i3, i4 = (D_SSM, D_SSM + D_ATT, D_SSM + 2 * D_ATT, D_SSM + 3 * D_ATT,
                          D_SSM + 3 * D_ATT + ATT_HEADS)
    for layer in range(depth):
        wl = w_in[layer]
        w_re = jnp.concatenate(
            [wl[:, :i0], wl[:, i0:i1] * (ATT_HEAD_DIM ** -0.5 * LOG2E), wl[:, i1:i3], wl[:, i4:],
             _pad_lanes(wl[:, i3:i4])], axis=1).astype(BF16)
        ws = (gm_w_s[layer].astype(F32) * tril[None]).astype(BF16)
        bs = jnp.repeat(gm_b_s[layer].astype(F32).T, GM_HEAD_DIM, axis=1)
        u, q, k, v, cum, ygm = _inproj(
            x2, mix_norm_g[layer].reshape(1, D_MODEL), w_re,
            _pad_lanes(b_forget[layer].reshape(1, ATT_HEADS).astype(F32)),
            gm_ln_g[layer].reshape(1, D_GM), gm_ln_b[layer].reshape(1, D_GM), ws, bs,
            seq=seq, tm=tm)

        mmat, fmat, emat, a_row = _ssm_matrices(
            ssm_lambda_re[layer], ssm_lambda_im[layer], ssm_log_dt[layer], ssm_b_re[layer],
            ssm_b_im[layer], ssm_c_re[layer], ssm_c_im[layer])
        s_loc = _ssm_state(u, fmat, tr=tr)
        xprev = _ssm_scan(s_loc.reshape(bsz, ncb, 2 * SSM_NSTATE), a_row, tc=tc)
        y_ssm = _ssm_out(u, xprev.reshape(nc, 2 * SSM_NSTATE), mmat, emat, tr=tr)

        cum_rows = cum[:, :ATT_HEADS].reshape(bsz, seq, ATT_HEADS // ATT_STEP_HEADS, ATT_STEP_HEADS)
        cum_rows = cum_rows.transpose(0, 2, 3, 1)
        y_att = _attention(q.reshape(bsz, seq, ATT_SLOTS), k.reshape(bsz, seq, ATT_SLOTS), v,
                           cum_rows, tq=tq).reshape(n, D_ATT)

        is_moe = layer % 2 == 1
        j = layer // 2
        rw = _pad_lanes(router_w[j].astype(F32)) if is_moe else None
        outs = _outproj(x2, y_ssm, u, y_att, ygm, ssm_d[layer].reshape(1, D_SSM).astype(F32),
                        ssm_w_glu[layer].astype(BF16), group_norm_g[layer].reshape(1, D_MODEL),
                        w_out[layer].astype(BF16), ffn_norm_g[layer].reshape(1, D_MODEL), rw, tm=tm)
        fin = final_norm_g.reshape(1, D_MODEL) if layer == depth - 1 else None
        if is_moe:
            x_mid, h, route, counts = outs
            dffe = moe_w_gate.shape[-1]
            tme = min(MOE_ROW_TILE, 2 * n)
            dest, tile_expert, n_used, rows = _moe_layout(route, counts, tm=tm, tme=tme)
            xs = _moe_dispatch(dest, h, jnp.zeros((rows, D_MODEL), F32), tm=tm)
            ys = _moe_ffn(tile_expert, n_used, xs, moe_w_gate[j].astype(BF16),
                          moe_w_up[j].astype(BF16), moe_w_down[j].astype(BF16), tme=tme,
                          tf=_mxu_tile(dffe, 1792))
            x2 = _moe_combine(dest, x_mid, route, ys, fin, tm=tm)
        else:
            x_mid, h = outs
            dff = dense_w_gate.shape[-1]
            x2 = _dense_ffn(h, x_mid, dense_w_gate[j].astype(BF16), dense_w_up[j].astype(BF16),
                            dense_w_down[j].astype(BF16), fin, tm=tm,
                            tf=_mxu_tile(dff, 2816))
    return x2.reshape(bsz, seq, D_MODEL).astype(x.dtype)
```

```python
import functools
import math

import jax
import jax.numpy as jnp
import numpy as np
from jax import lax
from jax.experimental import pallas as pl
from jax.experimental.pallas import tpu as pltpu

F32 = jnp.float32
BF16 = jnp.bfloat16

D_MODEL = 1024
D_SSM = 256
D_ATT = 512
D_GM = 256
SSM_GROUP = 16
SSM_GROUPS = 16
SSM_STATE = 64
ATT_HEADS = 8
ATT_HEAD_DIM = 64
ATT_SLOTS = ATT_HEADS * 128
GM_HEADS = 4
GM_HEAD_DIM = 64
CHUNK = 128
N_EXPERTS = 8
EPS = 1e-6
LOG2E = math.log2(math.e)

LANES = 128
SUBLANES = 8
MXU_WIDTH = 256
MOE_ROW_TILE = 512
OUTPROJ_PARTS = 2
ATT_STEP_HEADS = 2
SSM_T = 8
SSM_ROW = SSM_T * D_SSM
SSM_NSTATE = SSM_GROUPS * SSM_STATE

C_U = 0
C_Q = C_U + D_SSM
C_K = C_Q + D_ATT
C_V = C_K + D_ATT
C_Z = C_V + D_ATT
C_F = C_Z + 2 * D_GM
D_IN_PAD = C_F + LANES

VMEM_LIMIT = 56 * 1024 * 1024


def _cparams(*sem):
    return pltpu.CompilerParams(dimension_semantics=sem, vmem_limit_bytes=VMEM_LIMIT)


def _resident(shape):
    nd = len(shape)
    return pl.BlockSpec(shape, lambda *_: (0,) * nd, pipeline_mode=pl.Buffered(1))


def _rms(x, g):
    return x * lax.rsqrt(jnp.mean(x * x, axis=-1, keepdims=True) + EPS) * g


def _gelu(x):
    c = math.sqrt(2.0 / math.pi)
    return 0.5 * x * (1.0 + jnp.tanh(c * (x + 0.044715 * (x * x * x))))


def _sigmoid(x):
    return 1.0 / (1.0 + jnp.exp(-x))


def _split3(x):
    p1 = x.astype(BF16)
    r1 = x - p1.astype(F32)
    p2 = r1.astype(BF16)
    r2 = r1 - p2.astype(F32)
    return p1, p2, r2.astype(BF16)


def _dot(a, b):
    return jnp.dot(a, b, preferred_element_type=F32)


def _split_spec(rows):
    return pl.BlockSpec((D_SSM // LANES, rows, LANES), lambda i: (0, i, 0))


def _split_store(ref, value, rows=slice(None)):
    for h in range(D_SSM // LANES):
        ref[h, rows, :] = value[:, h * LANES:(h + 1) * LANES]


def _split_load(ref, rows=slice(None)):
    return jnp.concatenate([ref[h, rows, :] for h in range(D_SSM // LANES)], axis=1)


def _bias_lane(head):
    return 0 if head % 2 else ATT_HEAD_DIM


def _inproj_body(x_ref, g_ref, w_ref, wvt_ref, bf_ref, lng_ref, lnb_ref, ws_ref, bs_ref, place_ref, qc_ref,
                 u_ref, q_ref, k_ref, v_ref, cum_ref, ygm_ref, carry_ref, *, tiles_per_seq):
    i = pl.program_id(0)

    @pl.when(i % tiles_per_seq == 0)
    def _():
        carry_ref[...] = jnp.zeros_like(carry_ref)

    tm = x_ref.shape[0]
    hb = _rms(x_ref[...], g_ref[...]).astype(BF16)

    def proj(lo, width):
        return _dot(hb, w_ref[:, lo:lo + width])

    _split_store(u_ref, proj(C_U, D_SSM))
    v_ref[...] = lax.dot_general(wvt_ref[...], hb, (((1,), (1,)), ((), ())),
                                 preferred_element_type=F32).astype(BF16)

    f = proj(C_F, LANES) + bf_ref[...]
    logf = (jnp.minimum(f, 0.0) - jnp.log(1.0 + jnp.exp(-jnp.abs(f)))) * LOG2E
    row = lax.broadcasted_iota(jnp.int32, (tm, tm), 0)
    col = lax.broadcasted_iota(jnp.int32, (tm, tm), 1)
    tri = (col <= row).astype(BF16)
    sums = _dot(tri, jnp.concatenate(_split3(logf), axis=1))
    cum = sums[:, :LANES] + sums[:, LANES:2 * LANES] + sums[:, 2 * LANES:] + carry_ref[...]
    cum_ref[...] = cum
    carry_ref[...] = cum[tm - 1:tm, :]

    lane = lax.broadcasted_iota(jnp.int32, (tm, LANES), 1)
    c1, c2, c3 = [jnp.where(lane < ATT_HEADS, piece.astype(F32), 0.0) for piece in _split3(cum)]
    packed = c1 + pltpu.roll(c2, ATT_HEADS, axis=1) + pltpu.roll(c3, 2 * ATT_HEADS, axis=1)
    placed = _dot(packed.astype(BF16), place_ref[...])
    upper = lax.broadcasted_iota(jnp.int32, (tm, LANES), 1) >= ATT_HEAD_DIM
    qf = proj(C_Q, D_ATT)
    kf = proj(C_K, D_ATT)
    for h in range(ATT_HEADS):
        pair = slice((h // 2) * LANES, (h // 2 + 1) * LANES)
        slot = slice(h * LANES, (h + 1) * LANES)
        own = upper if h % 2 else ~upper
        q_ref[:, slot] = jnp.where(own, qf[:, pair], qc_ref[:, slot]).astype(BF16)
        k_ref[:, slot] = jnp.where(own, kf[:, pair], placed[:, slot]).astype(BF16)

    zg = _gelu(proj(C_Z, 2 * D_GM))
    ug = zg[:, :D_GM]
    vg = zg[:, D_GM:]
    mean = jnp.mean(vg, axis=-1, keepdims=True)
    cen = vg - mean
    var = jnp.mean(cen * cen, axis=-1, keepdims=True)
    vn = (cen * lax.rsqrt(var + EPS) * lng_ref[...] + lnb_ref[...]).astype(BF16)
    lane_head = lax.broadcasted_iota(jnp.int32, (CHUNK, D_GM), 1) >> 6
    for c in range(tm // CHUNK):
        rows = slice(c * CHUNK, (c + 1) * CHUNK)
        vc = vn[rows, :]
        mixed = _dot(ws_ref[0], vc)
        for g in range(1, GM_HEADS):
            mixed = jnp.where(lane_head == g, _dot(ws_ref[g], vc), mixed)
        mixed = mixed + bs_ref[...]
        ygm_ref[rows, :] = (ug[rows, :] * mixed).astype(BF16)


def _attn_slot_constants():
    place = np.zeros((LANES, ATT_SLOTS), np.float32)
    qconst = np.zeros((1, ATT_SLOTS), np.float32)
    for h in range(ATT_HEADS):
        for piece in range(3):
            lane = h * LANES + _bias_lane(h) + piece
            place[piece * ATT_HEADS + h, lane] = 1.0
            qconst[0, lane] = -1.0
    return jnp.asarray(place, BF16), jnp.asarray(qconst)


def _inproj(x2, g, w, bf, lng, lnb, ws, bs, *, seq, tm):
    n = x2.shape[0]
    tps = seq // tm
    place, qconst = _attn_slot_constants()
    wvt = w[:, C_V:C_V + D_ATT].T
    row = lambda width: pl.BlockSpec((tm, width), lambda i: (i, 0))
    return pl.pallas_call(
        functools.partial(_inproj_body, tiles_per_seq=tps),
        grid=(n // tm,),
        in_specs=[row(D_MODEL), _resident((1, D_MODEL)), _resident((D_MODEL, D_IN_PAD)),
                  _resident((D_ATT, D_MODEL)),
                  _resident((1, LANES)), _resident((1, D_GM)), _resident((1, D_GM)),
                  _resident((GM_HEADS, CHUNK, CHUNK)), _resident((CHUNK, D_GM)),
                  _resident((LANES, ATT_SLOTS)), _resident((1, ATT_SLOTS))],
        out_specs=[_split_spec(tm), row(ATT_SLOTS), row(ATT_SLOTS),
                   pl.BlockSpec((None, D_ATT, tm), lambda i: (i // tps, 0, i % tps)),
                   row(LANES), row(D_GM)],
        out_shape=[jax.ShapeDtypeStruct((D_SSM // LANES, n, LANES), F32),
                   jax.ShapeDtypeStruct((n, ATT_SLOTS), BF16),
                   jax.ShapeDtypeStruct((n, ATT_SLOTS), BF16),
                   jax.ShapeDtypeStruct((n // seq, D_ATT, seq), BF16),
                   jax.ShapeDtypeStruct((n, LANES), F32),
                   jax.ShapeDtypeStruct((n, D_GM), BF16)],
        scratch_shapes=[pltpu.VMEM((1, LANES), F32)],
        compiler_params=_cparams("arbitrary"),
        name="inproj",
    )(x2, g, w, wvt, bf, lng, lnb, ws, bs, place, qconst)


def _ssm_tokens(u_ref, s, tr):
    return _split_load(u_ref, pl.ds(s, tr, stride=SSM_T)).astype(BF16)


def _ssm_state_body(u_ref, f_ref, s_ref):
    tr = s_ref.shape[0]
    acc = _dot(_ssm_tokens(u_ref, 0, tr), f_ref[:D_SSM, :])
    for s in range(1, SSM_T):
        acc += _dot(_ssm_tokens(u_ref, s, tr), f_ref[s * D_SSM:(s + 1) * D_SSM, :])
    s_ref[...] = acc


def _ssm_state(u, fmat, *, tr):
    nc = u.shape[1] // SSM_T
    return pl.pallas_call(
        _ssm_state_body,
        grid=(nc // tr,),
        in_specs=[_split_spec(tr * SSM_T), _resident((SSM_ROW, 2 * SSM_NSTATE))],
        out_specs=pl.BlockSpec((tr, 2 * SSM_NSTATE), lambda i: (i, 0)),
        out_shape=jax.ShapeDtypeStruct((nc, 2 * SSM_NSTATE), F32),
        compiler_params=_cparams("parallel"),
        name="ssm_state",
    )(u, fmat)


def _ssm_scan_body(s_ref, a_ref, xp_ref, st_ref):
    @pl.when(pl.program_id(1) == 0)
    def _():
        st_ref[...] = jnp.zeros_like(st_ref)

    tc = s_ref.shape[0]
    are = a_ref[:, :SSM_NSTATE]
    aim = a_ref[:, SSM_NSTATE:]

    def step(c8, carry):
        xre, xim = carry
        base = pl.multiple_of(c8 * SUBLANES, SUBLANES)
        blk = s_ref[pl.ds(base, SUBLANES), :]
        prev_re, prev_im = [], []
        for r in range(SUBLANES):
            prev_re.append(xre)
            prev_im.append(xim)
            sre = blk[r:r + 1, :SSM_NSTATE]
            sim = blk[r:r + 1, SSM_NSTATE:]
            xre, xim = are * xre - aim * xim + sre, are * xim + aim * xre + sim
        xp_ref[pl.ds(base, SUBLANES), :SSM_NSTATE] = jnp.concatenate(prev_re, axis=0)
        xp_ref[pl.ds(base, SUBLANES), SSM_NSTATE:] = jnp.concatenate(prev_im, axis=0)
        return xre, xim

    xre, xim = lax.fori_loop(0, tc // SUBLANES, step,
                             (st_ref[:, :SSM_NSTATE], st_ref[:, SSM_NSTATE:]))
    st_ref[:, :SSM_NSTATE] = xre
    st_ref[:, SSM_NSTATE:] = xim


def _ssm_scan(s3, a, *, tc):
    b, ncb, w = s3.shape
    spec = pl.BlockSpec((None, tc, w), lambda bi, ci: (bi, ci, 0))
    return pl.pallas_call(
        _ssm_scan_body,
        grid=(b, ncb // tc),
        in_specs=[spec, pl.BlockSpec((1, w), lambda bi, ci: (0, 0))],
        out_specs=spec,
        out_shape=jax.ShapeDtypeStruct(s3.shape, F32),
        scratch_shapes=[pltpu.VMEM((1, w), F32)],
        compiler_params=_cparams("parallel", "arbitrary"),
        name="ssm_scan",
    )(s3, a)


def _ssm_out_body(u_ref, xp_ref, m_ref, e_ref, y_ref):
    tr = xp_ref.shape[0]
    us = [_ssm_tokens(u_ref, s, tr) for s in range(SSM_T)]
    xp = xp_ref[...].astype(BF16)
    for t in range(SSM_T):
        cols = slice(t * D_SSM, (t + 1) * D_SSM)
        acc = _dot(xp, e_ref[:, cols])
        for s in range(t + 1):
            acc += _dot(us[s], m_ref[s * D_SSM:(s + 1) * D_SSM, cols])
        _split_store(y_ref, acc, pl.ds(t, tr, stride=SSM_T))


def _ssm_out(u, xp, mmat, emat, *, tr):
    nc = xp.shape[0]
    return pl.pallas_call(
        _ssm_out_body,
        grid=(nc // tr,),
        in_specs=[_split_spec(tr * SSM_T),
                  pl.BlockSpec((tr, 2 * SSM_NSTATE), lambda i: (i, 0)),
                  _resident((SSM_ROW, SSM_ROW)), _resident((2 * SSM_NSTATE, SSM_ROW))],
        out_specs=_split_spec(tr * SSM_T),
        out_shape=jax.ShapeDtypeStruct((D_SSM // LANES, nc * SSM_T, LANES), F32),
        compiler_params=_cparams("parallel"),
        name="ssm_out",
    )(u, xp, mmat, emat)


def _ssm_matrices(lam_re, lam_im, log_dt, b_re, b_im, c_re, c_im):
    t, g, p, h = SSM_T, SSM_GROUPS, SSM_STATE, SSM_GROUP
    ns, nch = g * p, g * h
    hp = lax.Precision.HIGHEST
    lr, li = lam_re.astype(F32).reshape(ns), lam_im.astype(F32).reshape(ns)
    dt = jnp.repeat(jnp.exp(log_dt.astype(F32)), p)
    steps = jnp.arange(t + 1, dtype=F32)[:, None]
    mag = jnp.exp((lr * dt)[None] * steps)
    ang = (li * dt)[None] * steps
    pr, pi = mag * jnp.cos(ang), mag * jnp.sin(ang)
    nr, ni = pr[1] - 1.0, pi[1]
    den = lr * lr + li * li
    zr, zi = ((nr * lr + ni * li) / den)[:, None], ((ni * lr - nr * li) / den)[:, None]
    br, bi = b_re.astype(F32).reshape(ns, h), b_im.astype(F32).reshape(ns, h)
    cr = c_re.astype(F32).transpose(0, 2, 1).reshape(ns, h)
    ci = c_im.astype(F32).transpose(0, 2, 1).reshape(ns, h)
    same = jnp.asarray(np.repeat(np.repeat(np.eye(g, dtype=np.float32), p, axis=0), h, axis=1))
    spread = lambda w: jnp.tile(w, (1, g)) * same
    bhr, bhi = spread(zr * br - zi * bi), spread(zr * bi + zi * br)
    chr_, chi = spread(cr), spread(ci)
    prc, pic = pr.T[:, :, None], pi.T[:, :, None]
    er = (prc * chr_[:, None, :] - pic * chi[:, None, :]).reshape(ns, (t + 1) * nch)
    ei = (prc * chi[:, None, :] + pic * chr_[:, None, :]).reshape(ns, (t + 1) * nch)

    k_all = (jnp.dot(bhr.T, er[:, :t * nch], precision=hp)
             - jnp.dot(bhi.T, ei[:, :t * nch], precision=hp))
    mmat = jnp.concatenate(
        [jnp.pad(k_all[:, :(t - s) * nch], ((0, 0), (s * nch, 0))) for s in range(t)], axis=0)

    prr, pir = pr[:t][::-1][:, None, :], pi[:t][::-1][:, None, :]
    bhrt, bhit = bhr.T[None], bhi.T[None]
    fmat = jnp.concatenate([(bhrt * prr - bhit * pir).reshape(t * nch, ns),
                            (bhrt * pir + bhit * prr).reshape(t * nch, ns)], axis=1)

    emat = jnp.concatenate([er[:, nch:], -ei[:, nch:]], axis=0)

    a_row = jnp.concatenate([pr[t][None], pi[t][None]], axis=1)
    return mmat.astype(BF16), fmat.astype(BF16), emat.astype(BF16), a_row


def _attn_body(q_ref, k_ref, v_ref, cq_ref, o_ref,
               m_ref, acc_ref, s_ref, mt_ref, p_ref, al_ref, *, tq):
    qi = pl.program_id(2)
    nh = ATT_STEP_HEADS
    tk = tq // 2
    keypos = lax.broadcasted_iota(jnp.int32, (tk, tq), 0)
    qpos = lax.broadcasted_iota(jnp.int32, (tk, tq), 1)

    def slot_lanes(j):
        return slice(j * LANES, (j + 1) * LANES)

    def scores(t, slot, diag=None):
        start = pl.multiple_of(t * tk, tk)
        for j in range(nh):
            kt = k_ref[pl.ds(start, tk), slot_lanes(j)]
            s = lax.dot_general(kt, q_ref[:, slot_lanes(j)], (((1,), (1,)), ((), ())),
                                preferred_element_type=F32)
            if diag is not None:
                s = jnp.where(keypos + diag <= qpos, s, -jnp.inf)
            s_ref[slot, j] = s
            mt_ref[slot, j] = jnp.max(s, axis=0, keepdims=True)

    def probs(slot):
        for j in range(nh):
            m = m_ref[j]
            cq = cq_ref[j:j + 1, :]
            m_new = jnp.maximum(m, mt_ref[slot, j] + cq)
            p_ref[slot, j] = jnp.exp2(s_ref[slot, j] + (cq - m_new)).astype(BF16)
            al_ref[slot, j] = jnp.exp2(m - m_new)
            m_ref[j] = m_new

    def values(t, slot):
        start = pl.multiple_of(t * tk, tk)
        for j in range(nh):
            vt = v_ref[j * ATT_HEAD_DIM:(j + 1) * ATT_HEAD_DIM, pl.ds(start, tk)]
            vt = jnp.concatenate([vt, jnp.ones_like(vt)], axis=0)
            acc_ref[j] = al_ref[slot, j] * acc_ref[j] + _dot(vt, p_ref[slot, j])

    def trip(t, slot, diag=None):
        values(t - 1, 1 - slot)
        probs(slot)
        scores(t + 1, 1 - slot, diag)

    for j in range(nh):
        m_ref[j] = jnp.full((1, tq), -jnp.inf, F32)
        acc_ref[j] = jnp.zeros((LANES, tq), F32)

    last = 2 * qi + 1

    @pl.when(qi == 0)
    def _():
        scores(0, 0, 0)
        probs(0)
        scores(1, 1, tk)

    @pl.when(qi > 0)
    def _():
        scores(0, 0)
        probs(0)
        scores(1, 1)

        def pair(u, carry):
            t = 2 * u + 1
            trip(t, 1)
            trip(t + 1, 0)
            return carry

        lax.fori_loop(0, qi - 1, pair, 0)
        trip(last - 2, 1, 0)
        trip(last - 1, 0, tk)

    values(last - 1, 0)
    probs(1)
    values(last, 1)
    for pair in range(nh // 2):
        outs = []
        for j in (2 * pair, 2 * pair + 1):
            acc = acc_ref[j]
            outs.append(acc[:ATT_HEAD_DIM, :] / acc[ATT_HEAD_DIM:, :])
        o_ref[:, slot_lanes(pair)] = jnp.concatenate(outs, axis=0).T.astype(BF16)


def _attention(q3, k3, vt4, cum3, *, tq):
    b, s, _ = q3.shape
    nh = ATT_STEP_HEADS
    return pl.pallas_call(
        functools.partial(_attn_body, tq=tq),
        scratch_shapes=[pltpu.VMEM((nh, 1, tq), F32), pltpu.VMEM((nh, LANES, tq), F32),
                        pltpu.VMEM((2, nh, tq // 2, tq), F32), pltpu.VMEM((2, nh, 1, tq), F32),
                        pltpu.VMEM((2, nh, tq // 2, tq), BF16), pltpu.VMEM((2, nh, 1, tq), F32)],
        grid=(b, ATT_HEADS // nh, s // tq),
        in_specs=[pl.BlockSpec((None, tq, nh * LANES), lambda bi, hg, qi: (bi, qi, hg)),
                  pl.BlockSpec((None, s, nh * LANES), lambda bi, hg, qi: (bi, 0, hg)),
                  pl.BlockSpec((None, nh * ATT_HEAD_DIM, s), lambda bi, hg, qi: (bi, hg, 0)),
                  pl.BlockSpec((None, None, nh, tq), lambda bi, hg, qi: (bi, hg, 0, qi))],
        out_specs=pl.BlockSpec((None, tq, nh * ATT_HEAD_DIM), lambda bi, hg, qi: (bi, qi, hg)),
        out_shape=jax.ShapeDtypeStruct((b, s, D_ATT), BF16),
        compiler_params=_cparams("parallel", "parallel", "arbitrary"),
        name="attention",
    )(q3, k3, vt4, cum3)


R_E1, R_E2, R_W1, R_W2, R_RANK1, R_RANK2 = range(6)


def _route(logits, cnt_ref):
    tm = logits.shape[0]
    lane = lax.broadcasted_iota(jnp.int32, logits.shape, 1).astype(F32)
    lg = jnp.where(lane < N_EXPERTS, logits, -jnp.inf)
    m1 = jnp.max(lg, axis=-1, keepdims=True)
    i1 = jnp.min(jnp.where(lg == m1, lane, float(LANES)), axis=-1, keepdims=True)
    lg2 = jnp.where(lane == i1, -jnp.inf, lg)
    m2 = jnp.max(lg2, axis=-1, keepdims=True)
    i2 = jnp.min(jnp.where(lg2 == m2, lane, float(LANES)), axis=-1, keepdims=True)
    e = jnp.exp(m2 - m1)
    w1 = 1.0 / (1.0 + e)
    w2 = e / (1.0 + e)

    hit1, hit2 = lane == i1, lane == i2
    oh1, oh2 = hit1.astype(BF16), hit2.astype(BF16)
    row = lax.broadcasted_iota(jnp.int32, (tm, tm), 0)
    col = lax.broadcasted_iota(jnp.int32, (tm, tm), 1)
    before = (col < row).astype(BF16)
    carry = cnt_ref[...]
    tot1 = jnp.sum(oh1.astype(F32), axis=0, keepdims=True)
    tot2 = jnp.sum(oh2.astype(F32), axis=0, keepdims=True)
    pos1 = _dot(before, oh1) + carry
    pos2 = _dot(before, oh2) + (carry + tot1)
    rank1 = jnp.sum(jnp.where(hit1, pos1, 0.0), axis=-1, keepdims=True)
    rank2 = jnp.sum(jnp.where(hit2, pos2, 0.0), axis=-1, keepdims=True)
    cnt_ref[...] = carry + tot1 + tot2

    rec = jnp.zeros(logits.shape, F32)
    for idx, val in ((R_E1, i1), (R_E2, i2), (R_W1, w1), (R_W2, w2), (R_RANK1, rank1), (R_RANK2, rank2)):
        rec = jnp.where(lane == idx, val, rec)
    return rec


def _outproj_body(*refs, with_router):
    if with_router:
        (x_ref, ys_ref, us_ref, ya_ref, yg_ref, d_ref, wglu_ref, gn_ref, w_ref, fg_ref, rw_ref,
         xo_ref, h_ref, route_ref, cnt_ref) = refs

        @pl.when(pl.program_id(0) == 0)
        def _():
            cnt_ref[...] = jnp.zeros_like(cnt_ref)
    else:
        (x_ref, ys_ref, us_ref, ya_ref, yg_ref, d_ref, wglu_ref, gn_ref, w_ref, fg_ref,
         xo_ref, h_ref) = refs

    tm = x_ref.shape[0]
    part = tm // OUTPROJ_PARTS
    o1, o2 = D_SSM, D_SSM + D_ATT
    for k in range(OUTPROJ_PARTS):
        rows = slice(k * part, (k + 1) * part)
        y1 = _gelu(_split_load(ys_ref, rows) + d_ref[...] * _split_load(us_ref, rows))
        y_ssm = y1 * _sigmoid(_dot(y1.astype(BF16), wglu_ref[...]))

        acc = _dot(_rms(y_ssm, gn_ref[:, :o1]).astype(BF16), w_ref[:o1, :])
        acc += _dot(_rms(ya_ref[rows, :].astype(F32), gn_ref[:, o1:o2]).astype(BF16), w_ref[o1:o2, :])
        acc += _dot(_rms(yg_ref[rows, :].astype(F32), gn_ref[:, o2:]).astype(BF16), w_ref[o2:, :])
        xn = x_ref[rows, :] + acc
        xo_ref[rows, :] = xn
        hn = _rms(xn, fg_ref[...])
        h_ref[rows, :] = hn.astype(h_ref.dtype)
        if with_router:
            h1, h2, h3 = _split3(hn)
            r1, r2, r3 = _split3(rw_ref[...])
            logits = (_dot(h1, r1) + _dot(h1, r2) + _dot(h2, r1)
                      + _dot(h1, r3) + _dot(h2, r2) + _dot(h3, r1))
            route_ref[rows, :] = _route(logits, cnt_ref)


def _outproj(x2, ys, us, ya, yg, d, wglu, gn, w, fg, rw, *, tm):
    n = x2.shape[0]
    with_router = rw is not None
    row = lambda width: pl.BlockSpec((tm, width), lambda i: (i, 0))
    in_specs = [row(D_MODEL), _split_spec(tm), _split_spec(tm), row(D_ATT), row(D_GM),
                _resident((1, D_SSM)), _resident((D_SSM, D_SSM)), _resident((1, D_MODEL)),
                _resident((D_MODEL, D_MODEL)), _resident((1, D_MODEL))]
    out_specs = [row(D_MODEL), row(D_MODEL)]
    out_shape = [jax.ShapeDtypeStruct((n, D_MODEL), F32),
                 jax.ShapeDtypeStruct((n, D_MODEL), F32 if with_router else BF16)]
    args = [x2, ys, us, ya, yg, d, wglu, gn, w, fg]
    if with_router:
        in_specs.append(_resident((D_MODEL, LANES)))
        out_specs += [row(LANES), pl.BlockSpec((1, LANES), lambda i: (0, 0))]
        out_shape += [jax.ShapeDtypeStruct((n, LANES), F32), jax.ShapeDtypeStruct((1, LANES), F32)]
        args.append(rw)
    return pl.pallas_call(
        functools.partial(_outproj_body, with_router=with_router),
        grid=(n // tm,),
        in_specs=in_specs, out_specs=out_specs, out_shape=out_shape,
        compiler_params=_cparams("arbitrary" if with_router else "parallel"),
        name="outproj_router" if with_router else "outproj",
    )(*args)


def _swiglu_tile(h, wg, wu, wd):
    a = _dot(h, wg)
    return _dot((a * _sigmoid(a) * _dot(h, wu)).astype(BF16), wd)


def _finish(x, acc, fin_ref, o_ref):
    xn = x + acc
    o_ref[...] = xn if fin_ref is None else _rms(xn, fin_ref[...])


def _dense_ffn_body(*refs, final_norm):
    if final_norm:
        h_ref, x_ref, wg_ref, wu_ref, wd_ref, fin_ref, o_ref, acc_ref = refs
    else:
        h_ref, x_ref, wg_ref, wu_ref, wd_ref, o_ref, acc_ref = refs
        fin_ref = None
    f = pl.program_id(1)

    @pl.when(f == 0)
    def _():
        acc_ref[...] = jnp.zeros_like(acc_ref)

    acc_ref[...] += _swiglu_tile(h_ref[...], wg_ref[...], wu_ref[...], wd_ref[...])

    @pl.when(f == pl.num_programs(1) - 1)
    def _():
        _finish(x_ref[...], acc_ref[...], fin_ref, o_ref)


def _dense_ffn(h, x2, wg, wu, wd, fin, *, tm, tf):
    n = x2.shape[0]
    dff = wg.shape[1]
    row = lambda: pl.BlockSpec((tm, D_MODEL), lambda i, f: (i, 0))
    mode = dict(pipeline_mode=pl.Buffered(1)) if tf == dff else {}
    in_specs = [row(), row(),
                pl.BlockSpec((D_MODEL, tf), lambda i, f: (0, f), **mode),
                pl.BlockSpec((D_MODEL, tf), lambda i, f: (0, f), **mode),
                pl.BlockSpec((tf, D_MODEL), lambda i, f: (f, 0), **mode)]
    args = [h, x2, wg, wu, wd]
    if fin is not None:
        in_specs.append(pl.BlockSpec((1, D_MODEL), lambda i, f: (0, 0)))
        args.append(fin)
    return pl.pallas_call(
        functools.partial(_dense_ffn_body, final_norm=fin is not None),
        grid=(n // tm, dff // tf),
        in_specs=in_specs, out_specs=row(),
        out_shape=jax.ShapeDtypeStruct((n, D_MODEL), F32),
        scratch_shapes=[pltpu.VMEM((tm, D_MODEL), F32)],
        compiler_params=_cparams("parallel", "arbitrary"),
        name="dense_ffn",
    )(*args)


def _row_copy(src, src_row, dst, dst_row, sem):
    return pltpu.make_async_copy(src.at[pl.ds(src_row, 1)], dst.at[pl.ds(dst_row, 1)], sem)


def _moe_dispatch_body(dest_ref, h_ref, xs_in_ref, xs_ref, sem):
    del xs_in_ref
    tm = h_ref.shape[0]

    def issue(r, carry):
        for k in range(2):
            _row_copy(h_ref, r, xs_ref, dest_ref[k, r], sem).start()
        return carry

    lax.fori_loop(0, tm, issue, 0, unroll=8)
    for k in range(2):
        pltpu.make_async_copy(h_ref, xs_ref.at[pl.ds(0, tm)], sem).wait()


def _moe_dispatch(dest, h, xs_zero, *, tm):
    n = h.shape[0]
    return pl.pallas_call(
        _moe_dispatch_body,
        grid=(n // tm,),
        in_specs=[pl.BlockSpec((None, 2, tm), lambda i: (i, 0, 0), memory_space=pltpu.SMEM),
                  pl.BlockSpec((tm, D_MODEL), lambda i: (i, 0)),
                  pl.BlockSpec(memory_space=pl.ANY)],
        out_specs=pl.BlockSpec(memory_space=pl.ANY),
        out_shape=jax.ShapeDtypeStruct(xs_zero.shape, xs_zero.dtype),
        scratch_shapes=[pltpu.SemaphoreType.DMA(())],
        input_output_aliases={2: 0},
        compiler_params=pltpu.CompilerParams(dimension_semantics=("arbitrary",),
                                             vmem_limit_bytes=VMEM_LIMIT, has_side_effects=True),
        name="moe_dispatch",
    )(dest, h, xs_zero)


def _moe_ffn_body(te_ref, nu_ref, xs_ref, wg_ref, wu_ref, wd_ref, ys_ref, acc_ref):
    del te_ref
    i = pl.program_id(0)
    f = pl.program_id(1)

    @pl.when(i < nu_ref[0])
    def _():
        @pl.when(f == 0)
        def _():
            acc_ref[...] = jnp.zeros_like(acc_ref)

        acc_ref[...] += _swiglu_tile(xs_ref[...].astype(BF16), wg_ref[...], wu_ref[...], wd_ref[...])

        @pl.when(f == pl.num_programs(1) - 1)
        def _():
            ys_ref[...] = acc_ref[...]

    @pl.when((i >= nu_ref[0]) & (f == pl.num_programs(1) - 1))
    def _():
        ys_ref[...] = jnp.zeros_like(ys_ref)


def _moe_ffn(tile_expert, n_used, xs, wg, wu, wd, *, tme, tf):
    rows = xs.shape[0]
    dff = wg.shape[-1]
    nf = dff // tf
    row_idx = lambda i, f, te, nu: (jnp.minimum(i, nu[0] - 1), 0)
    col = lambda i, f, nu: jnp.where(i < nu[0], f, nf - 1)
    grid_spec = pltpu.PrefetchScalarGridSpec(
        num_scalar_prefetch=2,
        grid=(rows // tme, nf),
        in_specs=[pl.BlockSpec((tme, D_MODEL), row_idx),
                  pl.BlockSpec((None, D_MODEL, tf), lambda i, f, te, nu: (te[i], 0, col(i, f, nu))),
                  pl.BlockSpec((None, D_MODEL, tf), lambda i, f, te, nu: (te[i], 0, col(i, f, nu))),
                  pl.BlockSpec((None, tf, D_MODEL), lambda i, f, te, nu: (te[i], col(i, f, nu), 0))],
        out_specs=pl.BlockSpec((tme, D_MODEL), lambda i, f, te, nu: (i, 0)),
        scratch_shapes=[pltpu.VMEM((tme, D_MODEL), F32)])
    return pl.pallas_call(
        _moe_ffn_body,
        grid_spec=grid_spec,
        out_shape=jax.ShapeDtypeStruct((rows, D_MODEL), F32),
        compiler_params=_cparams("arbitrary", "arbitrary"),
        name="moe_ffn",
    )(tile_expert, n_used, xs, wg, wu, wd)


def _moe_combine_body(*refs, final_norm):
    if final_norm:
        dest_ref, x_ref, route_ref, ys_ref, fin_ref, o_ref, buf_ref, sem = refs
    else:
        dest_ref, x_ref, route_ref, ys_ref, o_ref, buf_ref, sem = refs
        fin_ref = None
    tm = x_ref.shape[0]

    def issue(r, carry):
        for k in range(2):
            _row_copy(ys_ref, dest_ref[k, r], buf_ref.at[k], r, sem).start()
        return carry

    lax.fori_loop(0, tm, issue, 0, unroll=8)
    for k in range(2):
        pltpu.make_async_copy(ys_ref.at[pl.ds(0, tm)], buf_ref.at[k], sem).wait()

    route = route_ref[...]
    lane = lax.broadcasted_iota(jnp.int32, route.shape, 1)
    w1 = jnp.sum(jnp.where(lane == R_W1, route, 0.0), axis=-1, keepdims=True)
    w2 = jnp.sum(jnp.where(lane == R_W2, route, 0.0), axis=-1, keepdims=True)
    _finish(x_ref[...], w1 * buf_ref[0] + w2 * buf_ref[1], fin_ref, o_ref)


def _moe_combine(dest, x2, route, ys, fin, *, tm):
    n = x2.shape[0]
    row = lambda width: pl.BlockSpec((tm, width), lambda i: (i, 0))
    in_specs = [pl.BlockSpec((None, 2, tm), lambda i: (i, 0, 0), memory_space=pltpu.SMEM),
                row(D_MODEL), row(LANES), pl.BlockSpec(memory_space=pl.ANY)]
    args = [dest, x2, route, ys]
    if fin is not None:
        in_specs.append(pl.BlockSpec((1, D_MODEL), lambda i: (0, 0)))
        args.append(fin)
    return pl.pallas_call(
        functools.partial(_moe_combine_body, final_norm=fin is not None),
        grid=(n // tm,),
        in_specs=in_specs, out_specs=row(D_MODEL),
        out_shape=jax.ShapeDtypeStruct((n, D_MODEL), F32),
        scratch_shapes=[pltpu.VMEM((2, tm, D_MODEL), F32), pltpu.SemaphoreType.DMA(())],
        compiler_params=_cparams("arbitrary"),
        name="moe_combine",
    )(*args)


def _moe_layout(route, counts, *, tm, tme):
    n = route.shape[0]
    cnt = counts[0, :N_EXPERTS].astype(jnp.int32)
    padded = (cnt + (tme - 1)) // tme * tme
    ends = jnp.cumsum(padded)
    starts = ends - padded
    experts = jnp.arange(N_EXPERTS, dtype=F32)

    def rows(e_lane, rank_lane):
        start = jnp.sum(jnp.where(route[:, e_lane, None] == experts[None], starts[None], 0), axis=1)
        return start + route[:, rank_lane].astype(jnp.int32)

    dest = jnp.stack([rows(R_E1, R_RANK1), rows(R_E2, R_RANK2)])
    dest = dest.reshape(2, n // tm, tm).transpose(1, 0, 2)
    n_tiles = (2 * n) // tme + N_EXPERTS
    tile_start = jnp.arange(n_tiles, dtype=jnp.int32) * tme
    tile_expert = jnp.minimum(jnp.sum(tile_start[:, None] >= ends[None, :], axis=1), N_EXPERTS - 1)
    n_used = (ends[-1] // tme).reshape(1)
    tile_expert = jnp.where(tile_start < ends[-1], tile_expert, tile_expert[jnp.maximum(n_used[0] - 1, 0)])
    return dest, tile_expert.astype(jnp.int32), n_used.astype(jnp.int32), n_tiles * tme


def _largest_tile(total, cap, mult):
    best = mult
    for t in range(mult, min(total, cap) + 1, mult):
        if total % t == 0:
            best = t
    return best


def _mxu_tile(total, cap):
    if total % MXU_WIDTH == 0:
        return _largest_tile(total, cap, MXU_WIDTH)
    return _largest_tile(total, cap, LANES)


def _pad_lanes(a):
    return jnp.pad(a, [(0, 0)] * (a.ndim - 1) + [(0, LANES - a.shape[-1])])


def kernel(x, mix_norm_g, w_in, b_forget, ssm_lambda_re, ssm_lambda_im, ssm_log_dt, ssm_b_re, ssm_b_im, ssm_c_re, ssm_c_im, ssm_d, ssm_w_glu, gm_ln_g, gm_ln_b, gm_w_s, gm_b_s, group_norm_g, w_out, ffn_norm_g, dense_w_gate, dense_w_up, dense_w_down, router_w, moe_w_gate, moe_w_up, moe_w_down, final_norm_g):
    bsz, seq, _ = x.shape
    n = bsz * seq
    depth = w_in.shape[0]
    assert seq % CHUNK == 0 and seq % SSM_T == 0
    tm = _largest_tile(seq, 512, CHUNK)
    tq = _largest_tile(seq, 1024, 2 * LANES)
    nc = n // SSM_T
    ncb = seq // SSM_T
    tr = _largest_tile(nc, 512, 8)
    tc = _largest_tile(ncb, 128, 8)

    x2 = x.reshape(n, D_MODEL).astype(F32)
    tril = jnp.tril(jnp.ones((CHUNK, CHUNK), F32))
    i0, i1, i2, i3, i4 = (D_SSM, D_SSM + D_ATT, D_SSM + 2 * D_ATT, D_SSM + 3 * D_ATT,
                          D_SSM + 3 * D_ATT + ATT_HEADS)
    for layer in range(depth):
        wl = w_in[layer]
        w_re = jnp.concatenate(
            [wl[:, :i0], wl[:, i0:i1] * (ATT_HEAD_DIM ** -0.5 * LOG2E), wl[:, i1:i3], wl[:, i4:],
             _pad_lanes(wl[:, i3:i4])], axis=1).astype(BF16)
        ws = (gm_w_s[layer].astype(F32) * tril[None]).astype(BF16)
        bs = jnp.repeat(gm_b_s[layer].astype(F32).T, GM_HEAD_DIM, axis=1)
        u, q, k, v, cum, ygm = _inproj(
            x2, mix_norm_g[layer].reshape(1, D_MODEL), w_re,
            _pad_lanes(b_forget[layer].reshape(1, ATT_HEADS).astype(F32)),
            gm_ln_g[layer].reshape(1, D_GM), gm_ln_b[layer].reshape(1, D_GM), ws, bs,
            seq=seq, tm=tm)

        mmat, fmat, emat, a_row = _ssm_matrices(
            ssm_lambda_re[layer], ssm_lambda_im[layer], ssm_log_dt[layer], ssm_b_re[layer],
            ssm_b_im[layer], ssm_c_re[layer], ssm_c_im[layer])
        s_loc = _ssm_state(u, fmat, tr=tr)
        xprev = _ssm_scan(s_loc.reshape(bsz, ncb, 2 * SSM_NSTATE), a_row, tc=tc)
        y_ssm = _ssm_out(u, xprev.reshape(nc, 2 * SSM_NSTATE), mmat, emat, tr=tr)

        cum_rows = cum[:, :ATT_HEADS].reshape(bsz, seq, ATT_HEADS // ATT_STEP_HEADS, ATT_STEP_HEADS)
        cum_rows = cum_rows.transpose(0, 2, 3, 1)
        y_att = _attention(q.reshape(bsz, seq, ATT_SLOTS), k.reshape(bsz, seq, ATT_SLOTS), v,
                           cum_rows, tq=tq).reshape(n, D_ATT)

        is_moe = layer % 2 == 1
        j = layer // 2
        rw = _pad_lanes(router_w[j].astype(F32)) if is_moe else None
        outs = _outproj(x2, y_ssm, u, y_att, ygm, ssm_d[layer].reshape(1, D_SSM).astype(F32),
                        ssm_w_glu[layer].astype(BF16), group_norm_g[layer].reshape(1, D_MODEL),
                        w_out[layer].astype(BF16), ffn_norm_g[layer].reshape(1, D_MODEL), rw, tm=tm)
        fin = final_norm_g.reshape(1, D_MODEL) if layer == depth - 1 else None
        if is_moe:
            x_mid, h, route, counts = outs
            dffe = moe_w_gate.shape[-1]
            tme = min(MOE_ROW_TILE, 2 * n)
            dest, tile_expert, n_used, rows = _moe_layout(route, counts, tm=tm, tme=tme)
            xs = _moe_dispatch(dest, h, jnp.zeros((rows, D_MODEL), F32), tm=tm)
            ys = _moe_ffn(tile_expert, n_used, xs, moe_w_gate[j].astype(BF16),
                          moe_w_up[j].astype(BF16), moe_w_down[j].astype(BF16), tme=tme,
                          tf=_mxu_tile(dffe, 1792))
            x2 = _moe_combine(dest, x_mid, route, ys, fin, tm=tm)
        else:
            x_mid, h = outs
            dff = dense_w_gate.shape[-1]
            x2 = _dense_ffn(h, x_mid, dense_w_gate[j].astype(BF16), dense_w_up[j].astype(BF16),
                            dense_w_down[j].astype(BF16), fin, tm=tm,
                            tf=_mxu_tile(dff, 2816))
    return x2.reshape(bsz, seq, D_MODEL).astype(x.dtype)
```

```python
import functools
import math

import jax
import jax.numpy as jnp
import numpy as np
from jax import lax
from jax.experimental import pallas as pl
from jax.experimental.pallas import tpu as pltpu

F32 = jnp.float32
BF16 = jnp.bfloat16

D_MODEL = 1024
D_SSM = 256
D_ATT = 512
D_GM = 256
SSM_GROUP = 16
SSM_GROUPS = 16
SSM_STATE = 64
ATT_HEADS = 8
ATT_HEAD_DIM = 64
GM_HEADS = 4
GM_HEAD_DIM = 64
CHUNK = 128
N_EXPERTS = 8
EPS = 1e-6
LOG2E = math.log2(math.e)

LANES = 128
SUBLANES = 8
MXU_WIDTH = 256
V7X_VMEM_BYTES = 64 * 1024 * 1024
ATT_SLOTS = ATT_HEADS * LANES
MOE_ROW_TILE = 512
OUTPROJ_PARTS = 2
ATT_STEP_HEADS = 2
SSM_T = 8
SSM_ROW = SSM_T * D_SSM
SSM_NSTATE = SSM_GROUPS * SSM_STATE

C_U = 0
C_Q = C_U + D_SSM
C_K = C_Q + D_ATT
C_Z = C_K + D_ATT
C_F = C_Z + 2 * D_GM
D_IN_PAD = C_F + LANES

VMEM_LIMIT = V7X_VMEM_BYTES * 7 // 8


def _cparams(*sem):
    return pltpu.CompilerParams(dimension_semantics=sem, vmem_limit_bytes=VMEM_LIMIT)


def _resident(shape):
    nd = len(shape)
    return pl.BlockSpec(shape, lambda *_: (0,) * nd, pipeline_mode=pl.Buffered(1))


def _rms(x, g):
    return x * lax.rsqrt(jnp.mean(x * x, axis=-1, keepdims=True) + EPS) * g


def _gelu(x):
    c = math.sqrt(2.0 / math.pi)
    return 0.5 * x * (1.0 + jnp.tanh(c * (x + 0.044715 * (x * x * x))))


def _sigmoid(x):
    return 1.0 / (1.0 + jnp.exp(-x))


def _split3(x):
    p1 = x.astype(BF16)
    r1 = x - p1.astype(F32)
    p2 = r1.astype(BF16)
    r2 = r1 - p2.astype(F32)
    return p1, p2, r2.astype(BF16)


def _dot(a, b):
    return jnp.dot(a, b, preferred_element_type=F32)


def _split_spec(rows):
    return pl.BlockSpec((D_SSM // LANES, rows, LANES), lambda i: (0, i, 0))


def _split_store(ref, value, rows=slice(None)):
    for h in range(D_SSM // LANES):
        ref[h, rows, :] = value[:, h * LANES:(h + 1) * LANES]


def _split_load(ref, rows=slice(None)):
    return jnp.concatenate([ref[h, rows, :] for h in range(D_SSM // LANES)], axis=1)


def _bias_lane(head):
    return 0 if head % 2 else ATT_HEAD_DIM


def _inproj_body(x_ref, g_ref, w_ref, wvt_ref, bf_ref, lng_ref, lnb_ref, ws_ref, bs_ref, place_ref, qc_ref,
                 u_ref, q_ref, k_ref, v_ref, cum_ref, ygm_ref, carry_ref, *, tiles_per_seq):
    i = pl.program_id(0)

    @pl.when(i % tiles_per_seq == 0)
    def _():
        carry_ref[...] = jnp.zeros_like(carry_ref)

    tm = x_ref.shape[0]
    hb = _rms(x_ref[...], g_ref[...]).astype(BF16)

    def proj(lo, width):
        return _dot(hb, w_ref[:, lo:lo + width])

    _split_store(u_ref, proj(C_U, D_SSM))
    v_ref[...] = lax.dot_general(wvt_ref[...], hb, (((1,), (1,)), ((), ())),
                                 preferred_element_type=F32).astype(BF16)

    f = proj(C_F, LANES) + bf_ref[...]
    logf = (jnp.minimum(f, 0.0) - jnp.log(1.0 + jnp.exp(-jnp.abs(f)))) * LOG2E
    row = lax.broadcasted_iota(jnp.int32, (tm, tm), 0)
    col = lax.broadcasted_iota(jnp.int32, (tm, tm), 1)
    tri = (col <= row).astype(BF16)
    sums = _dot(tri, jnp.concatenate(_split3(logf), axis=1))
    cum = sums[:, :LANES] + sums[:, LANES:2 * LANES] + sums[:, 2 * LANES:] + carry_ref[...]
    cum_ref[...] = cum
    carry_ref[...] = cum[tm - 1:tm, :]

    lane = lax.broadcasted_iota(jnp.int32, (tm, LANES), 1)
    c1, c2, c3 = [jnp.where(lane < ATT_HEADS, piece.astype(F32), 0.0) for piece in _split3(cum)]
    packed = c1 + pltpu.roll(c2, ATT_HEADS, axis=1) + pltpu.roll(c3, 2 * ATT_HEADS, axis=1)
    placed = _dot(packed.astype(BF16), place_ref[...])
    upper = lax.broadcasted_iota(jnp.int32, (tm, LANES), 1) >= ATT_HEAD_DIM
    qf = proj(C_Q, D_ATT)
    kf = proj(C_K, D_ATT)
    for h in range(ATT_HEADS):
        pair = slice((h // 2) * LANES, (h // 2 + 1) * LANES)
        slot = slice(h * LANES, (h + 1) * LANES)
        own = upper if h % 2 else ~upper
        q_ref[:, slot] = jnp.where(own, qf[:, pair], qc_ref[:, slot]).astype(BF16)
        k_ref[:, slot] = jnp.where(own, kf[:, pair], placed[:, slot]).astype(BF16)

    zg = _gelu(proj(C_Z, 2 * D_GM))
    ug = zg[:, :D_GM]
    vg = zg[:, D_GM:]
    mean = jnp.mean(vg, axis=-1, keepdims=True)
    cen = vg - mean
    var = jnp.mean(cen * cen, axis=-1, keepdims=True)
    vn = (cen * lax.rsqrt(var + EPS) * lng_ref[...] + lnb_ref[...]).astype(BF16)
    lane_head = lax.broadcasted_iota(jnp.int32, (CHUNK, D_GM), 1) >> 6
    for c in range(tm // CHUNK):
        rows = slice(c * CHUNK, (c + 1) * CHUNK)
        vc = vn[rows, :]
        mixed = _dot(ws_ref[0], vc)
        for g in range(1, GM_HEADS):
            mixed = jnp.where(lane_head == g, _dot(ws_ref[g], vc), mixed)
        mixed = mixed + bs_ref[...]
        ygm_ref[rows, :] = (ug[rows, :] * mixed).astype(BF16)


def _attn_slot_constants():
    place = np.zeros((LANES, ATT_SLOTS), np.float32)
    qconst = np.zeros((1, ATT_SLOTS), np.float32)
    for h in range(ATT_HEADS):
        for piece in range(3):
            lane = h * LANES + _bias_lane(h) + piece
            place[piece * ATT_HEADS + h, lane] = 1.0
            qconst[0, lane] = -1.0
    return jnp.asarray(place, BF16), jnp.asarray(qconst)


def _inproj(x2, g, w, wvt, bf, lng, lnb, ws, bs, *, seq, tm):
    n = x2.shape[0]
    tps = seq // tm
    place, qconst = _attn_slot_constants()
    row = lambda width: pl.BlockSpec((tm, width), lambda i: (i, 0))
    return pl.pallas_call(
        functools.partial(_inproj_body, tiles_per_seq=tps),
        grid=(n // tm,),
        in_specs=[row(D_MODEL), _resident((1, D_MODEL)), _resident((D_MODEL, D_IN_PAD)),
                  _resident((D_ATT, D_MODEL)),
                  _resident((1, LANES)), _resident((1, D_GM)), _resident((1, D_GM)),
                  _resident((GM_HEADS, CHUNK, CHUNK)), _resident((CHUNK, D_GM)),
                  _resident((LANES, ATT_SLOTS)), _resident((1, ATT_SLOTS))],
        out_specs=[_split_spec(tm), row(ATT_SLOTS), row(ATT_SLOTS),
                   pl.BlockSpec((None, D_ATT, tm), lambda i: (i // tps, 0, i % tps)),
                   row(LANES), row(D_GM)],
        out_shape=[jax.ShapeDtypeStruct((D_SSM // LANES, n, LANES), F32),
                   jax.ShapeDtypeStruct((n, ATT_SLOTS), BF16),
                   jax.ShapeDtypeStruct((n, ATT_SLOTS), BF16),
                   jax.ShapeDtypeStruct((n // seq, D_ATT, seq), BF16),
                   jax.ShapeDtypeStruct((n, LANES), F32),
                   jax.ShapeDtypeStruct((n, D_GM), BF16)],
        scratch_shapes=[pltpu.VMEM((1, LANES), F32)],
        compiler_params=_cparams("arbitrary"),
        name="inproj",
    )(x2, g, w, wvt, bf, lng, lnb, ws, bs, place, qconst)


def _ssm_tokens(u_ref, s, tr):
    return _split_load(u_ref, pl.ds(s, tr, stride=SSM_T)).astype(BF16)


def _ssm_state_body(u_ref, f_ref, s_ref):
    tr = s_ref.shape[0]
    acc = _dot(_ssm_tokens(u_ref, 0, tr), f_ref[:D_SSM, :])
    for s in range(1, SSM_T):
        acc += _dot(_ssm_tokens(u_ref, s, tr), f_ref[s * D_SSM:(s + 1) * D_SSM, :])
    s_ref[...] = acc


def _ssm_state(u, fmat, *, tr):
    nc = u.shape[1] // SSM_T
    return pl.pallas_call(
        _ssm_state_body,
        grid=(nc // tr,),
        in_specs=[_split_spec(tr * SSM_T), _resident((SSM_ROW, 2 * SSM_NSTATE))],
        out_specs=pl.BlockSpec((tr, 2 * SSM_NSTATE), lambda i: (i, 0)),
        out_shape=jax.ShapeDtypeStruct((nc, 2 * SSM_NSTATE), F32),
        compiler_params=_cparams("parallel"),
        name="ssm_state",
    )(u, fmat)


def _ssm_scan_body(s_ref, a_ref, xp_ref, st_ref):
    @pl.when(pl.program_id(1) == 0)
    def _():
        st_ref[...] = jnp.zeros_like(st_ref)

    tc = s_ref.shape[0]
    are = a_ref[:, :SSM_NSTATE]
    aim = a_ref[:, SSM_NSTATE:]

    def step(c8, carry):
        xre, xim = carry
        base = pl.multiple_of(c8 * SUBLANES, SUBLANES)
        blk = s_ref[pl.ds(base, SUBLANES), :]
        prev_re, prev_im = [], []
        for r in range(SUBLANES):
            prev_re.append(xre)
            prev_im.append(xim)
            sre = blk[r:r + 1, :SSM_NSTATE]
            sim = blk[r:r + 1, SSM_NSTATE:]
            xre, xim = are * xre - aim * xim + sre, are * xim + aim * xre + sim
        xp_ref[pl.ds(base, SUBLANES), :SSM_NSTATE] = jnp.concatenate(prev_re, axis=0)
        xp_ref[pl.ds(base, SUBLANES), SSM_NSTATE:] = jnp.concatenate(prev_im, axis=0)
        return xre, xim

    xre, xim = lax.fori_loop(0, tc // SUBLANES, step,
                             (st_ref[:, :SSM_NSTATE], st_ref[:, SSM_NSTATE:]))
    st_ref[:, :SSM_NSTATE] = xre
    st_ref[:, SSM_NSTATE:] = xim


def _ssm_scan(s3, a, *, tc):
    b, ncb, w = s3.shape
    spec = pl.BlockSpec((None, tc, w), lambda bi, ci: (bi, ci, 0))
    return pl.pallas_call(
        _ssm_scan_body,
        grid=(b, ncb // tc),
        in_specs=[spec, pl.BlockSpec((1, w), lambda bi, ci: (0, 0))],
        out_specs=spec,
        out_shape=jax.ShapeDtypeStruct(s3.shape, F32),
        scratch_shapes=[pltpu.VMEM((1, w), F32)],
        compiler_params=_cparams("parallel", "arbitrary"),
        name="ssm_scan",
    )(s3, a)


def _ssm_out_body(u_ref, xp_ref, m_ref, e_ref, y_ref):
    tr = xp_ref.shape[0]
    us = [_ssm_tokens(u_ref, s, tr) for s in range(SSM_T)]
    xp = xp_ref[...].astype(BF16)
    for t in range(SSM_T):
        cols = slice(t * D_SSM, (t + 1) * D_SSM)
        acc = _dot(xp, e_ref[:, cols])
        for s in range(t + 1):
            acc += _dot(us[s], m_ref[s * D_SSM:(s + 1) * D_SSM, cols])
        _split_store(y_ref, acc, pl.ds(t, tr, stride=SSM_T))


def _ssm_out(u, xp, mmat, emat, *, tr):
    nc = xp.shape[0]
    return pl.pallas_call(
        _ssm_out_body,
        grid=(nc // tr,),
        in_specs=[_split_spec(tr * SSM_T),
                  pl.BlockSpec((tr, 2 * SSM_NSTATE), lambda i: (i, 0)),
                  _resident((SSM_ROW, SSM_ROW)), _resident((2 * SSM_NSTATE, SSM_ROW))],
        out_specs=_split_spec(tr * SSM_T),
        out_shape=jax.ShapeDtypeStruct((D_SSM // LANES, nc * SSM_T, LANES), F32),
        compiler_params=_cparams("parallel"),
        name="ssm_out",
    )(u, xp, mmat, emat)


def _ssm_matrices(lam_re, lam_im, log_dt, b_re, b_im, c_re, c_im):
    t, g, p, h = SSM_T, SSM_GROUPS, SSM_STATE, SSM_GROUP
    ns, nch = g * p, g * h
    hp = lax.Precision.HIGHEST
    lr, li = lam_re.astype(F32).reshape(ns), lam_im.astype(F32).reshape(ns)
    dt = jnp.repeat(jnp.exp(log_dt.astype(F32)), p)
    steps = jnp.arange(t + 1, dtype=F32)[:, None]
    mag = jnp.exp((lr * dt)[None] * steps)
    ang = (li * dt)[None] * steps
    pr, pi = mag * jnp.cos(ang), mag * jnp.sin(ang)
    nr, ni = pr[1] - 1.0, pi[1]
    den = lr * lr + li * li
    zr, zi = ((nr * lr + ni * li) / den)[:, None], ((ni * lr - nr * li) / den)[:, None]
    br, bi = b_re.astype(F32).reshape(ns, h), b_im.astype(F32).reshape(ns, h)
    cr = c_re.astype(F32).transpose(0, 2, 1).reshape(ns, h)
    ci = c_im.astype(F32).transpose(0, 2, 1).reshape(ns, h)
    same = jnp.asarray(np.repeat(np.repeat(np.eye(g, dtype=np.float32), p, axis=0), h, axis=1))
    spread = lambda w: jnp.tile(w, (1, g)) * same
    bhr, bhi = spread(zr * br - zi * bi), spread(zr * bi + zi * br)
    chr_, chi = spread(cr), spread(ci)
    prc, pic = pr.T[:, :, None], pi.T[:, :, None]
    er = (prc * chr_[:, None, :] - pic * chi[:, None, :]).reshape(ns, (t + 1) * nch)
    ei = (prc * chi[:, None, :] + pic * chr_[:, None, :]).reshape(ns, (t + 1) * nch)

    k_all = (jnp.dot(bhr.T, er[:, :t * nch], precision=hp)
             - jnp.dot(bhi.T, ei[:, :t * nch], precision=hp))
    mmat = jnp.concatenate(
        [jnp.pad(k_all[:, :(t - s) * nch], ((0, 0), (s * nch, 0))) for s in range(t)], axis=0)

    prr, pir = pr[:t][::-1][:, None, :], pi[:t][::-1][:, None, :]
    bhrt, bhit = bhr.T[None], bhi.T[None]
    fmat = jnp.concatenate([(bhrt * prr - bhit * pir).reshape(t * nch, ns),
                            (bhrt * pir + bhit * prr).reshape(t * nch, ns)], axis=1)

    emat = jnp.concatenate([er[:, nch:], -ei[:, nch:]], axis=0)

    a_row = jnp.concatenate([pr[t][None], pi[t][None]], axis=1)
    return mmat.astype(BF16), fmat.astype(BF16), emat.astype(BF16), a_row


def _attn_body(q_ref, k_ref, v_ref, cq_ref, o_ref,
               m_ref, acc_ref, s_ref, mt_ref, p_ref, al_ref, *, tq):
    qi = pl.program_id(2)
    nh = ATT_STEP_HEADS
    tk = tq // 2
    keypos = lax.broadcasted_iota(jnp.int32, (tk, tq), 0)
    qpos = lax.broadcasted_iota(jnp.int32, (tk, tq), 1)

    def slot_lanes(j):
        return slice(j * LANES, (j + 1) * LANES)

    def scores(t, slot, diag=None):
        start = pl.multiple_of(t * tk, tk)
        for j in range(nh):
            kt = k_ref[pl.ds(start, tk), slot_lanes(j)]
            s = lax.dot_general(kt, q_ref[:, slot_lanes(j)], (((1,), (1,)), ((), ())),
                                preferred_element_type=F32)
            if diag is not None:
                s = jnp.where(keypos + diag <= qpos, s, -jnp.inf)
            s_ref[slot, j] = s
            mt_ref[slot, j] = jnp.max(s, axis=0, keepdims=True)

    def probs(slot):
        for j in range(nh):
            m = m_ref[j]
            cq = cq_ref[j:j + 1, :]
            m_new = jnp.maximum(m, mt_ref[slot, j] + cq)
            p_ref[slot, j] = jnp.exp2(s_ref[slot, j] + (cq - m_new)).astype(BF16)
            al_ref[slot, j] = jnp.exp2(m - m_new)
            m_ref[j] = m_new

    def values(t, slot):
        start = pl.multiple_of(t * tk, tk)
        for j in range(nh):
            vt = v_ref[j * ATT_HEAD_DIM:(j + 1) * ATT_HEAD_DIM, pl.ds(start, tk)]
            vt = jnp.concatenate([vt, jnp.ones_like(vt)], axis=0)
            acc_ref[j] = al_ref[slot, j] * acc_ref[j] + _dot(vt, p_ref[slot, j])

    def trip(t, slot, diag=None):
        values(t - 1, 1 - slot)
        probs(slot)
        scores(t + 1, 1 - slot, diag)

    for j in range(nh):
        m_ref[j] = jnp.full((1, tq), -jnp.inf, F32)
        acc_ref[j] = jnp.zeros((LANES, tq), F32)

    last = 2 * qi + 1

    @pl.when(qi == 0)
    def _():
        scores(0, 0, 0)
        probs(0)
        scores(1, 1, tk)

    @pl.when(qi > 0)
    def _():
        scores(0, 0)
        probs(0)
        scores(1, 1)

        def pair(u, carry):
            t = 2 * u + 1
            trip(t, 1)
            trip(t + 1, 0)
            return carry

        lax.fori_loop(0, qi - 1, pair, 0)
        trip(last - 2, 1, 0)
        trip(last - 1, 0, tk)

    values(last - 1, 0)
    probs(1)
    values(last, 1)
    for pair in range(nh // 2):
        outs = []
        for j in (2 * pair, 2 * pair + 1):
            acc = acc_ref[j]
            outs.append(acc[:ATT_HEAD_DIM, :] / acc[ATT_HEAD_DIM:, :])
        o_ref[:, slot_lanes(pair)] = jnp.concatenate(outs, axis=0).T.astype(BF16)


def _attention(q3, k3, vt4, cum3, *, tq):
    b, s, _ = q3.shape
    nh = ATT_STEP_HEADS
    return pl.pallas_call(
        functools.partial(_attn_body, tq=tq),
        scratch_shapes=[pltpu.VMEM((nh, 1, tq), F32), pltpu.VMEM((nh, LANES, tq), F32),
                        pltpu.VMEM((2, nh, tq // 2, tq), F32), pltpu.VMEM((2, nh, 1, tq), F32),
                        pltpu.VMEM((2, nh, tq // 2, tq), BF16), pltpu.VMEM((2, nh, 1, tq), F32)],
        grid=(b, ATT_HEADS // nh, s // tq),
        in_specs=[pl.BlockSpec((None, tq, nh * LANES), lambda bi, hg, qi: (bi, qi, hg)),
                  pl.BlockSpec((None, s, nh * LANES), lambda bi, hg, qi: (bi, 0, hg)),
                  pl.BlockSpec((None, nh * ATT_HEAD_DIM, s), lambda bi, hg, qi: (bi, hg, 0)),
                  pl.BlockSpec((None, None, nh, tq), lambda bi, hg, qi: (bi, hg, 0, qi))],
        out_specs=pl.BlockSpec((None, tq, nh * ATT_HEAD_DIM), lambda bi, hg, qi: (bi, qi, hg)),
        out_shape=jax.ShapeDtypeStruct((b, s, D_ATT), BF16),
        compiler_params=_cparams("parallel", "parallel", "arbitrary"),
        name="attention",
    )(q3, k3, vt4, cum3)


R_E1, R_E2, R_W1, R_W2, R_RANK1, R_RANK2 = range(6)


def _route(logits, cnt_ref):
    tm = logits.shape[0]
    lane = lax.broadcasted_iota(jnp.int32, logits.shape, 1).astype(F32)
    lg = jnp.where(lane < N_EXPERTS, logits, -jnp.inf)
    m1 = jnp.max(lg, axis=-1, keepdims=True)
    i1 = jnp.min(jnp.where(lg == m1, lane, float(LANES)), axis=-1, keepdims=True)
    lg2 = jnp.where(lane == i1, -jnp.inf, lg)
    m2 = jnp.max(lg2, axis=-1, keepdims=True)
    i2 = jnp.min(jnp.where(lg2 == m2, lane, float(LANES)), axis=-1, keepdims=True)
    e = jnp.exp(m2 - m1)
    w1 = 1.0 / (1.0 + e)
    w2 = e / (1.0 + e)

    hit1, hit2 = lane == i1, lane == i2
    oh1, oh2 = hit1.astype(BF16), hit2.astype(BF16)
    row = lax.broadcasted_iota(jnp.int32, (tm, tm), 0)
    col = lax.broadcasted_iota(jnp.int32, (tm, tm), 1)
    before = (col < row).astype(BF16)
    carry = cnt_ref[...]
    tot1 = jnp.sum(oh1.astype(F32), axis=0, keepdims=True)
    tot2 = jnp.sum(oh2.astype(F32), axis=0, keepdims=True)
    pos1 = _dot(before, oh1) + carry
    pos2 = _dot(before, oh2) + (carry + tot1)
    rank1 = jnp.sum(jnp.where(hit1, pos1, 0.0), axis=-1, keepdims=True)
    rank2 = jnp.sum(jnp.where(hit2, pos2, 0.0), axis=-1, keepdims=True)
    cnt_ref[...] = carry + tot1 + tot2

    rec = jnp.zeros(logits.shape, F32)
    for idx, val in ((R_E1, i1), (R_E2, i2), (R_W1, w1), (R_W2, w2), (R_RANK1, rank1), (R_RANK2, rank2)):
        rec = jnp.where(lane == idx, val, rec)
    return rec


def _outproj_body(*refs, with_router):
    if with_router:
        (x_ref, ys_ref, us_ref, ya_ref, yg_ref, d_ref, wglu_ref, gn_ref, w_ref, fg_ref, rw_ref,
         xo_ref, h_ref, route_ref, cnt_ref) = refs

        @pl.when(pl.program_id(0) == 0)
        def _():
            cnt_ref[...] = jnp.zeros_like(cnt_ref)
    else:
        (x_ref, ys_ref, us_ref, ya_ref, yg_ref, d_ref, wglu_ref, gn_ref, w_ref, fg_ref,
         xo_ref, h_ref) = refs

    tm = x_ref.shape[0]
    part = tm // OUTPROJ_PARTS
    o1, o2 = D_SSM, D_SSM + D_ATT
    for k in range(OUTPROJ_PARTS):
        rows = slice(k * part, (k + 1) * part)
        y1 = _gelu(_split_load(ys_ref, rows) + d_ref[...] * _split_load(us_ref, rows))
        y_ssm = y1 * _sigmoid(_dot(y1.astype(BF16), wglu_ref[...]))

        acc = _dot(_rms(y_ssm, gn_ref[:, :o1]).astype(BF16), w_ref[:o1, :])
        acc += _dot(_rms(ya_ref[rows, :].astype(F32), gn_ref[:, o1:o2]).astype(BF16), w_ref[o1:o2, :])
        acc += _dot(_rms(yg_ref[rows, :].astype(F32), gn_ref[:, o2:]).astype(BF16), w_ref[o2:, :])
        xn = x_ref[rows, :] + acc
        xo_ref[rows, :] = xn
        hn = _rms(xn, fg_ref[...])
        h_ref[rows, :] = hn.astype(h_ref.dtype)
        if with_router:
            rw = rw_ref[...]
            prod = sum(_dot(piece, rw) for piece in _split3(hn))
            logits = (prod + pltpu.roll(prod, LANES - N_EXPERTS, axis=1)
                      + pltpu.roll(prod, LANES - 2 * N_EXPERTS, axis=1))
            route_ref[rows, :] = _route(logits, cnt_ref)


def _outproj(x2, ys, us, ya, yg, d, wglu, gn, w, fg, rw, *, tm):
    n = x2.shape[0]
    with_router = rw is not None
    row = lambda width: pl.BlockSpec((tm, width), lambda i: (i, 0))
    in_specs = [row(D_MODEL), _split_spec(tm), _split_spec(tm), row(D_ATT), row(D_GM),
                _resident((1, D_SSM)), _resident((D_SSM, D_SSM)), _resident((1, D_MODEL)),
                _resident((D_MODEL, D_MODEL)), _resident((1, D_MODEL))]
    out_specs = [row(D_MODEL), row(D_MODEL)]
    out_shape = [jax.ShapeDtypeStruct((n, D_MODEL), F32),
                 jax.ShapeDtypeStruct((n, D_MODEL), F32 if with_router else BF16)]
    args = [x2, ys, us, ya, yg, d, wglu, gn, w, fg]
    if with_router:
        in_specs.append(_resident((D_MODEL, LANES)))
        out_specs += [row(LANES), pl.BlockSpec((1, LANES), lambda i: (0, 0))]
        out_shape += [jax.ShapeDtypeStruct((n, LANES), F32), jax.ShapeDtypeStruct((1, LANES), F32)]
        args.append(rw)
    return pl.pallas_call(
        functools.partial(_outproj_body, with_router=with_router),
        grid=(n // tm,),
        in_specs=in_specs, out_specs=out_specs, out_shape=out_shape,
        compiler_params=_cparams("arbitrary" if with_router else "parallel"),
        name="outproj_router" if with_router else "outproj",
    )(*args)


def _swiglu_tile(h, wg, wu, wd):
    a = _dot(h, wg)
    return _dot((a * _sigmoid(a) * _dot(h, wu)).astype(BF16), wd)


def _finish(x, acc, fin_ref, o_ref):
    xn = x + acc
    o_ref[...] = xn if fin_ref is None else _rms(xn, fin_ref[...])


def _dense_ffn_body(*refs, final_norm):
    if final_norm:
        h_ref, x_ref, wg_ref, wu_ref, wd_ref, fin_ref, o_ref, acc_ref = refs
    else:
        h_ref, x_ref, wg_ref, wu_ref, wd_ref, o_ref, acc_ref = refs
        fin_ref = None
    f = pl.program_id(1)

    @pl.when(f == 0)
    def _():
        acc_ref[...] = jnp.zeros_like(acc_ref)

    acc_ref[...] += _swiglu_tile(h_ref[...], wg_ref[...], wu_ref[...], wd_ref[...])

    @pl.when(f == pl.num_programs(1) - 1)
    def _():
        _finish(x_ref[...], acc_ref[...], fin_ref, o_ref)


def _dense_ffn(h, x2, wg, wu, wd, fin, *, tm, tf):
    n = x2.shape[0]
    dff = wg.shape[1]
    row = lambda: pl.BlockSpec((tm, D_MODEL), lambda i, f: (i, 0))
    mode = dict(pipeline_mode=pl.Buffered(1)) if tf == dff else {}
    in_specs = [row(), row(),
                pl.BlockSpec((D_MODEL, tf), lambda i, f: (0, f), **mode),
                pl.BlockSpec((D_MODEL, tf), lambda i, f: (0, f), **mode),
                pl.BlockSpec((tf, D_MODEL), lambda i, f: (f, 0), **mode)]
    args = [h, x2, wg, wu, wd]
    if fin is not None:
        in_specs.append(pl.BlockSpec((1, D_MODEL), lambda i, f: (0, 0)))
        args.append(fin)
    return pl.pallas_call(
        functools.partial(_dense_ffn_body, final_norm=fin is not None),
        grid=(n // tm, dff // tf),
        in_specs=in_specs, out_specs=row(),
        out_shape=jax.ShapeDtypeStruct((n, D_MODEL), F32),
        scratch_shapes=[pltpu.VMEM((tm, D_MODEL), F32)],
        compiler_params=_cparams("parallel", "arbitrary"),
        name="dense_ffn",
    )(*args)


def _row_copy(src, src_row, dst, dst_row, sem):
    return pltpu.make_async_copy(src.at[pl.ds(src_row, 1)], dst.at[pl.ds(dst_row, 1)], sem)


def _moe_dispatch_body(dest_ref, h_ref, xs_in_ref, xs_ref, sem):
    del xs_in_ref
    tm = h_ref.shape[0]

    def issue(r, carry):
        for k in range(2):
            _row_copy(h_ref, r, xs_ref, dest_ref[k, r], sem).start()
        return carry

    lax.fori_loop(0, tm, issue, 0, unroll=8)
    for k in range(2):
        pltpu.make_async_copy(h_ref, xs_ref.at[pl.ds(0, tm)], sem).wait()


def _moe_dispatch(dest, h, xs_zero, *, tm):
    n = h.shape[0]
    return pl.pallas_call(
        _moe_dispatch_body,
        grid=(n // tm,),
        in_specs=[pl.BlockSpec((None, 2, tm), lambda i: (i, 0, 0), memory_space=pltpu.SMEM),
                  pl.BlockSpec((tm, D_MODEL), lambda i: (i, 0)),
                  pl.BlockSpec(memory_space=pl.ANY)],
        out_specs=pl.BlockSpec(memory_space=pl.ANY),
        out_shape=jax.ShapeDtypeStruct(xs_zero.shape, xs_zero.dtype),
        scratch_shapes=[pltpu.SemaphoreType.DMA(())],
        input_output_aliases={2: 0},
        compiler_params=pltpu.CompilerParams(dimension_semantics=("arbitrary",),
                                             vmem_limit_bytes=VMEM_LIMIT, has_side_effects=True),
        name="moe_dispatch",
    )(dest, h, xs_zero)


def _moe_ffn_body(te_ref, nu_ref, xs_ref, wg_ref, wu_ref, wd_ref, ys_ref, acc_ref):
    del te_ref
    i = pl.program_id(0)
    f = pl.program_id(1)

    @pl.when(i < nu_ref[0])
    def _():
        @pl.when(f == 0)
        def _():
            acc_ref[...] = jnp.zeros_like(acc_ref)

        acc_ref[...] += _swiglu_tile(xs_ref[...].astype(BF16), wg_ref[...], wu_ref[...], wd_ref[...])

        @pl.when(f == pl.num_programs(1) - 1)
        def _():
            ys_ref[...] = acc_ref[...]

    @pl.when((i >= nu_ref[0]) & (f == pl.num_programs(1) - 1))
    def _():
        ys_ref[...] = jnp.zeros_like(ys_ref)


def _moe_ffn(tile_expert, n_used, xs, wg, wu, wd, *, tme, tf):
    rows = xs.shape[0]
    dff = wg.shape[-1]
    nf = dff // tf
    row_idx = lambda i, f, te, nu: (jnp.minimum(i, nu[0] - 1), 0)
    col = lambda i, f, nu: jnp.where(i < nu[0], f, nf - 1)
    mode = dict(pipeline_mode=pl.Buffered(1)) if nf == 1 else {}
    grid_spec = pltpu.PrefetchScalarGridSpec(
        num_scalar_prefetch=2,
        grid=(rows // tme, nf),
        in_specs=[pl.BlockSpec((tme, D_MODEL), row_idx),
                  pl.BlockSpec((None, D_MODEL, tf), lambda i, f, te, nu: (te[i], 0, col(i, f, nu)), **mode),
                  pl.BlockSpec((None, D_MODEL, tf), lambda i, f, te, nu: (te[i], 0, col(i, f, nu)), **mode),
                  pl.BlockSpec((None, tf, D_MODEL), lambda i, f, te, nu: (te[i], col(i, f, nu), 0), **mode)],
        out_specs=pl.BlockSpec((tme, D_MODEL), lambda i, f, te, nu: (i, 0)),
        scratch_shapes=[pltpu.VMEM((tme, D_MODEL), F32)])
    return pl.pallas_call(
        _moe_ffn_body,
        grid_spec=grid_spec,
        out_shape=jax.ShapeDtypeStruct((rows, D_MODEL), F32),
        compiler_params=_cparams("arbitrary", "arbitrary"),
        name="moe_ffn",
    )(tile_expert, n_used, xs, wg, wu, wd)


def _moe_combine_body(*refs, final_norm):
    if final_norm:
        dest_ref, x_ref, route_ref, ys_ref, fin_ref, o_ref, buf_ref, sem = refs
    else:
        dest_ref, x_ref, route_ref, ys_ref, o_ref, buf_ref, sem = refs
        fin_ref = None
    tm = x_ref.shape[0]

    def issue(r, carry):
        for k in range(2):
            _row_copy(ys_ref, dest_ref[k, r], buf_ref.at[k], r, sem).start()
        return carry

    lax.fori_loop(0, tm, issue, 0, unroll=8)
    for k in range(2):
        pltpu.make_async_copy(ys_ref.at[pl.ds(0, tm)], buf_ref.at[k], sem).wait()

    route = route_ref[...]
    lane = lax.broadcasted_iota(jnp.int32, route.shape, 1)
    w1 = jnp.sum(jnp.where(lane == R_W1, route, 0.0), axis=-1, keepdims=True)
    w2 = jnp.sum(jnp.where(lane == R_W2, route, 0.0), axis=-1, keepdims=True)
    _finish(x_ref[...], w1 * buf_ref[0] + w2 * buf_ref[1], fin_ref, o_ref)


def _moe_combine(dest, x2, route, ys, fin, *, tm):
    n = x2.shape[0]
    row = lambda width: pl.BlockSpec((tm, width), lambda i: (i, 0))
    in_specs = [pl.BlockSpec((None, 2, tm), lambda i: (i, 0, 0), memory_space=pltpu.SMEM),
                row(D_MODEL), row(LANES), pl.BlockSpec(memory_space=pl.ANY)]
    args = [dest, x2, route, ys]
    if fin is not None:
        in_specs.append(pl.BlockSpec((1, D_MODEL), lambda i: (0, 0)))
        args.append(fin)
    return pl.pallas_call(
        functools.partial(_moe_combine_body, final_norm=fin is not None),
        grid=(n // tm,),
        in_specs=in_specs, out_specs=row(D_MODEL),
        out_shape=jax.ShapeDtypeStruct((n, D_MODEL), F32),
        scratch_shapes=[pltpu.VMEM((2, tm, D_MODEL), F32), pltpu.SemaphoreType.DMA(())],
        compiler_params=_cparams("arbitrary"),
        name="moe_combine",
    )(*args)


def _moe_layout(route, counts, *, tm, tme):
    n = route.shape[0]
    cnt = counts[0, :N_EXPERTS].astype(jnp.int32)
    padded = (cnt + (tme - 1)) // tme * tme
    ends = jnp.cumsum(padded)
    starts = ends - padded
    experts = jnp.arange(N_EXPERTS, dtype=F32)

    def rows(e_lane, rank_lane):
        start = jnp.sum(jnp.where(route[:, e_lane, None] == experts[None], starts[None], 0), axis=1)
        return start + route[:, rank_lane].astype(jnp.int32)

    dest = jnp.stack([rows(R_E1, R_RANK1), rows(R_E2, R_RANK2)])
    dest = dest.reshape(2, n // tm, tm).transpose(1, 0, 2)
    n_tiles = (2 * n) // tme + N_EXPERTS
    tile_start = jnp.arange(n_tiles, dtype=jnp.int32) * tme
    tile_expert = jnp.minimum(jnp.sum(tile_start[:, None] >= ends[None, :], axis=1), N_EXPERTS - 1)
    n_used = (ends[-1] // tme).reshape(1)
    tile_expert = jnp.where(tile_start < ends[-1], tile_expert, tile_expert[jnp.maximum(n_used[0] - 1, 0)])
    return dest, tile_expert.astype(jnp.int32), n_used.astype(jnp.int32), n_tiles * tme


def _largest_tile(total, cap, mult):
    best = mult
    for t in range(mult, min(total, cap) + 1, mult):
        if total % t == 0:
            best = t
    return best


def _mxu_tile(total, cap):
    if total % MXU_WIDTH == 0:
        return _largest_tile(total, cap, MXU_WIDTH)
    return _largest_tile(total, cap, LANES)


def _pad_lanes(a):
    return jnp.pad(a, [(0, 0)] * (a.ndim - 1) + [(0, LANES - a.shape[-1])])


def kernel(x, mix_norm_g, w_in, b_forget, ssm_lambda_re, ssm_lambda_im, ssm_log_dt, ssm_b_re, ssm_b_im, ssm_c_re, ssm_c_im, ssm_d, ssm_w_glu, gm_ln_g, gm_ln_b, gm_w_s, gm_b_s, group_norm_g, w_out, ffn_norm_g, dense_w_gate, dense_w_up, dense_w_down, router_w, moe_w_gate, moe_w_up, moe_w_down, final_norm_g):
    bsz, seq, _ = x.shape
    n = bsz * seq
    depth = w_in.shape[0]
    assert seq % CHUNK == 0 and seq % SSM_T == 0
    tm = _largest_tile(seq, 512, CHUNK)
    tq = _largest_tile(seq, 1024, 2 * LANES)
    nc = n // SSM_T
    ncb = seq // SSM_T
    tr = _largest_tile(nc, 512, 8)
    tc = _largest_tile(ncb, 128, 8)

    x2 = x.reshape(n, D_MODEL).astype(F32)
    tril = jnp.tril(jnp.ones((CHUNK, CHUNK), F32))
    i0, i1, i2, i3, i4 = (D_SSM, D_SSM + D_ATT, D_SSM + 2 * D_ATT, D_SSM + 3 * D_ATT,
                          D_SSM + 3 * D_ATT + ATT_HEADS)
    for layer in range(depth):
        wl = w_in[layer]
        w_re = jnp.concatenate(
            [wl[:, :i0], wl[:, i0:i1] * (ATT_HEAD_DIM ** -0.5 * LOG2E), wl[:, i1:i2], wl[:, i4:],
             _pad_lanes(wl[:, i3:i4])], axis=1).astype(BF16)
        ws = (gm_w_s[layer].astype(F32) * tril[None]).astype(BF16)
        bs = jnp.repeat(gm_b_s[layer].astype(F32).T, GM_HEAD_DIM, axis=1)
        u, q, k, v, cum, ygm = _inproj(
            x2, mix_norm_g[layer].reshape(1, D_MODEL), w_re, wl[:, i2:i3].T.astype(BF16),
            _pad_lanes(b_forget[layer].reshape(1, ATT_HEADS).astype(F32)),
            gm_ln_g[layer].reshape(1, D_GM), gm_ln_b[layer].reshape(1, D_GM), ws, bs,
            seq=seq, tm=tm)

        mmat, fmat, emat, a_row = _ssm_matrices(
            ssm_lambda_re[layer], ssm_lambda_im[layer], ssm_log_dt[layer], ssm_b_re[layer],
            ssm_b_im[layer], ssm_c_re[layer], ssm_c_im[layer])
        s_loc = _ssm_state(u, fmat, tr=tr)
        xprev = _ssm_scan(s_loc.reshape(bsz, ncb, 2 * SSM_NSTATE), a_row, tc=tc)
        y_ssm = _ssm_out(u, xprev.reshape(nc, 2 * SSM_NSTATE), mmat, emat, tr=tr)

        cum_rows = cum[:, :ATT_HEADS].reshape(bsz, seq, ATT_HEADS // ATT_STEP_HEADS, ATT_STEP_HEADS)
        cum_rows = cum_rows.transpose(0, 2, 3, 1)
        y_att = _attention(q.reshape(bsz, seq, ATT_SLOTS), k.reshape(bsz, seq, ATT_SLOTS), v,
                           cum_rows, tq=tq).reshape(n, D_ATT)

        is_moe = layer % 2 == 1
        j = layer // 2
        rw = (_pad_lanes(jnp.concatenate(_split3(router_w[j].astype(F32)), axis=1))
              if is_moe else None)
        outs = _outproj(x2, y_ssm, u, y_att, ygm, ssm_d[layer].reshape(1, D_SSM).astype(F32),
                        ssm_w_glu[layer].astype(BF16), group_norm_g[layer].reshape(1, D_MODEL),
                        w_out[layer].astype(BF16), ffn_norm_g[layer].reshape(1, D_MODEL), rw, tm=tm)
        fin = final_norm_g.reshape(1, D_MODEL) if layer == depth - 1 else None
        if is_moe:
            x_mid, h, route, counts = outs
            dffe = moe_w_gate.shape[-1]
            tme = min(MOE_ROW_TILE, 2 * n)
            dest, tile_expert, n_used, rows = _moe_layout(route, counts, tm=tm, tme=tme)
            xs = _moe_dispatch(dest, h, jnp.zeros((rows, D_MODEL), F32), tm=tm)
            ys = _moe_ffn(tile_expert, n_used, xs, moe_w_gate[j].astype(BF16),
                          moe_w_up[j].astype(BF16), moe_w_down[j].astype(BF16), tme=tme,
                          tf=_mxu_tile(dffe, 3584))
            x2 = _moe_combine(dest, x_mid, route, ys, fin, tm=tm)
        else:
            x_mid, h = outs
            dff = dense_w_gate.shape[-1]
            x2 = _dense_ffn(h, x_mid, dense_w_gate[j].astype(BF16), dense_w_up[j].astype(BF16),
                            dense_w_down[j].astype(BF16), fin, tm=tm,
                            tf=_mxu_tile(dff, 2816))
    return x2.reshape(bsz, seq, D_MODEL).astype(x.dtype)
```

```python
import functools
import math

import jax
import jax.numpy as jnp
import numpy as np
from jax import lax
from jax.experimental import pallas as pl
from jax.experimental.pallas import tpu as pltpu

F32 = jnp.float32
BF16 = jnp.bfloat16

D_MODEL = 1024
D_SSM = 256
D_ATT = 512
D_GM = 256
SSM_GROUP = 16
SSM_GROUPS = 16
SSM_STATE = 64
ATT_HEADS = 8
ATT_HEAD_DIM = 64
GM_HEADS = 4
GM_HEAD_DIM = 64
CHUNK = 128
N_EXPERTS = 8
EPS = 1e-6
LOG2E = math.log2(math.e)

LANES = 128
SUBLANES = 8
MXU_WIDTH = 256
V7X_VMEM_BYTES = 64 * 1024 * 1024
ATT_SLOTS = ATT_HEADS * LANES
MOE_ROW_TILE = 512
OUTPROJ_PARTS = 2
ATT_STEP_HEADS = 2
SSM_T = 8
SSM_ROW = SSM_T * D_SSM
SSM_NSTATE = SSM_GROUPS * SSM_STATE

C_U = 0
C_Q = C_U + D_SSM
C_K = C_Q + D_ATT
C_Z = C_K + D_ATT
C_F = C_Z + 2 * D_GM
D_IN_PAD = C_F + LANES

VMEM_LIMIT = V7X_VMEM_BYTES * 7 // 8


def _cparams(*sem):
    return pltpu.CompilerParams(dimension_semantics=sem, vmem_limit_bytes=VMEM_LIMIT)


def _resident(shape):
    nd = len(shape)
    return pl.BlockSpec(shape, lambda *_: (0,) * nd, pipeline_mode=pl.Buffered(1))


def _rms(x, g):
    return x * lax.rsqrt(jnp.mean(x * x, axis=-1, keepdims=True) + EPS) * g


def _gelu(x):
    c = math.sqrt(2.0 / math.pi)
    return 0.5 * x * (1.0 + jnp.tanh(c * (x + 0.044715 * (x * x * x))))


def _sigmoid(x):
    return 1.0 / (1.0 + jnp.exp(-x))


def _split3(x):
    p1 = x.astype(BF16)
    r1 = x - p1.astype(F32)
    p2 = r1.astype(BF16)
    r2 = r1 - p2.astype(F32)
    return p1, p2, r2.astype(BF16)


def _dot(a, b):
    return jnp.dot(a, b, preferred_element_type=F32)


def _split_spec(rows):
    return pl.BlockSpec((D_SSM // LANES, rows, LANES), lambda i: (0, i, 0))


def _split_store(ref, value, rows=slice(None)):
    for h in range(D_SSM // LANES):
        ref[h, rows, :] = value[:, h * LANES:(h + 1) * LANES]


def _split_load(ref, rows=slice(None)):
    return jnp.concatenate([ref[h, rows, :] for h in range(D_SSM // LANES)], axis=1)


def _bias_lane(head):
    return 0 if head % 2 else ATT_HEAD_DIM


def _inproj_body(x_ref, g_ref, w_ref, wvt_ref, bf_ref, lng_ref, lnb_ref, ws_ref, bs_ref, place_ref, qc_ref,
                 u_ref, q_ref, k_ref, v_ref, cum_ref, ygm_ref, carry_ref, *, tiles_per_seq):
    i = pl.program_id(0)

    @pl.when(i % tiles_per_seq == 0)
    def _():
        carry_ref[...] = jnp.zeros_like(carry_ref)

    tm = x_ref.shape[0]
    hb = _rms(x_ref[...], g_ref[...]).astype(BF16)

    def proj(lo, width):
        return _dot(hb, w_ref[:, lo:lo + width])

    _split_store(u_ref, proj(C_U, D_SSM))
    v_ref[...] = lax.dot_general(wvt_ref[...], hb, (((1,), (1,)), ((), ())),
                                 preferred_element_type=F32).astype(BF16)

    f = proj(C_F, LANES) + bf_ref[...]
    logf = (jnp.minimum(f, 0.0) - jnp.log(1.0 + jnp.exp(-jnp.abs(f)))) * LOG2E
    row = lax.broadcasted_iota(jnp.int32, (tm, tm), 0)
    col = lax.broadcasted_iota(jnp.int32, (tm, tm), 1)
    tri = (col <= row).astype(BF16)
    sums = _dot(tri, jnp.concatenate(_split3(logf), axis=1))
    cum = sums[:, :LANES] + sums[:, LANES:2 * LANES] + sums[:, 2 * LANES:] + carry_ref[...]
    cum_ref[...] = cum
    carry_ref[...] = cum[tm - 1:tm, :]

    lane = lax.broadcasted_iota(jnp.int32, (tm, LANES), 1)
    c1, c2, c3 = [jnp.where(lane < ATT_HEADS, piece.astype(F32), 0.0) for piece in _split3(cum)]
    packed = c1 + pltpu.roll(c2, ATT_HEADS, axis=1) + pltpu.roll(c3, 2 * ATT_HEADS, axis=1)
    placed = _dot(packed.astype(BF16), place_ref[...])
    upper = lax.broadcasted_iota(jnp.int32, (tm, LANES), 1) >= ATT_HEAD_DIM
    qf = proj(C_Q, D_ATT)
    kf = proj(C_K, D_ATT)
    for h in range(ATT_HEADS):
        pair = slice((h // 2) * LANES, (h // 2 + 1) * LANES)
        slot = slice(h * LANES, (h + 1) * LANES)
        own = upper if h % 2 else ~upper
        q_ref[:, slot] = jnp.where(own, qf[:, pair], qc_ref[:, slot]).astype(BF16)
        k_ref[:, slot] = jnp.where(own, kf[:, pair], placed[:, slot]).astype(BF16)

    zg = _gelu(proj(C_Z, 2 * D_GM))
    ug = zg[:, :D_GM]
    vg = zg[:, D_GM:]
    mean = jnp.mean(vg, axis=-1, keepdims=True)
    cen = vg - mean
    var = jnp.mean(cen * cen, axis=-1, keepdims=True)
    vn = (cen * lax.rsqrt(var + EPS) * lng_ref[...] + lnb_ref[...]).astype(BF16)
    lane_head = lax.broadcasted_iota(jnp.int32, (CHUNK, D_GM), 1) >> 6
    for c in range(tm // CHUNK):
        rows = slice(c * CHUNK, (c + 1) * CHUNK)
        vc = vn[rows, :]
        mixed = _dot(ws_ref[0], vc)
        for g in range(1, GM_HEADS):
            mixed = jnp.where(lane_head == g, _dot(ws_ref[g], vc), mixed)
        mixed = mixed + bs_ref[...]
        ygm_ref[rows, :] = (ug[rows, :] * mixed).astype(BF16)


def _attn_slot_constants():
    place = np.zeros((LANES, ATT_SLOTS), np.float32)
    qconst = np.zeros((1, ATT_SLOTS), np.float32)
    for h in range(ATT_HEADS):
        for piece in range(3):
            lane = h * LANES + _bias_lane(h) + piece
            place[piece * ATT_HEADS + h, lane] = 1.0
            qconst[0, lane] = -1.0
    return jnp.asarray(place, BF16), jnp.asarray(qconst)


def _inproj(x2, g, w, wvt, bf, lng, lnb, ws, bs, *, seq, tm):
    n = x2.shape[0]
    tps = seq // tm
    place, qconst = _attn_slot_constants()
    row = lambda width: pl.BlockSpec((tm, width), lambda i: (i, 0))
    return pl.pallas_call(
        functools.partial(_inproj_body, tiles_per_seq=tps),
        grid=(n // tm,),
        in_specs=[row(D_MODEL), _resident((1, D_MODEL)), _resident((D_MODEL, D_IN_PAD)),
                  _resident((D_ATT, D_MODEL)),
                  _resident((1, LANES)), _resident((1, D_GM)), _resident((1, D_GM)),
                  _resident((GM_HEADS, CHUNK, CHUNK)), _resident((CHUNK, D_GM)),
                  _resident((LANES, ATT_SLOTS)), _resident((1, ATT_SLOTS))],
        out_specs=[_split_spec(tm), row(ATT_SLOTS), row(ATT_SLOTS),
                   pl.BlockSpec((None, D_ATT, tm), lambda i: (i // tps, 0, i % tps)),
                   row(LANES), row(D_GM)],
        out_shape=[jax.ShapeDtypeStruct((D_SSM // LANES, n, LANES), F32),
                   jax.ShapeDtypeStruct((n, ATT_SLOTS), BF16),
                   jax.ShapeDtypeStruct((n, ATT_SLOTS), BF16),
                   jax.ShapeDtypeStruct((n // seq, D_ATT, seq), BF16),
                   jax.ShapeDtypeStruct((n, LANES), F32),
                   jax.ShapeDtypeStruct((n, D_GM), BF16)],
        scratch_shapes=[pltpu.VMEM((1, LANES), F32)],
        compiler_params=_cparams("arbitrary"),
        name="inproj",
    )(x2, g, w, wvt, bf, lng, lnb, ws, bs, place, qconst)


def _ssm_tokens(u_ref, s, tr):
    return _split_load(u_ref, pl.ds(s, tr, stride=SSM_T)).astype(BF16)


def _ssm_state_body(u_ref, f_ref, s_ref):
    tr = s_ref.shape[0]
    acc = _dot(_ssm_tokens(u_ref, 0, tr), f_ref[:D_SSM, :])
    for s in range(1, SSM_T):
        acc += _dot(_ssm_tokens(u_ref, s, tr), f_ref[s * D_SSM:(s + 1) * D_SSM, :])
    s_ref[...] = acc


def _ssm_state(u, fmat, *, tr):
    nc = u.shape[1] // SSM_T
    return pl.pallas_call(
        _ssm_state_body,
        grid=(nc // tr,),
        in_specs=[_split_spec(tr * SSM_T), _resident((SSM_ROW, 2 * SSM_NSTATE))],
        out_specs=pl.BlockSpec((tr, 2 * SSM_NSTATE), lambda i: (i, 0)),
        out_shape=jax.ShapeDtypeStruct((nc, 2 * SSM_NSTATE), F32),
        compiler_params=_cparams("parallel"),
        name="ssm_state",
    )(u, fmat)


def _ssm_scan_body(s_ref, a_ref, xp_ref, st_ref):
    @pl.when(pl.program_id(1) == 0)
    def _():
        st_ref[...] = jnp.zeros_like(st_ref)

    tc = s_ref.shape[0]
    are = a_ref[:, :SSM_NSTATE]
    aim = a_ref[:, SSM_NSTATE:]

    def step(c8, carry):
        xre, xim = carry
        base = pl.multiple_of(c8 * SUBLANES, SUBLANES)
        blk = s_ref[pl.ds(base, SUBLANES), :]
        prev_re, prev_im = [], []
        for r in range(SUBLANES):
            prev_re.append(xre)
            prev_im.append(xim)
            sre = blk[r:r + 1, :SSM_NSTATE]
            sim = blk[r:r + 1, SSM_NSTATE:]
            xre, xim = are * xre - aim * xim + sre, are * xim + aim * xre + sim
        xp_ref[pl.ds(base, SUBLANES), :SSM_NSTATE] = jnp.concatenate(prev_re, axis=0)
        xp_ref[pl.ds(base, SUBLANES), SSM_NSTATE:] = jnp.concatenate(prev_im, axis=0)
        return xre, xim

    xre, xim = lax.fori_loop(0, tc // SUBLANES, step,
                             (st_ref[:, :SSM_NSTATE], st_ref[:, SSM_NSTATE:]))
    st_ref[:, :SSM_NSTATE] = xre
    st_ref[:, SSM_NSTATE:] = xim


def _ssm_scan(s3, a, *, tc):
    b, ncb, w = s3.shape
    spec = pl.BlockSpec((None, tc, w), lambda bi, ci: (bi, ci, 0))
    return pl.pallas_call(
        _ssm_scan_body,
        grid=(b, ncb // tc),
        in_specs=[spec, pl.BlockSpec((1, w), lambda bi, ci: (0, 0))],
        out_specs=spec,
        out_shape=jax.ShapeDtypeStruct(s3.shape, F32),
        scratch_shapes=[pltpu.VMEM((1, w), F32)],
        compiler_params=_cparams("parallel", "arbitrary"),
        name="ssm_scan",
    )(s3, a)


def _ssm_out_body(u_ref, xp_ref, m_ref, e_ref, y_ref):
    tr = xp_ref.shape[0]
    us = [_ssm_tokens(u_ref, s, tr) for s in range(SSM_T)]
    xp = xp_ref[...].astype(BF16)
    for t in range(SSM_T):
        cols = slice(t * D_SSM, (t + 1) * D_SSM)
        acc = _dot(xp, e_ref[:, cols])
        for s in range(t + 1):
            acc += _dot(us[s], m_ref[s * D_SSM:(s + 1) * D_SSM, cols])
        _split_store(y_ref, acc, pl.ds(t, tr, stride=SSM_T))


def _ssm_out(u, xp, mmat, emat, *, tr):
    nc = xp.shape[0]
    return pl.pallas_call(
        _ssm_out_body,
        grid=(nc // tr,),
        in_specs=[_split_spec(tr * SSM_T),
                  pl.BlockSpec((tr, 2 * SSM_NSTATE), lambda i: (i, 0)),
                  _resident((SSM_ROW, SSM_ROW)), _resident((2 * SSM_NSTATE, SSM_ROW))],
        out_specs=_split_spec(tr * SSM_T),
        out_shape=jax.ShapeDtypeStruct((D_SSM // LANES, nc * SSM_T, LANES), F32),
        compiler_params=_cparams("parallel"),
        name="ssm_out",
    )(u, xp, mmat, emat)


def _ssm_matrices(lam_re, lam_im, log_dt, b_re, b_im, c_re, c_im):
    t, g, p, h = SSM_T, SSM_GROUPS, SSM_STATE, SSM_GROUP
    ns, nch = g * p, g * h
    hp = lax.Precision.HIGHEST
    lr, li = lam_re.astype(F32).reshape(ns), lam_im.astype(F32).reshape(ns)
    dt = jnp.repeat(jnp.exp(log_dt.astype(F32)), p)
    steps = jnp.arange(t + 1, dtype=F32)[:, None]
    mag = jnp.exp((lr * dt)[None] * steps)
    ang = (li * dt)[None] * steps
    pr, pi = mag * jnp.cos(ang), mag * jnp.sin(ang)
    nr, ni = pr[1] - 1.0, pi[1]
    den = lr * lr + li * li
    zr, zi = ((nr * lr + ni * li) / den)[:, None], ((ni * lr - nr * li) / den)[:, None]
    br, bi = b_re.astype(F32).reshape(ns, h), b_im.astype(F32).reshape(ns, h)
    cr = c_re.astype(F32).transpose(0, 2, 1).reshape(ns, h)
    ci = c_im.astype(F32).transpose(0, 2, 1).reshape(ns, h)
    same = jnp.asarray(np.repeat(np.repeat(np.eye(g, dtype=np.float32), p, axis=0), h, axis=1))
    spread = lambda w: jnp.tile(w, (1, g)) * same
    bhr, bhi = spread(zr * br - zi * bi), spread(zr * bi + zi * br)
    chr_, chi = spread(cr), spread(ci)
    prc, pic = pr.T[:, :, None], pi.T[:, :, None]
    er = (prc * chr_[:, None, :] - pic * chi[:, None, :]).reshape(ns, (t + 1) * nch)
    ei = (prc * chi[:, None, :] + pic * chr_[:, None, :]).reshape(ns, (t + 1) * nch)

    k_all = (jnp.dot(bhr.T, er[:, :t * nch], precision=hp)
             - jnp.dot(bhi.T, ei[:, :t * nch], precision=hp))
    mmat = jnp.concatenate(
        [jnp.pad(k_all[:, :(t - s) * nch], ((0, 0), (s * nch, 0))) for s in range(t)], axis=0)

    prr, pir = pr[:t][::-1][:, None, :], pi[:t][::-1][:, None, :]
    bhrt, bhit = bhr.T[None], bhi.T[None]
    fmat = jnp.concatenate([(bhrt * prr - bhit * pir).reshape(t * nch, ns),
                            (bhrt * pir + bhit * prr).reshape(t * nch, ns)], axis=1)

    emat = jnp.concatenate([er[:, nch:], -ei[:, nch:]], axis=0)

    a_row = jnp.concatenate([pr[t][None], pi[t][None]], axis=1)
    return mmat.astype(BF16), fmat.astype(BF16), emat.astype(BF16), a_row


def _attn_body(q_ref, k_ref, v_ref, cq_ref, o_ref,
               m_ref, acc_ref, s_ref, mt_ref, p_ref, al_ref, *, tq):
    qi = pl.program_id(2)
    nh = ATT_STEP_HEADS
    tk = tq // 2
    every = slice(0, tq)
    late = slice(tk, tq)

    def slot_lanes(j):
        return slice(j * LANES, (j + 1) * LANES)

    def scores(t, slot, qs=every, masked=False):
        start = pl.multiple_of(t * tk, tk)
        width = qs.stop - qs.start
        for j in range(nh):
            kt = k_ref[pl.ds(start, tk), slot_lanes(j)]
            s = lax.dot_general(kt, q_ref[qs, slot_lanes(j)], (((1,), (1,)), ((), ())),
                                preferred_element_type=F32)
            if masked:
                keypos = lax.broadcasted_iota(jnp.int32, (tk, width), 0)
                qpos = lax.broadcasted_iota(jnp.int32, (tk, width), 1)
                s = jnp.where(keypos <= qpos, s, -jnp.inf)
            s_ref[slot, j, :, qs] = s
            mt_ref[slot, j, :, qs] = jnp.max(s, axis=0, keepdims=True)

    def probs(slot, qs=every):
        for j in range(nh):
            m = m_ref[j, :, qs]
            cq = cq_ref[j:j + 1, qs]
            m_new = jnp.maximum(m, mt_ref[slot, j, :, qs] + cq)
            p_ref[slot, j, :, qs] = jnp.exp2(s_ref[slot, j, :, qs] + (cq - m_new)).astype(BF16)
            al_ref[slot, j, :, qs] = jnp.exp2(m - m_new)
            m_ref[j, :, qs] = m_new

    def values(t, slot, qs=every):
        start = pl.multiple_of(t * tk, tk)
        for j in range(nh):
            vt = v_ref[j * ATT_HEAD_DIM:(j + 1) * ATT_HEAD_DIM, pl.ds(start, tk)]
            vt = jnp.concatenate([vt, jnp.ones_like(vt)], axis=0)
            acc_ref[j, :, qs] = (al_ref[slot, j, :, qs] * acc_ref[j, :, qs]
                                 + _dot(vt, p_ref[slot, j, :, qs]))

    def trip(t, slot, **next_tile):
        values(t - 1, 1 - slot)
        probs(slot)
        scores(t + 1, 1 - slot, **next_tile)

    for j in range(nh):
        m_ref[j] = jnp.full((1, tq), -jnp.inf, F32)
        acc_ref[j] = jnp.zeros((LANES, tq), F32)

    last = 2 * qi + 1

    @pl.when(qi == 0)
    def _():
        scores(0, 0, masked=True)
        probs(0)
        scores(1, 1, late, masked=True)

    @pl.when(qi > 0)
    def _():
        scores(0, 0)
        probs(0)
        scores(1, 1)

        def pair(u, carry):
            t = 2 * u + 1
            trip(t, 1)
            trip(t + 1, 0)
            return carry

        lax.fori_loop(0, qi - 1, pair, 0)
        trip(last - 2, 1, masked=True)
        trip(last - 1, 0, qs=late, masked=True)

    values(last - 1, 0)
    probs(1, late)
    values(last, 1, late)
    for pair in range(nh // 2):
        outs = []
        for j in (2 * pair, 2 * pair + 1):
            acc = acc_ref[j]
            outs.append(acc[:ATT_HEAD_DIM, :] / acc[ATT_HEAD_DIM:, :])
        o_ref[:, slot_lanes(pair)] = jnp.concatenate(outs, axis=0).T.astype(BF16)


def _attention(q3, k3, vt4, cum3, *, tq):
    b, s, _ = q3.shape
    nh = ATT_STEP_HEADS
    return pl.pallas_call(
        functools.partial(_attn_body, tq=tq),
        scratch_shapes=[pltpu.VMEM((nh, 1, tq), F32), pltpu.VMEM((nh, LANES, tq), F32),
                        pltpu.VMEM((2, nh, tq // 2, tq), F32), pltpu.VMEM((2, nh, 1, tq), F32),
                        pltpu.VMEM((2, nh, tq // 2, tq), BF16), pltpu.VMEM((2, nh, 1, tq), F32)],
        grid=(b, ATT_HEADS // nh, s // tq),
        in_specs=[pl.BlockSpec((None, tq, nh * LANES), lambda bi, hg, qi: (bi, qi, hg)),
                  pl.BlockSpec((None, s, nh * LANES), lambda bi, hg, qi: (bi, 0, hg)),
                  pl.BlockSpec((None, nh * ATT_HEAD_DIM, s), lambda bi, hg, qi: (bi, hg, 0)),
                  pl.BlockSpec((None, None, nh, tq), lambda bi, hg, qi: (bi, hg, 0, qi))],
        out_specs=pl.BlockSpec((None, tq, nh * ATT_HEAD_DIM), lambda bi, hg, qi: (bi, qi, hg)),
        out_shape=jax.ShapeDtypeStruct((b, s, D_ATT), BF16),
        compiler_params=_cparams("parallel", "parallel", "arbitrary"),
        name="attention",
    )(q3, k3, vt4, cum3)


R_E1, R_E2, R_W1, R_W2, R_RANK1, R_RANK2 = range(6)


def _route(logits, cnt_ref):
    tm = logits.shape[0]
    lane = lax.broadcasted_iota(jnp.int32, logits.shape, 1).astype(F32)
    lg = jnp.where(lane < N_EXPERTS, logits, -jnp.inf)
    m1 = jnp.max(lg, axis=-1, keepdims=True)
    i1 = jnp.min(jnp.where(lg == m1, lane, float(LANES)), axis=-1, keepdims=True)
    lg2 = jnp.where(lane == i1, -jnp.inf, lg)
    m2 = jnp.max(lg2, axis=-1, keepdims=True)
    i2 = jnp.min(jnp.where(lg2 == m2, lane, float(LANES)), axis=-1, keepdims=True)
    e = jnp.exp(m2 - m1)
    w1 = 1.0 / (1.0 + e)
    w2 = e / (1.0 + e)

    hit1, hit2 = lane == i1, lane == i2
    oh1, oh2 = hit1.astype(BF16), hit2.astype(BF16)
    row = lax.broadcasted_iota(jnp.int32, (tm, tm), 0)
    col = lax.broadcasted_iota(jnp.int32, (tm, tm), 1)
    before = (col < row).astype(BF16)
    carry = cnt_ref[...]
    tot1 = jnp.sum(oh1.astype(F32), axis=0, keepdims=True)
    tot2 = jnp.sum(oh2.astype(F32), axis=0, keepdims=True)
    pos1 = _dot(before, oh1) + carry
    pos2 = _dot(before, oh2) + (carry + tot1)
    rank1 = jnp.sum(jnp.where(hit1, pos1, 0.0), axis=-1, keepdims=True)
    rank2 = jnp.sum(jnp.where(hit2, pos2, 0.0), axis=-1, keepdims=True)
    cnt_ref[...] = carry + tot1 + tot2

    rec = jnp.zeros(logits.shape, F32)
    for idx, val in ((R_E1, i1), (R_E2, i2), (R_W1, w1), (R_W2, w2), (R_RANK1, rank1), (R_RANK2, rank2)):
        rec = jnp.where(lane == idx, val, rec)
    return rec


def _outproj_body(*refs, with_router):
    if with_router:
        (x_ref, ys_ref, us_ref, ya_ref, yg_ref, d_ref, wglu_ref, gn_ref, w_ref, fg_ref, rw_ref,
         xo_ref, h_ref, route_ref, cnt_ref) = refs

        @pl.when(pl.program_id(0) == 0)
        def _():
            cnt_ref[...] = jnp.zeros_like(cnt_ref)
    else:
        (x_ref, ys_ref, us_ref, ya_ref, yg_ref, d_ref, wglu_ref, gn_ref, w_ref, fg_ref,
         xo_ref, h_ref) = refs

    tm = x_ref.shape[0]
    part = tm // OUTPROJ_PARTS
    o1, o2 = D_SSM, D_SSM + D_ATT
    for k in range(OUTPROJ_PARTS):
        rows = slice(k * part, (k + 1) * part)
        y1 = _gelu(_split_load(ys_ref, rows) + d_ref[...] * _split_load(us_ref, rows))
        y_ssm = y1 * _sigmoid(_dot(y1.astype(BF16), wglu_ref[...]))

        acc = _dot(_rms(y_ssm, gn_ref[:, :o1]).astype(BF16), w_ref[:o1, :])
        acc += _dot(_rms(ya_ref[rows, :].astype(F32), gn_ref[:, o1:o2]).astype(BF16), w_ref[o1:o2, :])
        acc += _dot(_rms(yg_ref[rows, :].astype(F32), gn_ref[:, o2:]).astype(BF16), w_ref[o2:, :])
        xn = x_ref[rows, :] + acc
        xo_ref[rows, :] = xn
        hn = _rms(xn, fg_ref[...])
        h_ref[rows, :] = hn.astype(h_ref.dtype)
        if with_router:
            rw = rw_ref[...]
            prod = sum(_dot(piece, rw) for piece in _split3(hn))
            logits = (prod + pltpu.roll(prod, LANES - N_EXPERTS, axis=1)
                      + pltpu.roll(prod, LANES - 2 * N_EXPERTS, axis=1))
            route_ref[rows, :] = _route(logits, cnt_ref)


def _outproj(x2, ys, us, ya, yg, d, wglu, gn, w, fg, rw, *, tm):
    n = x2.shape[0]
    with_router = rw is not None
    row = lambda width: pl.BlockSpec((tm, width), lambda i: (i, 0))
    in_specs = [row(D_MODEL), _split_spec(tm), _split_spec(tm), row(D_ATT), row(D_GM),
                _resident((1, D_SSM)), _resident((D_SSM, D_SSM)), _resident((1, D_MODEL)),
                _resident((D_MODEL, D_MODEL)), _resident((1, D_MODEL))]
    out_specs = [row(D_MODEL), row(D_MODEL)]
    out_shape = [jax.ShapeDtypeStruct((n, D_MODEL), F32),
                 jax.ShapeDtypeStruct((n, D_MODEL), F32 if with_router else BF16)]
    args = [x2, ys, us, ya, yg, d, wglu, gn, w, fg]
    if with_router:
        in_specs.append(_resident((D_MODEL, LANES)))
        out_specs += [row(LANES), pl.BlockSpec((1, LANES), lambda i: (0, 0))]
        out_shape += [jax.ShapeDtypeStruct((n, LANES), F32), jax.ShapeDtypeStruct((1, LANES), F32)]
        args.append(rw)
    return pl.pallas_call(
        functools.partial(_outproj_body, with_router=with_router),
        grid=(n // tm,),
        in_specs=in_specs, out_specs=out_specs, out_shape=out_shape,
        compiler_params=_cparams("arbitrary" if with_router else "parallel"),
        name="outproj_router" if with_router else "outproj",
    )(*args)


def _swiglu_tile(h, wg, wu, wd):
    a = _dot(h, wg)
    return _dot((a * _sigmoid(a) * _dot(h, wu)).astype(BF16), wd)


def _finish(x, acc, fin_ref, o_ref):
    xn = x + acc
    o_ref[...] = xn if fin_ref is None else _rms(xn, fin_ref[...])


def _dense_ffn_body(*refs, final_norm):
    if final_norm:
        h_ref, x_ref, wg_ref, wu_ref, wd_ref, fin_ref, o_ref, acc_ref = refs
    else:
        h_ref, x_ref, wg_ref, wu_ref, wd_ref, o_ref, acc_ref = refs
        fin_ref = None
    f = pl.program_id(1)

    @pl.when(f == 0)
    def _():
        acc_ref[...] = jnp.zeros_like(acc_ref)

    acc_ref[...] += _swiglu_tile(h_ref[...], wg_ref[...], wu_ref[...], wd_ref[...])

    @pl.when(f == pl.num_programs(1) - 1)
    def _():
        _finish(x_ref[...], acc_ref[...], fin_ref, o_ref)


def _dense_ffn(h, x2, wg, wu, wd, fin, *, tm, tf):
    n = x2.shape[0]
    dff = wg.shape[1]
    row = lambda: pl.BlockSpec((tm, D_MODEL), lambda i, f: (i, 0))
    mode = dict(pipeline_mode=pl.Buffered(1)) if tf == dff else {}
    in_specs = [row(), row(),
                pl.BlockSpec((D_MODEL, tf), lambda i, f: (0, f), **mode),
                pl.BlockSpec((D_MODEL, tf), lambda i, f: (0, f), **mode),
                pl.BlockSpec((tf, D_MODEL), lambda i, f: (f, 0), **mode)]
    args = [h, x2, wg, wu, wd]
    if fin is not None:
        in_specs.append(pl.BlockSpec((1, D_MODEL), lambda i, f: (0, 0)))
        args.append(fin)
    return pl.pallas_call(
        functools.partial(_dense_ffn_body, final_norm=fin is not None),
        grid=(n // tm, dff // tf),
        in_specs=in_specs, out_specs=row(),
        out_shape=jax.ShapeDtypeStruct((n, D_MODEL), F32),
        scratch_shapes=[pltpu.VMEM((tm, D_MODEL), F32)],
        compiler_params=_cparams("parallel", "arbitrary"),
        name="dense_ffn",
    )(*args)


def _row_copy(src, src_row, dst, dst_row, sem):
    return pltpu.make_async_copy(src.at[pl.ds(src_row, 1)], dst.at[pl.ds(dst_row, 1)], sem)


def _moe_dispatch_body(dest_ref, h_ref, xs_in_ref, xs_ref, sem):
    del xs_in_ref
    tm = h_ref.shape[0]

    def issue(r, carry):
        for k in range(2):
            _row_copy(h_ref, r, xs_ref, dest_ref[k, r], sem).start()
        return carry

    lax.fori_loop(0, tm, issue, 0, unroll=8)
    for k in range(2):
        pltpu.make_async_copy(h_ref, xs_ref.at[pl.ds(0, tm)], sem).wait()


def _moe_dispatch(dest, h, xs_zero, *, tm):
    n = h.shape[0]
    return pl.pallas_call(
        _moe_dispatch_body,
        grid=(n // tm,),
        in_specs=[pl.BlockSpec((None, 2, tm), lambda i: (i, 0, 0), memory_space=pltpu.SMEM),
                  pl.BlockSpec((tm, D_MODEL), lambda i: (i, 0)),
                  pl.BlockSpec(memory_space=pl.ANY)],
        out_specs=pl.BlockSpec(memory_space=pl.ANY),
        out_shape=jax.ShapeDtypeStruct(xs_zero.shape, xs_zero.dtype),
        scratch_shapes=[pltpu.SemaphoreType.DMA(())],
        input_output_aliases={2: 0},
        compiler_params=pltpu.CompilerParams(dimension_semantics=("arbitrary",),
                                             vmem_limit_bytes=VMEM_LIMIT, has_side_effects=True),
        name="moe_dispatch",
    )(dest, h, xs_zero)


def _moe_ffn_body(te_ref, nu_ref, xs_ref, wg_ref, wu_ref, wd_ref, ys_ref, acc_ref):
    del te_ref
    i = pl.program_id(0)
    f = pl.program_id(1)

    @pl.when(i < nu_ref[0])
    def _():
        @pl.when(f == 0)
        def _():
            acc_ref[...] = jnp.zeros_like(acc_ref)

        acc_ref[...] += _swiglu_tile(xs_ref[...].astype(BF16), wg_ref[...], wu_ref[...], wd_ref[...])

        @pl.when(f == pl.num_programs(1) - 1)
        def _():
            ys_ref[...] = acc_ref[...]

    @pl.when((i >= nu_ref[0]) & (f == pl.num_programs(1) - 1))
    def _():
        ys_ref[...] = jnp.zeros_like(ys_ref)


def _moe_ffn(tile_expert, n_used, xs, wg, wu, wd, *, tme, tf):
    rows = xs.shape[0]
    dff = wg.shape[-1]
    nf = dff // tf
    row_idx = lambda i, f, te, nu: (jnp.minimum(i, nu[0] - 1), 0)
    col = lambda i, f, nu: jnp.where(i < nu[0], f, nf - 1)
    mode = dict(pipeline_mode=pl.Buffered(1)) if nf == 1 else {}
    grid_spec = pltpu.PrefetchScalarGridSpec(
        num_scalar_prefetch=2,
        grid=(rows // tme, nf),
        in_specs=[pl.BlockSpec((tme, D_MODEL), row_idx),
                  pl.BlockSpec((None, D_MODEL, tf), lambda i, f, te, nu: (te[i], 0, col(i, f, nu)), **mode),
                  pl.BlockSpec((None, D_MODEL, tf), lambda i, f, te, nu: (te[i], 0, col(i, f, nu)), **mode),
                  pl.BlockSpec((None, tf, D_MODEL), lambda i, f, te, nu: (te[i], col(i, f, nu), 0), **mode)],
        out_specs=pl.BlockSpec((tme, D_MODEL), lambda i, f, te, nu: (i, 0)),
        scratch_shapes=[pltpu.VMEM((tme, D_MODEL), F32)])
    return pl.pallas_call(
        _moe_ffn_body,
        grid_spec=grid_spec,
        out_shape=jax.ShapeDtypeStruct((rows, D_MODEL), F32),
        compiler_params=_cparams("arbitrary", "arbitrary"),
        name="moe_ffn",
    )(tile_expert, n_used, xs, wg, wu, wd)


def _moe_combine_body(*refs, final_norm):
    if final_norm:
        dest_ref, x_ref, route_ref, ys_ref, fin_ref, o_ref, buf_ref, sem = refs
    else:
        dest_ref, x_ref, route_ref, ys_ref, o_ref, buf_ref, sem = refs
        fin_ref = None
    tm = x_ref.shape[0]

    def issue(r, carry):
        for k in range(2):
            _row_copy(ys_ref, dest_ref[k, r], buf_ref.at[k], r, sem).start()
        return carry

    lax.fori_loop(0, tm, issue, 0, unroll=8)
    for k in range(2):
        pltpu.make_async_copy(ys_ref.at[pl.ds(0, tm)], buf_ref.at[k], sem).wait()

    route = route_ref[...]
    lane = lax.broadcasted_iota(jnp.int32, route.shape, 1)
    w1 = jnp.sum(jnp.where(lane == R_W1, route, 0.0), axis=-1, keepdims=True)
    w2 = jnp.sum(jnp.where(lane == R_W2, route, 0.0), axis=-1, keepdims=True)
    _finish(x_ref[...], w1 * buf_ref[0] + w2 * buf_ref[1], fin_ref, o_ref)


def _moe_combine(dest, x2, route, ys, fin, *, tm):
    n = x2.shape[0]
    row = lambda width: pl.BlockSpec((tm, width), lambda i: (i, 0))
    in_specs = [pl.BlockSpec((None, 2, tm), lambda i: (i, 0, 0), memory_space=pltpu.SMEM),
                row(D_MODEL), row(LANES), pl.BlockSpec(memory_space=pl.ANY)]
    args = [dest, x2, route, ys]
    if fin is not None:
        in_specs.append(pl.BlockSpec((1, D_MODEL), lambda i: (0, 0)))
        args.append(fin)
    return pl.pallas_call(
        functools.partial(_moe_combine_body, final_norm=fin is not None),
        grid=(n // tm,),
        in_specs=in_specs, out_specs=row(D_MODEL),
        out_shape=jax.ShapeDtypeStruct((n, D_MODEL), F32),
        scratch_shapes=[pltpu.VMEM((2, tm, D_MODEL), F32), pltpu.SemaphoreType.DMA(())],
        compiler_params=_cparams("arbitrary"),
        name="moe_combine",
    )(*args)


def _moe_layout(route, counts, *, tm, tme):
    n = route.shape[0]
    cnt = counts[0, :N_EXPERTS].astype(jnp.int32)
    padded = (cnt + (tme - 1)) // tme * tme
    ends = jnp.cumsum(padded)
    starts = ends - padded
    experts = jnp.arange(N_EXPERTS, dtype=F32)

    def rows(e_lane, rank_lane):
        start = jnp.sum(jnp.where(route[:, e_lane, None] == experts[None], starts[None], 0), axis=1)
        return start + route[:, rank_lane].astype(jnp.int32)

    dest = jnp.stack([rows(R_E1, R_RANK1), rows(R_E2, R_RANK2)])
    dest = dest.reshape(2, n // tm, tm).transpose(1, 0, 2)
    n_tiles = (2 * n) // tme + N_EXPERTS
    tile_start = jnp.arange(n_tiles, dtype=jnp.int32) * tme
    tile_expert = jnp.minimum(jnp.sum(tile_start[:, None] >= ends[None, :], axis=1), N_EXPERTS - 1)
    n_used = (ends[-1] // tme).reshape(1)
    tile_expert = jnp.where(tile_start < ends[-1], tile_expert, tile_expert[jnp.maximum(n_used[0] - 1, 0)])
    return dest, tile_expert.astype(jnp.int32), n_used.astype(jnp.int32), n_tiles * tme


def _largest_tile(total, cap, mult):
    best = mult
    for t in range(mult, min(total, cap) + 1, mult):
        if total % t == 0:
            best = t
    return best


def _mxu_tile(total, cap):
    if total % MXU_WIDTH == 0:
        return _largest_tile(total, cap, MXU_WIDTH)
    return _largest_tile(total, cap, LANES)


def _pad_lanes(a):
    return jnp.pad(a, [(0, 0)] * (a.ndim - 1) + [(0, LANES - a.shape[-1])])


def kernel(x, mix_norm_g, w_in, b_forget, ssm_lambda_re, ssm_lambda_im, ssm_log_dt, ssm_b_re, ssm_b_im, ssm_c_re, ssm_c_im, ssm_d, ssm_w_glu, gm_ln_g, gm_ln_b, gm_w_s, gm_b_s, group_norm_g, w_out, ffn_norm_g, dense_w_gate, dense_w_up, dense_w_down, router_w, moe_w_gate, moe_w_up, moe_w_down, final_norm_g):
    bsz, seq, _ = x.shape
    n = bsz * seq
    depth = w_in.shape[0]
    assert seq % CHUNK == 0 and seq % SSM_T == 0
    tm = _largest_tile(seq, 512, CHUNK)
    tq = _largest_tile(seq, 1024, 2 * LANES)
    nc = n // SSM_T
    ncb = seq // SSM_T
    tr = _largest_tile(nc, 512, 8)
    tc = _largest_tile(ncb, 128, 8)

    x2 = x.reshape(n, D_MODEL).astype(F32)
    tril = jnp.tril(jnp.ones((CHUNK, CHUNK), F32))
    i0, i1, i2, i3, i4 = (D_SSM, D_SSM + D_ATT, D_SSM + 2 * D_ATT, D_SSM + 3 * D_ATT,
                          D_SSM + 3 * D_ATT + ATT_HEADS)
    for layer in range(depth):
        wl = w_in[layer]
        w_re = jnp.concatenate(
            [wl[:, :i0], wl[:, i0:i1] * (ATT_HEAD_DIM ** -0.5 * LOG2E), wl[:, i1:i2], wl[:, i4:],
             _pad_lanes(wl[:, i3:i4])], axis=1).astype(BF16)
        ws = (gm_w_s[layer].astype(F32) * tril[None]).astype(BF16)
        bs = jnp.repeat(gm_b_s[layer].astype(F32).T, GM_HEAD_DIM, axis=1)
        u, q, k, v, cum, ygm = _inproj(
            x2, mix_norm_g[layer].reshape(1, D_MODEL), w_re, wl[:, i2:i3].T.astype(BF16),
            _pad_lanes(b_forget[layer].reshape(1, ATT_HEADS).astype(F32)),
            gm_ln_g[layer].reshape(1, D_GM), gm_ln_b[layer].reshape(1, D_GM), ws, bs,
            seq=seq, tm=tm)

        mmat, fmat, emat, a_row = _ssm_matrices(
            ssm_lambda_re[layer], ssm_lambda_im[layer], ssm_log_dt[layer], ssm_b_re[layer],
            ssm_b_im[layer], ssm_c_re[layer], ssm_c_im[layer])
        s_loc = _ssm_state(u, fmat, tr=tr)
        xprev = _ssm_scan(s_loc.reshape(bsz, ncb, 2 * SSM_NSTATE), a_row, tc=tc)
        y_ssm = _ssm_out(u, xprev.reshape(nc, 2 * SSM_NSTATE), mmat, emat, tr=tr)

        cum_rows = cum[:, :ATT_HEADS].reshape(bsz, seq, ATT_HEADS // ATT_STEP_HEADS, ATT_STEP_HEADS)
        cum_rows = cum_rows.transpose(0, 2, 3, 1)
        y_att = _attention(q.reshape(bsz, seq, ATT_SLOTS), k.reshape(bsz, seq, ATT_SLOTS), v,
                           cum_rows, tq=tq).reshape(n, D_ATT)

        is_moe = layer % 2 == 1
        j = layer // 2
        rw = (_pad_lanes(jnp.concatenate(_split3(router_w[j].astype(F32)), axis=1))
              if is_moe else None)
        outs = _outproj(x2, y_ssm, u, y_att, ygm, ssm_d[layer].reshape(1, D_SSM).astype(F32),
                        ssm_w_glu[layer].astype(BF16), group_norm_g[layer].reshape(1, D_MODEL),
                        w_out[layer].astype(BF16), ffn_norm_g[layer].reshape(1, D_MODEL), rw, tm=tm)
        fin = final_norm_g.reshape(1, D_MODEL) if layer == depth - 1 else None
        if is_moe:
            x_mid, h, route, counts = outs
            dffe = moe_w_gate.shape[-1]
            tme = min(MOE_ROW_TILE, 2 * n)
            dest, tile_expert, n_used, rows = _moe_layout(route, counts, tm=tm, tme=tme)
            xs = _moe_dispatch(dest, h, jnp.zeros((rows, D_MODEL), F32), tm=tm)
            ys = _moe_ffn(tile_expert, n_used, xs, moe_w_gate[j].astype(BF16),
                          moe_w_up[j].astype(BF16), moe_w_down[j].astype(BF16), tme=tme,
                          tf=_mxu_tile(dffe, 3584))
            x2 = _moe_combine(dest, x_mid, route, ys, fin, tm=tm)
        else:
            x_mid, h = outs
            dff = dense_w_gate.shape[-1]
            x2 = _dense_ffn(h, x_mid, dense_w_gate[j].astype(BF16), dense_w_up[j].astype(BF16),
                            dense_w_down[j].astype(BF16), fin, tm=tm,
                            tf=_mxu_tile(dff, 2816))
    return x2.reshape(bsz, seq, D_MODEL).astype(x.dtype)
```

```python
import functools
import math

import jax
import jax.numpy as jnp
import numpy as np
from jax import lax
from jax.experimental import pallas as pl
from jax.experimental.pallas import tpu as pltpu

F32 = jnp.float32
BF16 = jnp.bfloat16

D_MODEL = 1024
D_SSM = 256
D_ATT = 512
D_GM = 256
SSM_GROUP = 16
SSM_GROUPS = 16
SSM_STATE = 64
ATT_HEADS = 8
ATT_HEAD_DIM = 64
GM_HEADS = 4
GM_HEAD_DIM = 64
CHUNK = 128
N_EXPERTS = 8
EPS = 1e-6
LOG2E = math.log2(math.e)

LANES = 128
SUBLANES = 8
MXU_WIDTH = 256
V7X_VMEM_BYTES = 64 * 1024 * 1024
ATT_SLOTS = ATT_HEADS * LANES
MOE_ROW_TILE = 512
OUTPROJ_PARTS = 2
ATT_STEP_HEADS = 4
SSM_T = 8
SSM_ROW = SSM_T * D_SSM
SSM_NSTATE = SSM_GROUPS * SSM_STATE

C_U = 0
C_Q = C_U + D_SSM
C_K = C_Q + D_ATT
C_Z = C_K + D_ATT
C_F = C_Z + 2 * D_GM
D_IN_PAD = C_F + LANES

VMEM_LIMIT = V7X_VMEM_BYTES * 7 // 8


def _cparams(*sem):
    return pltpu.CompilerParams(dimension_semantics=sem, vmem_limit_bytes=VMEM_LIMIT)


def _resident(shape):
    nd = len(shape)
    return pl.BlockSpec(shape, lambda *_: (0,) * nd, pipeline_mode=pl.Buffered(1))


def _rms(x, g):
    return x * lax.rsqrt(jnp.mean(x * x, axis=-1, keepdims=True) + EPS) * g


def _gelu(x):
    c = math.sqrt(2.0 / math.pi)
    return 0.5 * x * (1.0 + jnp.tanh(c * (x + 0.044715 * (x * x * x))))


def _sigmoid(x):
    return 1.0 / (1.0 + jnp.exp(-x))


def _split3(x):
    p1 = x.astype(BF16)
    r1 = x - p1.astype(F32)
    p2 = r1.astype(BF16)
    r2 = r1 - p2.astype(F32)
    return p1, p2, r2.astype(BF16)


def _dot(a, b):
    return jnp.dot(a, b, preferred_element_type=F32)


def _split_spec(rows):
    return pl.BlockSpec((D_SSM // LANES, rows, LANES), lambda i: (0, i, 0))


def _split_store(ref, value, rows=slice(None)):
    for h in range(D_SSM // LANES):
        ref[h, rows, :] = value[:, h * LANES:(h + 1) * LANES]


def _split_load(ref, rows=slice(None)):
    return jnp.concatenate([ref[h, rows, :] for h in range(D_SSM // LANES)], axis=1)


def _bias_lane(head):
    return 0 if head % 2 else ATT_HEAD_DIM


def _inproj_body(x_ref, g_ref, w_ref, wvt_ref, bf_ref, lng_ref, lnb_ref, ws_ref, bs_ref, place_ref, qc_ref,
                 u_ref, q_ref, k_ref, v_ref, cum_ref, ygm_ref, carry_ref, *, tiles_per_seq):
    i = pl.program_id(0)

    @pl.when(i % tiles_per_seq == 0)
    def _():
        carry_ref[...] = jnp.zeros_like(carry_ref)

    tm = x_ref.shape[0]
    hb = _rms(x_ref[...], g_ref[...]).astype(BF16)

    def proj(lo, width):
        return _dot(hb, w_ref[:, lo:lo + width])

    _split_store(u_ref, proj(C_U, D_SSM))
    v_ref[...] = lax.dot_general(wvt_ref[...], hb, (((1,), (1,)), ((), ())),
                                 preferred_element_type=F32).astype(BF16)

    f = proj(C_F, LANES) + bf_ref[...]
    logf = (jnp.minimum(f, 0.0) - jnp.log(1.0 + jnp.exp(-jnp.abs(f)))) * LOG2E
    row = lax.broadcasted_iota(jnp.int32, (tm, tm), 0)
    col = lax.broadcasted_iota(jnp.int32, (tm, tm), 1)
    tri = (col <= row).astype(BF16)
    sums = _dot(tri, jnp.concatenate(_split3(logf), axis=1))
    cum = sums[:, :LANES] + sums[:, LANES:2 * LANES] + sums[:, 2 * LANES:] + carry_ref[...]
    cum_ref[...] = cum
    carry_ref[...] = cum[tm - 1:tm, :]

    lane = lax.broadcasted_iota(jnp.int32, (tm, LANES), 1)
    c1, c2, c3 = [jnp.where(lane < ATT_HEADS, piece.astype(F32), 0.0) for piece in _split3(cum)]
    packed = c1 + pltpu.roll(c2, ATT_HEADS, axis=1) + pltpu.roll(c3, 2 * ATT_HEADS, axis=1)
    placed = _dot(packed.astype(BF16), place_ref[...])
    upper = lax.broadcasted_iota(jnp.int32, (tm, LANES), 1) >= ATT_HEAD_DIM
    qf = proj(C_Q, D_ATT)
    kf = proj(C_K, D_ATT)
    for h in range(ATT_HEADS):
        pair = slice((h // 2) * LANES, (h // 2 + 1) * LANES)
        slot = slice(h * LANES, (h + 1) * LANES)
        own = upper if h % 2 else ~upper
        q_ref[:, slot] = jnp.where(own, qf[:, pair], qc_ref[:, slot]).astype(BF16)
        k_ref[:, slot] = jnp.where(own, kf[:, pair], placed[:, slot]).astype(BF16)

    zg = _gelu(proj(C_Z, 2 * D_GM))
    ug = zg[:, :D_GM]
    vg = zg[:, D_GM:]
    mean = jnp.mean(vg, axis=-1, keepdims=True)
    cen = vg - mean
    var = jnp.mean(cen * cen, axis=-1, keepdims=True)
    vn = (cen * lax.rsqrt(var + EPS) * lng_ref[...] + lnb_ref[...]).astype(BF16)
    lane_head = lax.broadcasted_iota(jnp.int32, (CHUNK, D_GM), 1) >> 6
    for c in range(tm // CHUNK):
        rows = slice(c * CHUNK, (c + 1) * CHUNK)
        vc = vn[rows, :]
        mixed = _dot(ws_ref[0], vc)
        for g in range(1, GM_HEADS):
            mixed = jnp.where(lane_head == g, _dot(ws_ref[g], vc), mixed)
        mixed = mixed + bs_ref[...]
        ygm_ref[rows, :] = (ug[rows, :] * mixed).astype(BF16)


def _attn_slot_constants():
    place = np.zeros((LANES, ATT_SLOTS), np.float32)
    qconst = np.zeros((1, ATT_SLOTS), np.float32)
    for h in range(ATT_HEADS):
        for piece in range(3):
            lane = h * LANES + _bias_lane(h) + piece
            place[piece * ATT_HEADS + h, lane] = 1.0
            qconst[0, lane] = -1.0
    return jnp.asarray(place, BF16), jnp.asarray(qconst)


def _inproj(x2, g, w, wvt, bf, lng, lnb, ws, bs, *, seq, tm):
    n = x2.shape[0]
    tps = seq // tm
    place, qconst = _attn_slot_constants()
    row = lambda width: pl.BlockSpec((tm, width), lambda i: (i, 0))
    return pl.pallas_call(
        functools.partial(_inproj_body, tiles_per_seq=tps),
        grid=(n // tm,),
        in_specs=[row(D_MODEL), _resident((1, D_MODEL)), _resident((D_MODEL, D_IN_PAD)),
                  _resident((D_ATT, D_MODEL)),
                  _resident((1, LANES)), _resident((1, D_GM)), _resident((1, D_GM)),
                  _resident((GM_HEADS, CHUNK, CHUNK)), _resident((CHUNK, D_GM)),
                  _resident((LANES, ATT_SLOTS)), _resident((1, ATT_SLOTS))],
        out_specs=[_split_spec(tm), row(ATT_SLOTS), row(ATT_SLOTS),
                   pl.BlockSpec((None, D_ATT, tm), lambda i: (i // tps, 0, i % tps)),
                   row(LANES), row(D_GM)],
        out_shape=[jax.ShapeDtypeStruct((D_SSM // LANES, n, LANES), F32),
                   jax.ShapeDtypeStruct((n, ATT_SLOTS), BF16),
                   jax.ShapeDtypeStruct((n, ATT_SLOTS), BF16),
                   jax.ShapeDtypeStruct((n // seq, D_ATT, seq), BF16),
                   jax.ShapeDtypeStruct((n, LANES), F32),
                   jax.ShapeDtypeStruct((n, D_GM), BF16)],
        scratch_shapes=[pltpu.VMEM((1, LANES), F32)],
        compiler_params=_cparams("arbitrary"),
        name="inproj",
    )(x2, g, w, wvt, bf, lng, lnb, ws, bs, place, qconst)


def _ssm_tokens(u_ref, s, tr):
    return _split_load(u_ref, pl.ds(s, tr, stride=SSM_T)).astype(BF16)


def _ssm_state_body(u_ref, f_ref, s_ref):
    tr = s_ref.shape[0]
    acc = _dot(_ssm_tokens(u_ref, 0, tr), f_ref[:D_SSM, :])
    for s in range(1, SSM_T):
        acc += _dot(_ssm_tokens(u_ref, s, tr), f_ref[s * D_SSM:(s + 1) * D_SSM, :])
    s_ref[...] = acc


def _ssm_state(u, fmat, *, tr):
    nc = u.shape[1] // SSM_T
    return pl.pallas_call(
        _ssm_state_body,
        grid=(nc // tr,),
        in_specs=[_split_spec(tr * SSM_T), _resident((SSM_ROW, 2 * SSM_NSTATE))],
        out_specs=pl.BlockSpec((tr, 2 * SSM_NSTATE), lambda i: (i, 0)),
        out_shape=jax.ShapeDtypeStruct((nc, 2 * SSM_NSTATE), F32),
        compiler_params=_cparams("parallel"),
        name="ssm_state",
    )(u, fmat)


def _ssm_scan_body(s_ref, a_ref, xp_ref, st_ref):
    @pl.when(pl.program_id(1) == 0)
    def _():
        st_ref[...] = jnp.zeros_like(st_ref)

    tc = s_ref.shape[0]
    are = a_ref[:, :SSM_NSTATE]
    aim = a_ref[:, SSM_NSTATE:]

    def step(c8, carry):
        xre, xim = carry
        base = pl.multiple_of(c8 * SUBLANES, SUBLANES)
        blk = s_ref[pl.ds(base, SUBLANES), :]
        prev_re, prev_im = [], []
        for r in range(SUBLANES):
            prev_re.append(xre)
            prev_im.append(xim)
            sre = blk[r:r + 1, :SSM_NSTATE]
            sim = blk[r:r + 1, SSM_NSTATE:]
            xre, xim = are * xre - aim * xim + sre, are * xim + aim * xre + sim
        xp_ref[pl.ds(base, SUBLANES), :SSM_NSTATE] = jnp.concatenate(prev_re, axis=0)
        xp_ref[pl.ds(base, SUBLANES), SSM_NSTATE:] = jnp.concatenate(prev_im, axis=0)
        return xre, xim

    xre, xim = lax.fori_loop(0, tc // SUBLANES, step,
                             (st_ref[:, :SSM_NSTATE], st_ref[:, SSM_NSTATE:]))
    st_ref[:, :SSM_NSTATE] = xre
    st_ref[:, SSM_NSTATE:] = xim


def _ssm_scan(s3, a, *, tc):
    b, ncb, w = s3.shape
    spec = pl.BlockSpec((None, tc, w), lambda bi, ci: (bi, ci, 0))
    return pl.pallas_call(
        _ssm_scan_body,
        grid=(b, ncb // tc),
        in_specs=[spec, pl.BlockSpec((1, w), lambda bi, ci: (0, 0))],
        out_specs=spec,
        out_shape=jax.ShapeDtypeStruct(s3.shape, F32),
        scratch_shapes=[pltpu.VMEM((1, w), F32)],
        compiler_params=_cparams("parallel", "arbitrary"),
        name="ssm_scan",
    )(s3, a)


def _ssm_out_body(u_ref, xp_ref, m_ref, e_ref, y_ref):
    tr = xp_ref.shape[0]
    us = [_ssm_tokens(u_ref, s, tr) for s in range(SSM_T)]
    xp = xp_ref[...].astype(BF16)
    for t in range(SSM_T):
        cols = slice(t * D_SSM, (t + 1) * D_SSM)
        acc = _dot(xp, e_ref[:, cols])
        for s in range(t + 1):
            acc += _dot(us[s], m_ref[s * D_SSM:(s + 1) * D_SSM, cols])
        _split_store(y_ref, acc, pl.ds(t, tr, stride=SSM_T))


def _ssm_out(u, xp, mmat, emat, *, tr):
    nc = xp.shape[0]
    return pl.pallas_call(
        _ssm_out_body,
        grid=(nc // tr,),
        in_specs=[_split_spec(tr * SSM_T),
                  pl.BlockSpec((tr, 2 * SSM_NSTATE), lambda i: (i, 0)),
                  _resident((SSM_ROW, SSM_ROW)), _resident((2 * SSM_NSTATE, SSM_ROW))],
        out_specs=_split_spec(tr * SSM_T),
        out_shape=jax.ShapeDtypeStruct((D_SSM // LANES, nc * SSM_T, LANES), F32),
        compiler_params=_cparams("parallel"),
        name="ssm_out",
    )(u, xp, mmat, emat)


def _ssm_matrices(lam_re, lam_im, log_dt, b_re, b_im, c_re, c_im):
    t, g, p, h = SSM_T, SSM_GROUPS, SSM_STATE, SSM_GROUP
    ns, nch = g * p, g * h
    hp = lax.Precision.HIGHEST
    lr, li = lam_re.astype(F32).reshape(ns), lam_im.astype(F32).reshape(ns)
    dt = jnp.repeat(jnp.exp(log_dt.astype(F32)), p)
    steps = jnp.arange(t + 1, dtype=F32)[:, None]
    mag = jnp.exp((lr * dt)[None] * steps)
    ang = (li * dt)[None] * steps
    pr, pi = mag * jnp.cos(ang), mag * jnp.sin(ang)
    nr, ni = pr[1] - 1.0, pi[1]
    den = lr * lr + li * li
    zr, zi = ((nr * lr + ni * li) / den)[:, None], ((ni * lr - nr * li) / den)[:, None]
    br, bi = b_re.astype(F32).reshape(ns, h), b_im.astype(F32).reshape(ns, h)
    cr = c_re.astype(F32).transpose(0, 2, 1).reshape(ns, h)
    ci = c_im.astype(F32).transpose(0, 2, 1).reshape(ns, h)
    same = jnp.asarray(np.repeat(np.repeat(np.eye(g, dtype=np.float32), p, axis=0), h, axis=1))
    spread = lambda w: jnp.tile(w, (1, g)) * same
    bhr, bhi = spread(zr * br - zi * bi), spread(zr * bi + zi * br)
    chr_, chi = spread(cr), spread(ci)
    prc, pic = pr.T[:, :, None], pi.T[:, :, None]
    er = (prc * chr_[:, None, :] - pic * chi[:, None, :]).reshape(ns, (t + 1) * nch)
    ei = (prc * chi[:, None, :] + pic * chr_[:, None, :]).reshape(ns, (t + 1) * nch)

    k_all = (jnp.dot(bhr.T, er[:, :t * nch], precision=hp)
             - jnp.dot(bhi.T, ei[:, :t * nch], precision=hp))
    mmat = jnp.concatenate(
        [jnp.pad(k_all[:, :(t - s) * nch], ((0, 0), (s * nch, 0))) for s in range(t)], axis=0)

    prr, pir = pr[:t][::-1][:, None, :], pi[:t][::-1][:, None, :]
    bhrt, bhit = bhr.T[None], bhi.T[None]
    fmat = jnp.concatenate([(bhrt * prr - bhit * pir).reshape(t * nch, ns),
                            (bhrt * pir + bhit * prr).reshape(t * nch, ns)], axis=1)

    emat = jnp.concatenate([er[:, nch:], -ei[:, nch:]], axis=0)

    a_row = jnp.concatenate([pr[t][None], pi[t][None]], axis=1)
    return mmat.astype(BF16), fmat.astype(BF16), emat.astype(BF16), a_row


def _attn_body(q_ref, k_ref, v_ref, cq_ref, o_ref,
               m_ref, acc_ref, s_ref, mt_ref, p_ref, al_ref, *, tq):
    qi = pl.program_id(2)
    nh = ATT_STEP_HEADS
    tk = tq // 2
    every = slice(0, tq)
    late = slice(tk, tq)

    def slot_lanes(j):
        return slice(j * LANES, (j + 1) * LANES)

    def scores(t, slot, qs=every, masked=False):
        start = pl.multiple_of(t * tk, tk)
        width = qs.stop - qs.start
        for j in range(nh):
            kt = k_ref[pl.ds(start, tk), slot_lanes(j)]
            s = lax.dot_general(kt, q_ref[qs, slot_lanes(j)], (((1,), (1,)), ((), ())),
                                preferred_element_type=F32)
            if masked:
                keypos = lax.broadcasted_iota(jnp.int32, (tk, width), 0)
                qpos = lax.broadcasted_iota(jnp.int32, (tk, width), 1)
                s = jnp.where(keypos <= qpos, s, -jnp.inf)
            s_ref[slot, j, :, qs] = s
            mt_ref[slot, j, :, qs] = jnp.max(s, axis=0, keepdims=True)

    def probs(slot, qs=every):
        for j in range(nh):
            m = m_ref[j, :, qs]
            cq = cq_ref[j:j + 1, qs]
            m_new = jnp.maximum(m, mt_ref[slot, j, :, qs] + cq)
            p_ref[slot, j, :, qs] = jnp.exp2(s_ref[slot, j, :, qs] + (cq - m_new)).astype(BF16)
            al_ref[slot, j, :, qs] = jnp.exp2(m - m_new)
            m_ref[j, :, qs] = m_new

    def values(t, slot, qs=every):
        start = pl.multiple_of(t * tk, tk)
        for j in range(nh):
            vt = v_ref[j * ATT_HEAD_DIM:(j + 1) * ATT_HEAD_DIM, pl.ds(start, tk)]
            vt = jnp.concatenate([vt, jnp.ones_like(vt)], axis=0)
            acc_ref[j, :, qs] = (al_ref[slot, j, :, qs] * acc_ref[j, :, qs]
                                 + _dot(vt, p_ref[slot, j, :, qs]))

    def trip(t, slot, **next_tile):
        values(t - 1, 1 - slot)
        probs(slot)
        scores(t + 1, 1 - slot, **next_tile)

    for j in range(nh):
        m_ref[j] = jnp.full((1, tq), -jnp.inf, F32)
        acc_ref[j] = jnp.zeros((LANES, tq), F32)

    last = 2 * qi + 1

    @pl.when(qi == 0)
    def _():
        scores(0, 0, masked=True)
        probs(0)
        scores(1, 1, late, masked=True)

    @pl.when(qi > 0)
    def _():
        scores(0, 0)
        probs(0)
        scores(1, 1)

        def pair(u, carry):
            t = 2 * u + 1
            trip(t, 1)
            trip(t + 1, 0)
            return carry

        lax.fori_loop(0, qi - 1, pair, 0)
        trip(last - 2, 1, masked=True)
        trip(last - 1, 0, qs=late, masked=True)

    values(last - 1, 0)
    probs(1, late)
    values(last, 1, late)
    for pair in range(nh // 2):
        outs = []
        for j in (2 * pair, 2 * pair + 1):
            acc = acc_ref[j]
            outs.append(acc[:ATT_HEAD_DIM, :] / acc[ATT_HEAD_DIM:, :])
        o_ref[:, slot_lanes(pair)] = jnp.concatenate(outs, axis=0).T.astype(BF16)


def _attention(q3, k3, vt4, cum3, *, tq):
    b, s, _ = q3.shape
    nh = ATT_STEP_HEADS
    return pl.pallas_call(
        functools.partial(_attn_body, tq=tq),
        scratch_shapes=[pltpu.VMEM((nh, 1, tq), F32), pltpu.VMEM((nh, LANES, tq), F32),
                        pltpu.VMEM((2, nh, tq // 2, tq), F32), pltpu.VMEM((2, nh, 1, tq), F32),
                        pltpu.VMEM((2, nh, tq // 2, tq), BF16), pltpu.VMEM((2, nh, 1, tq), F32)],
        grid=(b, ATT_HEADS // nh, s // tq),
        in_specs=[pl.BlockSpec((None, tq, nh * LANES), lambda bi, hg, qi: (bi, qi, hg)),
                  pl.BlockSpec((None, s, nh * LANES), lambda bi, hg, qi: (bi, 0, hg)),
                  pl.BlockSpec((None, nh * ATT_HEAD_DIM, s), lambda bi, hg, qi: (bi, hg, 0)),
                  pl.BlockSpec((None, None, nh, tq), lambda bi, hg, qi: (bi, hg, 0, qi))],
        out_specs=pl.BlockSpec((None, tq, nh * ATT_HEAD_DIM), lambda bi, hg, qi: (bi, qi, hg)),
        out_shape=jax.ShapeDtypeStruct((b, s, D_ATT), BF16),
        compiler_params=_cparams("parallel", "parallel", "arbitrary"),
        name="attention",
    )(q3, k3, vt4, cum3)


R_E1, R_E2, R_W1, R_W2, R_RANK1, R_RANK2 = range(6)


def _route(logits, cnt_ref):
    tm = logits.shape[0]
    lane = lax.broadcasted_iota(jnp.int32, logits.shape, 1).astype(F32)
    lg = jnp.where(lane < N_EXPERTS, logits, -jnp.inf)
    m1 = jnp.max(lg, axis=-1, keepdims=True)
    i1 = jnp.min(jnp.where(lg == m1, lane, float(LANES)), axis=-1, keepdims=True)
    lg2 = jnp.where(lane == i1, -jnp.inf, lg)
    m2 = jnp.max(lg2, axis=-1, keepdims=True)
    i2 = jnp.min(jnp.where(lg2 == m2, lane, float(LANES)), axis=-1, keepdims=True)
    e = jnp.exp(m2 - m1)
    w1 = 1.0 / (1.0 + e)
    w2 = e / (1.0 + e)

    hit1, hit2 = lane == i1, lane == i2
    oh1, oh2 = hit1.astype(BF16), hit2.astype(BF16)
    row = lax.broadcasted_iota(jnp.int32, (tm, tm), 0)
    col = lax.broadcasted_iota(jnp.int32, (tm, tm), 1)
    before = (col < row).astype(BF16)
    carry = cnt_ref[...]
    tot1 = jnp.sum(oh1.astype(F32), axis=0, keepdims=True)
    tot2 = jnp.sum(oh2.astype(F32), axis=0, keepdims=True)
    pos1 = _dot(before, oh1) + carry
    pos2 = _dot(before, oh2) + (carry + tot1)
    rank1 = jnp.sum(jnp.where(hit1, pos1, 0.0), axis=-1, keepdims=True)
    rank2 = jnp.sum(jnp.where(hit2, pos2, 0.0), axis=-1, keepdims=True)
    cnt_ref[...] = carry + tot1 + tot2

    rec = jnp.zeros(logits.shape, F32)
    for idx, val in ((R_E1, i1), (R_E2, i2), (R_W1, w1), (R_W2, w2), (R_RANK1, rank1), (R_RANK2, rank2)):
        rec = jnp.where(lane == idx, val, rec)
    return rec


def _outproj_body(*refs, with_router):
    if with_router:
        (x_ref, ys_ref, us_ref, ya_ref, yg_ref, d_ref, wglu_ref, gn_ref, w_ref, fg_ref, rw_ref,
         xo_ref, h_ref, route_ref, cnt_ref) = refs

        @pl.when(pl.program_id(0) == 0)
        def _():
            cnt_ref[...] = jnp.zeros_like(cnt_ref)
    else:
        (x_ref, ys_ref, us_ref, ya_ref, yg_ref, d_ref, wglu_ref, gn_ref, w_ref, fg_ref,
         xo_ref, h_ref) = refs

    tm = x_ref.shape[0]
    part = tm // OUTPROJ_PARTS
    o1, o2 = D_SSM, D_SSM + D_ATT
    for k in range(OUTPROJ_PARTS):
        rows = slice(k * part, (k + 1) * part)
        y1 = _gelu(_split_load(ys_ref, rows) + d_ref[...] * _split_load(us_ref, rows))
        y_ssm = y1 * _sigmoid(_dot(y1.astype(BF16), wglu_ref[...]))

        acc = _dot(_rms(y_ssm, gn_ref[:, :o1]).astype(BF16), w_ref[:o1, :])
        acc += _dot(_rms(ya_ref[rows, :].astype(F32), gn_ref[:, o1:o2]).astype(BF16), w_ref[o1:o2, :])
        acc += _dot(_rms(yg_ref[rows, :].astype(F32), gn_ref[:, o2:]).astype(BF16), w_ref[o2:, :])
        xn = x_ref[rows, :] + acc
        xo_ref[rows, :] = xn
        hn = _rms(xn, fg_ref[...])
        h_ref[rows, :] = hn.astype(h_ref.dtype)
        if with_router:
            rw = rw_ref[...]
            prod = sum(_dot(piece, rw) for piece in _split3(hn))
            logits = (prod + pltpu.roll(prod, LANES - N_EXPERTS, axis=1)
                      + pltpu.roll(prod, LANES - 2 * N_EXPERTS, axis=1))
            route_ref[rows, :] = _route(logits, cnt_ref)


def _outproj(x2, ys, us, ya, yg, d, wglu, gn, w, fg, rw, *, tm):
    n = x2.shape[0]
    with_router = rw is not None
    row = lambda width: pl.BlockSpec((tm, width), lambda i: (i, 0))
    in_specs = [row(D_MODEL), _split_spec(tm), _split_spec(tm), row(D_ATT), row(D_GM),
                _resident((1, D_SSM)), _resident((D_SSM, D_SSM)), _resident((1, D_MODEL)),
                _resident((D_MODEL, D_MODEL)), _resident((1, D_MODEL))]
    out_specs = [row(D_MODEL), row(D_MODEL)]
    out_shape = [jax.ShapeDtypeStruct((n, D_MODEL), F32),
                 jax.ShapeDtypeStruct((n, D_MODEL), F32 if with_router else BF16)]
    args = [x2, ys, us, ya, yg, d, wglu, gn, w, fg]
    if with_router:
        in_specs.append(_resident((D_MODEL, LANES)))
        out_specs += [row(LANES), pl.BlockSpec((1, LANES), lambda i: (0, 0))]
        out_shape += [jax.ShapeDtypeStruct((n, LANES), F32), jax.ShapeDtypeStruct((1, LANES), F32)]
        args.append(rw)
    return pl.pallas_call(
        functools.partial(_outproj_body, with_router=with_router),
        grid=(n // tm,),
        in_specs=in_specs, out_specs=out_specs, out_shape=out_shape,
        compiler_params=_cparams("arbitrary" if with_router else "parallel"),
        name="outproj_router" if with_router else "outproj",
    )(*args)


def _swiglu_tile(h, wg, wu, wd):
    a = _dot(h, wg)
    return _dot((a * _sigmoid(a) * _dot(h, wu)).astype(BF16), wd)


def _finish(x, acc, fin_ref, o_ref):
    xn = x + acc
    o_ref[...] = xn if fin_ref is None else _rms(xn, fin_ref[...])


def _dense_ffn_body(*refs, final_norm):
    if final_norm:
        h_ref, x_ref, wg_ref, wu_ref, wd_ref, fin_ref, o_ref, acc_ref = refs
    else:
        h_ref, x_ref, wg_ref, wu_ref, wd_ref, o_ref, acc_ref = refs
        fin_ref = None
    f = pl.program_id(1)

    @pl.when(f == 0)
    def _():
        acc_ref[...] = jnp.zeros_like(acc_ref)

    acc_ref[...] += _swiglu_tile(h_ref[...], wg_ref[...], wu_ref[...], wd_ref[...])

    @pl.when(f == pl.num_programs(1) - 1)
    def _():
        _finish(x_ref[...], acc_ref[...], fin_ref, o_ref)


def _dense_ffn(h, x2, wg, wu, wd, fin, *, tm, tf):
    n = x2.shape[0]
    dff = wg.shape[1]
    row = lambda: pl.BlockSpec((tm, D_MODEL), lambda i, f: (i, 0))
    mode = dict(pipeline_mode=pl.Buffered(1)) if tf == dff else {}
    in_specs = [row(), row(),
                pl.BlockSpec((D_MODEL, tf), lambda i, f: (0, f), **mode),
                pl.BlockSpec((D_MODEL, tf), lambda i, f: (0, f), **mode),
                pl.BlockSpec((tf, D_MODEL), lambda i, f: (f, 0), **mode)]
    args = [h, x2, wg, wu, wd]
    if fin is not None:
        in_specs.append(pl.BlockSpec((1, D_MODEL), lambda i, f: (0, 0)))
        args.append(fin)
    return pl.pallas_call(
        functools.partial(_dense_ffn_body, final_norm=fin is not None),
        grid=(n // tm, dff // tf),
        in_specs=in_specs, out_specs=row(),
        out_shape=jax.ShapeDtypeStruct((n, D_MODEL), F32),
        scratch_shapes=[pltpu.VMEM((tm, D_MODEL), F32)],
        compiler_params=_cparams("parallel", "arbitrary"),
        name="dense_ffn",
    )(*args)


def _row_copy(src, src_row, dst, dst_row, sem):
    return pltpu.make_async_copy(src.at[pl.ds(src_row, 1)], dst.at[pl.ds(dst_row, 1)], sem)


def _moe_dispatch_body(dest_ref, h_ref, xs_in_ref, xs_ref, sem):
    del xs_in_ref
    tm = h_ref.shape[0]

    def issue(r, carry):
        for k in range(2):
            _row_copy(h_ref, r, xs_ref, dest_ref[k, r], sem).start()
        return carry

    lax.fori_loop(0, tm, issue, 0, unroll=8)
    for k in range(2):
        pltpu.make_async_copy(h_ref, xs_ref.at[pl.ds(0, tm)], sem).wait()


def _moe_dispatch(dest, h, xs_zero, *, tm):
    n = h.shape[0]
    return pl.pallas_call(
        _moe_dispatch_body,
        grid=(n // tm,),
        in_specs=[pl.BlockSpec((None, 2, tm), lambda i: (i, 0, 0), memory_space=pltpu.SMEM),
                  pl.BlockSpec((tm, D_MODEL), lambda i: (i, 0)),
                  pl.BlockSpec(memory_space=pl.ANY)],
        out_specs=pl.BlockSpec(memory_space=pl.ANY),
        out_shape=jax.ShapeDtypeStruct(xs_zero.shape, xs_zero.dtype),
        scratch_shapes=[pltpu.SemaphoreType.DMA(())],
        input_output_aliases={2: 0},
        compiler_params=pltpu.CompilerParams(dimension_semantics=("arbitrary",),
                                             vmem_limit_bytes=VMEM_LIMIT, has_side_effects=True),
        name="moe_dispatch",
    )(dest, h, xs_zero)


def _moe_ffn_body(te_ref, nu_ref, xs_ref, wg_ref, wu_ref, wd_ref, ys_ref, acc_ref):
    del te_ref
    i = pl.program_id(0)
    f = pl.program_id(1)

    @pl.when(i < nu_ref[0])
    def _():
        @pl.when(f == 0)
        def _():
            acc_ref[...] = jnp.zeros_like(acc_ref)

        acc_ref[...] += _swiglu_tile(xs_ref[...].astype(BF16), wg_ref[...], wu_ref[...], wd_ref[...])

        @pl.when(f == pl.num_programs(1) - 1)
        def _():
            ys_ref[...] = acc_ref[...]

    @pl.when((i >= nu_ref[0]) & (f == pl.num_programs(1) - 1))
    def _():
        ys_ref[...] = jnp.zeros_like(ys_ref)


def _moe_ffn(tile_expert, n_used, xs, wg, wu, wd, *, tme, tf):
    rows = xs.shape[0]
    dff = wg.shape[-1]
    nf = dff // tf
    row_idx = lambda i, f, te, nu: (jnp.minimum(i, nu[0] - 1), 0)
    col = lambda i, f, nu: jnp.where(i < nu[0], f, nf - 1)
    mode = dict(pipeline_mode=pl.Buffered(1)) if nf == 1 else {}
    grid_spec = pltpu.PrefetchScalarGridSpec(
        num_scalar_prefetch=2,
        grid=(rows // tme, nf),
        in_specs=[pl.BlockSpec((tme, D_MODEL), row_idx),
                  pl.BlockSpec((None, D_MODEL, tf), lambda i, f, te, nu: (te[i], 0, col(i, f, nu)), **mode),
                  pl.BlockSpec((None, D_MODEL, tf), lambda i, f, te, nu: (te[i], 0, col(i, f, nu)), **mode),
                  pl.BlockSpec((None, tf, D_MODEL), lambda i, f, te, nu: (te[i], col(i, f, nu), 0), **mode)],
        out_specs=pl.BlockSpec((tme, D_MODEL), lambda i, f, te, nu: (i, 0)),
        scratch_shapes=[pltpu.VMEM((tme, D_MODEL), F32)])
    return pl.pallas_call(
        _moe_ffn_body,
        grid_spec=grid_spec,
        out_shape=jax.ShapeDtypeStruct((rows, D_MODEL), F32),
        compiler_params=_cparams("arbitrary", "arbitrary"),
        name="moe_ffn",
    )(tile_expert, n_used, xs, wg, wu, wd)


def _moe_combine_body(*refs, final_norm):
    if final_norm:
        dest_ref, x_ref, route_ref, ys_ref, fin_ref, o_ref, buf_ref, sem = refs
    else:
        dest_ref, x_ref, route_ref, ys_ref, o_ref, buf_ref, sem = refs
        fin_ref = None
    tm = x_ref.shape[0]

    def issue(r, carry):
        for k in range(2):
            _row_copy(ys_ref, dest_ref[k, r], buf_ref.at[k], r, sem).start()
        return carry

    lax.fori_loop(0, tm, issue, 0, unroll=8)
    for k in range(2):
        pltpu.make_async_copy(ys_ref.at[pl.ds(0, tm)], buf_ref.at[k], sem).wait()

    route = route_ref[...]
    lane = lax.broadcasted_iota(jnp.int32, route.shape, 1)
    w1 = jnp.sum(jnp.where(lane == R_W1, route, 0.0), axis=-1, keepdims=True)
    w2 = jnp.sum(jnp.where(lane == R_W2, route, 0.0), axis=-1, keepdims=True)
    _finish(x_ref[...], w1 * buf_ref[0] + w2 * buf_ref[1], fin_ref, o_ref)


def _moe_combine(dest, x2, route, ys, fin, *, tm):
    n = x2.shape[0]
    row = lambda width: pl.BlockSpec((tm, width), lambda i: (i, 0))
    in_specs = [pl.BlockSpec((None, 2, tm), lambda i: (i, 0, 0), memory_space=pltpu.SMEM),
                row(D_MODEL), row(LANES), pl.BlockSpec(memory_space=pl.ANY)]
    args = [dest, x2, route, ys]
    if fin is not None:
        in_specs.append(pl.BlockSpec((1, D_MODEL), lambda i: (0, 0)))
        args.append(fin)
    return pl.pallas_call(
        functools.partial(_moe_combine_body, final_norm=fin is not None),
        grid=(n // tm,),
        in_specs=in_specs, out_specs=row(D_MODEL),
        out_shape=jax.ShapeDtypeStruct((n, D_MODEL), F32),
        scratch_shapes=[pltpu.VMEM((2, tm, D_MODEL), F32), pltpu.SemaphoreType.DMA(())],
        compiler_params=_cparams("arbitrary"),
        name="moe_combine",
    )(*args)


def _moe_layout(route, counts, *, tm, tme):
    n = route.shape[0]
    cnt = counts[0, :N_EXPERTS].astype(jnp.int32)
    padded = (cnt + (tme - 1)) // tme * tme
    ends = jnp.cumsum(padded)
    starts = ends - padded
    experts = jnp.arange(N_EXPERTS, dtype=F32)

    def rows(e_lane, rank_lane):
        start = jnp.sum(jnp.where(route[:, e_lane, None] == experts[None], starts[None], 0), axis=1)
        return start + route[:, rank_lane].astype(jnp.int32)

    dest = jnp.stack([rows(R_E1, R_RANK1), rows(R_E2, R_RANK2)])
    dest = dest.reshape(2, n // tm, tm).transpose(1, 0, 2)
    n_tiles = (2 * n) // tme + N_EXPERTS
    tile_start = jnp.arange(n_tiles, dtype=jnp.int32) * tme
    tile_expert = jnp.minimum(jnp.sum(tile_start[:, None] >= ends[None, :], axis=1), N_EXPERTS - 1)
    n_used = (ends[-1] // tme).reshape(1)
    tile_expert = jnp.where(tile_start < ends[-1], tile_expert, tile_expert[jnp.maximum(n_used[0] - 1, 0)])
    return dest, tile_expert.astype(jnp.int32), n_used.astype(jnp.int32), n_tiles * tme


def _largest_tile(total, cap, mult):
    best = mult
    for t in range(mult, min(total, cap) + 1, mult):
        if total % t == 0:
            best = t
    return best


def _mxu_tile(total, cap):
    if total % MXU_WIDTH == 0:
        return _largest_tile(total, cap, MXU_WIDTH)
    return _largest_tile(total, cap, LANES)


def _pad_lanes(a):
    return jnp.pad(a, [(0, 0)] * (a.ndim - 1) + [(0, LANES - a.shape[-1])])


def kernel(x, mix_norm_g, w_in, b_forget, ssm_lambda_re, ssm_lambda_im, ssm_log_dt, ssm_b_re, ssm_b_im, ssm_c_re, ssm_c_im, ssm_d, ssm_w_glu, gm_ln_g, gm_ln_b, gm_w_s, gm_b_s, group_norm_g, w_out, ffn_norm_g, dense_w_gate, dense_w_up, dense_w_down, router_w, moe_w_gate, moe_w_up, moe_w_down, final_norm_g):
    bsz, seq, _ = x.shape
    n = bsz * seq
    depth = w_in.shape[0]
    assert seq % CHUNK == 0 and seq % SSM_T == 0
    tm = _largest_tile(seq, 512, CHUNK)
    tq = _largest_tile(seq, 512, 2 * LANES)
    nc = n // SSM_T
    ncb = seq // SSM_T
    tr = _largest_tile(nc, 512, 8)
    tc = _largest_tile(ncb, 128, 8)

    x2 = x.reshape(n, D_MODEL).astype(F32)
    tril = jnp.tril(jnp.ones((CHUNK, CHUNK), F32))
    i0, i1, i2, i3, i4 = (D_SSM, D_SSM + D_ATT, D_SSM + 2 * D_ATT, D_SSM + 3 * D_ATT,
                          D_SSM + 3 * D_ATT + ATT_HEADS)
    for layer in range(depth):
        wl = w_in[layer]
        w_re = jnp.concatenate(
            [wl[:, :i0], wl[:, i0:i1] * (ATT_HEAD_DIM ** -0.5 * LOG2E), wl[:, i1:i2], wl[:, i4:],
             _pad_lanes(wl[:, i3:i4])], axis=1).astype(BF16)
        ws = (gm_w_s[layer].astype(F32) * tril[None]).astype(BF16)
        bs = jnp.repeat(gm_b_s[layer].astype(F32).T, GM_HEAD_DIM, axis=1)
        u, q, k, v, cum, ygm = _inproj(
            x2, mix_norm_g[layer].reshape(1, D_MODEL), w_re, wl[:, i2:i3].T.astype(BF16),
            _pad_lanes(b_forget[layer].reshape(1, ATT_HEADS).astype(F32)),
            gm_ln_g[layer].reshape(1, D_GM), gm_ln_b[layer].reshape(1, D_GM), ws, bs,
            seq=seq, tm=tm)

        mmat, fmat, emat, a_row = _ssm_matrices(
            ssm_lambda_re[layer], ssm_lambda_im[layer], ssm_log_dt[layer], ssm_b_re[layer],
            ssm_b_im[layer], ssm_c_re[layer], ssm_c_im[layer])
        s_loc = _ssm_state(u, fmat, tr=tr)
        xprev = _ssm_scan(s_loc.reshape(bsz, ncb, 2 * SSM_NSTATE), a_row, tc=tc)
        y_ssm = _ssm_out(u, xprev.reshape(nc, 2 * SSM_NSTATE), mmat, emat, tr=tr)

        cum_rows = cum[:, :ATT_HEADS].reshape(bsz, seq, ATT_HEADS // ATT_STEP_HEADS, ATT_STEP_HEADS)
        cum_rows = cum_rows.transpose(0, 2, 3, 1)
        y_att = _attention(q.reshape(bsz, seq, ATT_SLOTS), k.reshape(bsz, seq, ATT_SLOTS), v,
                           cum_rows, tq=tq).reshape(n, D_ATT)

        is_moe = layer % 2 == 1
        j = layer // 2
        rw = (_pad_lanes(jnp.concatenate(_split3(router_w[j].astype(F32)), axis=1))
              if is_moe else None)
        outs = _outproj(x2, y_ssm, u, y_att, ygm, ssm_d[layer].reshape(1, D_SSM).astype(F32),
                        ssm_w_glu[layer].astype(BF16), group_norm_g[layer].reshape(1, D_MODEL),
                        w_out[layer].astype(BF16), ffn_norm_g[layer].reshape(1, D_MODEL), rw, tm=tm)
        fin = final_norm_g.reshape(1, D_MODEL) if layer == depth - 1 else None
        if is_moe:
            x_mid, h, route, counts = outs
            dffe = moe_w_gate.shape[-1]
            tme = min(MOE_ROW_TILE, 2 * n)
            dest, tile_expert, n_used, rows = _moe_layout(route, counts, tm=tm, tme=tme)
            xs = _moe_dispatch(dest, h, jnp.zeros((rows, D_MODEL), F32), tm=tm)
            ys = _moe_ffn(tile_expert, n_used, xs, moe_w_gate[j].astype(BF16),
                          moe_w_up[j].astype(BF16), moe_w_down[j].astype(BF16), tme=tme,
                          tf=_mxu_tile(dffe, 3584))
            x2 = _moe_combine(dest, x_mid, route, ys, fin, tm=tm)
        else:
            x_mid, h = outs
            dff = dense_w_gate.shape[-1]
            x2 = _dense_ffn(h, x_mid, dense_w_gate[j].astype(BF16), dense_w_up[j].astype(BF16),
                            dense_w_down[j].astype(BF16), fin, tm=tm,
                            tf=_mxu_tile(dff, 2816))
    return x2.reshape(bsz, seq, D_MODEL).astype(x.dtype)
```

```python
import functools
import math

import jax
import jax.numpy as jnp
import numpy as np
from jax import lax
from jax.experimental import pallas as pl
from jax.experimental.pallas import tpu as pltpu

F32 = jnp.float32
BF16 = jnp.bfloat16

D_MODEL = 1024
D_SSM = 256
D_ATT = 512
D_GM = 256
SSM_GROUP = 16
SSM_GROUPS = 16
SSM_STATE = 64
ATT_HEADS = 8
ATT_HEAD_DIM = 64
GM_HEADS = 4
GM_HEAD_DIM = 64
CHUNK = 128
N_EXPERTS = 8
EPS = 1e-6
LOG2E = math.log2(math.e)

LANES = 128
SUBLANES = 8
MXU_WIDTH = 256
V7X_VMEM_BYTES = 64 * 1024 * 1024
ATT_SLOTS = ATT_HEADS * LANES
MOE_ROW_TILE = 512
OUTPROJ_PARTS = 2
ATT_STEP_HEADS = 2
SSM_T = 8
SSM_ROW = SSM_T * D_SSM
SSM_NSTATE = SSM_GROUPS * SSM_STATE

C_U = 0
C_Q = C_U + D_SSM
C_K = C_Q + D_ATT
C_Z = C_K + D_ATT
C_F = C_Z + 2 * D_GM
D_IN_PAD = C_F + LANES

VMEM_LIMIT = V7X_VMEM_BYTES * 7 // 8


def _cparams(*sem):
    return pltpu.CompilerParams(dimension_semantics=sem, vmem_limit_bytes=VMEM_LIMIT)


def _resident(shape):
    nd = len(shape)
    return pl.BlockSpec(shape, lambda *_: (0,) * nd, pipeline_mode=pl.Buffered(1))


def _rms(x, g):
    return x * lax.rsqrt(jnp.mean(x * x, axis=-1, keepdims=True) + EPS) * g


def _gelu(x):
    c = math.sqrt(2.0 / math.pi)
    return 0.5 * x * (1.0 + jnp.tanh(c * (x + 0.044715 * (x * x * x))))


def _sigmoid(x):
    return 1.0 / (1.0 + jnp.exp(-x))


def _split3(x):
    p1 = x.astype(BF16)
    r1 = x - p1.astype(F32)
    p2 = r1.astype(BF16)
    r2 = r1 - p2.astype(F32)
    return p1, p2, r2.astype(BF16)


def _dot(a, b):
    return jnp.dot(a, b, preferred_element_type=F32)


def _split_spec(rows):
    return pl.BlockSpec((D_SSM // LANES, rows, LANES), lambda i: (0, i, 0))


def _split_store(ref, value, rows=slice(None)):
    for h in range(D_SSM // LANES):
        ref[h, rows, :] = value[:, h * LANES:(h + 1) * LANES]


def _split_load(ref, rows=slice(None)):
    return jnp.concatenate([ref[h, rows, :] for h in range(D_SSM // LANES)], axis=1)


def _bias_lane(head):
    return 0 if head % 2 else ATT_HEAD_DIM


def _inproj_body(x_ref, g_ref, w_ref, wvt_ref, bf_ref, lng_ref, lnb_ref, ws_ref, bs_ref, place_ref, qc_ref,
                 u_ref, q_ref, k_ref, v_ref, cum_ref, ygm_ref, carry_ref, *, tiles_per_seq):
    i = pl.program_id(0)

    @pl.when(i % tiles_per_seq == 0)
    def _():
        carry_ref[...] = jnp.zeros_like(carry_ref)

    tm = x_ref.shape[0]
    hb = _rms(x_ref[...], g_ref[...]).astype(BF16)

    def proj(lo, width):
        return _dot(hb, w_ref[:, lo:lo + width])

    _split_store(u_ref, proj(C_U, D_SSM))
    v_ref[...] = lax.dot_general(wvt_ref[...], hb, (((1,), (1,)), ((), ())),
                                 preferred_element_type=F32).astype(BF16)

    f = proj(C_F, LANES) + bf_ref[...]
    logf = (jnp.minimum(f, 0.0) - jnp.log(1.0 + jnp.exp(-jnp.abs(f)))) * LOG2E
    row = lax.broadcasted_iota(jnp.int32, (tm, tm), 0)
    col = lax.broadcasted_iota(jnp.int32, (tm, tm), 1)
    tri = (col <= row).astype(BF16)
    sums = _dot(tri, jnp.concatenate(_split3(logf), axis=1))
    cum = sums[:, :LANES] + sums[:, LANES:2 * LANES] + sums[:, 2 * LANES:] + carry_ref[...]
    cum_ref[...] = cum
    carry_ref[...] = cum[tm - 1:tm, :]

    lane = lax.broadcasted_iota(jnp.int32, (tm, LANES), 1)
    c1, c2, c3 = [jnp.where(lane < ATT_HEADS, piece.astype(F32), 0.0) for piece in _split3(cum)]
    packed = c1 + pltpu.roll(c2, ATT_HEADS, axis=1) + pltpu.roll(c3, 2 * ATT_HEADS, axis=1)
    placed = _dot(packed.astype(BF16), place_ref[...])
    upper = lax.broadcasted_iota(jnp.int32, (tm, LANES), 1) >= ATT_HEAD_DIM
    qf = proj(C_Q, D_ATT)
    kf = proj(C_K, D_ATT)
    for h in range(ATT_HEADS):
        pair = slice((h // 2) * LANES, (h // 2 + 1) * LANES)
        slot = slice(h * LANES, (h + 1) * LANES)
        own = upper if h % 2 else ~upper
        q_ref[:, slot] = jnp.where(own, qf[:, pair], qc_ref[:, slot]).astype(BF16)
        k_ref[:, slot] = jnp.where(own, kf[:, pair], placed[:, slot]).astype(BF16)

    zg = _gelu(proj(C_Z, 2 * D_GM))
    ug = zg[:, :D_GM]
    vg = zg[:, D_GM:]
    mean = jnp.mean(vg, axis=-1, keepdims=True)
    cen = vg - mean
    var = jnp.mean(cen * cen, axis=-1, keepdims=True)
    vn = (cen * lax.rsqrt(var + EPS) * lng_ref[...] + lnb_ref[...]).astype(BF16)
    lane_head = lax.broadcasted_iota(jnp.int32, (CHUNK, D_GM), 1) >> 6
    for c in range(tm // CHUNK):
        rows = slice(c * CHUNK, (c + 1) * CHUNK)
        vc = vn[rows, :]
        mixed = _dot(ws_ref[0], vc)
        for g in range(1, GM_HEADS):
            mixed = jnp.where(lane_head == g, _dot(ws_ref[g], vc), mixed)
        mixed = mixed + bs_ref[...]
        ygm_ref[rows, :] = (ug[rows, :] * mixed).astype(BF16)


def _attn_slot_constants():
    place = np.zeros((LANES, ATT_SLOTS), np.float32)
    qconst = np.zeros((1, ATT_SLOTS), np.float32)
    for h in range(ATT_HEADS):
        for piece in range(3):
            lane = h * LANES + _bias_lane(h) + piece
            place[piece * ATT_HEADS + h, lane] = 1.0
            qconst[0, lane] = -1.0
    return jnp.asarray(place, BF16), jnp.asarray(qconst)


def _inproj(x2, g, w, wvt, bf, lng, lnb, ws, bs, *, seq, tm):
    n = x2.shape[0]
    tps = seq // tm
    place, qconst = _attn_slot_constants()
    row = lambda width: pl.BlockSpec((tm, width), lambda i: (i, 0))
    return pl.pallas_call(
        functools.partial(_inproj_body, tiles_per_seq=tps),
        grid=(n // tm,),
        in_specs=[row(D_MODEL), _resident((1, D_MODEL)), _resident((D_MODEL, D_IN_PAD)),
                  _resident((D_ATT, D_MODEL)),
                  _resident((1, LANES)), _resident((1, D_GM)), _resident((1, D_GM)),
                  _resident((GM_HEADS, CHUNK, CHUNK)), _resident((CHUNK, D_GM)),
                  _resident((LANES, ATT_SLOTS)), _resident((1, ATT_SLOTS))],
        out_specs=[_split_spec(tm), row(ATT_SLOTS), row(ATT_SLOTS),
                   pl.BlockSpec((None, D_ATT, tm), lambda i: (i // tps, 0, i % tps)),
                   row(LANES), row(D_GM)],
        out_shape=[jax.ShapeDtypeStruct((D_SSM // LANES, n, LANES), F32),
                   jax.ShapeDtypeStruct((n, ATT_SLOTS), BF16),
                   jax.ShapeDtypeStruct((n, ATT_SLOTS), BF16),
                   jax.ShapeDtypeStruct((n // seq, D_ATT, seq), BF16),
                   jax.ShapeDtypeStruct((n, LANES), F32),
                   jax.ShapeDtypeStruct((n, D_GM), BF16)],
        scratch_shapes=[pltpu.VMEM((1, LANES), F32)],
        compiler_params=_cparams("arbitrary"),
        name="inproj",
    )(x2, g, w, wvt, bf, lng, lnb, ws, bs, place, qconst)


def _ssm_tokens(u_ref, s, tr):
    return _split_load(u_ref, pl.ds(s, tr, stride=SSM_T)).astype(BF16)


def _ssm_state_body(u_ref, f_ref, s_ref):
    tr = s_ref.shape[0]
    acc = _dot(_ssm_tokens(u_ref, 0, tr), f_ref[:D_SSM, :])
    for s in range(1, SSM_T):
        acc += _dot(_ssm_tokens(u_ref, s, tr), f_ref[s * D_SSM:(s + 1) * D_SSM, :])
    s_ref[...] = acc


def _ssm_state(u, fmat, *, tr):
    nc = u.shape[1] // SSM_T
    return pl.pallas_call(
        _ssm_state_body,
        grid=(nc // tr,),
        in_specs=[_split_spec(tr * SSM_T), _resident((SSM_ROW, 2 * SSM_NSTATE))],
        out_specs=pl.BlockSpec((tr, 2 * SSM_NSTATE), lambda i: (i, 0)),
        out_shape=jax.ShapeDtypeStruct((nc, 2 * SSM_NSTATE), F32),
        compiler_params=_cparams("parallel"),
        name="ssm_state",
    )(u, fmat)


def _ssm_scan_body(s_ref, a_ref, xp_ref, st_ref):
    @pl.when(pl.program_id(1) == 0)
    def _():
        st_ref[...] = jnp.zeros_like(st_ref)

    tc = s_ref.shape[0]
    are = a_ref[:, :SSM_NSTATE]
    aim = a_ref[:, SSM_NSTATE:]

    def step(c8, carry):
        xre, xim = carry
        base = pl.multiple_of(c8 * SUBLANES, SUBLANES)
        blk = s_ref[pl.ds(base, SUBLANES), :]
        prev_re, prev_im = [], []
        for r in range(SUBLANES):
            prev_re.append(xre)
            prev_im.append(xim)
            sre = blk[r:r + 1, :SSM_NSTATE]
            sim = blk[r:r + 1, SSM_NSTATE:]
            xre, xim = are * xre - aim * xim + sre, are * xim + aim * xre + sim
        xp_ref[pl.ds(base, SUBLANES), :SSM_NSTATE] = jnp.concatenate(prev_re, axis=0)
        xp_ref[pl.ds(base, SUBLANES), SSM_NSTATE:] = jnp.concatenate(prev_im, axis=0)
        return xre, xim

    xre, xim = lax.fori_loop(0, tc // SUBLANES, step,
                             (st_ref[:, :SSM_NSTATE], st_ref[:, SSM_NSTATE:]))
    st_ref[:, :SSM_NSTATE] = xre
    st_ref[:, SSM_NSTATE:] = xim


def _ssm_scan(s3, a, *, tc):
    b, ncb, w = s3.shape
    spec = pl.BlockSpec((None, tc, w), lambda bi, ci: (bi, ci, 0))
    return pl.pallas_call(
        _ssm_scan_body,
        grid=(b, ncb // tc),
        in_specs=[spec, pl.BlockSpec((1, w), lambda bi, ci: (0, 0))],
        out_specs=spec,
        out_shape=jax.ShapeDtypeStruct(s3.shape, F32),
        scratch_shapes=[pltpu.VMEM((1, w), F32)],
        compiler_params=_cparams("parallel", "arbitrary"),
        name="ssm_scan",
    )(s3, a)


def _ssm_out_body(u_ref, xp_ref, m_ref, e_ref, y_ref):
    tr = xp_ref.shape[0]
    us = [_ssm_tokens(u_ref, s, tr) for s in range(SSM_T)]
    xp = xp_ref[...].astype(BF16)
    for t in range(SSM_T):
        cols = slice(t * D_SSM, (t + 1) * D_SSM)
        acc = _dot(xp, e_ref[:, cols])
        for s in range(t + 1):
            acc += _dot(us[s], m_ref[s * D_SSM:(s + 1) * D_SSM, cols])
        _split_store(y_ref, acc, pl.ds(t, tr, stride=SSM_T))


def _ssm_out(u, xp, mmat, emat, *, tr):
    nc = xp.shape[0]
    return pl.pallas_call(
        _ssm_out_body,
        grid=(nc // tr,),
        in_specs=[_split_spec(tr * SSM_T),
                  pl.BlockSpec((tr, 2 * SSM_NSTATE), lambda i: (i, 0)),
                  _resident((SSM_ROW, SSM_ROW)), _resident((2 * SSM_NSTATE, SSM_ROW))],
        out_specs=_split_spec(tr * SSM_T),
        out_shape=jax.ShapeDtypeStruct((D_SSM // LANES, nc * SSM_T, LANES), F32),
        compiler_params=_cparams("parallel"),
        name="ssm_out",
    )(u, xp, mmat, emat)


def _ssm_matrices(lam_re, lam_im, log_dt, b_re, b_im, c_re, c_im):
    t, g, p, h = SSM_T, SSM_GROUPS, SSM_STATE, SSM_GROUP
    ns, nch = g * p, g * h
    hp = lax.Precision.HIGHEST
    lr, li = lam_re.astype(F32).reshape(ns), lam_im.astype(F32).reshape(ns)
    dt = jnp.repeat(jnp.exp(log_dt.astype(F32)), p)
    steps = jnp.arange(t + 1, dtype=F32)[:, None]
    mag = jnp.exp((lr * dt)[None] * steps)
    ang = (li * dt)[None] * steps
    pr, pi = mag * jnp.cos(ang), mag * jnp.sin(ang)
    nr, ni = pr[1] - 1.0, pi[1]
    den = lr * lr + li * li
    zr, zi = ((nr * lr + ni * li) / den)[:, None], ((ni * lr - nr * li) / den)[:, None]
    br, bi = b_re.astype(F32).reshape(ns, h), b_im.astype(F32).reshape(ns, h)
    cr = c_re.astype(F32).transpose(0, 2, 1).reshape(ns, h)
    ci = c_im.astype(F32).transpose(0, 2, 1).reshape(ns, h)
    same = jnp.asarray(np.repeat(np.repeat(np.eye(g, dtype=np.float32), p, axis=0), h, axis=1))
    spread = lambda w: jnp.tile(w, (1, g)) * same
    bhr, bhi = spread(zr * br - zi * bi), spread(zr * bi + zi * br)
    chr_, chi = spread(cr), spread(ci)
    prc, pic = pr.T[:, :, None], pi.T[:, :, None]
    er = (prc * chr_[:, None, :] - pic * chi[:, None, :]).reshape(ns, (t + 1) * nch)
    ei = (prc * chi[:, None, :] + pic * chr_[:, None, :]).reshape(ns, (t + 1) * nch)

    k_all = (jnp.dot(bhr.T, er[:, :t * nch], precision=hp)
             - jnp.dot(bhi.T, ei[:, :t * nch], precision=hp))
    mmat = jnp.concatenate(
        [jnp.pad(k_all[:, :(t - s) * nch], ((0, 0), (s * nch, 0))) for s in range(t)], axis=0)

    prr, pir = pr[:t][::-1][:, None, :], pi[:t][::-1][:, None, :]
    bhrt, bhit = bhr.T[None], bhi.T[None]
    fmat = jnp.concatenate([(bhrt * prr - bhit * pir).reshape(t * nch, ns),
                            (bhrt * pir + bhit * prr).reshape(t * nch, ns)], axis=1)

    emat = jnp.concatenate([er[:, nch:], -ei[:, nch:]], axis=0)

    a_row = jnp.concatenate([pr[t][None], pi[t][None]], axis=1)
    return mmat.astype(BF16), fmat.astype(BF16), emat.astype(BF16), a_row


def _attn_body(q_ref, k_ref, v_ref, cq_ref, o_ref,
               m_ref, acc_ref, s_ref, mt_ref, p_ref, al_ref, *, tq):
    qi = pl.program_id(2)
    nh = ATT_STEP_HEADS
    tk = tq // 2
    every = slice(0, tq)
    late = slice(tk, tq)

    def slot_lanes(j):
        return slice(j * LANES, (j + 1) * LANES)

    def scores(t, slot, qs=every, masked=False):
        start = pl.multiple_of(t * tk, tk)
        width = qs.stop - qs.start
        for j in range(nh):
            kt = k_ref[pl.ds(start, tk), slot_lanes(j)]
            s = lax.dot_general(kt, q_ref[qs, slot_lanes(j)], (((1,), (1,)), ((), ())),
                                preferred_element_type=F32)
            if masked:
                keypos = lax.broadcasted_iota(jnp.int32, (tk, width), 0)
                qpos = lax.broadcasted_iota(jnp.int32, (tk, width), 1)
                s = jnp.where(keypos <= qpos, s, -jnp.inf)
            s_ref[slot, j, :, qs] = s
            mt_ref[slot, j, :, qs] = jnp.max(s, axis=0, keepdims=True)

    def probs(slot, qs=every):
        for j in range(nh):
            m = m_ref[j, :, qs]
            cq = cq_ref[j:j + 1, qs]
            m_new = jnp.maximum(m, mt_ref[slot, j, :, qs] + cq)
            p_ref[slot, j, :, qs] = jnp.exp2(s_ref[slot, j, :, qs] + (cq - m_new)).astype(BF16)
            al_ref[slot, j, :, qs] = jnp.exp2(m - m_new)
            m_ref[j, :, qs] = m_new

    def values(t, slot, qs=every):
        start = pl.multiple_of(t * tk, tk)
        for j in range(nh):
            vt = v_ref[j * ATT_HEAD_DIM:(j + 1) * ATT_HEAD_DIM, pl.ds(start, tk)]
            vt = jnp.concatenate([vt, jnp.ones_like(vt)], axis=0)
            acc_ref[j, :, qs] = (al_ref[slot, j, :, qs] * acc_ref[j, :, qs]
                                 + _dot(vt, p_ref[slot, j, :, qs]))

    def trip(t, slot, **next_tile):
        values(t - 1, 1 - slot)
        probs(slot)
        scores(t + 1, 1 - slot, **next_tile)

    for j in range(nh):
        m_ref[j] = jnp.full((1, tq), -jnp.inf, F32)
        acc_ref[j] = jnp.zeros((LANES, tq), F32)

    last = 2 * qi + 1

    @pl.when(qi == 0)
    def _():
        scores(0, 0, masked=True)
        probs(0)
        scores(1, 1, late, masked=True)

    @pl.when(qi > 0)
    def _():
        scores(0, 0)
        probs(0)
        scores(1, 1)

        def pair(u, carry):
            t = 2 * u + 1
            trip(t, 1)
            trip(t + 1, 0)
            return carry

        lax.fori_loop(0, qi - 1, pair, 0)
        trip(last - 2, 1, masked=True)
        trip(last - 1, 0, qs=late, masked=True)

    values(last - 1, 0)
    probs(1, late)
    values(last, 1, late)
    for pair in range(nh // 2):
        outs = []
        for j in (2 * pair, 2 * pair + 1):
            acc = acc_ref[j]
            outs.append(acc[:ATT_HEAD_DIM, :] / acc[ATT_HEAD_DIM:, :])
        o_ref[:, slot_lanes(pair)] = jnp.concatenate(outs, axis=0).T.astype(BF16)


def _attention(q3, k3, vt4, cum3, *, tq):
    b, s, _ = q3.shape
    nh = ATT_STEP_HEADS
    return pl.pallas_call(
        functools.partial(_attn_body, tq=tq),
        scratch_shapes=[pltpu.VMEM((nh, 1, tq), F32), pltpu.VMEM((nh, LANES, tq), F32),
                        pltpu.VMEM((2, nh, tq // 2, tq), F32), pltpu.VMEM((2, nh, 1, tq), F32),
                        pltpu.VMEM((2, nh, tq // 2, tq), BF16), pltpu.VMEM((2, nh, 1, tq), F32)],
        grid=(b, ATT_HEADS // nh, s // tq),
        in_specs=[pl.BlockSpec((None, tq, nh * LANES), lambda bi, hg, qi: (bi, qi, hg)),
                  pl.BlockSpec((None, s, nh * LANES), lambda bi, hg, qi: (bi, 0, hg)),
                  pl.BlockSpec((None, nh * ATT_HEAD_DIM, s), lambda bi, hg, qi: (bi, hg, 0)),
                  pl.BlockSpec((None, None, nh, tq), lambda bi, hg, qi: (bi, hg, 0, qi))],
        out_specs=pl.BlockSpec((None, tq, nh * ATT_HEAD_DIM), lambda bi, hg, qi: (bi, qi, hg)),
        out_shape=jax.ShapeDtypeStruct((b, s, D_ATT), BF16),
        compiler_params=_cparams("parallel", "parallel", "arbitrary"),
        name="attention",
    )(q3, k3, vt4, cum3)


R_E1, R_E2, R_W1, R_W2, R_RANK1, R_RANK2 = range(6)


def _route(logits, cnt_ref):
    tm = logits.shape[0]
    lane = lax.broadcasted_iota(jnp.int32, logits.shape, 1).astype(F32)
    lg = jnp.where(lane < N_EXPERTS, logits, -jnp.inf)
    m1 = jnp.max(lg, axis=-1, keepdims=True)
    i1 = jnp.min(jnp.where(lg == m1, lane, float(LANES)), axis=-1, keepdims=True)
    lg2 = jnp.where(lane == i1, -jnp.inf, lg)
    m2 = jnp.max(lg2, axis=-1, keepdims=True)
    i2 = jnp.min(jnp.where(lg2 == m2, lane, float(LANES)), axis=-1, keepdims=True)
    e = jnp.exp(m2 - m1)
    w1 = 1.0 / (1.0 + e)
    w2 = e / (1.0 + e)

    hit1, hit2 = lane == i1, lane == i2
    oh1, oh2 = hit1.astype(BF16), hit2.astype(BF16)
    row = lax.broadcasted_iota(jnp.int32, (tm, tm), 0)
    col = lax.broadcasted_iota(jnp.int32, (tm, tm), 1)
    before = (col < row).astype(BF16)
    carry = cnt_ref[...]
    tot1 = jnp.sum(oh1.astype(F32), axis=0, keepdims=True)
    tot2 = jnp.sum(oh2.astype(F32), axis=0, keepdims=True)
    pos1 = _dot(before, oh1) + carry
    pos2 = _dot(before, oh2) + (carry + tot1)
    rank1 = jnp.sum(jnp.where(hit1, pos1, 0.0), axis=-1, keepdims=True)
    rank2 = jnp.sum(jnp.where(hit2, pos2, 0.0), axis=-1, keepdims=True)
    cnt_ref[...] = carry + tot1 + tot2

    rec = jnp.zeros(logits.shape, F32)
    for idx, val in ((R_E1, i1), (R_E2, i2), (R_W1, w1), (R_W2, w2), (R_RANK1, rank1), (R_RANK2, rank2)):
        rec = jnp.where(lane == idx, val, rec)
    return rec


def _outproj_body(*refs, with_router):
    if with_router:
        (x_ref, ys_ref, us_ref, ya_ref, yg_ref, d_ref, wglu_ref, gn_ref, w_ref, fg_ref, rw_ref,
         xo_ref, h_ref, route_ref, cnt_ref) = refs

        @pl.when(pl.program_id(0) == 0)
        def _():
            cnt_ref[...] = jnp.zeros_like(cnt_ref)
    else:
        (x_ref, ys_ref, us_ref, ya_ref, yg_ref, d_ref, wglu_ref, gn_ref, w_ref, fg_ref,
         xo_ref, h_ref) = refs

    tm = x_ref.shape[0]
    part = tm // OUTPROJ_PARTS
    o1, o2 = D_SSM, D_SSM + D_ATT
    for k in range(OUTPROJ_PARTS):
        rows = slice(k * part, (k + 1) * part)
        y1 = _gelu(_split_load(ys_ref, rows) + d_ref[...] * _split_load(us_ref, rows))
        y_ssm = y1 * _sigmoid(_dot(y1.astype(BF16), wglu_ref[...]))

        acc = _dot(_rms(y_ssm, gn_ref[:, :o1]).astype(BF16), w_ref[:o1, :])
        acc += _dot(_rms(ya_ref[rows, :].astype(F32), gn_ref[:, o1:o2]).astype(BF16), w_ref[o1:o2, :])
        acc += _dot(_rms(yg_ref[rows, :].astype(F32), gn_ref[:, o2:]).astype(BF16), w_ref[o2:, :])
        xn = x_ref[rows, :] + acc
        xo_ref[rows, :] = xn
        hn = _rms(xn, fg_ref[...])
        h_ref[rows, :] = hn.astype(h_ref.dtype)
        if with_router:
            rw = rw_ref[...]
            prod = sum(_dot(piece, rw) for piece in _split3(hn))
            logits = (prod + pltpu.roll(prod, LANES - N_EXPERTS, axis=1)
                      + pltpu.roll(prod, LANES - 2 * N_EXPERTS, axis=1))
            route_ref[rows, :] = _route(logits, cnt_ref)


def _outproj(x2, ys, us, ya, yg, d, wglu, gn, w, fg, rw, *, tm):
    n = x2.shape[0]
    with_router = rw is not None
    row = lambda width: pl.BlockSpec((tm, width), lambda i: (i, 0))
    in_specs = [row(D_MODEL), _split_spec(tm), _split_spec(tm), row(D_ATT), row(D_GM),
                _resident((1, D_SSM)), _resident((D_SSM, D_SSM)), _resident((1, D_MODEL)),
                _resident((D_MODEL, D_MODEL)), _resident((1, D_MODEL))]
    out_specs = [row(D_MODEL), row(D_MODEL)]
    out_shape = [jax.ShapeDtypeStruct((n, D_MODEL), F32),
                 jax.ShapeDtypeStruct((n, D_MODEL), F32 if with_router else BF16)]
    args = [x2, ys, us, ya, yg, d, wglu, gn, w, fg]
    if with_router:
        in_specs.append(_resident((D_MODEL, LANES)))
        out_specs += [row(LANES), pl.BlockSpec((1, LANES), lambda i: (0, 0))]
        out_shape += [jax.ShapeDtypeStruct((n, LANES), F32), jax.ShapeDtypeStruct((1, LANES), F32)]
        args.append(rw)
    return pl.pallas_call(
        functools.partial(_outproj_body, with_router=with_router),
        grid=(n // tm,),
        in_specs=in_specs, out_specs=out_specs, out_shape=out_shape,
        compiler_params=_cparams("arbitrary" if with_router else "parallel"),
        name="outproj_router" if with_router else "outproj",
    )(*args)


def _swiglu_tile(h, wg, wu, wd):
    a = _dot(h, wg)
    return _dot((a * _sigmoid(a) * _dot(h, wu)).astype(BF16), wd)


def _finish(x, acc, fin_ref, o_ref):
    xn = x + acc
    o_ref[...] = xn if fin_ref is None else _rms(xn, fin_ref[...])


def _dense_ffn_body(*refs, final_norm):
    if final_norm:
        h_ref, x_ref, wg_ref, wu_ref, wd_ref, fin_ref, o_ref, acc_ref = refs
    else:
        h_ref, x_ref, wg_ref, wu_ref, wd_ref, o_ref, acc_ref = refs
        fin_ref = None
    f = pl.program_id(1)

    @pl.when(f == 0)
    def _():
        acc_ref[...] = jnp.zeros_like(acc_ref)

    acc_ref[...] += _swiglu_tile(h_ref[...], wg_ref[...], wu_ref[...], wd_ref[...])

    @pl.when(f == pl.num_programs(1) - 1)
    def _():
        _finish(x_ref[...], acc_ref[...], fin_ref, o_ref)


def _dense_ffn(h, x2, wg, wu, wd, fin, *, tm, tf):
    n = x2.shape[0]
    dff = wg.shape[1]
    row = lambda: pl.BlockSpec((tm, D_MODEL), lambda i, f: (i, 0))
    mode = dict(pipeline_mode=pl.Buffered(1)) if tf == dff else {}
    in_specs = [row(), row(),
                pl.BlockSpec((D_MODEL, tf), lambda i, f: (0, f), **mode),
                pl.BlockSpec((D_MODEL, tf), lambda i, f: (0, f), **mode),
                pl.BlockSpec((tf, D_MODEL), lambda i, f: (f, 0), **mode)]
    args = [h, x2, wg, wu, wd]
    if fin is not None:
        in_specs.append(pl.BlockSpec((1, D_MODEL), lambda i, f: (0, 0)))
        args.append(fin)
    return pl.pallas_call(
        functools.partial(_dense_ffn_body, final_norm=fin is not None),
        grid=(n // tm, dff // tf),
        in_specs=in_specs, out_specs=row(),
        out_shape=jax.ShapeDtypeStruct((n, D_MODEL), F32),
        scratch_shapes=[pltpu.VMEM((tm, D_MODEL), F32)],
        compiler_params=_cparams("parallel", "arbitrary"),
        name="dense_ffn",
    )(*args)


def _row_copy(src, src_row, dst, dst_row, sem):
    return pltpu.make_async_copy(src.at[pl.ds(src_row, 1)], dst.at[pl.ds(dst_row, 1)], sem)


def _moe_dispatch_body(dest_ref, gaps_ref, h_ref, xs_ref, zero_ref, sem, zsem, *, tme):
    tm = h_ref.shape[0]

    @pl.when(pl.program_id(0) == 0)
    def _():
        zero_ref[...] = jnp.zeros_like(zero_ref)

        def zero_copy(gap, i):
            if gap == N_EXPERTS:
                return pltpu.make_async_copy(zero_ref, xs_ref.at[pl.ds(i * tme, tme)], zsem)
            return _row_copy(zero_ref, 0, xs_ref, i, zsem)

        for gap in range(N_EXPERTS + 1):
            lo, hi = gaps_ref[0, gap], gaps_ref[1, gap]

            def start(i, carry, gap=gap):
                zero_copy(gap, i).start()
                return carry

            def wait(i, carry, gap=gap):
                zero_copy(gap, i).wait()
                return carry

            lax.fori_loop(lo, hi, start, 0)
            lax.fori_loop(lo, hi, wait, 0)

    def issue(r, carry):
        for k in range(2):
            _row_copy(h_ref, r, xs_ref, dest_ref[k, r], sem).start()
        return carry

    lax.fori_loop(0, tm, issue, 0, unroll=8)
    for k in range(2):
        pltpu.make_async_copy(h_ref, xs_ref.at[pl.ds(0, tm)], sem).wait()


def _moe_dispatch(dest, gaps, h, *, rows, tm, tme):
    n = h.shape[0]
    return pl.pallas_call(
        functools.partial(_moe_dispatch_body, tme=tme),
        grid=(n // tm,),
        in_specs=[pl.BlockSpec((None, 2, tm), lambda i: (i, 0, 0), memory_space=pltpu.SMEM),
                  pl.BlockSpec(memory_space=pltpu.SMEM),
                  pl.BlockSpec((tm, D_MODEL), lambda i: (i, 0))],
        out_specs=pl.BlockSpec(memory_space=pl.ANY),
        out_shape=jax.ShapeDtypeStruct((rows, D_MODEL), F32),
        scratch_shapes=[pltpu.VMEM((tme, D_MODEL), F32), pltpu.SemaphoreType.DMA(()),
                        pltpu.SemaphoreType.DMA(())],
        compiler_params=pltpu.CompilerParams(dimension_semantics=("arbitrary",),
                                             vmem_limit_bytes=VMEM_LIMIT, has_side_effects=True),
        name="moe_dispatch",
    )(dest, gaps, h)


def _moe_ffn_body(te_ref, nu_ref, xs_ref, wg_ref, wu_ref, wd_ref, ys_ref, acc_ref):
    del te_ref
    i = pl.program_id(0)
    f = pl.program_id(1)

    @pl.when(i < nu_ref[0])
    def _():
        @pl.when(f == 0)
        def _():
            acc_ref[...] = jnp.zeros_like(acc_ref)

        acc_ref[...] += _swiglu_tile(xs_ref[...].astype(BF16), wg_ref[...], wu_ref[...], wd_ref[...])

        @pl.when(f == pl.num_programs(1) - 1)
        def _():
            ys_ref[...] = acc_ref[...]

    @pl.when((i >= nu_ref[0]) & (f == pl.num_programs(1) - 1))
    def _():
        ys_ref[...] = jnp.zeros_like(ys_ref)


def _moe_ffn(tile_expert, n_used, xs, wg, wu, wd, *, tme, tf):
    rows = xs.shape[0]
    dff = wg.shape[-1]
    nf = dff // tf
    row_idx = lambda i, f, te, nu: (jnp.minimum(i, nu[0] - 1), 0)
    col = lambda i, f, nu: jnp.where(i < nu[0], f, nf - 1)
    mode = dict(pipeline_mode=pl.Buffered(1)) if nf == 1 else {}
    grid_spec = pltpu.PrefetchScalarGridSpec(
        num_scalar_prefetch=2,
        grid=(rows // tme, nf),
        in_specs=[pl.BlockSpec((tme, D_MODEL), row_idx),
                  pl.BlockSpec((None, D_MODEL, tf), lambda i, f, te, nu: (te[i], 0, col(i, f, nu)), **mode),
                  pl.BlockSpec((None, D_MODEL, tf), lambda i, f, te, nu: (te[i], 0, col(i, f, nu)), **mode),
                  pl.BlockSpec((None, tf, D_MODEL), lambda i, f, te, nu: (te[i], col(i, f, nu), 0), **mode)],
        out_specs=pl.BlockSpec((tme, D_MODEL), lambda i, f, te, nu: (i, 0)),
        scratch_shapes=[pltpu.VMEM((tme, D_MODEL), F32)])
    return pl.pallas_call(
        _moe_ffn_body,
        grid_spec=grid_spec,
        out_shape=jax.ShapeDtypeStruct((rows, D_MODEL), F32),
        compiler_params=_cparams("arbitrary", "arbitrary"),
        name="moe_ffn",
    )(tile_expert, n_used, xs, wg, wu, wd)


def _moe_combine_body(*refs, final_norm):
    if final_norm:
        dest_ref, x_ref, route_ref, ys_ref, fin_ref, o_ref, buf_ref, sem = refs
    else:
        dest_ref, x_ref, route_ref, ys_ref, o_ref, buf_ref, sem = refs
        fin_ref = None
    tm = x_ref.shape[0]

    def issue(r, carry):
        for k in range(2):
            _row_copy(ys_ref, dest_ref[k, r], buf_ref.at[k], r, sem).start()
        return carry

    lax.fori_loop(0, tm, issue, 0, unroll=8)
    for k in range(2):
        pltpu.make_async_copy(ys_ref.at[pl.ds(0, tm)], buf_ref.at[k], sem).wait()

    route = route_ref[...]
    lane = lax.broadcasted_iota(jnp.int32, route.shape, 1)
    w1 = jnp.sum(jnp.where(lane == R_W1, route, 0.0), axis=-1, keepdims=True)
    w2 = jnp.sum(jnp.where(lane == R_W2, route, 0.0), axis=-1, keepdims=True)
    _finish(x_ref[...], w1 * buf_ref[0] + w2 * buf_ref[1], fin_ref, o_ref)


def _moe_combine(dest, x2, route, ys, fin, *, tm):
    n = x2.shape[0]
    row = lambda width: pl.BlockSpec((tm, width), lambda i: (i, 0))
    in_specs = [pl.BlockSpec((None, 2, tm), lambda i: (i, 0, 0), memory_space=pltpu.SMEM),
                row(D_MODEL), row(LANES), pl.BlockSpec(memory_space=pl.ANY)]
    args = [dest, x2, route, ys]
    if fin is not None:
        in_specs.append(pl.BlockSpec((1, D_MODEL), lambda i: (0, 0)))
        args.append(fin)
    return pl.pallas_call(
        functools.partial(_moe_combine_body, final_norm=fin is not None),
        grid=(n // tm,),
        in_specs=in_specs, out_specs=row(D_MODEL),
        out_shape=jax.ShapeDtypeStruct((n, D_MODEL), F32),
        scratch_shapes=[pltpu.VMEM((2, tm, D_MODEL), F32), pltpu.SemaphoreType.DMA(())],
        compiler_params=_cparams("arbitrary"),
        name="moe_combine",
    )(*args)


def _moe_layout(route, counts, *, tm, tme):
    n = route.shape[0]
    cnt = counts[0, :N_EXPERTS].astype(jnp.int32)
    padded = (cnt + (tme - 1)) // tme * tme
    ends = jnp.cumsum(padded)
    starts = ends - padded
    experts = jnp.arange(N_EXPERTS, dtype=F32)

    def rows(e_lane, rank_lane):
        start = jnp.sum(jnp.where(route[:, e_lane, None] == experts[None], starts[None], 0), axis=1)
        return start + route[:, rank_lane].astype(jnp.int32)

    dest = jnp.stack([rows(R_E1, R_RANK1), rows(R_E2, R_RANK2)])
    dest = dest.reshape(2, n // tm, tm).transpose(1, 0, 2)
    n_tiles = (2 * n) // tme + N_EXPERTS
    tile_start = jnp.arange(n_tiles, dtype=jnp.int32) * tme
    tile_expert = jnp.minimum(jnp.sum(tile_start[:, None] >= ends[None, :], axis=1), N_EXPERTS - 1)
    n_used = (ends[-1] // tme).reshape(1)
    tile_expert = jnp.where(tile_start < ends[-1], tile_expert, tile_expert[jnp.maximum(n_used[0] - 1, 0)])
    gaps = jnp.stack([jnp.concatenate([starts + cnt, n_used]),
                      jnp.concatenate([ends, jnp.full((1,), n_tiles, jnp.int32)])]).astype(jnp.int32)
    return dest, gaps, tile_expert.astype(jnp.int32), n_used.astype(jnp.int32), n_tiles * tme


def _largest_tile(total, cap, mult):
    best = mult
    for t in range(mult, min(total, cap) + 1, mult):
        if total % t == 0:
            best = t
    return best


def _mxu_tile(total, cap):
    if total % MXU_WIDTH == 0:
        return _largest_tile(total, cap, MXU_WIDTH)
    return _largest_tile(total, cap, LANES)


def _pad_lanes(a):
    return jnp.pad(a, [(0, 0)] * (a.ndim - 1) + [(0, LANES - a.shape[-1])])


def kernel(x, mix_norm_g, w_in, b_forget, ssm_lambda_re, ssm_lambda_im, ssm_log_dt, ssm_b_re, ssm_b_im, ssm_c_re, ssm_c_im, ssm_d, ssm_w_glu, gm_ln_g, gm_ln_b, gm_w_s, gm_b_s, group_norm_g, w_out, ffn_norm_g, dense_w_gate, dense_w_up, dense_w_down, router_w, moe_w_gate, moe_w_up, moe_w_down, final_norm_g):
    bsz, seq, _ = x.shape
    n = bsz * seq
    depth = w_in.shape[0]
    assert seq % CHUNK == 0 and seq % SSM_T == 0
    tm = _largest_tile(seq, 512, CHUNK)
    tq = _largest_tile(seq, 1024, 2 * LANES)
    nc = n // SSM_T
    ncb = seq // SSM_T
    tr = _largest_tile(nc, 512, 8)
    tc = _largest_tile(ncb, 128, 8)

    x2 = x.reshape(n, D_MODEL).astype(F32)
    tril = jnp.tril(jnp.ones((CHUNK, CHUNK), F32))
    i0, i1, i2, i3, i4 = (D_SSM, D_SSM + D_ATT, D_SSM + 2 * D_ATT, D_SSM + 3 * D_ATT,
                          D_SSM + 3 * D_ATT + ATT_HEADS)
    for layer in range(depth):
        wl = w_in[layer]
        w_re = jnp.concatenate(
            [wl[:, :i0], wl[:, i0:i1] * (ATT_HEAD_DIM ** -0.5 * LOG2E), wl[:, i1:i2], wl[:, i4:],
             _pad_lanes(wl[:, i3:i4])], axis=1).astype(BF16)
        ws = (gm_w_s[layer].astype(F32) * tril[None]).astype(BF16)
        bs = jnp.repeat(gm_b_s[layer].astype(F32).T, GM_HEAD_DIM, axis=1)
        u, q, k, v, cum, ygm = _inproj(
            x2, mix_norm_g[layer].reshape(1, D_MODEL), w_re, wl[:, i2:i3].T.astype(BF16),
            _pad_lanes(b_forget[layer].reshape(1, ATT_HEADS).astype(F32)),
            gm_ln_g[layer].reshape(1, D_GM), gm_ln_b[layer].reshape(1, D_GM), ws, bs,
            seq=seq, tm=tm)

        mmat, fmat, emat, a_row = _ssm_matrices(
            ssm_lambda_re[layer], ssm_lambda_im[layer], ssm_log_dt[layer], ssm_b_re[layer],
            ssm_b_im[layer], ssm_c_re[layer], ssm_c_im[layer])
        s_loc = _ssm_state(u, fmat, tr=tr)
        xprev = _ssm_scan(s_loc.reshape(bsz, ncb, 2 * SSM_NSTATE), a_row, tc=tc)
        y_ssm = _ssm_out(u, xprev.reshape(nc, 2 * SSM_NSTATE), mmat, emat, tr=tr)

        cum_rows = cum[:, :ATT_HEADS].reshape(bsz, seq, ATT_HEADS // ATT_STEP_HEADS, ATT_STEP_HEADS)
        cum_rows = cum_rows.transpose(0, 2, 3, 1)
        y_att = _attention(q.reshape(bsz, seq, ATT_SLOTS), k.reshape(bsz, seq, ATT_SLOTS), v,
                           cum_rows, tq=tq).reshape(n, D_ATT)

        is_moe = layer % 2 == 1
        j = layer // 2
        rw = (_pad_lanes(jnp.concatenate(_split3(router_w[j].astype(F32)), axis=1))
              if is_moe else None)
        outs = _outproj(x2, y_ssm, u, y_att, ygm, ssm_d[layer].reshape(1, D_SSM).astype(F32),
                        ssm_w_glu[layer].astype(BF16), group_norm_g[layer].reshape(1, D_MODEL),
                        w_out[layer].astype(BF16), ffn_norm_g[layer].reshape(1, D_MODEL), rw, tm=tm)
        fin = final_norm_g.reshape(1, D_MODEL) if layer == depth - 1 else None
        if is_moe:
            x_mid, h, route, counts = outs
            dffe = moe_w_gate.shape[-1]
            tme = min(MOE_ROW_TILE, 2 * n)
            dest, gaps, tile_expert, n_used, rows = _moe_layout(route, counts, tm=tm, tme=tme)
            xs = _moe_dispatch(dest, gaps, h, rows=rows, tm=tm, tme=tme)
            ys = _moe_ffn(tile_expert, n_used, xs, moe_w_gate[j].astype(BF16),
                          moe_w_up[j].astype(BF16), moe_w_down[j].astype(BF16), tme=tme,
                          tf=_mxu_tile(dffe, 3584))
            x2 = _moe_combine(dest, x_mid, route, ys, fin, tm=tm)
        else:
            x_mid, h = outs
            dff = dense_w_gate.shape[-1]
            x2 = _dense_ffn(h, x_mid, dense_w_gate[j].astype(BF16), dense_w_up[j].astype(BF16),
                            dense_w_down[j].astype(BF16), fin, tm=tm,
                            tf=_mxu_tile(dff, 2816))
    return x2.reshape(bsz, seq, D_MODEL).astype(x.dtype)
```

```python
import functools
import math

import jax
import jax.numpy as jnp
import numpy as np
from jax import lax
from jax.experimental import pallas as pl
from jax.experimental.pallas import tpu as pltpu

F32 = jnp.float32
BF16 = jnp.bfloat16

D_MODEL = 1024
D_SSM = 256
D_ATT = 512
D_GM = 256
SSM_GROUP = 16
SSM_GROUPS = 16
SSM_STATE = 64
ATT_HEADS = 8
ATT_HEAD_DIM = 64
GM_HEADS = 4
GM_HEAD_DIM = 64
CHUNK = 128
N_EXPERTS = 8
EPS = 1e-6
LOG2E = math.log2(math.e)

LANES = 128
SUBLANES = 8
MXU_WIDTH = 256
V7X_VMEM_BYTES = 64 * 1024 * 1024
ATT_SLOTS = ATT_HEADS * LANES
MOE_ROW_TILE = 512
MOE_TOKEN_TILE = 1024
OUTPROJ_PARTS = 2
ATT_STEP_HEADS = 2
SSM_T = 8
SSM_ROW = SSM_T * D_SSM
SSM_NSTATE = SSM_GROUPS * SSM_STATE

C_U = 0
C_Q = C_U + D_SSM
C_K = C_Q + D_ATT
C_Z = C_K + D_ATT
C_F = C_Z + 2 * D_GM
D_IN_PAD = C_F + LANES

VMEM_LIMIT = V7X_VMEM_BYTES * 7 // 8


def _cparams(*sem):
    return pltpu.CompilerParams(dimension_semantics=sem, vmem_limit_bytes=VMEM_LIMIT)


def _resident(shape):
    nd = len(shape)
    return pl.BlockSpec(shape, lambda *_: (0,) * nd, pipeline_mode=pl.Buffered(1))


def _rms(x, g):
    return x * lax.rsqrt(jnp.mean(x * x, axis=-1, keepdims=True) + EPS) * g


def _gelu(x):
    c = math.sqrt(2.0 / math.pi)
    return 0.5 * x * (1.0 + jnp.tanh(c * (x + 0.044715 * (x * x * x))))


def _sigmoid(x):
    return 1.0 / (1.0 + jnp.exp(-x))


def _split3(x):
    p1 = x.astype(BF16)
    r1 = x - p1.astype(F32)
    p2 = r1.astype(BF16)
    r2 = r1 - p2.astype(F32)
    return p1, p2, r2.astype(BF16)


def _dot(a, b):
    return jnp.dot(a, b, preferred_element_type=F32)


def _split_spec(rows):
    return pl.BlockSpec((D_SSM // LANES, rows, LANES), lambda i: (0, i, 0))


def _split_store(ref, value, rows=slice(None)):
    for h in range(D_SSM // LANES):
        ref[h, rows, :] = value[:, h * LANES:(h + 1) * LANES]


def _split_load(ref, rows=slice(None)):
    return jnp.concatenate([ref[h, rows, :] for h in range(D_SSM // LANES)], axis=1)


def _bias_lane(head):
    return 0 if head % 2 else ATT_HEAD_DIM


def _inproj_body(x_ref, g_ref, w_ref, wvt_ref, bf_ref, lng_ref, lnb_ref, ws_ref, bs_ref, place_ref, qc_ref,
                 u_ref, q_ref, k_ref, v_ref, cum_ref, ygm_ref, carry_ref, *, tiles_per_seq):
    i = pl.program_id(0)

    @pl.when(i % tiles_per_seq == 0)
    def _():
        carry_ref[...] = jnp.zeros_like(carry_ref)

    tm = x_ref.shape[0]
    hb = _rms(x_ref[...], g_ref[...]).astype(BF16)

    def proj(lo, width):
        return _dot(hb, w_ref[:, lo:lo + width])

    _split_store(u_ref, proj(C_U, D_SSM))
    v_ref[...] = lax.dot_general(wvt_ref[...], hb, (((1,), (1,)), ((), ())),
                                 preferred_element_type=F32).astype(BF16)

    f = proj(C_F, LANES) + bf_ref[...]
    logf = (jnp.minimum(f, 0.0) - jnp.log(1.0 + jnp.exp(-jnp.abs(f)))) * LOG2E
    row = lax.broadcasted_iota(jnp.int32, (tm, tm), 0)
    col = lax.broadcasted_iota(jnp.int32, (tm, tm), 1)
    tri = (col <= row).astype(BF16)
    sums = _dot(tri, jnp.concatenate(_split3(logf), axis=1))
    cum = sums[:, :LANES] + sums[:, LANES:2 * LANES] + sums[:, 2 * LANES:] + carry_ref[...]
    cum_ref[...] = cum
    carry_ref[...] = cum[tm - 1:tm, :]

    lane = lax.broadcasted_iota(jnp.int32, (tm, LANES), 1)
    c1, c2, c3 = [jnp.where(lane < ATT_HEADS, piece.astype(F32), 0.0) for piece in _split3(cum)]
    packed = c1 + pltpu.roll(c2, ATT_HEADS, axis=1) + pltpu.roll(c3, 2 * ATT_HEADS, axis=1)
    placed = _dot(packed.astype(BF16), place_ref[...])
    upper = lax.broadcasted_iota(jnp.int32, (tm, LANES), 1) >= ATT_HEAD_DIM
    qf = proj(C_Q, D_ATT)
    kf = proj(C_K, D_ATT)
    for h in range(ATT_HEADS):
        pair = slice((h // 2) * LANES, (h // 2 + 1) * LANES)
        slot = slice(h * LANES, (h + 1) * LANES)
        own = upper if h % 2 else ~upper
        q_ref[:, slot] = jnp.where(own, qf[:, pair], qc_ref[:, slot]).astype(BF16)
        k_ref[:, slot] = jnp.where(own, kf[:, pair], placed[:, slot]).astype(BF16)

    zg = _gelu(proj(C_Z, 2 * D_GM))
    ug = zg[:, :D_GM]
    vg = zg[:, D_GM:]
    mean = jnp.mean(vg, axis=-1, keepdims=True)
    cen = vg - mean
    var = jnp.mean(cen * cen, axis=-1, keepdims=True)
    vn = (cen * lax.rsqrt(var + EPS) * lng_ref[...] + lnb_ref[...]).astype(BF16)
    lane_head = lax.broadcasted_iota(jnp.int32, (CHUNK, D_GM), 1) >> 6
    for c in range(tm // CHUNK):
        rows = slice(c * CHUNK, (c + 1) * CHUNK)
        vc = vn[rows, :]
        mixed = _dot(ws_ref[0], vc)
        for g in range(1, GM_HEADS):
            mixed = jnp.where(lane_head == g, _dot(ws_ref[g], vc), mixed)
        mixed = mixed + bs_ref[...]
        ygm_ref[rows, :] = (ug[rows, :] * mixed).astype(BF16)


def _attn_slot_constants():
    place = np.zeros((LANES, ATT_SLOTS), np.float32)
    qconst = np.zeros((1, ATT_SLOTS), np.float32)
    for h in range(ATT_HEADS):
        for piece in range(3):
            lane = h * LANES + _bias_lane(h) + piece
            place[piece * ATT_HEADS + h, lane] = 1.0
            qconst[0, lane] = -1.0
    return jnp.asarray(place, BF16), jnp.asarray(qconst)


def _inproj(x2, g, w, wvt, bf, lng, lnb, ws, bs, *, seq, tm):
    n = x2.shape[0]
    tps = seq // tm
    place, qconst = _attn_slot_constants()
    row = lambda width: pl.BlockSpec((tm, width), lambda i: (i, 0))
    return pl.pallas_call(
        functools.partial(_inproj_body, tiles_per_seq=tps),
        grid=(n // tm,),
        in_specs=[row(D_MODEL), _resident((1, D_MODEL)), _resident((D_MODEL, D_IN_PAD)),
                  _resident((D_ATT, D_MODEL)),
                  _resident((1, LANES)), _resident((1, D_GM)), _resident((1, D_GM)),
                  _resident((GM_HEADS, CHUNK, CHUNK)), _resident((CHUNK, D_GM)),
                  _resident((LANES, ATT_SLOTS)), _resident((1, ATT_SLOTS))],
        out_specs=[_split_spec(tm), row(ATT_SLOTS), row(ATT_SLOTS),
                   pl.BlockSpec((None, D_ATT, tm), lambda i: (i // tps, 0, i % tps)),
                   row(LANES), row(D_GM)],
        out_shape=[jax.ShapeDtypeStruct((D_SSM // LANES, n, LANES), F32),
                   jax.ShapeDtypeStruct((n, ATT_SLOTS), BF16),
                   jax.ShapeDtypeStruct((n, ATT_SLOTS), BF16),
                   jax.ShapeDtypeStruct((n // seq, D_ATT, seq), BF16),
                   jax.ShapeDtypeStruct((n, LANES), F32),
                   jax.ShapeDtypeStruct((n, D_GM), BF16)],
        scratch_shapes=[pltpu.VMEM((1, LANES), F32)],
        compiler_params=_cparams("arbitrary"),
        name="inproj",
    )(x2, g, w, wvt, bf, lng, lnb, ws, bs, place, qconst)


def _ssm_tokens(u_ref, s, tr):
    return _split_load(u_ref, pl.ds(s, tr, stride=SSM_T)).astype(BF16)


def _ssm_state_body(u_ref, f_ref, s_ref):
    tr = s_ref.shape[0]
    acc = _dot(_ssm_tokens(u_ref, 0, tr), f_ref[:D_SSM, :])
    for s in range(1, SSM_T):
        acc += _dot(_ssm_tokens(u_ref, s, tr), f_ref[s * D_SSM:(s + 1) * D_SSM, :])
    s_ref[...] = acc


def _ssm_state(u, fmat, *, tr):
    nc = u.shape[1] // SSM_T
    return pl.pallas_call(
        _ssm_state_body,
        grid=(nc // tr,),
        in_specs=[_split_spec(tr * SSM_T), _resident((SSM_ROW, 2 * SSM_NSTATE))],
        out_specs=pl.BlockSpec((tr, 2 * SSM_NSTATE), lambda i: (i, 0)),
        out_shape=jax.ShapeDtypeStruct((nc, 2 * SSM_NSTATE), F32),
        compiler_params=_cparams("parallel"),
        name="ssm_state",
    )(u, fmat)


def _ssm_scan_body(s_ref, a_ref, xp_ref, st_ref):
    @pl.when(pl.program_id(1) == 0)
    def _():
        st_ref[...] = jnp.zeros_like(st_ref)

    tc = s_ref.shape[0]
    are = a_ref[:, :SSM_NSTATE]
    aim = a_ref[:, SSM_NSTATE:]

    def step(c8, carry):
        xre, xim = carry
        base = pl.multiple_of(c8 * SUBLANES, SUBLANES)
        blk = s_ref[pl.ds(base, SUBLANES), :]
        prev_re, prev_im = [], []
        for r in range(SUBLANES):
            prev_re.append(xre)
            prev_im.append(xim)
            sre = blk[r:r + 1, :SSM_NSTATE]
            sim = blk[r:r + 1, SSM_NSTATE:]
            xre, xim = are * xre - aim * xim + sre, are * xim + aim * xre + sim
        xp_ref[pl.ds(base, SUBLANES), :SSM_NSTATE] = jnp.concatenate(prev_re, axis=0)
        xp_ref[pl.ds(base, SUBLANES), SSM_NSTATE:] = jnp.concatenate(prev_im, axis=0)
        return xre, xim

    xre, xim = lax.fori_loop(0, tc // SUBLANES, step,
                             (st_ref[:, :SSM_NSTATE], st_ref[:, SSM_NSTATE:]))
    st_ref[:, :SSM_NSTATE] = xre
    st_ref[:, SSM_NSTATE:] = xim


def _ssm_scan(s3, a, *, tc):
    b, ncb, w = s3.shape
    spec = pl.BlockSpec((None, tc, w), lambda bi, ci: (bi, ci, 0))
    return pl.pallas_call(
        _ssm_scan_body,
        grid=(b, ncb // tc),
        in_specs=[spec, pl.BlockSpec((1, w), lambda bi, ci: (0, 0))],
        out_specs=spec,
        out_shape=jax.ShapeDtypeStruct(s3.shape, F32),
        scratch_shapes=[pltpu.VMEM((1, w), F32)],
        compiler_params=_cparams("parallel", "arbitrary"),
        name="ssm_scan",
    )(s3, a)


def _ssm_out_body(u_ref, xp_ref, m_ref, e_ref, y_ref):
    tr = xp_ref.shape[0]
    us = [_ssm_tokens(u_ref, s, tr) for s in range(SSM_T)]
    xp = xp_ref[...].astype(BF16)
    for t in range(SSM_T):
        cols = slice(t * D_SSM, (t + 1) * D_SSM)
        acc = _dot(xp, e_ref[:, cols])
        for s in range(t + 1):
            acc += _dot(us[s], m_ref[s * D_SSM:(s + 1) * D_SSM, cols])
        _split_store(y_ref, acc, pl.ds(t, tr, stride=SSM_T))


def _ssm_out(u, xp, mmat, emat, *, tr):
    nc = xp.shape[0]
    return pl.pallas_call(
        _ssm_out_body,
        grid=(nc // tr,),
        in_specs=[_split_spec(tr * SSM_T),
                  pl.BlockSpec((tr, 2 * SSM_NSTATE), lambda i: (i, 0)),
                  _resident((SSM_ROW, SSM_ROW)), _resident((2 * SSM_NSTATE, SSM_ROW))],
        out_specs=_split_spec(tr * SSM_T),
        out_shape=jax.ShapeDtypeStruct((D_SSM // LANES, nc * SSM_T, LANES), F32),
        compiler_params=_cparams("parallel"),
        name="ssm_out",
    )(u, xp, mmat, emat)


def _ssm_matrices(lam_re, lam_im, log_dt, b_re, b_im, c_re, c_im):
    t, g, p, h = SSM_T, SSM_GROUPS, SSM_STATE, SSM_GROUP
    ns, nch = g * p, g * h
    hp = lax.Precision.HIGHEST
    lr, li = lam_re.astype(F32).reshape(ns), lam_im.astype(F32).reshape(ns)
    dt = jnp.repeat(jnp.exp(log_dt.astype(F32)), p)
    steps = jnp.arange(t + 1, dtype=F32)[:, None]
    mag = jnp.exp((lr * dt)[None] * steps)
    ang = (li * dt)[None] * steps
    pr, pi = mag * jnp.cos(ang), mag * jnp.sin(ang)
    nr, ni = pr[1] - 1.0, pi[1]
    den = lr * lr + li * li
    zr, zi = ((nr * lr + ni * li) / den)[:, None], ((ni * lr - nr * li) / den)[:, None]
    br, bi = b_re.astype(F32).reshape(ns, h), b_im.astype(F32).reshape(ns, h)
    cr = c_re.astype(F32).transpose(0, 2, 1).reshape(ns, h)
    ci = c_im.astype(F32).transpose(0, 2, 1).reshape(ns, h)
    same = jnp.asarray(np.repeat(np.repeat(np.eye(g, dtype=np.float32), p, axis=0), h, axis=1))
    spread = lambda w: jnp.tile(w, (1, g)) * same
    bhr, bhi = spread(zr * br - zi * bi), spread(zr * bi + zi * br)
    chr_, chi = spread(cr), spread(ci)
    prc, pic = pr.T[:, :, None], pi.T[:, :, None]
    er = (prc * chr_[:, None, :] - pic * chi[:, None, :]).reshape(ns, (t + 1) * nch)
    ei = (prc * chi[:, None, :] + pic * chr_[:, None, :]).reshape(ns, (t + 1) * nch)

    k_all = (jnp.dot(bhr.T, er[:, :t * nch], precision=hp)
             - jnp.dot(bhi.T, ei[:, :t * nch], precision=hp))
    mmat = jnp.concatenate(
        [jnp.pad(k_all[:, :(t - s) * nch], ((0, 0), (s * nch, 0))) for s in range(t)], axis=0)

    prr, pir = pr[:t][::-1][:, None, :], pi[:t][::-1][:, None, :]
    bhrt, bhit = bhr.T[None], bhi.T[None]
    fmat = jnp.concatenate([(bhrt * prr - bhit * pir).reshape(t * nch, ns),
                            (bhrt * pir + bhit * prr).reshape(t * nch, ns)], axis=1)

    emat = jnp.concatenate([er[:, nch:], -ei[:, nch:]], axis=0)

    a_row = jnp.concatenate([pr[t][None], pi[t][None]], axis=1)
    return mmat.astype(BF16), fmat.astype(BF16), emat.astype(BF16), a_row


def _attn_body(q_ref, k_ref, v_ref, cq_ref, o_ref,
               m_ref, acc_ref, s_ref, mt_ref, p_ref, al_ref, *, tq):
    qi = pl.program_id(2)
    nh = ATT_STEP_HEADS
    tk = tq // 2
    every = slice(0, tq)
    late = slice(tk, tq)

    def slot_lanes(j):
        return slice(j * LANES, (j + 1) * LANES)

    def scores(t, slot, qs=every, masked=False):
        start = pl.multiple_of(t * tk, tk)
        width = qs.stop - qs.start
        for j in range(nh):
            kt = k_ref[pl.ds(start, tk), slot_lanes(j)]
            s = lax.dot_general(kt, q_ref[qs, slot_lanes(j)], (((1,), (1,)), ((), ())),
                                preferred_element_type=F32)
            if masked:
                keypos = lax.broadcasted_iota(jnp.int32, (tk, width), 0)
                qpos = lax.broadcasted_iota(jnp.int32, (tk, width), 1)
                s = jnp.where(keypos <= qpos, s, -jnp.inf)
            s_ref[slot, j, :, qs] = s
            mt_ref[slot, j, :, qs] = jnp.max(s, axis=0, keepdims=True)

    def probs(slot, qs=every):
        for j in range(nh):
            m = m_ref[j, :, qs]
            cq = cq_ref[j:j + 1, qs]
            m_new = jnp.maximum(m, mt_ref[slot, j, :, qs] + cq)
            p_ref[slot, j, :, qs] = jnp.exp2(s_ref[slot, j, :, qs] + (cq - m_new)).astype(BF16)
            al_ref[slot, j, :, qs] = jnp.exp2(m - m_new)
            m_ref[j, :, qs] = m_new

    def values(t, slot, qs=every):
        start = pl.multiple_of(t * tk, tk)
        for j in range(nh):
            vt = v_ref[j * ATT_HEAD_DIM:(j + 1) * ATT_HEAD_DIM, pl.ds(start, tk)]
            vt = jnp.concatenate([vt, jnp.ones_like(vt)], axis=0)
            acc_ref[j, :, qs] = (al_ref[slot, j, :, qs] * acc_ref[j, :, qs]
                                 + _dot(vt, p_ref[slot, j, :, qs]))

    def trip(t, slot, **next_tile):
        values(t - 1, 1 - slot)
        probs(slot)
        scores(t + 1, 1 - slot, **next_tile)

    for j in range(nh):
        m_ref[j] = jnp.full((1, tq), -jnp.inf, F32)
        acc_ref[j] = jnp.zeros((LANES, tq), F32)

    last = 2 * qi + 1

    @pl.when(qi == 0)
    def _():
        scores(0, 0, masked=True)
        probs(0)
        scores(1, 1, late, masked=True)

    @pl.when(qi > 0)
    def _():
        scores(0, 0)
        probs(0)
        scores(1, 1)

        def pair(u, carry):
            t = 2 * u + 1
            trip(t, 1)
            trip(t + 1, 0)
            return carry

        lax.fori_loop(0, qi - 1, pair, 0)
        trip(last - 2, 1, masked=True)
        trip(last - 1, 0, qs=late, masked=True)

    values(last - 1, 0)
    probs(1, late)
    values(last, 1, late)
    for pair in range(nh // 2):
        outs = []
        for j in (2 * pair, 2 * pair + 1):
            acc = acc_ref[j]
            outs.append(acc[:ATT_HEAD_DIM, :] / acc[ATT_HEAD_DIM:, :])
        o_ref[:, slot_lanes(pair)] = jnp.concatenate(outs, axis=0).T.astype(BF16)


def _attention(q3, k3, vt4, cum3, *, tq):
    b, s, _ = q3.shape
    nh = ATT_STEP_HEADS
    return pl.pallas_call(
        functools.partial(_attn_body, tq=tq),
        scratch_shapes=[pltpu.VMEM((nh, 1, tq), F32), pltpu.VMEM((nh, LANES, tq), F32),
                        pltpu.VMEM((2, nh, tq // 2, tq), F32), pltpu.VMEM((2, nh, 1, tq), F32),
                        pltpu.VMEM((2, nh, tq // 2, tq), BF16), pltpu.VMEM((2, nh, 1, tq), F32)],
        grid=(b, ATT_HEADS // nh, s // tq),
        in_specs=[pl.BlockSpec((None, tq, nh * LANES), lambda bi, hg, qi: (bi, qi, hg)),
                  pl.BlockSpec((None, s, nh * LANES), lambda bi, hg, qi: (bi, 0, hg)),
                  pl.BlockSpec((None, nh * ATT_HEAD_DIM, s), lambda bi, hg, qi: (bi, hg, 0)),
                  pl.BlockSpec((None, None, nh, tq), lambda bi, hg, qi: (bi, hg, 0, qi))],
        out_specs=pl.BlockSpec((None, tq, nh * ATT_HEAD_DIM), lambda bi, hg, qi: (bi, qi, hg)),
        out_shape=jax.ShapeDtypeStruct((b, s, D_ATT), BF16),
        compiler_params=_cparams("parallel", "parallel", "arbitrary"),
        name="attention",
    )(q3, k3, vt4, cum3)


R_E1, R_E2, R_W1, R_W2, R_RANK1, R_RANK2 = range(6)


def _route(logits, cnt_ref):
    tm = logits.shape[0]
    lane = lax.broadcasted_iota(jnp.int32, logits.shape, 1).astype(F32)
    lg = jnp.where(lane < N_EXPERTS, logits, -jnp.inf)
    m1 = jnp.max(lg, axis=-1, keepdims=True)
    i1 = jnp.min(jnp.where(lg == m1, lane, float(LANES)), axis=-1, keepdims=True)
    lg2 = jnp.where(lane == i1, -jnp.inf, lg)
    m2 = jnp.max(lg2, axis=-1, keepdims=True)
    i2 = jnp.min(jnp.where(lg2 == m2, lane, float(LANES)), axis=-1, keepdims=True)
    e = jnp.exp(m2 - m1)
    w1 = 1.0 / (1.0 + e)
    w2 = e / (1.0 + e)

    hit1, hit2 = lane == i1, lane == i2
    oh1, oh2 = hit1.astype(BF16), hit2.astype(BF16)
    row = lax.broadcasted_iota(jnp.int32, (tm, tm), 0)
    col = lax.broadcasted_iota(jnp.int32, (tm, tm), 1)
    before = (col < row).astype(BF16)
    carry = cnt_ref[...]
    tot1 = jnp.sum(oh1.astype(F32), axis=0, keepdims=True)
    tot2 = jnp.sum(oh2.astype(F32), axis=0, keepdims=True)
    pos1 = _dot(before, oh1) + carry
    pos2 = _dot(before, oh2) + (carry + tot1)
    rank1 = jnp.sum(jnp.where(hit1, pos1, 0.0), axis=-1, keepdims=True)
    rank2 = jnp.sum(jnp.where(hit2, pos2, 0.0), axis=-1, keepdims=True)
    cnt_ref[...] = carry + tot1 + tot2

    rec = jnp.zeros(logits.shape, F32)
    for idx, val in ((R_E1, i1), (R_E2, i2), (R_W1, w1), (R_W2, w2), (R_RANK1, rank1), (R_RANK2, rank2)):
        rec = jnp.where(lane == idx, val, rec)
    return rec


def _outproj_body(*refs, with_router):
    if with_router:
        (x_ref, ys_ref, us_ref, ya_ref, yg_ref, d_ref, wglu_ref, gn_ref, w_ref, fg_ref, rw_ref,
         xo_ref, h_ref, route_ref, cnt_ref) = refs

        @pl.when(pl.program_id(0) == 0)
        def _():
            cnt_ref[...] = jnp.zeros_like(cnt_ref)
    else:
        (x_ref, ys_ref, us_ref, ya_ref, yg_ref, d_ref, wglu_ref, gn_ref, w_ref, fg_ref,
         xo_ref, h_ref) = refs

    tm = x_ref.shape[0]
    part = tm // OUTPROJ_PARTS
    o1, o2 = D_SSM, D_SSM + D_ATT
    for k in range(OUTPROJ_PARTS):
        rows = slice(k * part, (k + 1) * part)
        y1 = _gelu(_split_load(ys_ref, rows) + d_ref[...] * _split_load(us_ref, rows))
        y_ssm = y1 * _sigmoid(_dot(y1.astype(BF16), wglu_ref[...]))

        acc = _dot(_rms(y_ssm, gn_ref[:, :o1]).astype(BF16), w_ref[:o1, :])
        acc += _dot(_rms(ya_ref[rows, :].astype(F32), gn_ref[:, o1:o2]).astype(BF16), w_ref[o1:o2, :])
        acc += _dot(_rms(yg_ref[rows, :].astype(F32), gn_ref[:, o2:]).astype(BF16), w_ref[o2:, :])
        xn = x_ref[rows, :] + acc
        xo_ref[rows, :] = xn
        hn = _rms(xn, fg_ref[...])
        h_ref[rows, :] = hn.astype(h_ref.dtype)
        if with_router:
            rw = rw_ref[...]
            prod = sum(_dot(piece, rw) for piece in _split3(hn))
            logits = (prod + pltpu.roll(prod, LANES - N_EXPERTS, axis=1)
                      + pltpu.roll(prod, LANES - 2 * N_EXPERTS, axis=1))
            route_ref[rows, :] = _route(logits, cnt_ref)


def _outproj(x2, ys, us, ya, yg, d, wglu, gn, w, fg, rw, *, tm):
    n = x2.shape[0]
    with_router = rw is not None
    row = lambda width: pl.BlockSpec((tm, width), lambda i: (i, 0))
    in_specs = [row(D_MODEL), _split_spec(tm), _split_spec(tm), row(D_ATT), row(D_GM),
                _resident((1, D_SSM)), _resident((D_SSM, D_SSM)), _resident((1, D_MODEL)),
                _resident((D_MODEL, D_MODEL)), _resident((1, D_MODEL))]
    out_specs = [row(D_MODEL), row(D_MODEL)]
    out_shape = [jax.ShapeDtypeStruct((n, D_MODEL), F32),
                 jax.ShapeDtypeStruct((n, D_MODEL), F32 if with_router else BF16)]
    args = [x2, ys, us, ya, yg, d, wglu, gn, w, fg]
    if with_router:
        in_specs.append(_resident((D_MODEL, LANES)))
        out_specs += [row(LANES), pl.BlockSpec((1, LANES), lambda i: (0, 0))]
        out_shape += [jax.ShapeDtypeStruct((n, LANES), F32), jax.ShapeDtypeStruct((1, LANES), F32)]
        args.append(rw)
    return pl.pallas_call(
        functools.partial(_outproj_body, with_router=with_router),
        grid=(n // tm,),
        in_specs=in_specs, out_specs=out_specs, out_shape=out_shape,
        compiler_params=_cparams("arbitrary" if with_router else "parallel"),
        name="outproj_router" if with_router else "outproj",
    )(*args)


def _swiglu_tile(h, wg, wu, wd):
    a = _dot(h, wg)
    return _dot((a * _sigmoid(a) * _dot(h, wu)).astype(BF16), wd)


def _finish(x, acc, fin_ref, o_ref):
    xn = x + acc
    o_ref[...] = xn if fin_ref is None else _rms(xn, fin_ref[...])


def _dense_ffn_body(*refs, final_norm):
    if final_norm:
        h_ref, x_ref, wg_ref, wu_ref, wd_ref, fin_ref, o_ref, acc_ref = refs
    else:
        h_ref, x_ref, wg_ref, wu_ref, wd_ref, o_ref, acc_ref = refs
        fin_ref = None
    f = pl.program_id(1)

    @pl.when(f == 0)
    def _():
        acc_ref[...] = jnp.zeros_like(acc_ref)

    acc_ref[...] += _swiglu_tile(h_ref[...], wg_ref[...], wu_ref[...], wd_ref[...])

    @pl.when(f == pl.num_programs(1) - 1)
    def _():
        _finish(x_ref[...], acc_ref[...], fin_ref, o_ref)


def _dense_ffn(h, x2, wg, wu, wd, fin, *, tm, tf):
    n = x2.shape[0]
    dff = wg.shape[1]
    row = lambda: pl.BlockSpec((tm, D_MODEL), lambda i, f: (i, 0))
    mode = dict(pipeline_mode=pl.Buffered(1)) if tf == dff else {}
    in_specs = [row(), row(),
                pl.BlockSpec((D_MODEL, tf), lambda i, f: (0, f), **mode),
                pl.BlockSpec((D_MODEL, tf), lambda i, f: (0, f), **mode),
                pl.BlockSpec((tf, D_MODEL), lambda i, f: (f, 0), **mode)]
    args = [h, x2, wg, wu, wd]
    if fin is not None:
        in_specs.append(pl.BlockSpec((1, D_MODEL), lambda i, f: (0, 0)))
        args.append(fin)
    return pl.pallas_call(
        functools.partial(_dense_ffn_body, final_norm=fin is not None),
        grid=(n // tm, dff // tf),
        in_specs=in_specs, out_specs=row(),
        out_shape=jax.ShapeDtypeStruct((n, D_MODEL), F32),
        scratch_shapes=[pltpu.VMEM((tm, D_MODEL), F32)],
        compiler_params=_cparams("parallel", "arbitrary"),
        name="dense_ffn",
    )(*args)


def _row_copy(src, src_row, dst, dst_row, sem):
    return pltpu.make_async_copy(src.at[pl.ds(src_row, 1)], dst.at[pl.ds(dst_row, 1)], sem)


def _moe_dispatch_body(dest_ref, gaps_ref, h_ref, xs_ref, zero_ref, sem, zsem, *, tme):
    tm = h_ref.shape[0]

    @pl.when(pl.program_id(0) == 0)
    def _():
        zero_ref[...] = jnp.zeros_like(zero_ref)

        def zero_copy(gap, i):
            if gap == N_EXPERTS:
                return pltpu.make_async_copy(zero_ref, xs_ref.at[pl.ds(i * tme, tme)], zsem)
            return _row_copy(zero_ref, 0, xs_ref, i, zsem)

        for gap in range(N_EXPERTS + 1):
            lo, hi = gaps_ref[0, gap], gaps_ref[1, gap]

            def start(i, carry, gap=gap):
                zero_copy(gap, i).start()
                return carry

            def wait(i, carry, gap=gap):
                zero_copy(gap, i).wait()
                return carry

            lax.fori_loop(lo, hi, start, 0)
            lax.fori_loop(lo, hi, wait, 0)

    def issue(r, carry):
        for k in range(2):
            _row_copy(h_ref, r, xs_ref, dest_ref[k, r], sem).start()
        return carry

    lax.fori_loop(0, tm, issue, 0, unroll=8)
    for k in range(2):
        pltpu.make_async_copy(h_ref, xs_ref.at[pl.ds(0, tm)], sem).wait()


def _moe_dispatch(dest, gaps, h, *, rows, tm, tme):
    n = h.shape[0]
    return pl.pallas_call(
        functools.partial(_moe_dispatch_body, tme=tme),
        grid=(n // tm,),
        in_specs=[pl.BlockSpec((None, 2, tm), lambda i: (i, 0, 0), memory_space=pltpu.SMEM),
                  pl.BlockSpec(memory_space=pltpu.SMEM),
                  pl.BlockSpec((tm, D_MODEL), lambda i: (i, 0))],
        out_specs=pl.BlockSpec(memory_space=pl.ANY),
        out_shape=jax.ShapeDtypeStruct((rows, D_MODEL), F32),
        scratch_shapes=[pltpu.VMEM((tme, D_MODEL), F32), pltpu.SemaphoreType.DMA(()),
                        pltpu.SemaphoreType.DMA(())],
        compiler_params=pltpu.CompilerParams(dimension_semantics=("arbitrary",),
                                             vmem_limit_bytes=VMEM_LIMIT, has_side_effects=True),
        name="moe_dispatch",
    )(dest, gaps, h)


def _moe_ffn_body(te_ref, nu_ref, xs_ref, wg_ref, wu_ref, wd_ref, ys_ref, acc_ref):
    del te_ref
    i = pl.program_id(0)
    f = pl.program_id(1)

    @pl.when(i < nu_ref[0])
    def _():
        @pl.when(f == 0)
        def _():
            acc_ref[...] = jnp.zeros_like(acc_ref)

        acc_ref[...] += _swiglu_tile(xs_ref[...].astype(BF16), wg_ref[...], wu_ref[...], wd_ref[...])

        @pl.when(f == pl.num_programs(1) - 1)
        def _():
            ys_ref[...] = acc_ref[...]

    @pl.when((i >= nu_ref[0]) & (f == pl.num_programs(1) - 1))
    def _():
        ys_ref[...] = jnp.zeros_like(ys_ref)


def _moe_ffn(tile_expert, n_used, xs, wg, wu, wd, *, tme, tf):
    rows = xs.shape[0]
    dff = wg.shape[-1]
    nf = dff // tf
    row_idx = lambda i, f, te, nu: (jnp.minimum(i, nu[0] - 1), 0)
    col = lambda i, f, nu: jnp.where(i < nu[0], f, nf - 1)
    mode = dict(pipeline_mode=pl.Buffered(1)) if nf == 1 else {}
    grid_spec = pltpu.PrefetchScalarGridSpec(
        num_scalar_prefetch=2,
        grid=(rows // tme, nf),
        in_specs=[pl.BlockSpec((tme, D_MODEL), row_idx),
                  pl.BlockSpec((None, D_MODEL, tf), lambda i, f, te, nu: (te[i], 0, col(i, f, nu)), **mode),
                  pl.BlockSpec((None, D_MODEL, tf), lambda i, f, te, nu: (te[i], 0, col(i, f, nu)), **mode),
                  pl.BlockSpec((None, tf, D_MODEL), lambda i, f, te, nu: (te[i], col(i, f, nu), 0), **mode)],
        out_specs=pl.BlockSpec((tme, D_MODEL), lambda i, f, te, nu: (i, 0)),
        scratch_shapes=[pltpu.VMEM((tme, D_MODEL), F32)])
    return pl.pallas_call(
        _moe_ffn_body,
        grid_spec=grid_spec,
        out_shape=jax.ShapeDtypeStruct((rows, D_MODEL), F32),
        compiler_params=_cparams("arbitrary", "arbitrary"),
        name="moe_ffn",
    )(tile_expert, n_used, xs, wg, wu, wd)


def _moe_combine_body(*refs, final_norm):
    if final_norm:
        dest_ref, x_ref, route_ref, ys_ref, fin_ref, o_ref, buf_ref, sem = refs
    else:
        dest_ref, x_ref, route_ref, ys_ref, o_ref, buf_ref, sem = refs
        fin_ref = None
    tm = x_ref.shape[0]

    def issue(r, carry):
        for k in range(2):
            _row_copy(ys_ref, dest_ref[k, r], buf_ref.at[k], r, sem).start()
        return carry

    lax.fori_loop(0, tm, issue, 0, unroll=8)
    for k in range(2):
        pltpu.make_async_copy(ys_ref.at[pl.ds(0, tm)], buf_ref.at[k], sem).wait()

    route = route_ref[...]
    lane = lax.broadcasted_iota(jnp.int32, route.shape, 1)
    w1 = jnp.sum(jnp.where(lane == R_W1, route, 0.0), axis=-1, keepdims=True)
    w2 = jnp.sum(jnp.where(lane == R_W2, route, 0.0), axis=-1, keepdims=True)
    _finish(x_ref[...], w1 * buf_ref[0] + w2 * buf_ref[1], fin_ref, o_ref)


def _moe_combine(dest, x2, route, ys, fin, *, tm):
    n = x2.shape[0]
    row = lambda width: pl.BlockSpec((tm, width), lambda i: (i, 0))
    in_specs = [pl.BlockSpec((None, 2, tm), lambda i: (i, 0, 0), memory_space=pltpu.SMEM),
                row(D_MODEL), row(LANES), pl.BlockSpec(memory_space=pl.ANY)]
    args = [dest, x2, route, ys]
    if fin is not None:
        in_specs.append(pl.BlockSpec((1, D_MODEL), lambda i: (0, 0)))
        args.append(fin)
    return pl.pallas_call(
        functools.partial(_moe_combine_body, final_norm=fin is not None),
        grid=(n // tm,),
        in_specs=in_specs, out_specs=row(D_MODEL),
        out_shape=jax.ShapeDtypeStruct((n, D_MODEL), F32),
        scratch_shapes=[pltpu.VMEM((2, tm, D_MODEL), F32), pltpu.SemaphoreType.DMA(())],
        compiler_params=_cparams("arbitrary"),
        name="moe_combine",
    )(*args)


def _moe_layout(route, counts, *, tm, tme):
    n = route.shape[0]
    cnt = counts[0, :N_EXPERTS].astype(jnp.int32)
    padded = (cnt + (tme - 1)) // tme * tme
    ends = jnp.cumsum(padded)
    starts = ends - padded
    experts = jnp.arange(N_EXPERTS, dtype=F32)

    def rows(e_lane, rank_lane):
        start = jnp.sum(jnp.where(route[:, e_lane, None] == experts[None], starts[None], 0), axis=1)
        return start + route[:, rank_lane].astype(jnp.int32)

    dest = jnp.stack([rows(R_E1, R_RANK1), rows(R_E2, R_RANK2)])
    dest = dest.reshape(2, n // tm, tm).transpose(1, 0, 2)
    n_tiles = (2 * n) // tme + N_EXPERTS
    tile_start = jnp.arange(n_tiles, dtype=jnp.int32) * tme
    tile_expert = jnp.minimum(jnp.sum(tile_start[:, None] >= ends[None, :], axis=1), N_EXPERTS - 1)
    n_used = (ends[-1] // tme).reshape(1)
    tile_expert = jnp.where(tile_start < ends[-1], tile_expert, tile_expert[jnp.maximum(n_used[0] - 1, 0)])
    gaps = jnp.stack([jnp.concatenate([starts + cnt, n_used]),
                      jnp.concatenate([ends, jnp.full((1,), n_tiles, jnp.int32)])]).astype(jnp.int32)
    return dest, gaps, tile_expert.astype(jnp.int32), n_used.astype(jnp.int32), n_tiles * tme


def _largest_tile(total, cap, mult):
    best = mult
    for t in range(mult, min(total, cap) + 1, mult):
        if total % t == 0:
            best = t
    return best


def _mxu_tile(total, cap):
    if total % MXU_WIDTH == 0:
        return _largest_tile(total, cap, MXU_WIDTH)
    return _largest_tile(total, cap, LANES)


def _pad_lanes(a):
    return jnp.pad(a, [(0, 0)] * (a.ndim - 1) + [(0, LANES - a.shape[-1])])


def kernel(x, mix_norm_g, w_in, b_forget, ssm_lambda_re, ssm_lambda_im, ssm_log_dt, ssm_b_re, ssm_b_im, ssm_c_re, ssm_c_im, ssm_d, ssm_w_glu, gm_ln_g, gm_ln_b, gm_w_s, gm_b_s, group_norm_g, w_out, ffn_norm_g, dense_w_gate, dense_w_up, dense_w_down, router_w, moe_w_gate, moe_w_up, moe_w_down, final_norm_g):
    bsz, seq, _ = x.shape
    n = bsz * seq
    depth = w_in.shape[0]
    assert seq % CHUNK == 0 and seq % SSM_T == 0
    tm = _largest_tile(seq, 512, CHUNK)
    tq = _largest_tile(seq, 1024, 2 * LANES)
    nc = n // SSM_T
    ncb = seq // SSM_T
    tr = _largest_tile(nc, 512, 8)
    tc = _largest_tile(ncb, 128, 8)

    x2 = x.reshape(n, D_MODEL).astype(F32)
    tril = jnp.tril(jnp.ones((CHUNK, CHUNK), F32))
    i0, i1, i2, i3, i4 = (D_SSM, D_SSM + D_ATT, D_SSM + 2 * D_ATT, D_SSM + 3 * D_ATT,
                          D_SSM + 3 * D_ATT + ATT_HEADS)
    for layer in range(depth):
        wl = w_in[layer]
        w_re = jnp.concatenate(
            [wl[:, :i0], wl[:, i0:i1] * (ATT_HEAD_DIM ** -0.5 * LOG2E), wl[:, i1:i2], wl[:, i4:],
             _pad_lanes(wl[:, i3:i4])], axis=1).astype(BF16)
        ws = (gm_w_s[layer].astype(F32) * tril[None]).astype(BF16)
        bs = jnp.repeat(gm_b_s[layer].astype(F32).T, GM_HEAD_DIM, axis=1)
        u, q, k, v, cum, ygm = _inproj(
            x2, mix_norm_g[layer].reshape(1, D_MODEL), w_re, wl[:, i2:i3].T.astype(BF16),
            _pad_lanes(b_forget[layer].reshape(1, ATT_HEADS).astype(F32)),
            gm_ln_g[layer].reshape(1, D_GM), gm_ln_b[layer].reshape(1, D_GM), ws, bs,
            seq=seq, tm=tm)

        mmat, fmat, emat, a_row = _ssm_matrices(
            ssm_lambda_re[layer], ssm_lambda_im[layer], ssm_log_dt[layer], ssm_b_re[layer],
            ssm_b_im[layer], ssm_c_re[layer], ssm_c_im[layer])
        s_loc = _ssm_state(u, fmat, tr=tr)
        xprev = _ssm_scan(s_loc.reshape(bsz, ncb, 2 * SSM_NSTATE), a_row, tc=tc)
        y_ssm = _ssm_out(u, xprev.reshape(nc, 2 * SSM_NSTATE), mmat, emat, tr=tr)

        cum_rows = cum[:, :ATT_HEADS].reshape(bsz, seq, ATT_HEADS // ATT_STEP_HEADS, ATT_STEP_HEADS)
        cum_rows = cum_rows.transpose(0, 2, 3, 1)
        y_att = _attention(q.reshape(bsz, seq, ATT_SLOTS), k.reshape(bsz, seq, ATT_SLOTS), v,
                           cum_rows, tq=tq).reshape(n, D_ATT)

        is_moe = layer % 2 == 1
        j = layer // 2
        rw = (_pad_lanes(jnp.concatenate(_split3(router_w[j].astype(F32)), axis=1))
              if is_moe else None)
        outs = _outproj(x2, y_ssm, u, y_att, ygm, ssm_d[layer].reshape(1, D_SSM).astype(F32),
                        ssm_w_glu[layer].astype(BF16), group_norm_g[layer].reshape(1, D_MODEL),
                        w_out[layer].astype(BF16), ffn_norm_g[layer].reshape(1, D_MODEL), rw, tm=tm)
        fin = final_norm_g.reshape(1, D_MODEL) if layer == depth - 1 else None
        if is_moe:
            x_mid, h, route, counts = outs
            dffe = moe_w_gate.shape[-1]
            tme = min(MOE_ROW_TILE, 2 * n)
            tmm = _largest_tile(n, MOE_TOKEN_TILE, tm)
            dest, gaps, tile_expert, n_used, rows = _moe_layout(route, counts, tm=tmm, tme=tme)
            xs = _moe_dispatch(dest, gaps, h, rows=rows, tm=tmm, tme=tme)
            ys = _moe_ffn(tile_expert, n_used, xs, moe_w_gate[j].astype(BF16),
                          moe_w_up[j].astype(BF16), moe_w_down[j].astype(BF16), tme=tme,
                          tf=_mxu_tile(dffe, 3584))
            x2 = _moe_combine(dest, x_mid, route, ys, fin, tm=tmm)
        else:
            x_mid, h = outs
            dff = dense_w_gate.shape[-1]
            x2 = _dense_ffn(h, x_mid, dense_w_gate[j].astype(BF16), dense_w_up[j].astype(BF16),
                            dense_w_down[j].astype(BF16), fin, tm=tm,
                            tf=_mxu_tile(dff, 2816))
    return x2.reshape(bsz, seq, D_MODEL).astype(x.dtype)
```

```python
import functools
import math

import jax
import jax.numpy as jnp
import numpy as np
from jax import lax
from jax.experimental import pallas as pl
from jax.experimental.pallas import tpu as pltpu

F32 = jnp.float32
BF16 = jnp.bfloat16

D_MODEL = 1024
D_SSM = 256
D_ATT = 512
D_GM = 256
SSM_GROUP = 16
SSM_GROUPS = 16
SSM_STATE = 64
ATT_HEADS = 8
ATT_HEAD_DIM = 64
GM_HEADS = 4
GM_HEAD_DIM = 64
CHUNK = 128
N_EXPERTS = 8
EPS = 1e-6
LOG2E = math.log2(math.e)

LANES = 128
SUBLANES = 8
MXU_WIDTH = 256
V7X_VMEM_BYTES = 64 * 1024 * 1024
ATT_SLOTS = ATT_HEADS * LANES
MOE_ROW_TILE = 512
MOE_TOKEN_TILE = 1024
OUTPROJ_PARTS = 4
ATT_STEP_HEADS = 2
SSM_T = 8
SSM_ROW = SSM_T * D_SSM
SSM_NSTATE = SSM_GROUPS * SSM_STATE

C_U = 0
C_Q = C_U + D_SSM
C_K = C_Q + D_ATT
C_Z = C_K + D_ATT
C_F = C_Z + 2 * D_GM
D_IN_PAD = C_F + LANES

VMEM_LIMIT = V7X_VMEM_BYTES * 7 // 8


def _cparams(*sem):
    return pltpu.CompilerParams(dimension_semantics=sem, vmem_limit_bytes=VMEM_LIMIT)


def _resident(shape):
    nd = len(shape)
    return pl.BlockSpec(shape, lambda *_: (0,) * nd, pipeline_mode=pl.Buffered(1))


def _rms(x, g):
    return x * lax.rsqrt(jnp.mean(x * x, axis=-1, keepdims=True) + EPS) * g


def _gelu(x):
    c = math.sqrt(2.0 / math.pi)
    return 0.5 * x * (1.0 + jnp.tanh(c * (x + 0.044715 * (x * x * x))))


def _sigmoid(x):
    return 1.0 / (1.0 + jnp.exp(-x))


def _split3(x):
    p1 = x.astype(BF16)
    r1 = x - p1.astype(F32)
    p2 = r1.astype(BF16)
    r2 = r1 - p2.astype(F32)
    return p1, p2, r2.astype(BF16)


def _dot(a, b):
    return jnp.dot(a, b, preferred_element_type=F32)


def _split_spec(rows):
    return pl.BlockSpec((D_SSM // LANES, rows, LANES), lambda i: (0, i, 0))


def _split_store(ref, value, rows=slice(None)):
    for h in range(D_SSM // LANES):
        ref[h, rows, :] = value[:, h * LANES:(h + 1) * LANES]


def _split_load(ref, rows=slice(None)):
    return jnp.concatenate([ref[h, rows, :] for h in range(D_SSM // LANES)], axis=1)


def _bias_lane(head):
    return 0 if head % 2 else ATT_HEAD_DIM


def _inproj_body(x_ref, g_ref, w_ref, wvt_ref, bf_ref, lng_ref, lnb_ref, ws_ref, bs_ref, place_ref, qc_ref,
                 u_ref, q_ref, k_ref, v_ref, cum_ref, ygm_ref, carry_ref, *, tiles_per_seq):
    i = pl.program_id(0)

    @pl.when(i % tiles_per_seq == 0)
    def _():
        carry_ref[...] = jnp.zeros_like(carry_ref)

    tm = x_ref.shape[0]
    hb = _rms(x_ref[...], g_ref[...]).astype(BF16)

    def proj(lo, width):
        return _dot(hb, w_ref[:, lo:lo + width])

    _split_store(u_ref, proj(C_U, D_SSM))
    v_ref[...] = lax.dot_general(wvt_ref[...], hb, (((1,), (1,)), ((), ())),
                                 preferred_element_type=F32).astype(BF16)

    f = proj(C_F, LANES) + bf_ref[...]
    logf = (jnp.minimum(f, 0.0) - jnp.log(1.0 + jnp.exp(-jnp.abs(f)))) * LOG2E
    row = lax.broadcasted_iota(jnp.int32, (tm, tm), 0)
    col = lax.broadcasted_iota(jnp.int32, (tm, tm), 1)
    tri = (col <= row).astype(BF16)
    sums = _dot(tri, jnp.concatenate(_split3(logf), axis=1))
    cum = sums[:, :LANES] + sums[:, LANES:2 * LANES] + sums[:, 2 * LANES:] + carry_ref[...]
    cum_ref[...] = cum
    carry_ref[...] = cum[tm - 1:tm, :]

    lane = lax.broadcasted_iota(jnp.int32, (tm, LANES), 1)
    c1, c2, c3 = [jnp.where(lane < ATT_HEADS, piece.astype(F32), 0.0) for piece in _split3(cum)]
    packed = c1 + pltpu.roll(c2, ATT_HEADS, axis=1) + pltpu.roll(c3, 2 * ATT_HEADS, axis=1)
    placed = _dot(packed.astype(BF16), place_ref[...])
    upper = lax.broadcasted_iota(jnp.int32, (tm, LANES), 1) >= ATT_HEAD_DIM
    qf = proj(C_Q, D_ATT)
    kf = proj(C_K, D_ATT)
    for h in range(ATT_HEADS):
        pair = slice((h // 2) * LANES, (h // 2 + 1) * LANES)
        slot = slice(h * LANES, (h + 1) * LANES)
        own = upper if h % 2 else ~upper
        q_ref[:, slot] = jnp.where(own, qf[:, pair], qc_ref[:, slot]).astype(BF16)
        k_ref[:, slot] = jnp.where(own, kf[:, pair], placed[:, slot]).astype(BF16)

    zg = _gelu(proj(C_Z, 2 * D_GM))
    ug = zg[:, :D_GM]
    vg = zg[:, D_GM:]
    mean = jnp.mean(vg, axis=-1, keepdims=True)
    cen = vg - mean
    var = jnp.mean(cen * cen, axis=-1, keepdims=True)
    vn = (cen * lax.rsqrt(var + EPS) * lng_ref[...] + lnb_ref[...]).astype(BF16)
    lane_head = lax.broadcasted_iota(jnp.int32, (CHUNK, D_GM), 1) >> 6
    for c in range(tm // CHUNK):
        rows = slice(c * CHUNK, (c + 1) * CHUNK)
        vc = vn[rows, :]
        mixed = _dot(ws_ref[0], vc)
        for g in range(1, GM_HEADS):
            mixed = jnp.where(lane_head == g, _dot(ws_ref[g], vc), mixed)
        mixed = mixed + bs_ref[...]
        ygm_ref[rows, :] = (ug[rows, :] * mixed).astype(BF16)


def _attn_slot_constants():
    place = np.zeros((LANES, ATT_SLOTS), np.float32)
    qconst = np.zeros((1, ATT_SLOTS), np.float32)
    for h in range(ATT_HEADS):
        for piece in range(3):
            lane = h * LANES + _bias_lane(h) + piece
            place[piece * ATT_HEADS + h, lane] = 1.0
            qconst[0, lane] = -1.0
    return jnp.asarray(place, BF16), jnp.asarray(qconst)


def _inproj(x2, g, w, wvt, bf, lng, lnb, ws, bs, *, seq, tm):
    n = x2.shape[0]
    tps = seq // tm
    place, qconst = _attn_slot_constants()
    row = lambda width: pl.BlockSpec((tm, width), lambda i: (i, 0))
    return pl.pallas_call(
        functools.partial(_inproj_body, tiles_per_seq=tps),
        grid=(n // tm,),
        in_specs=[row(D_MODEL), _resident((1, D_MODEL)), _resident((D_MODEL, D_IN_PAD)),
                  _resident((D_ATT, D_MODEL)),
                  _resident((1, LANES)), _resident((1, D_GM)), _resident((1, D_GM)),
                  _resident((GM_HEADS, CHUNK, CHUNK)), _resident((CHUNK, D_GM)),
                  _resident((LANES, ATT_SLOTS)), _resident((1, ATT_SLOTS))],
        out_specs=[_split_spec(tm), row(ATT_SLOTS), row(ATT_SLOTS),
                   pl.BlockSpec((None, D_ATT, tm), lambda i: (i // tps, 0, i % tps)),
                   row(LANES), row(D_GM)],
        out_shape=[jax.ShapeDtypeStruct((D_SSM // LANES, n, LANES), F32),
                   jax.ShapeDtypeStruct((n, ATT_SLOTS), BF16),
                   jax.ShapeDtypeStruct((n, ATT_SLOTS), BF16),
                   jax.ShapeDtypeStruct((n // seq, D_ATT, seq), BF16),
                   jax.ShapeDtypeStruct((n, LANES), F32),
                   jax.ShapeDtypeStruct((n, D_GM), BF16)],
        scratch_shapes=[pltpu.VMEM((1, LANES), F32)],
        compiler_params=_cparams("arbitrary"),
        name="inproj",
    )(x2, g, w, wvt, bf, lng, lnb, ws, bs, place, qconst)


def _ssm_tokens(u_ref, s, tr):
    return _split_load(u_ref, pl.ds(s, tr, stride=SSM_T)).astype(BF16)


def _ssm_state_body(u_ref, f_ref, s_ref):
    tr = s_ref.shape[0]
    acc = _dot(_ssm_tokens(u_ref, 0, tr), f_ref[:D_SSM, :])
    for s in range(1, SSM_T):
        acc += _dot(_ssm_tokens(u_ref, s, tr), f_ref[s * D_SSM:(s + 1) * D_SSM, :])
    s_ref[...] = acc


def _ssm_state(u, fmat, *, tr):
    nc = u.shape[1] // SSM_T
    return pl.pallas_call(
        _ssm_state_body,
        grid=(nc // tr,),
        in_specs=[_split_spec(tr * SSM_T), _resident((SSM_ROW, 2 * SSM_NSTATE))],
        out_specs=pl.BlockSpec((tr, 2 * SSM_NSTATE), lambda i: (i, 0)),
        out_shape=jax.ShapeDtypeStruct((nc, 2 * SSM_NSTATE), F32),
        compiler_params=_cparams("parallel"),
        name="ssm_state",
    )(u, fmat)


def _ssm_scan_body(s_ref, a_ref, xp_ref, st_ref):
    @pl.when(pl.program_id(1) == 0)
    def _():
        st_ref[...] = jnp.zeros_like(st_ref)

    tc = s_ref.shape[0]
    are = a_ref[:, :SSM_NSTATE]
    aim = a_ref[:, SSM_NSTATE:]

    def step(c8, carry):
        xre, xim = carry
        base = pl.multiple_of(c8 * SUBLANES, SUBLANES)
        blk = s_ref[pl.ds(base, SUBLANES), :]
        prev_re, prev_im = [], []
        for r in range(SUBLANES):
            prev_re.append(xre)
            prev_im.append(xim)
            sre = blk[r:r + 1, :SSM_NSTATE]
            sim = blk[r:r + 1, SSM_NSTATE:]
            xre, xim = are * xre - aim * xim + sre, are * xim + aim * xre + sim
        xp_ref[pl.ds(base, SUBLANES), :SSM_NSTATE] = jnp.concatenate(prev_re, axis=0)
        xp_ref[pl.ds(base, SUBLANES), SSM_NSTATE:] = jnp.concatenate(prev_im, axis=0)
        return xre, xim

    xre, xim = lax.fori_loop(0, tc // SUBLANES, step,
                             (st_ref[:, :SSM_NSTATE], st_ref[:, SSM_NSTATE:]))
    st_ref[:, :SSM_NSTATE] = xre
    st_ref[:, SSM_NSTATE:] = xim


def _ssm_scan(s3, a, *, tc):
    b, ncb, w = s3.shape
    spec = pl.BlockSpec((None, tc, w), lambda bi, ci: (bi, ci, 0))
    return pl.pallas_call(
        _ssm_scan_body,
        grid=(b, ncb // tc),
        in_specs=[spec, pl.BlockSpec((1, w), lambda bi, ci: (0, 0))],
        out_specs=spec,
        out_shape=jax.ShapeDtypeStruct(s3.shape, F32),
        scratch_shapes=[pltpu.VMEM((1, w), F32)],
        compiler_params=_cparams("parallel", "arbitrary"),
        name="ssm_scan",
    )(s3, a)


def _ssm_out_body(u_ref, xp_ref, m_ref, e_ref, y_ref):
    tr = xp_ref.shape[0]
    us = [_ssm_tokens(u_ref, s, tr) for s in range(SSM_T)]
    xp = xp_ref[...].astype(BF16)
    for t in range(SSM_T):
        cols = slice(t * D_SSM, (t + 1) * D_SSM)
        acc = _dot(xp, e_ref[:, cols])
        for s in range(t + 1):
            acc += _dot(us[s], m_ref[s * D_SSM:(s + 1) * D_SSM, cols])
        _split_store(y_ref, acc, pl.ds(t, tr, stride=SSM_T))


def _ssm_out(u, xp, mmat, emat, *, tr):
    nc = xp.shape[0]
    return pl.pallas_call(
        _ssm_out_body,
        grid=(nc // tr,),
        in_specs=[_split_spec(tr * SSM_T),
                  pl.BlockSpec((tr, 2 * SSM_NSTATE), lambda i: (i, 0)),
                  _resident((SSM_ROW, SSM_ROW)), _resident((2 * SSM_NSTATE, SSM_ROW))],
        out_specs=_split_spec(tr * SSM_T),
        out_shape=jax.ShapeDtypeStruct((D_SSM // LANES, nc * SSM_T, LANES), F32),
        compiler_params=_cparams("parallel"),
        name="ssm_out",
    )(u, xp, mmat, emat)


def _ssm_matrices(lam_re, lam_im, log_dt, b_re, b_im, c_re, c_im):
    t, g, p, h = SSM_T, SSM_GROUPS, SSM_STATE, SSM_GROUP
    ns, nch = g * p, g * h
    hp = lax.Precision.HIGHEST
    lr, li = lam_re.astype(F32).reshape(ns), lam_im.astype(F32).reshape(ns)
    dt = jnp.repeat(jnp.exp(log_dt.astype(F32)), p)
    steps = jnp.arange(t + 1, dtype=F32)[:, None]
    mag = jnp.exp((lr * dt)[None] * steps)
    ang = (li * dt)[None] * steps
    pr, pi = mag * jnp.cos(ang), mag * jnp.sin(ang)
    nr, ni = pr[1] - 1.0, pi[1]
    den = lr * lr + li * li
    zr, zi = ((nr * lr + ni * li) / den)[:, None], ((ni * lr - nr * li) / den)[:, None]
    br, bi = b_re.astype(F32).reshape(ns, h), b_im.astype(F32).reshape(ns, h)
    cr = c_re.astype(F32).transpose(0, 2, 1).reshape(ns, h)
    ci = c_im.astype(F32).transpose(0, 2, 1).reshape(ns, h)
    same = jnp.asarray(np.repeat(np.repeat(np.eye(g, dtype=np.float32), p, axis=0), h, axis=1))
    spread = lambda w: jnp.tile(w, (1, g)) * same
    bhr, bhi = spread(zr * br - zi * bi), spread(zr * bi + zi * br)
    chr_, chi = spread(cr), spread(ci)
    prc, pic = pr.T[:, :, None], pi.T[:, :, None]
    er = (prc * chr_[:, None, :] - pic * chi[:, None, :]).reshape(ns, (t + 1) * nch)
    ei = (prc * chi[:, None, :] + pic * chr_[:, None, :]).reshape(ns, (t + 1) * nch)

    k_all = (jnp.dot(bhr.T, er[:, :t * nch], precision=hp)
             - jnp.dot(bhi.T, ei[:, :t * nch], precision=hp))
    mmat = jnp.concatenate(
        [jnp.pad(k_all[:, :(t - s) * nch], ((0, 0), (s * nch, 0))) for s in range(t)], axis=0)

    prr, pir = pr[:t][::-1][:, None, :], pi[:t][::-1][:, None, :]
    bhrt, bhit = bhr.T[None], bhi.T[None]
    fmat = jnp.concatenate([(bhrt * prr - bhit * pir).reshape(t * nch, ns),
                            (bhrt * pir + bhit * prr).reshape(t * nch, ns)], axis=1)

    emat = jnp.concatenate([er[:, nch:], -ei[:, nch:]], axis=0)

    a_row = jnp.concatenate([pr[t][None], pi[t][None]], axis=1)
    return mmat.astype(BF16), fmat.astype(BF16), emat.astype(BF16), a_row


def _attn_body(q_ref, k_ref, v_ref, cq_ref, o_ref,
               m_ref, acc_ref, s_ref, mt_ref, p_ref, al_ref, *, tq):
    qi = pl.program_id(2)
    nh = ATT_STEP_HEADS
    tk = tq // 2
    every = slice(0, tq)
    late = slice(tk, tq)

    def slot_lanes(j):
        return slice(j * LANES, (j + 1) * LANES)

    def scores(t, slot, qs=every, masked=False):
        start = pl.multiple_of(t * tk, tk)
        width = qs.stop - qs.start
        for j in range(nh):
            kt = k_ref[pl.ds(start, tk), slot_lanes(j)]
            s = lax.dot_general(kt, q_ref[qs, slot_lanes(j)], (((1,), (1,)), ((), ())),
                                preferred_element_type=F32)
            if masked:
                keypos = lax.broadcasted_iota(jnp.int32, (tk, width), 0)
                qpos = lax.broadcasted_iota(jnp.int32, (tk, width), 1)
                s = jnp.where(keypos <= qpos, s, -jnp.inf)
            s_ref[slot, j, :, qs] = s
            mt_ref[slot, j, :, qs] = jnp.max(s, axis=0, keepdims=True)

    def probs(slot, qs=every):
        for j in range(nh):
            m = m_ref[j, :, qs]
            cq = cq_ref[j:j + 1, qs]
            m_new = jnp.maximum(m, mt_ref[slot, j, :, qs] + cq)
            p_ref[slot, j, :, qs] = jnp.exp2(s_ref[slot, j, :, qs] + (cq - m_new)).astype(BF16)
            al_ref[slot, j, :, qs] = jnp.exp2(m - m_new)
            m_ref[j, :, qs] = m_new

    def values(t, slot, qs=every):
        start = pl.multiple_of(t * tk, tk)
        for j in range(nh):
            vt = v_ref[j * ATT_HEAD_DIM:(j + 1) * ATT_HEAD_DIM, pl.ds(start, tk)]
            vt = jnp.concatenate([vt, jnp.ones_like(vt)], axis=0)
            acc_ref[j, :, qs] = (al_ref[slot, j, :, qs] * acc_ref[j, :, qs]
                                 + _dot(vt, p_ref[slot, j, :, qs]))

    def trip(t, slot, **next_tile):
        values(t - 1, 1 - slot)
        probs(slot)
        scores(t + 1, 1 - slot, **next_tile)

    for j in range(nh):
        m_ref[j] = jnp.full((1, tq), -jnp.inf, F32)
        acc_ref[j] = jnp.zeros((LANES, tq), F32)

    last = 2 * qi + 1

    @pl.when(qi == 0)
    def _():
        scores(0, 0, masked=True)
        probs(0)
        scores(1, 1, late, masked=True)

    @pl.when(qi > 0)
    def _():
        scores(0, 0)
        probs(0)
        scores(1, 1)

        def pair(u, carry):
            t = 2 * u + 1
            trip(t, 1)
            trip(t + 1, 0)
            return carry

        lax.fori_loop(0, qi - 1, pair, 0)
        trip(last - 2, 1, masked=True)
        trip(last - 1, 0, qs=late, masked=True)

    values(last - 1, 0)
    probs(1, late)
    values(last, 1, late)
    for pair in range(nh // 2):
        outs = []
        for j in (2 * pair, 2 * pair + 1):
            acc = acc_ref[j]
            outs.append(acc[:ATT_HEAD_DIM, :] / acc[ATT_HEAD_DIM:, :])
        o_ref[:, slot_lanes(pair)] = jnp.concatenate(outs, axis=0).T.astype(BF16)


def _attention(q3, k3, vt4, cum3, *, tq):
    b, s, _ = q3.shape
    nh = ATT_STEP_HEADS
    return pl.pallas_call(
        functools.partial(_attn_body, tq=tq),
        scratch_shapes=[pltpu.VMEM((nh, 1, tq), F32), pltpu.VMEM((nh, LANES, tq), F32),
                        pltpu.VMEM((2, nh, tq // 2, tq), F32), pltpu.VMEM((2, nh, 1, tq), F32),
                        pltpu.VMEM((2, nh, tq // 2, tq), BF16), pltpu.VMEM((2, nh, 1, tq), F32)],
        grid=(b, ATT_HEADS // nh, s // tq),
        in_specs=[pl.BlockSpec((None, tq, nh * LANES), lambda bi, hg, qi: (bi, qi, hg)),
                  pl.BlockSpec((None, s, nh * LANES), lambda bi, hg, qi: (bi, 0, hg)),
                  pl.BlockSpec((None, nh * ATT_HEAD_DIM, s), lambda bi, hg, qi: (bi, hg, 0)),
                  pl.BlockSpec((None, None, nh, tq), lambda bi, hg, qi: (bi, hg, 0, qi))],
        out_specs=pl.BlockSpec((None, tq, nh * ATT_HEAD_DIM), lambda bi, hg, qi: (bi, qi, hg)),
        out_shape=jax.ShapeDtypeStruct((b, s, D_ATT), BF16),
        compiler_params=_cparams("parallel", "parallel", "arbitrary"),
        name="attention",
    )(q3, k3, vt4, cum3)


R_E1, R_E2, R_W1, R_W2, R_RANK1, R_RANK2 = range(6)


def _route(logits, cnt_ref):
    tm = logits.shape[0]
    lane = lax.broadcasted_iota(jnp.int32, logits.shape, 1).astype(F32)
    lg = jnp.where(lane < N_EXPERTS, logits, -jnp.inf)
    m1 = jnp.max(lg, axis=-1, keepdims=True)
    i1 = jnp.min(jnp.where(lg == m1, lane, float(LANES)), axis=-1, keepdims=True)
    lg2 = jnp.where(lane == i1, -jnp.inf, lg)
    m2 = jnp.max(lg2, axis=-1, keepdims=True)
    i2 = jnp.min(jnp.where(lg2 == m2, lane, float(LANES)), axis=-1, keepdims=True)
    e = jnp.exp(m2 - m1)
    w1 = 1.0 / (1.0 + e)
    w2 = e / (1.0 + e)

    hit1, hit2 = lane == i1, lane == i2
    oh1, oh2 = hit1.astype(BF16), hit2.astype(BF16)
    row = lax.broadcasted_iota(jnp.int32, (tm, tm), 0)
    col = lax.broadcasted_iota(jnp.int32, (tm, tm), 1)
    before = (col < row).astype(BF16)
    carry = cnt_ref[...]
    tot1 = jnp.sum(oh1.astype(F32), axis=0, keepdims=True)
    tot2 = jnp.sum(oh2.astype(F32), axis=0, keepdims=True)
    pos1 = _dot(before, oh1) + carry
    pos2 = _dot(before, oh2) + (carry + tot1)
    rank1 = jnp.sum(jnp.where(hit1, pos1, 0.0), axis=-1, keepdims=True)
    rank2 = jnp.sum(jnp.where(hit2, pos2, 0.0), axis=-1, keepdims=True)
    cnt_ref[...] = carry + tot1 + tot2

    rec = jnp.zeros(logits.shape, F32)
    for idx, val in ((R_E1, i1), (R_E2, i2), (R_W1, w1), (R_W2, w2), (R_RANK1, rank1), (R_RANK2, rank2)):
        rec = jnp.where(lane == idx, val, rec)
    return rec


def _outproj_body(*refs, with_router):
    if with_router:
        (x_ref, ys_ref, us_ref, ya_ref, yg_ref, d_ref, wglu_ref, gn_ref, w_ref, fg_ref, rw_ref,
         xo_ref, h_ref, route_ref, cnt_ref) = refs

        @pl.when(pl.program_id(0) == 0)
        def _():
            cnt_ref[...] = jnp.zeros_like(cnt_ref)
    else:
        (x_ref, ys_ref, us_ref, ya_ref, yg_ref, d_ref, wglu_ref, gn_ref, w_ref, fg_ref,
         xo_ref, h_ref) = refs

    tm = x_ref.shape[0]
    part = tm // OUTPROJ_PARTS
    o1, o2 = D_SSM, D_SSM + D_ATT
    for k in range(OUTPROJ_PARTS):
        rows = slice(k * part, (k + 1) * part)
        y1 = _gelu(_split_load(ys_ref, rows) + d_ref[...] * _split_load(us_ref, rows))
        y_ssm = y1 * _sigmoid(_dot(y1.astype(BF16), wglu_ref[...]))

        acc = _dot(_rms(y_ssm, gn_ref[:, :o1]).astype(BF16), w_ref[:o1, :])
        acc += _dot(_rms(ya_ref[rows, :].astype(F32), gn_ref[:, o1:o2]).astype(BF16), w_ref[o1:o2, :])
        acc += _dot(_rms(yg_ref[rows, :].astype(F32), gn_ref[:, o2:]).astype(BF16), w_ref[o2:, :])
        xn = x_ref[rows, :] + acc
        xo_ref[rows, :] = xn
        hn = _rms(xn, fg_ref[...])
        h_ref[rows, :] = hn.astype(h_ref.dtype)
        if with_router:
            rw = rw_ref[...]
            prod = sum(_dot(piece, rw) for piece in _split3(hn))
            logits = (prod + pltpu.roll(prod, LANES - N_EXPERTS, axis=1)
                      + pltpu.roll(prod, LANES - 2 * N_EXPERTS, axis=1))
            route_ref[rows, :] = _route(logits, cnt_ref)


def _outproj(x2, ys, us, ya, yg, d, wglu, gn, w, fg, rw, *, tm):
    n = x2.shape[0]
    with_router = rw is not None
    row = lambda width: pl.BlockSpec((tm, width), lambda i: (i, 0))
    in_specs = [row(D_MODEL), _split_spec(tm), _split_spec(tm), row(D_ATT), row(D_GM),
                _resident((1, D_SSM)), _resident((D_SSM, D_SSM)), _resident((1, D_MODEL)),
                _resident((D_MODEL, D_MODEL)), _resident((1, D_MODEL))]
    out_specs = [row(D_MODEL), row(D_MODEL)]
    out_shape = [jax.ShapeDtypeStruct((n, D_MODEL), F32),
                 jax.ShapeDtypeStruct((n, D_MODEL), F32 if with_router else BF16)]
    args = [x2, ys, us, ya, yg, d, wglu, gn, w, fg]
    if with_router:
        in_specs.append(_resident((D_MODEL, LANES)))
        out_specs += [row(LANES), pl.BlockSpec((1, LANES), lambda i: (0, 0))]
        out_shape += [jax.ShapeDtypeStruct((n, LANES), F32), jax.ShapeDtypeStruct((1, LANES), F32)]
        args.append(rw)
    return pl.pallas_call(
        functools.partial(_outproj_body, with_router=with_router),
        grid=(n // tm,),
        in_specs=in_specs, out_specs=out_specs, out_shape=out_shape,
        compiler_params=_cparams("arbitrary" if with_router else "parallel"),
        name="outproj_router" if with_router else "outproj",
    )(*args)


def _swiglu_tile(h, wg, wu, wd):
    a = _dot(h, wg)
    return _dot((a * _sigmoid(a) * _dot(h, wu)).astype(BF16), wd)


def _finish(x, acc, fin_ref, o_ref):
    xn = x + acc
    o_ref[...] = xn if fin_ref is None else _rms(xn, fin_ref[...])


def _dense_ffn_body(*refs, final_norm):
    if final_norm:
        h_ref, x_ref, wg_ref, wu_ref, wd_ref, fin_ref, o_ref, acc_ref = refs
    else:
        h_ref, x_ref, wg_ref, wu_ref, wd_ref, o_ref, acc_ref = refs
        fin_ref = None
    f = pl.program_id(1)

    @pl.when(f == 0)
    def _():
        acc_ref[...] = jnp.zeros_like(acc_ref)

    acc_ref[...] += _swiglu_tile(h_ref[...], wg_ref[...], wu_ref[...], wd_ref[...])

    @pl.when(f == pl.num_programs(1) - 1)
    def _():
        _finish(x_ref[...], acc_ref[...], fin_ref, o_ref)


def _dense_ffn(h, x2, wg, wu, wd, fin, *, tm, tf):
    n = x2.shape[0]
    dff = wg.shape[1]
    row = lambda: pl.BlockSpec((tm, D_MODEL), lambda i, f: (i, 0))
    mode = dict(pipeline_mode=pl.Buffered(1)) if tf == dff else {}
    in_specs = [row(), row(),
                pl.BlockSpec((D_MODEL, tf), lambda i, f: (0, f), **mode),
                pl.BlockSpec((D_MODEL, tf), lambda i, f: (0, f), **mode),
                pl.BlockSpec((tf, D_MODEL), lambda i, f: (f, 0), **mode)]
    args = [h, x2, wg, wu, wd]
    if fin is not None:
        in_specs.append(pl.BlockSpec((1, D_MODEL), lambda i, f: (0, 0)))
        args.append(fin)
    return pl.pallas_call(
        functools.partial(_dense_ffn_body, final_norm=fin is not None),
        grid=(n // tm, dff // tf),
        in_specs=in_specs, out_specs=row(),
        out_shape=jax.ShapeDtypeStruct((n, D_MODEL), F32),
        scratch_shapes=[pltpu.VMEM((tm, D_MODEL), F32)],
        compiler_params=_cparams("parallel", "arbitrary"),
        name="dense_ffn",
    )(*args)


def _row_copy(src, src_row, dst, dst_row, sem):
    return pltpu.make_async_copy(src.at[pl.ds(src_row, 1)], dst.at[pl.ds(dst_row, 1)], sem)


def _moe_dispatch_body(dest_ref, gaps_ref, h_ref, xs_ref, zero_ref, sem, zsem, *, tme):
    tm = h_ref.shape[0]

    @pl.when(pl.program_id(0) == 0)
    def _():
        zero_ref[...] = jnp.zeros_like(zero_ref)

        def zero_copy(gap, i):
            if gap == N_EXPERTS:
                return pltpu.make_async_copy(zero_ref, xs_ref.at[pl.ds(i * tme, tme)], zsem)
            return _row_copy(zero_ref, 0, xs_ref, i, zsem)

        for gap in range(N_EXPERTS + 1):
            lo, hi = gaps_ref[0, gap], gaps_ref[1, gap]

            def start(i, carry, gap=gap):
                zero_copy(gap, i).start()
                return carry

            def wait(i, carry, gap=gap):
                zero_copy(gap, i).wait()
                return carry

            lax.fori_loop(lo, hi, start, 0)
            lax.fori_loop(lo, hi, wait, 0)

    def issue(r, carry):
        for k in range(2):
            _row_copy(h_ref, r, xs_ref, dest_ref[k, r], sem).start()
        return carry

    lax.fori_loop(0, tm, issue, 0, unroll=8)
    for k in range(2):
        pltpu.make_async_copy(h_ref, xs_ref.at[pl.ds(0, tm)], sem).wait()


def _moe_dispatch(dest, gaps, h, *, rows, tm, tme):
    n = h.shape[0]
    return pl.pallas_call(
        functools.partial(_moe_dispatch_body, tme=tme),
        grid=(n // tm,),
        in_specs=[pl.BlockSpec((None, 2, tm), lambda i: (i, 0, 0), memory_space=pltpu.SMEM),
                  pl.BlockSpec(memory_space=pltpu.SMEM),
                  pl.BlockSpec((tm, D_MODEL), lambda i: (i, 0))],
        out_specs=pl.BlockSpec(memory_space=pl.ANY),
        out_shape=jax.ShapeDtypeStruct((rows, D_MODEL), F32),
        scratch_shapes=[pltpu.VMEM((tme, D_MODEL), F32), pltpu.SemaphoreType.DMA(()),
                        pltpu.SemaphoreType.DMA(())],
        compiler_params=pltpu.CompilerParams(dimension_semantics=("arbitrary",),
                                             vmem_limit_bytes=VMEM_LIMIT, has_side_effects=True),
        name="moe_dispatch",
    )(dest, gaps, h)


def _moe_ffn_body(te_ref, nu_ref, xs_ref, wg_ref, wu_ref, wd_ref, ys_ref, acc_ref):
    del te_ref
    i = pl.program_id(0)
    f = pl.program_id(1)

    @pl.when(i < nu_ref[0])
    def _():
        @pl.when(f == 0)
        def _():
            acc_ref[...] = jnp.zeros_like(acc_ref)

        acc_ref[...] += _swiglu_tile(xs_ref[...].astype(BF16), wg_ref[...], wu_ref[...], wd_ref[...])

        @pl.when(f == pl.num_programs(1) - 1)
        def _():
            ys_ref[...] = acc_ref[...]

    @pl.when((i >= nu_ref[0]) & (f == pl.num_programs(1) - 1))
    def _():
        ys_ref[...] = jnp.zeros_like(ys_ref)


def _moe_ffn(tile_expert, n_used, xs, wg, wu, wd, *, tme, tf):
    rows = xs.shape[0]
    dff = wg.shape[-1]
    nf = dff // tf
    row_idx = lambda i, f, te, nu: (jnp.minimum(i, nu[0] - 1), 0)
    col = lambda i, f, nu: jnp.where(i < nu[0], f, nf - 1)
    mode = dict(pipeline_mode=pl.Buffered(1)) if nf == 1 else {}
    grid_spec = pltpu.PrefetchScalarGridSpec(
        num_scalar_prefetch=2,
        grid=(rows // tme, nf),
        in_specs=[pl.BlockSpec((tme, D_MODEL), row_idx),
                  pl.BlockSpec((None, D_MODEL, tf), lambda i, f, te, nu: (te[i], 0, col(i, f, nu)), **mode),
                  pl.BlockSpec((None, D_MODEL, tf), lambda i, f, te, nu: (te[i], 0, col(i, f, nu)), **mode),
                  pl.BlockSpec((None, tf, D_MODEL), lambda i, f, te, nu: (te[i], col(i, f, nu), 0), **mode)],
        out_specs=pl.BlockSpec((tme, D_MODEL), lambda i, f, te, nu: (i, 0)),
        scratch_shapes=[pltpu.VMEM((tme, D_MODEL), F32)])
    return pl.pallas_call(
        _moe_ffn_body,
        grid_spec=grid_spec,
        out_shape=jax.ShapeDtypeStruct((rows, D_MODEL), F32),
        compiler_params=_cparams("arbitrary", "arbitrary"),
        name="moe_ffn",
    )(tile_expert, n_used, xs, wg, wu, wd)


def _moe_combine_body(*refs, final_norm):
    if final_norm:
        dest_ref, x_ref, route_ref, ys_ref, fin_ref, o_ref, buf_ref, sem = refs
    else:
        dest_ref, x_ref, route_ref, ys_ref, o_ref, buf_ref, sem = refs
        fin_ref = None
    tm = x_ref.shape[0]

    def issue(r, carry):
        for k in range(2):
            _row_copy(ys_ref, dest_ref[k, r], buf_ref.at[k], r, sem).start()
        return carry

    lax.fori_loop(0, tm, issue, 0, unroll=8)
    for k in range(2):
        pltpu.make_async_copy(ys_ref.at[pl.ds(0, tm)], buf_ref.at[k], sem).wait()

    route = route_ref[...]
    lane = lax.broadcasted_iota(jnp.int32, route.shape, 1)
    w1 = jnp.sum(jnp.where(lane == R_W1, route, 0.0), axis=-1, keepdims=True)
    w2 = jnp.sum(jnp.where(lane == R_W2, route, 0.0), axis=-1, keepdims=True)
    _finish(x_ref[...], w1 * buf_ref[0] + w2 * buf_ref[1], fin_ref, o_ref)


def _moe_combine(dest, x2, route, ys, fin, *, tm):
    n = x2.shape[0]
    row = lambda width: pl.BlockSpec((tm, width), lambda i: (i, 0))
    in_specs = [pl.BlockSpec((None, 2, tm), lambda i: (i, 0, 0), memory_space=pltpu.SMEM),
                row(D_MODEL), row(LANES), pl.BlockSpec(memory_space=pl.ANY)]
    args = [dest, x2, route, ys]
    if fin is not None:
        in_specs.append(pl.BlockSpec((1, D_MODEL), lambda i: (0, 0)))
        args.append(fin)
    return pl.pallas_call(
        functools.partial(_moe_combine_body, final_norm=fin is not None),
        grid=(n // tm,),
        in_specs=in_specs, out_specs=row(D_MODEL),
        out_shape=jax.ShapeDtypeStruct((n, D_MODEL), F32),
        scratch_shapes=[pltpu.VMEM((2, tm, D_MODEL), F32), pltpu.SemaphoreType.DMA(())],
        compiler_params=_cparams("arbitrary"),
        name="moe_combine",
    )(*args)


def _moe_layout(route, counts, *, tm, tme):
    n = route.shape[0]
    cnt = counts[0, :N_EXPERTS].astype(jnp.int32)
    padded = (cnt + (tme - 1)) // tme * tme
    ends = jnp.cumsum(padded)
    starts = ends - padded
    experts = jnp.arange(N_EXPERTS, dtype=F32)

    def rows(e_lane, rank_lane):
        start = jnp.sum(jnp.where(route[:, e_lane, None] == experts[None], starts[None], 0), axis=1)
        return start + route[:, rank_lane].astype(jnp.int32)

    dest = jnp.stack([rows(R_E1, R_RANK1), rows(R_E2, R_RANK2)])
    dest = dest.reshape(2, n // tm, tm).transpose(1, 0, 2)
    n_tiles = (2 * n) // tme + N_EXPERTS
    tile_start = jnp.arange(n_tiles, dtype=jnp.int32) * tme
    tile_expert = jnp.minimum(jnp.sum(tile_start[:, None] >= ends[None, :], axis=1), N_EXPERTS - 1)
    n_used = (ends[-1] // tme).reshape(1)
    tile_expert = jnp.where(tile_start < ends[-1], tile_expert, tile_expert[jnp.maximum(n_used[0] - 1, 0)])
    gaps = jnp.stack([jnp.concatenate([starts + cnt, n_used]),
                      jnp.concatenate([ends, jnp.full((1,), n_tiles, jnp.int32)])]).astype(jnp.int32)
    return dest, gaps, tile_expert.astype(jnp.int32), n_used.astype(jnp.int32), n_tiles * tme


def _largest_tile(total, cap, mult):
    best = mult
    for t in range(mult, min(total, cap) + 1, mult):
        if total % t == 0:
            best = t
    return best


def _mxu_tile(total, cap):
    if total % MXU_WIDTH == 0:
        return _largest_tile(total, cap, MXU_WIDTH)
    return _largest_tile(total, cap, LANES)


def _pad_lanes(a):
    return jnp.pad(a, [(0, 0)] * (a.ndim - 1) + [(0, LANES - a.shape[-1])])


def kernel(x, mix_norm_g, w_in, b_forget, ssm_lambda_re, ssm_lambda_im, ssm_log_dt, ssm_b_re, ssm_b_im, ssm_c_re, ssm_c_im, ssm_d, ssm_w_glu, gm_ln_g, gm_ln_b, gm_w_s, gm_b_s, group_norm_g, w_out, ffn_norm_g, dense_w_gate, dense_w_up, dense_w_down, router_w, moe_w_gate, moe_w_up, moe_w_down, final_norm_g):
    bsz, seq, _ = x.shape
    n = bsz * seq
    depth = w_in.shape[0]
    assert seq % CHUNK == 0 and seq % SSM_T == 0
    tm = _largest_tile(seq, 512, CHUNK)
    tq = _largest_tile(seq, 1024, 2 * LANES)
    nc = n // SSM_T
    ncb = seq // SSM_T
    tr = _largest_tile(nc, 512, 8)
    tc = _largest_tile(ncb, 128, 8)

    x2 = x.reshape(n, D_MODEL).astype(F32)
    tril = jnp.tril(jnp.ones((CHUNK, CHUNK), F32))
    i0, i1, i2, i3, i4 = (D_SSM, D_SSM + D_ATT, D_SSM + 2 * D_ATT, D_SSM + 3 * D_ATT,
                          D_SSM + 3 * D_ATT + ATT_HEADS)
    for layer in range(depth):
        wl = w_in[layer]
        w_re = jnp.concatenate(
            [wl[:, :i0], wl[:, i0:i1] * (ATT_HEAD_DIM ** -0.5 * LOG2E), wl[:, i1:i2], wl[:, i4:],
             _pad_lanes(wl[:, i3:i4])], axis=1).astype(BF16)
        ws = (gm_w_s[layer].astype(F32) * tril[None]).astype(BF16)
        bs = jnp.repeat(gm_b_s[layer].astype(F32).T, GM_HEAD_DIM, axis=1)
        u, q, k, v, cum, ygm = _inproj(
            x2, mix_norm_g[layer].reshape(1, D_MODEL), w_re, wl[:, i2:i3].T.astype(BF16),
            _pad_lanes(b_forget[layer].reshape(1, ATT_HEADS).astype(F32)),
            gm_ln_g[layer].reshape(1, D_GM), gm_ln_b[layer].reshape(1, D_GM), ws, bs,
            seq=seq, tm=tm)

        mmat, fmat, emat, a_row = _ssm_matrices(
            ssm_lambda_re[layer], ssm_lambda_im[layer], ssm_log_dt[layer], ssm_b_re[layer],
            ssm_b_im[layer], ssm_c_re[layer], ssm_c_im[layer])
        s_loc = _ssm_state(u, fmat, tr=tr)
        xprev = _ssm_scan(s_loc.reshape(bsz, ncb, 2 * SSM_NSTATE), a_row, tc=tc)
        y_ssm = _ssm_out(u, xprev.reshape(nc, 2 * SSM_NSTATE), mmat, emat, tr=tr)

        cum_rows = cum[:, :ATT_HEADS].reshape(bsz, seq, ATT_HEADS // ATT_STEP_HEADS, ATT_STEP_HEADS)
        cum_rows = cum_rows.transpose(0, 2, 3, 1)
        y_att = _attention(q.reshape(bsz, seq, ATT_SLOTS), k.reshape(bsz, seq, ATT_SLOTS), v,
                           cum_rows, tq=tq).reshape(n, D_ATT)

        is_moe = layer % 2 == 1
        j = layer // 2
        rw = (_pad_lanes(jnp.concatenate(_split3(router_w[j].astype(F32)), axis=1))
              if is_moe else None)
        outs = _outproj(x2, y_ssm, u, y_att, ygm, ssm_d[layer].reshape(1, D_SSM).astype(F32),
                        ssm_w_glu[layer].astype(BF16), group_norm_g[layer].reshape(1, D_MODEL),
                        w_out[layer].astype(BF16), ffn_norm_g[layer].reshape(1, D_MODEL), rw,
                        tm=_largest_tile(n, OUTPROJ_PARTS * MXU_WIDTH, tm))
        fin = final_norm_g.reshape(1, D_MODEL) if layer == depth - 1 else None
        if is_moe:
            x_mid, h, route, counts = outs
            dffe = moe_w_gate.shape[-1]
            tme = min(MOE_ROW_TILE, 2 * n)
            tmm = _largest_tile(n, MOE_TOKEN_TILE, tm)
            dest, gaps, tile_expert, n_used, rows = _moe_layout(route, counts, tm=tmm, tme=tme)
            xs = _moe_dispatch(dest, gaps, h, rows=rows, tm=tmm, tme=tme)
            ys = _moe_ffn(tile_expert, n_used, xs, moe_w_gate[j].astype(BF16),
                          moe_w_up[j].astype(BF16), moe_w_down[j].astype(BF16), tme=tme,
                          tf=_mxu_tile(dffe, 3584))
            x2 = _moe_combine(dest, x_mid, route, ys, fin, tm=tmm)
        else:
            x_mid, h = outs
            dff = dense_w_gate.shape[-1]
            x2 = _dense_ffn(h, x_mid, dense_w_gate[j].astype(BF16), dense_w_up[j].astype(BF16),
                            dense_w_down[j].astype(BF16), fin, tm=tm,
                            tf=_mxu_tile(dff, 2816))
    return x2.reshape(bsz, seq, D_MODEL).astype(x.dtype)
```

```python
import functools
import math

import jax
import jax.numpy as jnp
import numpy as np
from jax import lax
from jax.experimental import pallas as pl
from jax.experimental.pallas import tpu as pltpu

F32 = jnp.float32
BF16 = jnp.bfloat16

D_MODEL = 1024
D_SSM = 256
D_ATT = 512
D_GM = 256
SSM_GROUP = 16
SSM_GROUPS = 16
SSM_STATE = 64
ATT_HEADS = 8
ATT_HEAD_DIM = 64
GM_HEADS = 4
GM_HEAD_DIM = 64
CHUNK = 128
N_EXPERTS = 8
EPS = 1e-6
LOG2E = math.log2(math.e)

LANES = 128
SUBLANES = 8
MXU_WIDTH = 256
V7X_VMEM_BYTES = 64 * 1024 * 1024
ATT_SLOTS = ATT_HEADS * LANES
MOE_ROW_TILE = 512
MOE_TOKEN_TILE = 1024
OUTPROJ_PARTS = 4
ATT_STEP_HEADS = 2
SSM_T = 8
SSM_ROW = SSM_T * D_SSM
SSM_NSTATE = SSM_GROUPS * SSM_STATE

C_U = 0
C_Q = C_U + D_SSM
C_K = C_Q + D_ATT
C_Z = C_K + D_ATT
C_F = C_Z + 2 * D_GM
D_IN_PAD = C_F + LANES

VMEM_LIMIT = V7X_VMEM_BYTES * 7 // 8


def _cparams(*sem):
    return pltpu.CompilerParams(dimension_semantics=sem, vmem_limit_bytes=VMEM_LIMIT)


def _resident(shape):
    nd = len(shape)
    return pl.BlockSpec(shape, lambda *_: (0,) * nd, pipeline_mode=pl.Buffered(1))


def _rms(x, g):
    return x * lax.rsqrt(jnp.mean(x * x, axis=-1, keepdims=True) + EPS) * g


def _gelu(x):
    c = math.sqrt(2.0 / math.pi)
    return 0.5 * x * (1.0 + jnp.tanh(c * (x + 0.044715 * (x * x * x))))


def _sigmoid(x):
    return 1.0 / (1.0 + jnp.exp(-x))


def _split3(x):
    p1 = x.astype(BF16)
    r1 = x - p1.astype(F32)
    p2 = r1.astype(BF16)
    r2 = r1 - p2.astype(F32)
    return p1, p2, r2.astype(BF16)


def _dot(a, b):
    return jnp.dot(a, b, preferred_element_type=F32)


def _split_spec(rows):
    return pl.BlockSpec((D_SSM // LANES, rows, LANES), lambda i: (0, i, 0))


def _split_store(ref, value, rows=slice(None)):
    for h in range(D_SSM // LANES):
        ref[h, rows, :] = value[:, h * LANES:(h + 1) * LANES]


def _split_load(ref, rows=slice(None)):
    return jnp.concatenate([ref[h, rows, :] for h in range(D_SSM // LANES)], axis=1)


def _bias_lane(head):
    return 0 if head % 2 else ATT_HEAD_DIM


def _inproj_body(x_ref, g_ref, w_ref, wvt_ref, bf_ref, lng_ref, lnb_ref, ws_ref, bs_ref, place_ref, qc_ref,
                 u_ref, q_ref, k_ref, v_ref, cum_ref, ygm_ref, carry_ref, *, tiles_per_seq):
    i = pl.program_id(0)

    @pl.when(i % tiles_per_seq == 0)
    def _():
        carry_ref[...] = jnp.zeros_like(carry_ref)

    tm = x_ref.shape[0]
    hb = _rms(x_ref[...], g_ref[...]).astype(BF16)

    def proj(lo, width):
        return _dot(hb, w_ref[:, lo:lo + width])

    _split_store(u_ref, proj(C_U, D_SSM))
    v_ref[...] = lax.dot_general(wvt_ref[...], hb, (((1,), (1,)), ((), ())),
                                 preferred_element_type=F32).astype(BF16)

    f = proj(C_F, LANES) + bf_ref[...]
    logf = (jnp.minimum(f, 0.0) - jnp.log(1.0 + jnp.exp(-jnp.abs(f)))) * LOG2E
    row = lax.broadcasted_iota(jnp.int32, (tm, tm), 0)
    col = lax.broadcasted_iota(jnp.int32, (tm, tm), 1)
    tri = (col <= row).astype(BF16)
    sums = _dot(tri, jnp.concatenate(_split3(logf), axis=1))
    cum = sums[:, :LANES] + sums[:, LANES:2 * LANES] + sums[:, 2 * LANES:] + carry_ref[...]
    cum_ref[...] = cum
    carry_ref[...] = cum[tm - 1:tm, :]

    lane = lax.broadcasted_iota(jnp.int32, (tm, LANES), 1)
    c1, c2, c3 = [jnp.where(lane < ATT_HEADS, piece.astype(F32), 0.0) for piece in _split3(cum)]
    packed = c1 + pltpu.roll(c2, ATT_HEADS, axis=1) + pltpu.roll(c3, 2 * ATT_HEADS, axis=1)
    placed = _dot(packed.astype(BF16), place_ref[...])
    upper = lax.broadcasted_iota(jnp.int32, (tm, LANES), 1) >= ATT_HEAD_DIM
    qf = proj(C_Q, D_ATT)
    kf = proj(C_K, D_ATT)
    for h in range(ATT_HEADS):
        pair = slice((h // 2) * LANES, (h // 2 + 1) * LANES)
        slot = slice(h * LANES, (h + 1) * LANES)
        own = upper if h % 2 else ~upper
        q_ref[:, slot] = jnp.where(own, qf[:, pair], qc_ref[:, slot]).astype(BF16)
        k_ref[:, slot] = jnp.where(own, kf[:, pair], placed[:, slot]).astype(BF16)

    zg = _gelu(proj(C_Z, 2 * D_GM))
    ug = zg[:, :D_GM]
    vg = zg[:, D_GM:]
    mean = jnp.mean(vg, axis=-1, keepdims=True)
    cen = vg - mean
    var = jnp.mean(cen * cen, axis=-1, keepdims=True)
    vn = (cen * lax.rsqrt(var + EPS) * lng_ref[...] + lnb_ref[...]).astype(BF16)
    lane_head = lax.broadcasted_iota(jnp.int32, (CHUNK, D_GM), 1) >> 6
    for c in range(tm // CHUNK):
        rows = slice(c * CHUNK, (c + 1) * CHUNK)
        vc = vn[rows, :]
        mixed = _dot(ws_ref[0], vc)
        for g in range(1, GM_HEADS):
            mixed = jnp.where(lane_head == g, _dot(ws_ref[g], vc), mixed)
        mixed = mixed + bs_ref[...]
        ygm_ref[rows, :] = (ug[rows, :] * mixed).astype(BF16)


def _attn_slot_constants():
    place = np.zeros((LANES, ATT_SLOTS), np.float32)
    qconst = np.zeros((1, ATT_SLOTS), np.float32)
    for h in range(ATT_HEADS):
        for piece in range(3):
            lane = h * LANES + _bias_lane(h) + piece
            place[piece * ATT_HEADS + h, lane] = 1.0
            qconst[0, lane] = -1.0
    return jnp.asarray(place, BF16), jnp.asarray(qconst)


def _inproj(x2, g, w, wvt, bf, lng, lnb, ws, bs, *, seq, tm):
    n = x2.shape[0]
    tps = seq // tm
    place, qconst = _attn_slot_constants()
    row = lambda width: pl.BlockSpec((tm, width), lambda i: (i, 0))
    return pl.pallas_call(
        functools.partial(_inproj_body, tiles_per_seq=tps),
        grid=(n // tm,),
        in_specs=[row(D_MODEL), _resident((1, D_MODEL)), _resident((D_MODEL, D_IN_PAD)),
                  _resident((D_ATT, D_MODEL)),
                  _resident((1, LANES)), _resident((1, D_GM)), _resident((1, D_GM)),
                  _resident((GM_HEADS, CHUNK, CHUNK)), _resident((CHUNK, D_GM)),
                  _resident((LANES, ATT_SLOTS)), _resident((1, ATT_SLOTS))],
        out_specs=[_split_spec(tm), row(ATT_SLOTS), row(ATT_SLOTS),
                   pl.BlockSpec((None, D_ATT, tm), lambda i: (i // tps, 0, i % tps)),
                   row(LANES), row(D_GM)],
        out_shape=[jax.ShapeDtypeStruct((D_SSM // LANES, n, LANES), F32),
                   jax.ShapeDtypeStruct((n, ATT_SLOTS), BF16),
                   jax.ShapeDtypeStruct((n, ATT_SLOTS), BF16),
                   jax.ShapeDtypeStruct((n // seq, D_ATT, seq), BF16),
                   jax.ShapeDtypeStruct((n, LANES), F32),
                   jax.ShapeDtypeStruct((n, D_GM), BF16)],
        scratch_shapes=[pltpu.VMEM((1, LANES), F32)],
        compiler_params=_cparams("arbitrary"),
        name="inproj",
    )(x2, g, w, wvt, bf, lng, lnb, ws, bs, place, qconst)


def _ssm_tokens(u_ref, s, tr):
    return _split_load(u_ref, pl.ds(s, tr, stride=SSM_T)).astype(BF16)


def _ssm_state_body(u_ref, f_ref, s_ref):
    tr = s_ref.shape[0]
    acc = _dot(_ssm_tokens(u_ref, 0, tr), f_ref[:D_SSM, :])
    for s in range(1, SSM_T):
        acc += _dot(_ssm_tokens(u_ref, s, tr), f_ref[s * D_SSM:(s + 1) * D_SSM, :])
    s_ref[...] = acc


def _ssm_state(u, fmat, *, tr):
    nc = u.shape[1] // SSM_T
    return pl.pallas_call(
        _ssm_state_body,
        grid=(nc // tr,),
        in_specs=[_split_spec(tr * SSM_T), _resident((SSM_ROW, 2 * SSM_NSTATE))],
        out_specs=pl.BlockSpec((tr, 2 * SSM_NSTATE), lambda i: (i, 0)),
        out_shape=jax.ShapeDtypeStruct((nc, 2 * SSM_NSTATE), F32),
        compiler_params=_cparams("parallel"),
        name="ssm_state",
    )(u, fmat)


def _ssm_scan_body(s_ref, a_ref, xp_ref, st_ref):
    @pl.when(pl.program_id(1) == 0)
    def _():
        st_ref[...] = jnp.zeros_like(st_ref)

    tc = s_ref.shape[0]
    are = a_ref[:, :SSM_NSTATE]
    aim = a_ref[:, SSM_NSTATE:]

    def step(c8, carry):
        xre, xim = carry
        base = pl.multiple_of(c8 * SUBLANES, SUBLANES)
        blk = s_ref[pl.ds(base, SUBLANES), :]
        prev_re, prev_im = [], []
        for r in range(SUBLANES):
            prev_re.append(xre)
            prev_im.append(xim)
            sre = blk[r:r + 1, :SSM_NSTATE]
            sim = blk[r:r + 1, SSM_NSTATE:]
            xre, xim = are * xre - aim * xim + sre, are * xim + aim * xre + sim
        xp_ref[pl.ds(base, SUBLANES), :SSM_NSTATE] = jnp.concatenate(prev_re, axis=0)
        xp_ref[pl.ds(base, SUBLANES), SSM_NSTATE:] = jnp.concatenate(prev_im, axis=0)
        return xre, xim

    xre, xim = lax.fori_loop(0, tc // SUBLANES, step,
                             (st_ref[:, :SSM_NSTATE], st_ref[:, SSM_NSTATE:]))
    st_ref[:, :SSM_NSTATE] = xre
    st_ref[:, SSM_NSTATE:] = xim


def _ssm_scan(s3, a, *, tc):
    b, ncb, w = s3.shape
    spec = pl.BlockSpec((None, tc, w), lambda bi, ci: (bi, ci, 0))
    return pl.pallas_call(
        _ssm_scan_body,
        grid=(b, ncb // tc),
        in_specs=[spec, pl.BlockSpec((1, w), lambda bi, ci: (0, 0))],
        out_specs=spec,
        out_shape=jax.ShapeDtypeStruct(s3.shape, F32),
        scratch_shapes=[pltpu.VMEM((1, w), F32)],
        compiler_params=_cparams("parallel", "arbitrary"),
        name="ssm_scan",
    )(s3, a)


def _ssm_out_body(u_ref, xp_ref, m_ref, e_ref, y_ref):
    tr = xp_ref.shape[0]
    us = [_ssm_tokens(u_ref, s, tr) for s in range(SSM_T)]
    xp = xp_ref[...].astype(BF16)
    for t in range(SSM_T):
        cols = slice(t * D_SSM, (t + 1) * D_SSM)
        acc = _dot(xp, e_ref[:, cols])
        for s in range(t + 1):
            acc += _dot(us[s], m_ref[s * D_SSM:(s + 1) * D_SSM, cols])
        _split_store(y_ref, acc, pl.ds(t, tr, stride=SSM_T))


def _ssm_out(u, xp, mmat, emat, *, tr):
    nc = xp.shape[0]
    return pl.pallas_call(
        _ssm_out_body,
        grid=(nc // tr,),
        in_specs=[_split_spec(tr * SSM_T),
                  pl.BlockSpec((tr, 2 * SSM_NSTATE), lambda i: (i, 0)),
                  _resident((SSM_ROW, SSM_ROW)), _resident((2 * SSM_NSTATE, SSM_ROW))],
        out_specs=_split_spec(tr * SSM_T),
        out_shape=jax.ShapeDtypeStruct((D_SSM // LANES, nc * SSM_T, LANES), F32),
        compiler_params=_cparams("parallel"),
        name="ssm_out",
    )(u, xp, mmat, emat)


def _ssm_matrices(lam_re, lam_im, log_dt, b_re, b_im, c_re, c_im):
    t, g, p, h = SSM_T, SSM_GROUPS, SSM_STATE, SSM_GROUP
    ns, nch = g * p, g * h
    hp = lax.Precision.HIGHEST
    lr, li = lam_re.astype(F32).reshape(ns), lam_im.astype(F32).reshape(ns)
    dt = jnp.repeat(jnp.exp(log_dt.astype(F32)), p)
    steps = jnp.arange(t + 1, dtype=F32)[:, None]
    mag = jnp.exp((lr * dt)[None] * steps)
    ang = (li * dt)[None] * steps
    pr, pi = mag * jnp.cos(ang), mag * jnp.sin(ang)
    nr, ni = pr[1] - 1.0, pi[1]
    den = lr * lr + li * li
    zr, zi = ((nr * lr + ni * li) / den)[:, None], ((ni * lr - nr * li) / den)[:, None]
    br, bi = b_re.astype(F32).reshape(ns, h), b_im.astype(F32).reshape(ns, h)
    cr = c_re.astype(F32).transpose(0, 2, 1).reshape(ns, h)
    ci = c_im.astype(F32).transpose(0, 2, 1).reshape(ns, h)
    same = jnp.asarray(np.repeat(np.repeat(np.eye(g, dtype=np.float32), p, axis=0), h, axis=1))
    spread = lambda w: jnp.tile(w, (1, g)) * same
    bhr, bhi = spread(zr * br - zi * bi), spread(zr * bi + zi * br)
    chr_, chi = spread(cr), spread(ci)
    prc, pic = pr.T[:, :, None], pi.T[:, :, None]
    er = (prc * chr_[:, None, :] - pic * chi[:, None, :]).reshape(ns, (t + 1) * nch)
    ei = (prc * chi[:, None, :] + pic * chr_[:, None, :]).reshape(ns, (t + 1) * nch)

    k_all = (jnp.dot(bhr.T, er[:, :t * nch], precision=hp)
             - jnp.dot(bhi.T, ei[:, :t * nch], precision=hp))
    mmat = jnp.concatenate(
        [jnp.pad(k_all[:, :(t - s) * nch], ((0, 0), (s * nch, 0))) for s in range(t)], axis=0)

    prr, pir = pr[:t][::-1][:, None, :], pi[:t][::-1][:, None, :]
    bhrt, bhit = bhr.T[None], bhi.T[None]
    fmat = jnp.concatenate([(bhrt * prr - bhit * pir).reshape(t * nch, ns),
                            (bhrt * pir + bhit * prr).reshape(t * nch, ns)], axis=1)

    emat = jnp.concatenate([er[:, nch:], -ei[:, nch:]], axis=0)

    a_row = jnp.concatenate([pr[t][None], pi[t][None]], axis=1)
    return mmat.astype(BF16), fmat.astype(BF16), emat.astype(BF16), a_row


def _attn_body(q_ref, k_ref, v_ref, cq_ref, o_ref,
               m_ref, acc_ref, s_ref, mt_ref, p_ref, al_ref, *, tq):
    qi = pl.program_id(2)
    nh = ATT_STEP_HEADS
    tk = tq // 2
    every = slice(0, tq)
    late = slice(tk, tq)

    def slot_lanes(j):
        return slice(j * LANES, (j + 1) * LANES)

    def scores(t, slot, qs=every, masked=False):
        start = pl.multiple_of(t * tk, tk)
        width = qs.stop - qs.start
        for j in range(nh):
            kt = k_ref[pl.ds(start, tk), slot_lanes(j)]
            s = lax.dot_general(kt, q_ref[qs, slot_lanes(j)], (((1,), (1,)), ((), ())),
                                preferred_element_type=F32)
            if masked:
                keypos = lax.broadcasted_iota(jnp.int32, (tk, width), 0)
                qpos = lax.broadcasted_iota(jnp.int32, (tk, width), 1)
                s = jnp.where(keypos <= qpos, s, -jnp.inf)
            s_ref[slot, j, :, qs] = s
            mt_ref[slot, j, :, qs] = jnp.max(s, axis=0, keepdims=True)

    def probs(slot, qs=every):
        for j in range(nh):
            m = m_ref[j, :, qs]
            cq = cq_ref[j:j + 1, qs]
            m_new = jnp.maximum(m, mt_ref[slot, j, :, qs] + cq)
            p_ref[slot, j, :, qs] = jnp.exp2(s_ref[slot, j, :, qs] + (cq - m_new)).astype(BF16)
            al_ref[slot, j, :, qs] = jnp.exp2(m - m_new)
            m_ref[j, :, qs] = m_new

    def values(t, slot, qs=every):
        start = pl.multiple_of(t * tk, tk)
        for j in range(nh):
            vt = v_ref[j * ATT_HEAD_DIM:(j + 1) * ATT_HEAD_DIM, pl.ds(start, tk)]
            vt = jnp.concatenate([vt, jnp.ones_like(vt)], axis=0)
            acc_ref[j, :, qs] = (al_ref[slot, j, :, qs] * acc_ref[j, :, qs]
                                 + _dot(vt, p_ref[slot, j, :, qs]))

    def trip(t, slot, **next_tile):
        values(t - 1, 1 - slot)
        probs(slot)
        scores(t + 1, 1 - slot, **next_tile)

    for j in range(nh):
        m_ref[j] = jnp.full((1, tq), -jnp.inf, F32)
        acc_ref[j] = jnp.zeros((LANES, tq), F32)

    last = 2 * qi + 1

    @pl.when(qi == 0)
    def _():
        scores(0, 0, masked=True)
        probs(0)
        scores(1, 1, late, masked=True)

    @pl.when(qi > 0)
    def _():
        scores(0, 0)
        probs(0)
        scores(1, 1)

        def pair(u, carry):
            t = 2 * u + 1
            trip(t, 1)
            trip(t + 1, 0)
            return carry

        lax.fori_loop(0, qi - 1, pair, 0)
        trip(last - 2, 1, masked=True)
        trip(last - 1, 0, qs=late, masked=True)

    values(last - 1, 0)
    probs(1, late)
    values(last, 1, late)
    for pair in range(nh // 2):
        outs = []
        for j in (2 * pair, 2 * pair + 1):
            acc = acc_ref[j]
            outs.append(acc[:ATT_HEAD_DIM, :] / acc[ATT_HEAD_DIM:, :])
        o_ref[:, slot_lanes(pair)] = jnp.concatenate(outs, axis=0).T.astype(BF16)


def _attention(q3, k3, vt4, cum3, *, tq):
    b, s, _ = q3.shape
    nh = ATT_STEP_HEADS
    return pl.pallas_call(
        functools.partial(_attn_body, tq=tq),
        scratch_shapes=[pltpu.VMEM((nh, 1, tq), F32), pltpu.VMEM((nh, LANES, tq), F32),
                        pltpu.VMEM((2, nh, tq // 2, tq), F32), pltpu.VMEM((2, nh, 1, tq), F32),
                        pltpu.VMEM((2, nh, tq // 2, tq), BF16), pltpu.VMEM((2, nh, 1, tq), F32)],
        grid=(b, ATT_HEADS // nh, s // tq),
        in_specs=[pl.BlockSpec((None, tq, nh * LANES), lambda bi, hg, qi: (bi, qi, hg)),
                  pl.BlockSpec((None, s, nh * LANES), lambda bi, hg, qi: (bi, 0, hg)),
                  pl.BlockSpec((None, nh * ATT_HEAD_DIM, s), lambda bi, hg, qi: (bi, hg, 0)),
                  pl.BlockSpec((None, None, nh, tq), lambda bi, hg, qi: (bi, hg, 0, qi))],
        out_specs=pl.BlockSpec((None, tq, nh * ATT_HEAD_DIM), lambda bi, hg, qi: (bi, qi, hg)),
        out_shape=jax.ShapeDtypeStruct((b, s, D_ATT), BF16),
        compiler_params=_cparams("parallel", "parallel", "arbitrary"),
        name="attention",
    )(q3, k3, vt4, cum3)


R_E1, R_E2, R_W1, R_W2, R_RANK1, R_RANK2 = range(6)


def _route(logits, cnt_ref):
    tm = logits.shape[0]
    lane = lax.broadcasted_iota(jnp.int32, logits.shape, 1).astype(F32)
    lg = jnp.where(lane < N_EXPERTS, logits, -jnp.inf)
    m1 = jnp.max(lg, axis=-1, keepdims=True)
    i1 = jnp.min(jnp.where(lg == m1, lane, float(LANES)), axis=-1, keepdims=True)
    lg2 = jnp.where(lane == i1, -jnp.inf, lg)
    m2 = jnp.max(lg2, axis=-1, keepdims=True)
    i2 = jnp.min(jnp.where(lg2 == m2, lane, float(LANES)), axis=-1, keepdims=True)
    e = jnp.exp(m2 - m1)
    w1 = 1.0 / (1.0 + e)
    w2 = e / (1.0 + e)

    hit1, hit2 = lane == i1, lane == i2
    oh1, oh2 = hit1.astype(BF16), hit2.astype(BF16)
    row = lax.broadcasted_iota(jnp.int32, (tm, tm), 0)
    col = lax.broadcasted_iota(jnp.int32, (tm, tm), 1)
    before = (col < row).astype(BF16)
    carry = cnt_ref[...]
    tot1 = jnp.sum(oh1.astype(F32), axis=0, keepdims=True)
    tot2 = jnp.sum(oh2.astype(F32), axis=0, keepdims=True)
    pos1 = _dot(before, oh1) + carry
    pos2 = _dot(before, oh2) + (carry + tot1)
    rank1 = jnp.sum(jnp.where(hit1, pos1, 0.0), axis=-1, keepdims=True)
    rank2 = jnp.sum(jnp.where(hit2, pos2, 0.0), axis=-1, keepdims=True)
    cnt_ref[...] = carry + tot1 + tot2

    rec = jnp.zeros(logits.shape, F32)
    for idx, val in ((R_E1, i1), (R_E2, i2), (R_W1, w1), (R_W2, w2), (R_RANK1, rank1), (R_RANK2, rank2)):
        rec = jnp.where(lane == idx, val, rec)
    return rec


def _outproj_body(*refs, with_router):
    if with_router:
        (x_ref, ys_ref, us_ref, ya_ref, yg_ref, d_ref, wglu_ref, gn_ref, w_ref, fg_ref, rw_ref,
         xo_ref, h_ref, route_ref, cnt_ref) = refs

        @pl.when(pl.program_id(0) == 0)
        def _():
            cnt_ref[...] = jnp.zeros_like(cnt_ref)
    else:
        (x_ref, ys_ref, us_ref, ya_ref, yg_ref, d_ref, wglu_ref, gn_ref, w_ref, fg_ref,
         xo_ref, h_ref) = refs

    tm = x_ref.shape[0]
    part = tm // OUTPROJ_PARTS
    o1, o2 = D_SSM, D_SSM + D_ATT
    for k in range(OUTPROJ_PARTS):
        rows = slice(k * part, (k + 1) * part)
        y1 = _gelu(_split_load(ys_ref, rows) + d_ref[...] * _split_load(us_ref, rows))
        y_ssm = y1 * _sigmoid(_dot(y1.astype(BF16), wglu_ref[...]))

        acc = _dot(_rms(y_ssm, gn_ref[:, :o1]).astype(BF16), w_ref[:o1, :])
        acc += _dot(_rms(ya_ref[rows, :].astype(F32), gn_ref[:, o1:o2]).astype(BF16), w_ref[o1:o2, :])
        acc += _dot(_rms(yg_ref[rows, :].astype(F32), gn_ref[:, o2:]).astype(BF16), w_ref[o2:, :])
        xn = x_ref[rows, :] + acc
        xo_ref[rows, :] = xn
        hn = _rms(xn, fg_ref[...])
        h_ref[rows, :] = hn.astype(h_ref.dtype)
        if with_router:
            rw = rw_ref[...]
            prod = sum(_dot(piece, rw) for piece in _split3(hn))
            logits = (prod + pltpu.roll(prod, LANES - N_EXPERTS, axis=1)
                      + pltpu.roll(prod, LANES - 2 * N_EXPERTS, axis=1))
            route_ref[rows, :] = _route(logits, cnt_ref)


def _outproj(x2, ys, us, ya, yg, d, wglu, gn, w, fg, rw, *, tm):
    n = x2.shape[0]
    with_router = rw is not None
    row = lambda width: pl.BlockSpec((tm, width), lambda i: (i, 0))
    in_specs = [row(D_MODEL), _split_spec(tm), _split_spec(tm), row(D_ATT), row(D_GM),
                _resident((1, D_SSM)), _resident((D_SSM, D_SSM)), _resident((1, D_MODEL)),
                _resident((D_MODEL, D_MODEL)), _resident((1, D_MODEL))]
    out_specs = [row(D_MODEL), row(D_MODEL)]
    out_shape = [jax.ShapeDtypeStruct((n, D_MODEL), F32),
                 jax.ShapeDtypeStruct((n, D_MODEL), F32 if with_router else BF16)]
    args = [x2, ys, us, ya, yg, d, wglu, gn, w, fg]
    if with_router:
        in_specs.append(_resident((D_MODEL, LANES)))
        out_specs += [row(LANES), pl.BlockSpec((1, LANES), lambda i: (0, 0))]
        out_shape += [jax.ShapeDtypeStruct((n, LANES), F32), jax.ShapeDtypeStruct((1, LANES), F32)]
        args.append(rw)
    return pl.pallas_call(
        functools.partial(_outproj_body, with_router=with_router),
        grid=(n // tm,),
        in_specs=in_specs, out_specs=out_specs, out_shape=out_shape,
        compiler_params=_cparams("arbitrary" if with_router else "parallel"),
        name="outproj_router" if with_router else "outproj",
    )(*args)


def _swiglu_tile(h, wg, wu, wd):
    a = _dot(h, wg)
    return _dot((a * _sigmoid(a) * _dot(h, wu)).astype(BF16), wd)


def _finish(x, acc, fin_ref, o_ref):
    xn = x + acc
    o_ref[...] = xn if fin_ref is None else _rms(xn, fin_ref[...])


def _dense_ffn_body(*refs, final_norm):
    if final_norm:
        h_ref, x_ref, wg_ref, wu_ref, wd_ref, fin_ref, o_ref, acc_ref = refs
    else:
        h_ref, x_ref, wg_ref, wu_ref, wd_ref, o_ref, acc_ref = refs
        fin_ref = None
    f = pl.program_id(1)

    @pl.when(f == 0)
    def _():
        acc_ref[...] = jnp.zeros_like(acc_ref)

    acc_ref[...] += _swiglu_tile(h_ref[...], wg_ref[...], wu_ref[...], wd_ref[...])

    @pl.when(f == pl.num_programs(1) - 1)
    def _():
        _finish(x_ref[...], acc_ref[...], fin_ref, o_ref)


def _dense_ffn(h, x2, wg, wu, wd, fin, *, tm, tf):
    n = x2.shape[0]
    dff = wg.shape[1]
    row = lambda: pl.BlockSpec((tm, D_MODEL), lambda i, f: (i, 0))
    mode = dict(pipeline_mode=pl.Buffered(1)) if tf == dff else {}
    in_specs = [row(), row(),
                pl.BlockSpec((D_MODEL, tf), lambda i, f: (0, f), **mode),
                pl.BlockSpec((D_MODEL, tf), lambda i, f: (0, f), **mode),
                pl.BlockSpec((tf, D_MODEL), lambda i, f: (f, 0), **mode)]
    args = [h, x2, wg, wu, wd]
    if fin is not None:
        in_specs.append(pl.BlockSpec((1, D_MODEL), lambda i, f: (0, 0)))
        args.append(fin)
    return pl.pallas_call(
        functools.partial(_dense_ffn_body, final_norm=fin is not None),
        grid=(n // tm, dff // tf),
        in_specs=in_specs, out_specs=row(),
        out_shape=jax.ShapeDtypeStruct((n, D_MODEL), F32),
        scratch_shapes=[pltpu.VMEM((tm, D_MODEL), F32)],
        compiler_params=_cparams("parallel", "arbitrary"),
        name="dense_ffn",
    )(*args)


def _row_copy(src, src_row, dst, dst_row, sem):
    return pltpu.make_async_copy(src.at[pl.ds(src_row, 1)], dst.at[pl.ds(dst_row, 1)], sem)


def _moe_dispatch_body(dest_ref, gaps_ref, h_ref, xs_ref, zero_ref, sem, zsem, *, tme):
    tm = h_ref.shape[0]

    @pl.when(pl.program_id(0) == 0)
    def _():
        zero_ref[...] = jnp.zeros_like(zero_ref)

        def zero_copy(gap, i):
            if gap == N_EXPERTS:
                return pltpu.make_async_copy(zero_ref, xs_ref.at[pl.ds(i * tme, tme)], zsem)
            return _row_copy(zero_ref, 0, xs_ref, i, zsem)

        for gap in range(N_EXPERTS + 1):
            lo, hi = gaps_ref[0, gap], gaps_ref[1, gap]

            def start(i, carry, gap=gap):
                zero_copy(gap, i).start()
                return carry

            def wait(i, carry, gap=gap):
                zero_copy(gap, i).wait()
                return carry

            lax.fori_loop(lo, hi, start, 0)
            lax.fori_loop(lo, hi, wait, 0)

    def issue(r, carry):
        for k in range(2):
            _row_copy(h_ref, r, xs_ref, dest_ref[k, r], sem).start(priority=k)
        return carry

    lax.fori_loop(0, tm, issue, 0, unroll=8)
    for k in range(2):
        pltpu.make_async_copy(h_ref, xs_ref.at[pl.ds(0, tm)], sem).wait()


def _moe_dispatch(dest, gaps, h, *, rows, tm, tme):
    n = h.shape[0]
    return pl.pallas_call(
        functools.partial(_moe_dispatch_body, tme=tme),
        grid=(n // tm,),
        in_specs=[pl.BlockSpec((None, 2, tm), lambda i: (i, 0, 0), memory_space=pltpu.SMEM),
                  pl.BlockSpec(memory_space=pltpu.SMEM),
                  pl.BlockSpec((tm, D_MODEL), lambda i: (i, 0))],
        out_specs=pl.BlockSpec(memory_space=pl.ANY),
        out_shape=jax.ShapeDtypeStruct((rows, D_MODEL), F32),
        scratch_shapes=[pltpu.VMEM((tme, D_MODEL), F32), pltpu.SemaphoreType.DMA(()),
                        pltpu.SemaphoreType.DMA(())],
        compiler_params=pltpu.CompilerParams(dimension_semantics=("arbitrary",),
                                             vmem_limit_bytes=VMEM_LIMIT, has_side_effects=True),
        name="moe_dispatch",
    )(dest, gaps, h)


def _moe_ffn_body(te_ref, nu_ref, xs_ref, wg_ref, wu_ref, wd_ref, ys_ref, acc_ref):
    del te_ref
    i = pl.program_id(0)
    f = pl.program_id(1)

    @pl.when(i < nu_ref[0])
    def _():
        @pl.when(f == 0)
        def _():
            acc_ref[...] = jnp.zeros_like(acc_ref)

        acc_ref[...] += _swiglu_tile(xs_ref[...].astype(BF16), wg_ref[...], wu_ref[...], wd_ref[...])

        @pl.when(f == pl.num_programs(1) - 1)
        def _():
            ys_ref[...] = acc_ref[...]

    @pl.when((i >= nu_ref[0]) & (f == pl.num_programs(1) - 1))
    def _():
        ys_ref[...] = jnp.zeros_like(ys_ref)


def _moe_ffn(tile_expert, n_used, xs, wg, wu, wd, *, tme, tf):
    rows = xs.shape[0]
    dff = wg.shape[-1]
    nf = dff // tf
    row_idx = lambda i, f, te, nu: (jnp.minimum(i, nu[0] - 1), 0)
    col = lambda i, f, nu: jnp.where(i < nu[0], f, nf - 1)
    mode = dict(pipeline_mode=pl.Buffered(1)) if nf == 1 else {}
    grid_spec = pltpu.PrefetchScalarGridSpec(
        num_scalar_prefetch=2,
        grid=(rows // tme, nf),
        in_specs=[pl.BlockSpec((tme, D_MODEL), row_idx),
                  pl.BlockSpec((None, D_MODEL, tf), lambda i, f, te, nu: (te[i], 0, col(i, f, nu)), **mode),
                  pl.BlockSpec((None, D_MODEL, tf), lambda i, f, te, nu: (te[i], 0, col(i, f, nu)), **mode),
                  pl.BlockSpec((None, tf, D_MODEL), lambda i, f, te, nu: (te[i], col(i, f, nu), 0), **mode)],
        out_specs=pl.BlockSpec((tme, D_MODEL), lambda i, f, te, nu: (i, 0)),
        scratch_shapes=[pltpu.VMEM((tme, D_MODEL), F32)])
    return pl.pallas_call(
        _moe_ffn_body,
        grid_spec=grid_spec,
        out_shape=jax.ShapeDtypeStruct((rows, D_MODEL), F32),
        compiler_params=_cparams("arbitrary", "arbitrary"),
        name="moe_ffn",
    )(tile_expert, n_used, xs, wg, wu, wd)


def _moe_combine_body(*refs, final_norm):
    if final_norm:
        dest_ref, x_ref, route_ref, ys_ref, fin_ref, o_ref, buf_ref, sem = refs
    else:
        dest_ref, x_ref, route_ref, ys_ref, o_ref, buf_ref, sem = refs
        fin_ref = None
    tm = x_ref.shape[0]

    def issue(r, carry):
        for k in range(2):
            _row_copy(ys_ref, dest_ref[k, r], buf_ref.at[k], r, sem).start(priority=k)
        return carry

    lax.fori_loop(0, tm, issue, 0, unroll=8)
    for k in range(2):
        pltpu.make_async_copy(ys_ref.at[pl.ds(0, tm)], buf_ref.at[k], sem).wait()

    route = route_ref[...]
    lane = lax.broadcasted_iota(jnp.int32, route.shape, 1)
    w1 = jnp.sum(jnp.where(lane == R_W1, route, 0.0), axis=-1, keepdims=True)
    w2 = jnp.sum(jnp.where(lane == R_W2, route, 0.0), axis=-1, keepdims=True)
    _finish(x_ref[...], w1 * buf_ref[0] + w2 * buf_ref[1], fin_ref, o_ref)


def _moe_combine(dest, x2, route, ys, fin, *, tm):
    n = x2.shape[0]
    row = lambda width: pl.BlockSpec((tm, width), lambda i: (i, 0))
    in_specs = [pl.BlockSpec((None, 2, tm), lambda i: (i, 0, 0), memory_space=pltpu.SMEM),
                row(D_MODEL), row(LANES), pl.BlockSpec(memory_space=pl.ANY)]
    args = [dest, x2, route, ys]
    if fin is not None:
        in_specs.append(pl.BlockSpec((1, D_MODEL), lambda i: (0, 0)))
        args.append(fin)
    return pl.pallas_call(
        functools.partial(_moe_combine_body, final_norm=fin is not None),
        grid=(n // tm,),
        in_specs=in_specs, out_specs=row(D_MODEL),
        out_shape=jax.ShapeDtypeStruct((n, D_MODEL), F32),
        scratch_shapes=[pltpu.VMEM((2, tm, D_MODEL), F32), pltpu.SemaphoreType.DMA(())],
        compiler_params=_cparams("arbitrary"),
        name="moe_combine",
    )(*args)


def _moe_layout(route, counts, *, tm, tme):
    n = route.shape[0]
    cnt = counts[0, :N_EXPERTS].astype(jnp.int32)
    padded = (cnt + (tme - 1)) // tme * tme
    ends = jnp.cumsum(padded)
    starts = ends - padded
    experts = jnp.arange(N_EXPERTS, dtype=F32)

    def rows(e_lane, rank_lane):
        start = jnp.sum(jnp.where(route[:, e_lane, None] == experts[None], starts[None], 0), axis=1)
        return start + route[:, rank_lane].astype(jnp.int32)

    dest = jnp.stack([rows(R_E1, R_RANK1), rows(R_E2, R_RANK2)])
    dest = dest.reshape(2, n // tm, tm).transpose(1, 0, 2)
    n_tiles = (2 * n) // tme + N_EXPERTS
    tile_start = jnp.arange(n_tiles, dtype=jnp.int32) * tme
    tile_expert = jnp.minimum(jnp.sum(tile_start[:, None] >= ends[None, :], axis=1), N_EXPERTS - 1)
    n_used = (ends[-1] // tme).reshape(1)
    tile_expert = jnp.where(tile_start < ends[-1], tile_expert, tile_expert[jnp.maximum(n_used[0] - 1, 0)])
    gaps = jnp.stack([jnp.concatenate([starts + cnt, n_used]),
                      jnp.concatenate([ends, jnp.full((1,), n_tiles, jnp.int32)])]).astype(jnp.int32)
    return dest, gaps, tile_expert.astype(jnp.int32), n_used.astype(jnp.int32), n_tiles * tme


def _largest_tile(total, cap, mult):
    best = mult
    for t in range(mult, min(total, cap) + 1, mult):
        if total % t == 0:
            best = t
    return best


def _mxu_tile(total, cap):
    if total % MXU_WIDTH == 0:
        return _largest_tile(total, cap, MXU_WIDTH)
    return _largest_tile(total, cap, LANES)


def _pad_lanes(a):
    return jnp.pad(a, [(0, 0)] * (a.ndim - 1) + [(0, LANES - a.shape[-1])])


def kernel(x, mix_norm_g, w_in, b_forget, ssm_lambda_re, ssm_lambda_im, ssm_log_dt, ssm_b_re, ssm_b_im, ssm_c_re, ssm_c_im, ssm_d, ssm_w_glu, gm_ln_g, gm_ln_b, gm_w_s, gm_b_s, group_norm_g, w_out, ffn_norm_g, dense_w_gate, dense_w_up, dense_w_down, router_w, moe_w_gate, moe_w_up, moe_w_down, final_norm_g):
    bsz, seq, _ = x.shape
    n = bsz * seq
    depth = w_in.shape[0]
    assert seq % CHUNK == 0 and seq % SSM_T == 0
    tm = _largest_tile(seq, 512, CHUNK)
    tq = _largest_tile(seq, 1024, 2 * LANES)
    nc = n // SSM_T
    ncb = seq // SSM_T
    tr = _largest_tile(nc, 512, 8)
    tc = _largest_tile(ncb, 128, 8)

    x2 = x.reshape(n, D_MODEL).astype(F32)
    tril = jnp.tril(jnp.ones((CHUNK, CHUNK), F32))
    i0, i1, i2, i3, i4 = (D_SSM, D_SSM + D_ATT, D_SSM + 2 * D_ATT, D_SSM + 3 * D_ATT,
                          D_SSM + 3 * D_ATT + ATT_HEADS)
    for layer in range(depth):
        wl = w_in[layer]
        w_re = jnp.concatenate(
            [wl[:, :i0], wl[:, i0:i1] * (ATT_HEAD_DIM ** -0.5 * LOG2E), wl[:, i1:i2], wl[:, i4:],
             _pad_lanes(wl[:, i3:i4])], axis=1).astype(BF16)
        ws = (gm_w_s[layer].astype(F32) * tril[None]).astype(BF16)
        bs = jnp.repeat(gm_b_s[layer].astype(F32).T, GM_HEAD_DIM, axis=1)
        u, q, k, v, cum, ygm = _inproj(
            x2, mix_norm_g[layer].reshape(1, D_MODEL), w_re, wl[:, i2:i3].T.astype(BF16),
            _pad_lanes(b_forget[layer].reshape(1, ATT_HEADS).astype(F32)),
            gm_ln_g[layer].reshape(1, D_GM), gm_ln_b[layer].reshape(1, D_GM), ws, bs,
            seq=seq, tm=tm)

        mmat, fmat, emat, a_row = _ssm_matrices(
            ssm_lambda_re[layer], ssm_lambda_im[layer], ssm_log_dt[layer], ssm_b_re[layer],
            ssm_b_im[layer], ssm_c_re[layer], ssm_c_im[layer])
        s_loc = _ssm_state(u, fmat, tr=tr)
        xprev = _ssm_scan(s_loc.reshape(bsz, ncb, 2 * SSM_NSTATE), a_row, tc=tc)
        y_ssm = _ssm_out(u, xprev.reshape(nc, 2 * SSM_NSTATE), mmat, emat, tr=tr)

        cum_rows = cum[:, :ATT_HEADS].reshape(bsz, seq, ATT_HEADS // ATT_STEP_HEADS, ATT_STEP_HEADS)
        cum_rows = cum_rows.transpose(0, 2, 3, 1)
        y_att = _attention(q.reshape(bsz, seq, ATT_SLOTS), k.reshape(bsz, seq, ATT_SLOTS), v,
                           cum_rows, tq=tq).reshape(n, D_ATT)

        is_moe = layer % 2 == 1
        j = layer // 2
        rw = (_pad_lanes(jnp.concatenate(_split3(router_w[j].astype(F32)), axis=1))
              if is_moe else None)
        outs = _outproj(x2, y_ssm, u, y_att, ygm, ssm_d[layer].reshape(1, D_SSM).astype(F32),
                        ssm_w_glu[layer].astype(BF16), group_norm_g[layer].reshape(1, D_MODEL),
                        w_out[layer].astype(BF16), ffn_norm_g[layer].reshape(1, D_MODEL), rw,
                        tm=_largest_tile(n, OUTPROJ_PARTS * MXU_WIDTH, tm))
        fin = final_norm_g.reshape(1, D_MODEL) if layer == depth - 1 else None
        if is_moe:
            x_mid, h, route, counts = outs
            dffe = moe_w_gate.shape[-1]
            tme = min(MOE_ROW_TILE, 2 * n)
            tmm = _largest_tile(n, MOE_TOKEN_TILE, tm)
            dest, gaps, tile_expert, n_used, rows = _moe_layout(route, counts, tm=tmm, tme=tme)
            xs = _moe_dispatch(dest, gaps, h, rows=rows, tm=tmm, tme=tme)
            ys = _moe_ffn(tile_expert, n_used, xs, moe_w_gate[j].astype(BF16),
                          moe_w_up[j].astype(BF16), moe_w_down[j].astype(BF16), tme=tme,
                          tf=_mxu_tile(dffe, 3584))
            x2 = _moe_combine(dest, x_mid, route, ys, fin, tm=tmm)
        else:
            x_mid, h = outs
            dff = dense_w_gate.shape[-1]
            x2 = _dense_ffn(h, x_mid, dense_w_gate[j].astype(BF16), dense_w_up[j].astype(BF16),
                            dense_w_down[j].astype(BF16), fin, tm=tm,
                            tf=_mxu_tile(dff, 2816))
    return x2.reshape(bsz, seq, D_MODEL).astype(x.dtype)
```
